```python
import math
import jax, jax.numpy as jnp
from jax import lax
import numpy as np

D_MODEL = 1024
BATCH = 32
SEQ = 256
DEPTH = 2
DEC_BATCH = 2
DEC_SEQ = 2048
PAST_LEN = 256

GRID_W = 64
H_A = 4
DQK_A = 64
DV_A = 128
W_A = H_A * DV_A
H_B = 4
DK_B = 64
DV_B = 64
W_B = H_B * DV_B
CHUNK = 64
H_C = 4
W_C = 256
BW_C = W_C // H_C
CONV_K = 4
CONV_PAD_L = 1
RG_C = 8.0
D_FF = 4 * D_MODEL
Q_BLOCK = 128
ROPE_BASE = 10000.0
EPS = 1e-6
SPLITS = (H_A * 2 * DQK_A, H_A * 2 * DQK_A, W_A,
          H_B * DK_B, H_B * DK_B, H_B * DK_B, W_B, W_B,
          W_C, W_C)
N_IN = sum(SPLITS)
F32 = jnp.float32

kernel_name = 'hybrid_diffattn_hgrn2_rglru_prefix_dit_step'


def _rms(x, g):
    xf = x.astype(F32)
    y = xf * lax.rsqrt(jnp.mean(xf * xf, axis=-1, keepdims=True) + EPS)
    return (y * g.astype(F32)).astype(x.dtype)


def _rev(a):
    return jnp.flip(a, axis=1)


def _rope_axis(x, pos):
    nf = x.shape[-1] // 2
    inv = ROPE_BASE ** (-jnp.arange(nf, dtype=F32) / nf)
    ang = pos.astype(F32)[:, None] * inv
    cos = jnp.cos(ang)[None, :, None, None, :]
    sin = jnp.sin(ang)[None, :, None, None, :]
    xf = x.astype(F32)
    x1, x2 = xf[..., :nf], xf[..., nf:]
    return jnp.concatenate([x1 * cos - x2 * sin, x1 * sin + x2 * cos], axis=-1).astype(x.dtype)


def _rope_2d(x):
    rows = x.shape[1] // GRID_W
    row = jnp.repeat(jnp.arange(rows), GRID_W)
    col = jnp.tile(jnp.arange(GRID_W), rows)
    half = x.shape[-1] // 2
    return jnp.concatenate([_rope_axis(x[..., :half], row), _rope_axis(x[..., half:], col)], axis=-1)


def _diff_attention(q, k, v, lam):
    B, Lq, H, _, Dk = q.shape
    nb = Lq // Q_BLOCK
    qb = q.reshape(B, nb, Q_BLOCK, H, 2, Dk).transpose(1, 0, 2, 3, 4, 5)
    scale = Dk ** -0.5

    def one_block(qblk):
        s = jnp.einsum('bqhmd,bkhmd->bhmqk', qblk, k).astype(F32) * scale
        p = jax.nn.softmax(s, axis=-1)
        w = p[:, :, 0] - lam * p[:, :, 1]
        return jnp.einsum('bhqk,bkhv->bqhv', w.astype(v.dtype), v)

    o = lax.map(one_block, qb)
    return o.transpose(1, 0, 2, 3, 4).reshape(B, Lq, H, v.shape[-1])


def _hgrn2_scan(q, k, v, logf, s0):
    B, L, H, Dk = q.shape
    Dv = v.shape[-1]
    n = L // CHUNK

    def chunks(a):
        return a.astype(F32).reshape(B, n, CHUNK, H, a.shape[-1]).transpose(1, 0, 3, 2, 4)

    causal = jnp.tril(jnp.ones((CHUNK, CHUNK), dtype=bool))[None, None, :, :, None]

    def step(s, inp):
        qc, kc, vc, gc = inp
        b = jnp.cumsum(gc, axis=2)
        rel = jnp.where(causal, b[:, :, :, None, :] - b[:, :, None, :, :], -jnp.inf)
        att = jnp.einsum('bhtd,bhsd,bhtsd->bhts', qc, kc, jnp.exp(rel))
        o = (jnp.einsum('bhts,bhsv->bhtv', att, vc)
             + jnp.einsum('bhtd,bhdv->bhtv', qc * jnp.exp(b), s))
        b_last = b[:, :, -1:, :]
        s_new = (jnp.exp(b_last[:, :, 0, :])[..., None] * s
                 + jnp.einsum('bhsd,bhsv->bhdv', kc * jnp.exp(b_last - b), vc))
        return s_new, o

    s_fin, o = lax.scan(step, s0.astype(F32), (chunks(q), chunks(k), chunks(v), chunks(logf)))
    return o.transpose(1, 0, 3, 2, 4).reshape(B, L, H, Dv), s_fin


def _lin_combine(left, right):
    a1, b1 = left
    a2, b2 = right
    return a1 * a2, a2 * b1 + b2


def _rglru_scan(x, w_r, b_r, w_i, b_i, lam, h0):
    B, L, _ = x.shape
    xb = x.reshape(B, L, H_C, BW_C)
    r = jax.nn.sigmoid(jnp.einsum('blhi,hij->blhj', xb, w_r.astype(F32)).reshape(B, L, W_C) + b_r.astype(F32))
    i = jax.nn.sigmoid(jnp.einsum('blhi,hij->blhj', xb, w_i.astype(F32)).reshape(B, L, W_C) + b_i.astype(F32))
    log_a = -RG_C * r * jax.nn.softplus(-lam.astype(F32))
    a = jnp.exp(log_a)
    u = jnp.sqrt(-jnp.expm1(2.0 * log_a)) * (i * x)
    u = u.at[:, 0].add(a[:, 0] * h0.astype(F32))
    _, h = lax.associative_scan(_lin_combine, (a, u), axis=1)
    return h, h[:, -1]


def _dwconv(x, w, b):
    L = x.shape[1]
    xp = jnp.pad(x, ((0, 0), (CONV_PAD_L, CONV_K - 1 - CONV_PAD_L), (0, 0)))
    y = b
    for j in range(CONV_K):
        y = y + xp[:, j:j + L] * w[j]
    return y


def _mixers(h, lp, lb, lam_init, ctx):
    B, L, _ = h.shape
    dt = h.dtype
    idx = np.cumsum(np.array(SPLITS))[:-1].tolist()
    q_a, k_a, v_a, q_b, ff_b, fb_b, i_b, g_b, x_c, g_c = jnp.split(h @ lp['w_in'], idx, axis=-1)

    q = _rms(q_a.reshape(B, L, H_A, 2, DQK_A), lp['qk_norm'][0])
    k = _rms(k_a.reshape(B, L, H_A, 2, DQK_A), lp['qk_norm'][1])
    v = v_a.reshape(B, L, H_A, DV_A)
    lv = lp['diff_lambda'].astype(F32)
    lam = jnp.exp(jnp.sum(lv[0] * lv[1])) - jnp.exp(jnp.sum(lv[2] * lv[3])) + lam_init
    if ctx is None:
        k_all, v_all = k, v
    else:
        q = _rope_2d(q)
        k_all = jnp.concatenate([_rope_2d(k), ctx[0].astype(dt)], axis=1)
        v_all = jnp.concatenate([v, ctx[1].astype(dt)], axis=1)
    o_a = _diff_attention(q, k_all, v_all, lam)
    o_a = (_rms(o_a, lp['subln']) * (1.0 - lam_init)).reshape(B, L, W_A)

    qh = jax.nn.silu(q_b).reshape(B, L, H_B, DK_B)
    ih = i_b.reshape(B, L, H_B, DV_B)

    def forget(fpre, lbd):
        fpre = fpre.astype(F32)
        logf = jnp.logaddexp(jnp.log(lbd), jnp.log1p(-lbd) + jax.nn.log_sigmoid(fpre))
        kk = (1.0 - lbd) * jax.nn.sigmoid(-fpre)
        return logf.reshape(B, L, H_B, DK_B), kk.reshape(B, L, H_B, DK_B)

    lf_f, k_f = forget(ff_b, lb[0])
    lf_b, k_b = forget(fb_b, lb[1])
    if ctx is None:
        s0f = jnp.zeros((B, H_B, DK_B, DV_B), F32)
        s0b = s0f
    else:
        s0f, s0b = ctx[2][:, 0], ctx[2][:, 1]
    o_f, s_f = _hgrn2_scan(qh, k_f, ih, lf_f, s0f)
    o_r, s_b = _hgrn2_scan(_rev(qh), _rev(k_b), _rev(ih), _rev(lf_b), s0b)
    o_b = (o_f + _rev(o_r)).astype(dt)
    o_b = (_rms(o_b, lp['hgrn_onorm']) * jax.nn.silu(g_b).reshape(B, L, H_B, DV_B)).reshape(B, L, W_B)

    xc = _dwconv(x_c, lp['conv_w'], lp['conv_b']).astype(F32)
    if ctx is None:
        h0f = jnp.zeros((B, W_C), F32)
        h0b = h0f
    else:
        h0f, h0b = ctx[3][:, 0], ctx[3][:, 1]
    rw, rb, rl = lp['rg_w'], lp['rg_b'], lp['rg_lambda']
    y_f, hf = _rglru_scan(xc, rw[0, 0], rb[0, 0], rw[0, 1], rb[0, 1], rl[0], h0f)
    y_r, hb = _rglru_scan(_rev(xc), rw[1, 0], rb[1, 0], rw[1, 1], rb[1, 1], rl[1], h0b)
    o_c = (y_f + _rev(y_r)).astype(dt) * jax.nn.gelu(g_c)

    out = jnp.concatenate([o_a, o_b, o_c], axis=-1) @ lp['w_out']
    if ctx is None:
        new_ctx = (k, v, jnp.stack([s_f, s_b], axis=1).astype(dt), jnp.stack([hf, hb], axis=1).astype(dt))
    else:
        new_ctx = None
    return out, new_ctx


def _block(x, mod, lp, lb, lam_init, ctx):
    sh1, sc1, g1, sh2, sc2, g2 = jnp.split(mod[:, None, :], 6, axis=-1)
    h = _rms(x, lp['norm1']) * (1 + sc1) + sh1
    mix, new_ctx = _mixers(h, lp, lb, lam_init, ctx)
    x = x + g1 * mix
    h = _rms(x, lp['norm2']) * (1 + sc2) + sh2
    x = x + g2 * (jnp.square(jax.nn.relu(h @ lp['w_ff1'])) @ lp['w_ff2'])
    return x, new_ctx


def setup_inputs(seed: int = 0) -> dict:
    key = jax.random.key(seed)
    ks = jax.random.split(key, 26)
    D = D_MODEL

    def nrm(k, shape, s):
        return jax.random.normal(k, shape, F32) * s

    u = jax.random.uniform(ks[25], (DEPTH, 2, W_C), F32, 0.9, 0.999)
    a_base = u ** (1.0 / RG_C)
    rg_lambda = jnp.log(a_base) - jnp.log1p(-a_base)
    return {
        'x_prompt': nrm(ks[0], (BATCH, SEQ, D), 1.0),
        'x_sample': nrm(ks[1], (DEC_BATCH, DEC_SEQ, D), 1.0),
        'cache_k': nrm(ks[2], (DEC_BATCH, DEPTH, PAST_LEN, H_A, 2, DQK_A), 1.0),
        'cache_v': nrm(ks[3], (DEC_BATCH, DEPTH, PAST_LEN, H_A, DV_A), 1.0),
        'state_hgrn': nrm(ks[4], (DEC_BATCH, DEPTH, 2, H_B, DK_B, DV_B), 0.5),
        'state_rglru': nrm(ks[5], (DEC_BATCH, DEPTH, 2, W_C), 0.5),
        'c': nrm(ks[6], (DEC_BATCH, D), 1.0),
        'c_ctx': nrm(ks[7], (D,), 1.0),
        'w_mod': nrm(ks[8], (DEPTH, D, 6 * D), 0.5 * D ** -0.5),
        'b_mod': nrm(ks[9], (DEPTH, 6 * D), 0.02),
        'norm1': 1.0 + nrm(ks[10], (DEPTH, D), 0.02),
        'norm2': 1.0 + nrm(ks[11], (DEPTH, D), 0.02),
        'w_in': nrm(ks[12], (DEPTH, D, N_IN), D ** -0.5),
        'w_out': nrm(ks[13], (DEPTH, D, D), D ** -0.5),
        'qk_norm': 1.0 + nrm(ks[14], (DEPTH, 2, DQK_A), 0.02),
        'diff_lambda': nrm(ks[15], (DEPTH, 4, DQK_A), 0.1),
        'subln': 1.0 + nrm(ks[16], (DEPTH, DV_A), 0.02),
        'hgrn_lb': nrm(ks[17], (DEPTH, 2, H_B * DK_B), 0.1),
        'hgrn_onorm': 1.0 + nrm(ks[18], (DEPTH, DV_B), 0.02),
        'conv_w': nrm(ks[19], (DEPTH, CONV_K, W_C), CONV_K ** -0.5),
        'conv_b': nrm(ks[20], (DEPTH, W_C), 0.02),
        'rg_w': nrm(ks[21], (DEPTH, 2, 2, H_C, BW_C, BW_C), BW_C ** -0.5),
        'rg_b': nrm(ks[22], (DEPTH, 2, 2, W_C), 0.02),
        'rg_lambda': rg_lambda,
        'w_ff1': nrm(ks[23], (DEPTH, D, D_FF), D ** -0.5),
        'w_ff2': nrm(ks[24], (DEPTH, D_FF, D), D_FF ** -0.5),
    }


def reference(x_prompt, x_sample, cache_k, cache_v, state_hgrn, state_rglru, c, c_ctx,
              w_mod, b_mod, norm1, norm2, w_in, w_out, qk_norm, diff_lambda, subln,
              hgrn_lb, hgrn_onorm, conv_w, conv_b, rg_w, rg_b, rg_lambda, w_ff1, w_ff2):
    lb_all = jnp.cumsum(jax.nn.softmax(hgrn_lb.astype(F32), axis=0), axis=0)
    yp, ys = x_prompt, x_sample
    ks, vs, shs, srs = [], [], [], []
    for l in range(DEPTH):
        lp = {'norm1': norm1[l], 'norm2': norm2[l], 'w_in': w_in[l], 'w_out': w_out[l],
              'qk_norm': qk_norm[l], 'diff_lambda': diff_lambda[l], 'subln': subln[l],
              'hgrn_onorm': hgrn_onorm[l], 'conv_w': conv_w[l], 'conv_b': conv_b[l],
              'rg_w': rg_w[l], 'rg_b': rg_b[l], 'rg_lambda': rg_lambda[l],
              'w_ff1': w_ff1[l], 'w_ff2': w_ff2[l]}
        lb = lb_all[l] - lb_all[0]
        lam_init = 0.8 - 0.6 * math.exp(-0.3 * l)
        mod_ctx = (jax.nn.silu(c_ctx) @ w_mod[l] + b_mod[l])[None]
        mod_lat = jax.nn.silu(c) @ w_mod[l] + b_mod[l]
        yp, (k_l, v_l, sh_l, sr_l) = _block(yp, mod_ctx, lp, lb, lam_init, None)
        ys, _ = _block(ys, mod_lat, lp, lb, lam_init,
                       (cache_k[:, l], cache_v[:, l], state_hgrn[:, l], state_rglru[:, l]))
        ks.append(k_l)
        vs.append(v_l)
        shs.append(sh_l)
        srs.append(sr_l)
    return (yp, ys, jnp.stack(ks, axis=1), jnp.stack(vs, axis=1), jnp.stack(shs, axis=1), jnp.stack(srs, axis=1))
```

```python
import functools
import math

import numpy as np
import jax
import jax.numpy as jnp
from jax import lax
from jax.experimental import pallas as pl
from jax.experimental.pallas import tpu as pltpu

F32 = jnp.float32
BF16 = jnp.bfloat16

GRID_W = 64
H_A, DQK_A, DV_A = 4, 64, 128
H_B, DK_B, DV_B = 4, 64, 64
W_B = H_B * DV_B
H_C, W_C = 4, 256
CONV_K = 4
RG_C = 8.0
ROPE_BASE = 10000.0
EPS = 1e-6
N_MOD = 6
D_FF_CHUNK = 1024

TILE = 256
V7X_VMEM_BYTES = 64 * 1024 * 1024
VMEM_LIMIT = V7X_VMEM_BYTES - 8 * 1024 * 1024
NEG_INF = float("-inf")


def _dot(a, b):
    return jnp.dot(a, b, preferred_element_type=F32)


def _dot_nt(a, b):
    return lax.dot_general(a, b, (((1,), (1,)), ((), ())), preferred_element_type=F32)


def _dot_tn(a, b):
    return lax.dot_general(a, b, (((0,), (0,)), ((), ())), preferred_element_type=F32)


def _group_sum(x, ones_bd):
    hi = x.astype(BF16)
    lo = (x - hi.astype(F32)).astype(BF16)
    return _dot(hi, ones_bd) + _dot(lo, ones_bd)


def _rms_rows(x, gain):
    return x * lax.rsqrt(jnp.mean(x * x, axis=-1, keepdims=True) + EPS) * gain


def _silu(x):
    return x * jax.nn.sigmoid(x)


def _softplus(x):
    return jnp.maximum(x, 0.0) + jnp.log1p(jnp.exp(-jnp.abs(x)))


def _log_sigmoid(x):
    return jnp.minimum(x, 0.0) - jnp.log1p(jnp.exp(-jnp.abs(x)))


def _gelu_tanh(x):
    return x * (0.5 * (1.0 + jnp.tanh(math.sqrt(2.0 / math.pi) * (x + 0.044715 * (x * x * x)))))


def _params(semantics):
    return pltpu.CompilerParams(dimension_semantics=semantics, vmem_limit_bytes=VMEM_LIMIT)


def _mod_kernel(c_ref, w_ref, b_ref, o_ref):
    c = c_ref[...]
    o_ref[0] = _dot(_silu(c).astype(BF16), w_ref[0].astype(BF16)) + b_ref[0]


def _modulation(c8, w_mod, b_mod):
    depth, d, n = w_mod.shape
    tn = d
    return pl.pallas_call(
        _mod_kernel,
        grid=(depth, n // tn),
        in_specs=[
            pl.BlockSpec((8, d), lambda l, j: (0, 0)),
            pl.BlockSpec((1, d, tn), lambda l, j: (l, 0, j)),
            pl.BlockSpec((1, 1, tn), lambda l, j: (l, 0, j)),
        ],
        out_specs=pl.BlockSpec((1, 8, tn), lambda l, j: (l, 0, j)),
        out_shape=jax.ShapeDtypeStruct((depth, 8, n), F32),
        compiler_params=_params(("arbitrary", "arbitrary")),
        name="mod",
    )(c8, w_mod, b_mod.reshape(depth, 1, n))


def _proj_kernel(x_ref, mod_ref, n1_ref, w_ref, gq_ref, gk_ref, ones_ref, o_ref):
    x = x_ref[...]
    d = x.shape[-1]
    sh = mod_ref[0, :, 0:d]
    sc = mod_ref[0, :, d:2 * d]
    h = _rms_rows(x, n1_ref[...]) * (1.0 + sc) + sh
    p = _dot(h.astype(BF16), w_ref[...])
    ones_bd = ones_ref[...]
    wq = H_A * 2 * DQK_A

    def qk_norm(z, gain):
        ms = _group_sum(z * z, ones_bd) * (1.0 / DQK_A)
        return z * lax.rsqrt(ms + EPS) * gain

    o_ref[:, 0:wq] = qk_norm(p[:, 0:wq], gq_ref[...])
    o_ref[:, wq:2 * wq] = qk_norm(p[:, wq:2 * wq], gk_ref[...])
    o_ref[:, 2 * wq:] = p[:, 2 * wq:]


def _in_projection(x, mod, tiles_per_mod, n1, w_in, gq, gk, ones_bd):
    t, d = x.shape
    n_in = w_in.shape[1]
    wq = gq.shape[1]
    return pl.pallas_call(
        _proj_kernel,
        grid=(t // TILE,),
        in_specs=[
            pl.BlockSpec((TILE, d), lambda i: (i, 0)),
            pl.BlockSpec((1, 1, N_MOD * d), lambda i: (i // tiles_per_mod, 0, 0)),
            pl.BlockSpec((1, d), lambda i: (0, 0)),
            pl.BlockSpec((d, n_in), lambda i: (0, 0)),
            pl.BlockSpec((1, wq), lambda i: (0, 0)),
            pl.BlockSpec((1, wq), lambda i: (0, 0)),
            pl.BlockSpec((wq, wq), lambda i: (0, 0)),
        ],
        out_specs=pl.BlockSpec((TILE, n_in), lambda i: (i, 0)),
        out_shape=jax.ShapeDtypeStruct((t, n_in), F32),
        compiler_params=_params(("arbitrary",)),
        name="proj",
    )(x, mod, n1, w_in, gq, gk, ones_bd)


def _diff_lambda(dl_ref, lam_init):
    lv = dl_ref[...]
    a = jnp.sum(lv[0:1] * lv[1:2], axis=-1, keepdims=True)
    b = jnp.sum(lv[2:3] * lv[3:4], axis=-1, keepdims=True)
    return jnp.exp(a) - jnp.exp(b) + lam_init


def _diff_attention(q, k_of, v_of, lam, sub_gain, lam_init, o_ref):
    hw = 2 * DQK_A
    first = lax.broadcasted_iota(jnp.int32, (1, hw), 1) < DQK_A
    for h in range(H_A):
        hs = slice(h * hw, (h + 1) * hw)
        qh = q[:, hs]
        kh = k_of(hs)
        s0 = _dot_nt(jnp.where(first, qh, 0.0).astype(BF16), kh)
        s1 = _dot_nt(jnp.where(first, 0.0, qh).astype(BF16), kh)
        e0 = jnp.exp(s0 - jnp.max(s0, axis=-1, keepdims=True))
        e1 = jnp.exp(s1 - jnp.max(s1, axis=-1, keepdims=True))
        r0 = 1.0 / jnp.sum(e0, axis=-1, keepdims=True)
        r1 = lam / jnp.sum(e1, axis=-1, keepdims=True)
        w = (e0 * r0 - e1 * r1).astype(BF16)
        oh = _dot(w, v_of(hs))
        o_ref[:, hs] = _rms_rows(oh, sub_gain) * (1.0 - lam_init)


def _attn_ctx_kernel(q_ref, k_ref, v_ref, dl_ref, sg_ref, o_ref, *, lam_init):
    lam = _diff_lambda(dl_ref, lam_init)
    k = k_ref[...].astype(BF16)
    v = v_ref[...].astype(BF16)
    _diff_attention(q_ref[...], lambda hs: k[:, hs], lambda hs: v[:, hs], lam, sg_ref[...],
                    lam_init, o_ref)


def _attention_ctx(proj, seq, dl, sg, lam_init):
    t = proj.shape[0]
    w = H_A * DV_A
    return pl.pallas_call(
        functools.partial(_attn_ctx_kernel, lam_init=lam_init),
        grid=(t // seq,),
        in_specs=[
            pl.BlockSpec((seq, w), lambda b: (b, 0)),
            pl.BlockSpec((seq, w), lambda b: (b, 1)),
            pl.BlockSpec((seq, w), lambda b: (b, 2)),
            pl.BlockSpec(dl.shape, lambda b: (0, 0)),
            pl.BlockSpec(sg.shape, lambda b: (0, 0)),
        ],
        out_specs=pl.BlockSpec((seq, w), lambda b: (b, 0)),
        out_shape=jax.ShapeDtypeStruct((t, w), F32),
        compiler_params=_params(("arbitrary",)),
        name="attn_ctx",
    )(proj, proj, proj, dl, sg)


def _rope(x, cos, sin_lo, sin_hi):
    return (x * cos + pltpu.roll(x, 2 * DQK_A - 16, 1) * sin_lo + pltpu.roll(x, 16, 1) * sin_hi)


def _attn_lat_kernel(q_ref, k_ref, v_ref, ck_ref, cv_ref, qcos_ref, qslo_ref, qshi_ref,
                     kcos_ref, kslo_ref, kshi_ref, dl_ref, sg_ref, o_ref, kall, vall, *, lam_init):
    hw = 2 * DQK_A
    n_new = k_ref.shape[0]

    @pl.when(pl.program_id(1) == 0)
    def _():
        kcos, kslo, kshi = kcos_ref[...], kslo_ref[...], kshi_ref[...]
        for h in range(H_A):
            hs = slice(h * hw, (h + 1) * hw)
            kall[0:n_new, hs] = _rope(k_ref[:, hs], kcos, kslo, kshi).astype(BF16)
        kall[n_new:, :] = ck_ref[...].astype(BF16)
        vall[0:n_new, :] = v_ref[...].astype(BF16)
        vall[n_new:, :] = cv_ref[...].astype(BF16)

    lam = _diff_lambda(dl_ref, lam_init)
    qcos, qslo, qshi = qcos_ref[...], qslo_ref[...], qshi_ref[...]
    q = jnp.concatenate(
        [_rope(q_ref[:, h * hw:(h + 1) * hw], qcos, qslo, qshi) for h in range(H_A)], axis=-1)
    _diff_attention(q, lambda hs: kall[:, hs], lambda hs: vall[:, hs], lam, sg_ref[...],
                    lam_init, o_ref)


def _attention_lat(proj, n_batch, cache_k, cache_v, layer, rope_tabs, dl, sg, lam_init):
    t = proj.shape[0]
    seq = t // n_batch
    past = cache_k.shape[2]
    w = H_A * DV_A
    hw = 2 * DQK_A
    nq = seq // TILE
    cos, slo, shi = rope_tabs
    tab_q = pl.BlockSpec((TILE, hw), lambda b, j: (j, 0))
    tab_k = pl.BlockSpec((seq, hw), lambda b, j: (0, 0))
    return pl.pallas_call(
        functools.partial(_attn_lat_kernel, lam_init=lam_init),
        grid=(n_batch, nq),
        in_specs=[
            pl.BlockSpec((TILE, w), lambda b, j: (b * nq + j, 0)),
            pl.BlockSpec((seq, w), lambda b, j: (b, 1)),
            pl.BlockSpec((seq, w), lambda b, j: (b, 2)),
            pl.BlockSpec((None, None, past, w), lambda b, j: (b, layer, 0, 0)),
            pl.BlockSpec((None, None, past, w), lambda b, j: (b, layer, 0, 0)),
            tab_q, tab_q, tab_q, tab_k, tab_k, tab_k,
            pl.BlockSpec(dl.shape, lambda b, j: (0, 0)),
            pl.BlockSpec(sg.shape, lambda b, j: (0, 0)),
        ],
        out_specs=pl.BlockSpec((TILE, w), lambda b, j: (b * nq + j, 0)),
        out_shape=jax.ShapeDtypeStruct((t, w), F32),
        scratch_shapes=[pltpu.VMEM((seq + past, w), BF16), pltpu.VMEM((seq + past, w), BF16)],
        compiler_params=_params(("arbitrary", "arbitrary")),
        name="attn_lat",
    )(proj, proj, proj, cache_k, cache_v, cos, slo, shi, cos, slo, shi, dl, sg)


def _tile_aux():
    rows = lax.broadcasted_iota(jnp.int32, (TILE, 1), 0)
    lane = lax.broadcasted_iota(jnp.int32, (1, W_B), 1)
    head_masks = [(lane // DK_B) == h for h in range(H_B)]
    xor = (lax.broadcasted_iota(jnp.int32, (TILE, TILE), 0)
           ^ lax.broadcasted_iota(jnp.int32, (TILE, TILE), 1))
    return rows, head_masks, xor


def _hgrn_lower_bound(lb_ref, layer, direction, depth):
    xs = [lb_ref[2 * j + direction:2 * j + direction + 1, :] for j in range(depth)]
    m = functools.reduce(jnp.maximum, xs)
    es = [jnp.exp(x - m) for x in xs]
    return sum(es[1:layer + 1]) / sum(es)


def _hgrn_gates(fpre, lbd):
    if lbd is None:
        return _log_sigmoid(fpre), jax.nn.sigmoid(-fpre)
    a = jnp.log(lbd)
    b = jnp.log1p(-lbd) + _log_sigmoid(fpre)
    logf = jnp.maximum(a, b) + jnp.log1p(jnp.exp(-jnp.abs(a - b)))
    return logf, (1.0 - lbd) * jax.nn.sigmoid(-fpre)


def _hgrn_tile(qh, kk, vv, g, reverse, aux, ones_bd, state_t):
    rows, head_masks, xor = aux
    prefix = g
    total = g
    att = [jnp.zeros((TILE, TILE), F32) for _ in range(H_B)]
    m = 1
    while m < TILE:
        right = (rows & m) != 0
        if not reverse:
            eq = jnp.where(right, prefix, NEG_INF)
            ek = jnp.where(right, NEG_INF, total - prefix)
        else:
            eq = jnp.where(right, NEG_INF, total - prefix + g)
            ek = jnp.where(right, prefix - g, NEG_INF)
        qt = (qh * jnp.exp(eq)).astype(BF16)
        kt = (kk * jnp.exp(ek)).astype(BF16)
        same_block = xor < 2 * m
        for h in range(H_B):
            pm = _dot_nt(jnp.where(head_masks[h], qt, 0.0).astype(BF16), kt)
            att[h] = att[h] + jnp.where(same_block, pm, 0.0)
        left_total = pltpu.roll(total, m, 0)
        right_total = pltpu.roll(total, TILE - m, 0)
        prefix = prefix + jnp.where(right, left_total, 0.0)
        total = total + jnp.where(right, left_total, right_total)
        m *= 2
    out = _group_sum(qh * kk, ones_bd) * vv
    vb = vv.astype(BF16)
    for h in range(H_B):
        out = out + _dot(att[h].astype(BF16), jnp.where(head_masks[h], vb, 0.0).astype(BF16))
    if not reverse:
        q_decay, k_decay = prefix, total - prefix
    else:
        q_decay, k_decay = total - prefix + g, prefix - g
    if state_t is not None:
        out = out + _dot_nt((qh * jnp.exp(q_decay)).astype(BF16), state_t.astype(BF16))
    k_hat = (kk * jnp.exp(k_decay)).astype(BF16)
    return out, k_hat, total[0:1, :]


def _block_diag_mask():
    r = lax.broadcasted_iota(jnp.int32, (W_B, W_B), 0) // DK_B
    c = lax.broadcasted_iota(jnp.int32, (W_B, W_B), 1) // DV_B
    return r == c


def _conv(x, prev_row, next_rows, cw_ref, cb_ref, rows):
    x_m1 = jnp.where(rows == 0, prev_row, pltpu.roll(x, 1, 0))
    x_p1 = jnp.where(rows == TILE - 1, next_rows[0:1], pltpu.roll(x, TILE - 1, 0))
    x_p2 = jnp.where(rows == TILE - 2, next_rows[0:1],
                     jnp.where(rows == TILE - 1, next_rows[1:2], pltpu.roll(x, TILE - 2, 0)))
    y = cb_ref[...] + x_m1 * cw_ref[0:1, :]
    y = y + x * cw_ref[1:2, :]
    y = y + x_p1 * cw_ref[2:3, :]
    return y + x_p2 * cw_ref[3:4, :]


def _rglru_tile(xc, r_pre, i_pre, lam_row, reverse, rows, h_in):
    r = jax.nn.sigmoid(r_pre)
    i = jax.nn.sigmoid(i_pre)
    log_a = -RG_C * r * _softplus(-lam_row)
    a = jnp.exp(log_a)
    u = jnp.sqrt(jnp.tanh(-log_a) * (a * a + 1.0)) * (i * xc)
    k = 1
    while k < TILE:
        if not reverse:
            valid = rows >= k
            a_s, u_s = pltpu.roll(a, k, 0), pltpu.roll(u, k, 0)
        else:
            valid = rows < TILE - k
            a_s, u_s = pltpu.roll(a, TILE - k, 0), pltpu.roll(u, TILE - k, 0)
        u = u + a * jnp.where(valid, u_s, 0.0)
        a = a * jnp.where(valid, a_s, 1.0)
        k *= 2
    if h_in is not None:
        u = u + a * h_in
    return u


def _rec_inputs(refs):
    qb, ff, fb, ib, gb, xcol, gcol = [r[...] for r in refs]
    return _silu(qb), ff, fb, ib, gb, xcol, gcol


def _hgrn_out(o_sum, gb, onorm_gain, ones_bd):
    ms = _group_sum(o_sum * o_sum, ones_bd) * (1.0 / DV_B)
    return o_sum * lax.rsqrt(ms + EPS) * onorm_gain * _silu(gb)


def _rec_ctx_kernel(q_ref, ff_ref, fb_ref, i_ref, g_ref, x_ref, gc_ref, lb_ref, on_ref, cw_ref,
                    cb_ref, wg_ref, bg_ref, lam_ref, ones_ref, ob_ref, oc_ref, st_ref, hs_ref,
                    *, layer, depth):
    aux = _tile_aux()
    rows = aux[0]
    ones_bd = ones_ref[...]
    qh, ff, fb, ib, gb, xcol, gcol = _rec_inputs((q_ref, ff_ref, fb_ref, i_ref, g_ref, x_ref, gc_ref))
    bd = _block_diag_mask()
    o_sum = None
    for direction, fpre in enumerate((ff, fb)):
        lbd = None if layer == 0 else _hgrn_lower_bound(lb_ref, layer, direction, depth)
        g, kk = _hgrn_gates(fpre, lbd)
        o, k_hat, _ = _hgrn_tile(qh, kk, ib, g, direction == 1, aux, ones_bd, None)
        o_sum = o if o_sum is None else o_sum + o
        st_ref[0, direction] = jnp.where(bd, _dot_tn(k_hat, ib.astype(BF16)), 0.0)
    ob_ref[...] = _hgrn_out(o_sum, gb, on_ref[...], ones_bd)

    zero_row = jnp.zeros((1, W_C), F32)
    xc = _conv(xcol, zero_row, jnp.zeros((2, W_C), F32), cw_ref, cb_ref, rows)
    gates = _dot(xc.astype(BF16), wg_ref[...]) + bg_ref[...]
    y_f = _rglru_tile(xc, gates[:, 0:W_C], gates[:, W_C:2 * W_C], lam_ref[0:1, :], False, rows, None)
    y_b = _rglru_tile(xc, gates[:, 2 * W_C:3 * W_C], gates[:, 3 * W_C:], lam_ref[1:2, :], True,
                      rows, None)
    oc_ref[...] = (y_f + y_b) * _gelu_tanh(gcol)
    hs_ref[0, 0:1, :] = y_f[TILE - 1:TILE, :]
    hs_ref[0, 1:2, :] = y_b[0:1, :]


def _col_spec(col, row_map):
    return pl.BlockSpec((TILE, W_B), lambda *idx: (row_map(*idx), col))


_REC_COLS = (6, 7, 8, 9, 10, 11, 12)


def _full_spec(a):
    nd = a.ndim
    return pl.BlockSpec(a.shape, lambda *idx: (0,) * nd)


def _recurrence_ctx(proj, layer, depth, consts):
    t = proj.shape[0]
    n = t // TILE
    return pl.pallas_call(
        functools.partial(_rec_ctx_kernel, layer=layer, depth=depth),
        grid=(n,),
        in_specs=[_col_spec(c, lambda b: b) for c in _REC_COLS] + [_full_spec(a) for a in consts],
        out_specs=[
            pl.BlockSpec((TILE, W_B), lambda b: (b, 0)),
            pl.BlockSpec((TILE, W_C), lambda b: (b, 0)),
            pl.BlockSpec((1, 2, W_B, W_B), lambda b: (b, 0, 0, 0)),
            pl.BlockSpec((1, 2, W_C), lambda b: (b, 0, 0)),
        ],
        out_shape=[
            jax.ShapeDtypeStruct((t, W_B), F32),
            jax.ShapeDtypeStruct((t, W_C), F32),
            jax.ShapeDtypeStruct((n, 2, W_B, W_B), F32),
            jax.ShapeDtypeStruct((n, 2, W_C), F32),
        ],
        compiler_params=_params(("arbitrary",)),
        name="rec_ctx",
    )(*([proj] * len(_REC_COLS)), *consts)


def _rec_lat_kernel(q_ref, f_ref, i_ref, g_ref, x_ref, xp_ref, xn_ref, gc_ref, s0_ref, h0_ref,
                    *rest, layer, depth, direction, n_tiles):
    if direction == 0:
        (lb_ref, on_ref, cw_ref, cb_ref, wg_ref, bg_ref, lam_ref, ones_ref,
         o_ref, y_ref, state, hstate) = rest
    else:
        (of_ref, yf_ref, lb_ref, on_ref, cw_ref, cb_ref, wg_ref, bg_ref, lam_ref, ones_ref,
         o_ref, y_ref, state, hstate) = rest
    step = pl.program_id(1)
    tile = step if direction == 0 else n_tiles - 1 - step
    reverse = direction == 1

    @pl.when(step == 0)
    def _():
        state[...] = s0_ref[0, 0]
        hstate[...] = h0_ref[0, 0]

    aux = _tile_aux()
    rows = aux[0]
    ones_bd = ones_ref[...]
    qh = _silu(q_ref[...])
    ib = i_ref[...]
    lbd = None if layer == 0 else _hgrn_lower_bound(lb_ref, layer, direction, depth)
    g, kk = _hgrn_gates(f_ref[...], lbd)
    o, k_hat, total = _hgrn_tile(qh, kk, ib, g, reverse, aux, ones_bd, state[...])
    upd = jnp.where(_block_diag_mask(), _dot_tn(ib.astype(BF16), k_hat), 0.0)
    state[...] = state[...] * jnp.exp(total) + upd

    prev_row = jnp.where(tile == 0, 0.0, xp_ref[TILE - 1:TILE, :])
    next_rows = jnp.where(tile == n_tiles - 1, 0.0, xn_ref[0:2, :])
    xc = _conv(x_ref[...], prev_row, next_rows, cw_ref, cb_ref, rows)
    gates = _dot(xc.astype(BF16), wg_ref[...]) + bg_ref[...]
    c0 = 2 * W_C * direction
    y = _rglru_tile(xc, gates[:, c0:c0 + W_C], gates[:, c0 + W_C:c0 + 2 * W_C],
                    lam_ref[direction:direction + 1, :], reverse, rows, hstate[...])
    hstate[...] = y[0:1, :] if reverse else y[TILE - 1:TILE, :]

    if direction == 0:
        o_ref[...] = o
        y_ref[...] = y
    else:
        o_ref[...] = _hgrn_out(of_ref[...] + o, g_ref[...], on_ref[...], ones_bd)
        y_ref[...] = (yf_ref[...] + y) * _gelu_tanh(gc_ref[...])


def _recurrence_lat(proj, n_batch, layer, depth, direction, state0, h0, prev, consts):
    t = proj.shape[0]
    n_tiles = t // (TILE * n_batch)

    def tile_of(b, s):
        return s if direction == 0 else n_tiles - 1 - s

    def row(b, s):
        return b * n_tiles + tile_of(b, s)

    def row_prev(b, s):
        return b * n_tiles + jnp.maximum(tile_of(b, s) - 1, 0)

    def row_next(b, s):
        return b * n_tiles + jnp.minimum(tile_of(b, s) + 1, n_tiles - 1)

    f_col = 7 + direction
    in_specs = [_col_spec(6, row), _col_spec(f_col, row), _col_spec(9, row), _col_spec(10, row),
                _col_spec(11, row), _col_spec(11, row_prev), _col_spec(11, row_next),
                _col_spec(12, row),
                pl.BlockSpec((1, 1, W_B, W_B), lambda b, s: (b, direction, 0, 0)),
                pl.BlockSpec((1, 1, 1, W_C), lambda b, s: (b, direction, 0, 0))]
    args = [proj] * 8 + [state0, h0]
    if direction == 1:
        in_specs += [pl.BlockSpec((TILE, W_B), lambda b, s: (row(b, s), 0)),
                     pl.BlockSpec((TILE, W_C), lambda b, s: (row(b, s), 0))]
        args += list(prev)
    in_specs += [_full_spec(a) for a in consts]
    args += list(consts)
    return pl.pallas_call(
        functools.partial(_rec_lat_kernel, layer=layer, depth=depth, direction=direction,
                          n_tiles=n_tiles),
        grid=(n_batch, n_tiles),
        in_specs=in_specs,
        out_specs=[pl.BlockSpec((TILE, W_B), lambda b, s: (row(b, s), 0)),
                   pl.BlockSpec((TILE, W_C), lambda b, s: (row(b, s), 0))],
        out_shape=[jax.ShapeDtypeStruct((t, W_B), F32), jax.ShapeDtypeStruct((t, W_C), F32)],
        scratch_shapes=[pltpu.VMEM((W_B, W_B), F32), pltpu.VMEM((1, W_C), F32)],
        compiler_params=_params(("arbitrary", "arbitrary")),
        name="rec_lat_fwd" if direction == 0 else "rec_lat_bwd",
    )(*args)


def _out_kernel(x_ref, oa_ref, ob_ref, oc_ref, mod_ref, n2_ref, wo_ref, w1_ref, w2_ref, y_ref):
    x = x_ref[...]
    d = x.shape[-1]
    g1 = mod_ref[0, :, 2 * d:3 * d]
    sh2 = mod_ref[0, :, 3 * d:4 * d]
    sc2 = mod_ref[0, :, 4 * d:5 * d]
    g2 = mod_ref[0, :, 5 * d:6 * d]
    wa = H_A * DV_A
    mix = _dot(oa_ref[...].astype(BF16), wo_ref[0:wa, :])
    mix = mix + _dot(ob_ref[...].astype(BF16), wo_ref[wa:wa + W_B, :])
    mix = mix + _dot(oc_ref[...].astype(BF16), wo_ref[wa + W_B:, :])
    x1 = x + g1 * mix
    h = (_rms_rows(x1, n2_ref[...]) * (1.0 + sc2) + sh2).astype(BF16)
    acc = jnp.zeros_like(x)
    for c in range(w1_ref.shape[1] // D_FF_CHUNK):
        cs = slice(c * D_FF_CHUNK, (c + 1) * D_FF_CHUNK)
        f = jnp.maximum(_dot(h, w1_ref[:, cs]), 0.0)
        acc = acc + _dot((f * f).astype(BF16), w2_ref[cs, :])
    y_ref[...] = x1 + g2 * acc


def _out_mlp(x, oa, ob, oc, mod, tiles_per_mod, n2, w_out, w_ff1, w_ff2):
    t, d = x.shape
    resident = functools.partial(pl.BlockSpec, pipeline_mode=pl.Buffered(1))
    return pl.pallas_call(
        _out_kernel,
        grid=(t // TILE,),
        in_specs=[
            pl.BlockSpec((TILE, d), lambda i: (i, 0)),
            pl.BlockSpec((TILE, oa.shape[1]), lambda i: (i, 0)),
            pl.BlockSpec((TILE, ob.shape[1]), lambda i: (i, 0)),
            pl.BlockSpec((TILE, oc.shape[1]), lambda i: (i, 0)),
            pl.BlockSpec((1, 1, N_MOD * d), lambda i: (i // tiles_per_mod, 0, 0)),
            pl.BlockSpec((1, d), lambda i: (0, 0)),
            resident(w_out.shape, lambda i: (0, 0)),
            resident(w_ff1.shape, lambda i: (0, 0)),
            resident(w_ff2.shape, lambda i: (0, 0)),
        ],
        out_specs=pl.BlockSpec((TILE, d), lambda i: (i, 0)),
        out_shape=jax.ShapeDtypeStruct((t, d), F32),
        compiler_params=_params(("arbitrary",)),
        name="out_mlp",
    )(x, oa, ob, oc, mod, n2, w_out, w_ff1, w_ff2)


def _ones_block_diag(n, group):
    idx = np.arange(n) // group
    return jnp.asarray((idx[:, None] == idx[None, :]).astype(np.float32), dtype=BF16)


def _rope_tables(seq):
    nf = DQK_A // 4
    inv = ROPE_BASE ** (-jnp.arange(nf, dtype=F32) / nf)
    pos = jnp.arange(seq)
    ang_r = (pos // GRID_W).astype(F32)[:, None] * inv
    ang_c = (pos % GRID_W).astype(F32)[:, None] * inv
    z = jnp.zeros_like(ang_r)
    cos = jnp.concatenate([jnp.cos(ang_r)] * 2 + [jnp.cos(ang_c)] * 2, axis=-1)
    sin_lo = jnp.concatenate([-jnp.sin(ang_r), z, -jnp.sin(ang_c), z], axis=-1)
    sin_hi = jnp.concatenate([z, jnp.sin(ang_r), z, jnp.sin(ang_c)], axis=-1)
    return tuple(jnp.tile(a, (1, 2)) for a in (cos, sin_lo, sin_hi))


def _gate_weights(rg_w_l):
    eye = jnp.eye(H_C, dtype=rg_w_l.dtype)
    w = jnp.einsum('dghij,hk->dghikj', rg_w_l, eye)
    w = w.reshape(2, 2, W_C, W_C)
    return jnp.transpose(w, (2, 0, 1, 3)).reshape(W_C, 4 * W_C).astype(BF16)


def _state_block_diag_t(s):
    eye = jnp.eye(H_B, dtype=s.dtype)
    w = jnp.einsum('...hdv,hk->...hvkd', s, eye)
    return w.reshape(s.shape[:-3] + (W_B, W_B))


def _state_from_block_diag(full):
    n = full.shape[0]
    f = full.reshape(n, 2, H_B, DK_B, H_B, DV_B)
    return jnp.stack([f[:, :, h, :, h, :] for h in range(H_B)], axis=2)


def kernel(x_prompt, x_sample, cache_k, cache_v, state_hgrn, state_rglru, c, c_ctx, w_mod, b_mod,
           norm1, norm2, w_in, w_out, qk_norm, diff_lambda, subln, hgrn_lb, hgrn_onorm, conv_w,
           conv_b, rg_w, rg_b, rg_lambda, w_ff1, w_ff2):
    batch, seq, d = x_prompt.shape
    dec_batch, dec_seq, _ = x_sample.shape
    depth = w_in.shape[0]
    past = cache_k.shape[2]
    wq = H_A * 2 * DQK_A

    c8 = jnp.concatenate([c_ctx[None], c, jnp.zeros((8 - 1 - dec_batch, d), F32)], axis=0)
    mod_all = _modulation(c8, w_mod, b_mod)

    ones_qk = _ones_block_diag(wq, DQK_A)
    ones_b = _ones_block_diag(W_B, DV_B)
    rope_tabs = _rope_tables(dec_seq)
    ck = cache_k.reshape(dec_batch, depth, past, wq)
    cv = cache_v.reshape(dec_batch, depth, past, H_A * DV_A)
    lb2 = hgrn_lb.reshape(depth * 2, W_B)

    yp = x_prompt.reshape(batch * seq, d)
    ys = x_sample.reshape(dec_batch * dec_seq, d)
    ks, vs, shs, srs = [], [], [], []
    for l in range(depth):
        lam_init = 0.8 - 0.6 * math.exp(-0.3 * l)
        mod_ctx = mod_all[l, 0:1][:, None, :]
        mod_lat = mod_all[l, 1:1 + dec_batch][:, None, :]
        n1 = norm1[l][None]
        n2 = norm2[l][None]
        w_in_l = w_in[l].astype(BF16)
        w_out_l = w_out[l].astype(BF16)
        w1_l = w_ff1[l].astype(BF16)
        w2_l = w_ff2[l].astype(BF16)
        gq = jnp.tile(qk_norm[l, 0], H_A * 2)[None] * (DQK_A ** -0.5)
        gk = jnp.tile(qk_norm[l, 1], H_A * 2)[None]
        dl = diff_lambda[l]
        sg = subln[l][None]
        consts = (lb2, jnp.tile(hgrn_onorm[l], H_B)[None], conv_w[l], conv_b[l][None],
                  _gate_weights(rg_w[l]), rg_b[l].reshape(1, 4 * W_C), rg_lambda[l], ones_b)

        pc = _in_projection(yp, mod_ctx, batch * seq // TILE, n1, w_in_l, gq, gk, ones_qk)
        oa = _attention_ctx(pc, seq, dl, sg, lam_init)
        ob, oc, st, hs = _recurrence_ctx(pc, l, depth, consts)
        yp = _out_mlp(yp, oa, ob, oc, mod_ctx, batch * seq // TILE, n2, w_out_l, w1_l, w2_l)
        ks.append(pc[:, wq:2 * wq].reshape(batch, seq, H_A, 2, DQK_A))
        vs.append(pc[:, 2 * wq:2 * wq + H_A * DV_A].reshape(batch, seq, H_A, DV_A))
        shs.append(_state_from_block_diag(st))
        srs.append(hs)

        pls = _in_projection(ys, mod_lat, dec_seq // TILE, n1, w_in_l, gq, gk, ones_qk)
        oa = _attention_lat(pls, dec_batch, ck, cv, l, rope_tabs, dl, sg, lam_init)
        s0 = _state_block_diag_t(state_hgrn[:, l])
        h0 = state_rglru[:, l][:, :, None, :]
        fwd = _recurrence_lat(pls, dec_batch, l, depth, 0, s0, h0, None, consts)
        ob, oc = _recurrence_lat(pls, dec_batch, l, depth, 1, s0, h0, fwd, consts)
        ys = _out_mlp(ys, oa, ob, oc, mod_lat, dec_seq // TILE, n2, w_out_l, w1_l, w2_l)

    return (yp.reshape(batch, seq, d), ys.reshape(dec_batch, dec_seq, d),
            jnp.stack(ks, axis=1), jnp.stack(vs, axis=1), jnp.stack(shs, axis=1),
            jnp.stack(srs, axis=1))
```

```python
import functools
import math

import numpy as np
import jax
import jax.numpy as jnp
from jax import lax
from jax.experimental import pallas as pl
from jax.experimental.pallas import tpu as pltpu

F32 = jnp.float32
BF16 = jnp.bfloat16

GRID_W = 64
H_A, DQK_A, DV_A = 4, 64, 128
H_B, DK_B, DV_B = 4, 64, 64
W_B = H_B * DV_B
H_C, W_C = 4, 256
CONV_K = 4
RG_C = 8.0
ROPE_BASE = 10000.0
EPS = 1e-6
N_MOD = 6
D_FF_CHUNK = 1024

TILE = 256
HALF = TILE // 2
LOG2E = math.log2(math.e)
V7X_VMEM_BYTES = 64 * 1024 * 1024
VMEM_LIMIT = V7X_VMEM_BYTES - 8 * 1024 * 1024


def _dot(a, b):
    return jnp.dot(a, b, preferred_element_type=F32)


def _dot_nt(a, b):
    return lax.dot_general(a, b, (((1,), (1,)), ((), ())), preferred_element_type=F32)


def _dot_tn(a, b):
    return lax.dot_general(a, b, (((0,), (0,)), ((), ())), preferred_element_type=F32)


def _group_sum(x, ones_bd):
    hi = x.astype(BF16)
    lo = (x - hi.astype(F32)).astype(BF16)
    return _dot(hi, ones_bd) + _dot(lo, ones_bd)


def _rms_rows(x, gain):
    return x * lax.rsqrt(jnp.mean(x * x, axis=-1, keepdims=True) + EPS) * gain


def _silu(x):
    return x * jax.nn.sigmoid(x)


def _softplus(x):
    return jnp.maximum(x, 0.0) + jnp.log1p(jnp.exp(-jnp.abs(x)))


def _log_sigmoid(x):
    return jnp.minimum(x, 0.0) - jnp.log1p(jnp.exp(-jnp.abs(x)))


def _gelu_tanh(x):
    return x * (0.5 * (1.0 + jnp.tanh(math.sqrt(2.0 / math.pi) * (x + 0.044715 * (x * x * x)))))


def _params(semantics):
    return pltpu.CompilerParams(dimension_semantics=semantics, vmem_limit_bytes=VMEM_LIMIT)


def _mod_kernel(c_ref, w_ref, b_ref, o_ref):
    c = c_ref[...]
    o_ref[0] = _dot(_silu(c).astype(BF16), w_ref[0].astype(BF16)) + b_ref[0]


def _modulation(c8, w_mod, b_mod):
    depth, d, n = w_mod.shape
    tn = d
    return pl.pallas_call(
        _mod_kernel,
        grid=(depth, n // tn),
        in_specs=[
            pl.BlockSpec((8, d), lambda l, j: (0, 0)),
            pl.BlockSpec((1, d, tn), lambda l, j: (l, 0, j)),
            pl.BlockSpec((1, 1, tn), lambda l, j: (l, 0, j)),
        ],
        out_specs=pl.BlockSpec((1, 8, tn), lambda l, j: (l, 0, j)),
        out_shape=jax.ShapeDtypeStruct((depth, 8, n), F32),
        compiler_params=_params(("arbitrary", "arbitrary")),
        name="mod",
    )(c8, w_mod, b_mod.reshape(depth, 1, n))


def _proj_kernel(x_ref, mod_ref, n1_ref, w_ref, gq_ref, gk_ref, ones_ref, q_ref, k_ref, v_ref, r_ref):
    x = x_ref[...]
    d = x.shape[-1]
    sh = mod_ref[0, :, 0:d]
    sc = mod_ref[0, :, d:2 * d]
    h = _rms_rows(x, n1_ref[...]) * (1.0 + sc) + sh
    p = _dot(h.astype(BF16), w_ref[...])
    ones_bd = ones_ref[...]
    wq = H_A * 2 * DQK_A

    def qk_norm(z, gain):
        ms = _group_sum(z * z, ones_bd) * (1.0 / DQK_A)
        return z * lax.rsqrt(ms + EPS) * gain

    wv = H_A * DV_A
    q_ref[...] = qk_norm(p[:, 0:wq], gq_ref[...])
    k_ref[...] = qk_norm(p[:, wq:2 * wq], gk_ref[...])
    v_ref[...] = p[:, 2 * wq:2 * wq + wv]
    r_ref[...] = p[:, 2 * wq + wv:]


def _in_projection(x, mod, tiles_per_mod, n1, w_in, layer, gq, gk, ones_bd):
    t, d = x.shape
    n_in = w_in.shape[2]
    wq = gq.shape[1]
    wv = H_A * DV_A
    widths = (wq, wq, wv, n_in - 2 * wq - wv)
    return pl.pallas_call(
        _proj_kernel,
        grid=(t // TILE,),
        in_specs=[
            pl.BlockSpec((TILE, d), lambda i: (i, 0)),
            pl.BlockSpec((1, 1, N_MOD * d), lambda i: (i // tiles_per_mod, 0, 0)),
            pl.BlockSpec((1, d), lambda i: (0, 0)),
            pl.BlockSpec((None, d, n_in), lambda i: (layer, 0, 0)),
            pl.BlockSpec((1, wq), lambda i: (0, 0)),
            pl.BlockSpec((1, wq), lambda i: (0, 0)),
            pl.BlockSpec((wq, wq), lambda i: (0, 0)),
        ],
        out_specs=[pl.BlockSpec((TILE, w), lambda i: (i, 0)) for w in widths],
        out_shape=[jax.ShapeDtypeStruct((t, w), F32) for w in widths],
        compiler_params=_params(("arbitrary",)),
        name="proj",
    )(x, mod, n1, w_in, gq, gk, ones_bd)


def _diff_lambda(dl_ref, lam_init):
    lv = dl_ref[...]
    a = jnp.sum(lv[0:1] * lv[1:2], axis=-1, keepdims=True)
    b = jnp.sum(lv[2:3] * lv[3:4], axis=-1, keepdims=True)
    return jnp.exp(a) - jnp.exp(b) + lam_init


def _diff_attention(q, k_of, vext_of, lam, sub_gain, lam_init, o_ref):
    hw = 2 * DQK_A
    first = lax.broadcasted_iota(jnp.int32, (1, hw), 1) < DQK_A
    for h in range(H_A):
        hs = slice(h * hw, (h + 1) * hw)
        qh = q[:, hs]
        kh = k_of(h)
        vext = vext_of(h)
        maps = []
        for qm in (jnp.where(first, qh, 0.0), jnp.where(first, 0.0, qh)):
            s = _dot_nt(qm.astype(BF16), kh)
            e = jnp.exp2(s - jnp.max(s, axis=-1, keepdims=True)).astype(BF16)
            oe = _dot(e, vext)
            maps.append(oe[:, 0:DV_A] / oe[:, DV_A:])
        oh = maps[0] - lam * maps[1]
        o_ref[:, hs] = _rms_rows(oh, sub_gain) * (1.0 - lam_init)


def _attn_ctx_kernel(q_ref, k_ref, v_ref, dl_ref, sg_ref, o_ref, *, lam_init):
    lam = _diff_lambda(dl_ref, lam_init)
    k = k_ref[...].astype(BF16)
    v = v_ref[...].astype(BF16)
    ones = jnp.ones((v.shape[0], DV_A), BF16)
    hw = 2 * DQK_A
    _diff_attention(q_ref[...], lambda h: k[:, h * hw:(h + 1) * hw],
                    lambda h: jnp.concatenate([v[:, h * DV_A:(h + 1) * DV_A], ones], axis=1),
                    lam, sg_ref[...], lam_init, o_ref)


def _attention_ctx(q, k, v, seq, dl, sg, lam_init):
    t = q.shape[0]
    w = H_A * DV_A
    return pl.pallas_call(
        functools.partial(_attn_ctx_kernel, lam_init=lam_init),
        grid=(t // seq,),
        in_specs=[
            pl.BlockSpec((seq, w), lambda b: (b, 0)),
            pl.BlockSpec((seq, w), lambda b: (b, 0)),
            pl.BlockSpec((seq, w), lambda b: (b, 0)),
            pl.BlockSpec(dl.shape, lambda b: (0, 0)),
            pl.BlockSpec(sg.shape, lambda b: (0, 0)),
        ],
        out_specs=pl.BlockSpec((seq, w), lambda b: (b, 0)),
        out_shape=jax.ShapeDtypeStruct((t, w), F32),
        compiler_params=_params(("arbitrary",)),
        name="attn_ctx",
    )(q, k, v, dl, sg)


def _rope(x, cos, sin_lo, sin_hi):
    return (x * cos + pltpu.roll(x, 2 * DQK_A - 16, 1) * sin_lo + pltpu.roll(x, 16, 1) * sin_hi)


def _attn_lat_kernel(q_ref, k_ref, v_ref, ck_ref, cv_ref, qcos_ref, qslo_ref, qshi_ref,
                     kcos_ref, kslo_ref, kshi_ref, dl_ref, sg_ref, o_ref, kall, vall, *, lam_init):
    hw = 2 * DQK_A
    n_new = k_ref.shape[0]

    @pl.when(pl.program_id(1) == 0)
    def _():
        kcos, kslo, kshi = kcos_ref[...], kslo_ref[...], kshi_ref[...]
        for h in range(H_A):
            hs = slice(h * hw, (h + 1) * hw)
            kall[0:n_new, hs] = _rope(k_ref[:, hs], kcos, kslo, kshi).astype(BF16)
        kall[n_new:, :] = ck_ref[...].astype(BF16)
        ones = jnp.ones((vall.shape[0], DV_A), BF16)
        for h in range(H_A):
            vs = slice(h * DV_A, (h + 1) * DV_A)
            vall[0:n_new, 2 * h * DV_A:(2 * h + 1) * DV_A] = v_ref[:, vs].astype(BF16)
            vall[n_new:, 2 * h * DV_A:(2 * h + 1) * DV_A] = cv_ref[:, vs].astype(BF16)
            vall[:, (2 * h + 1) * DV_A:(2 * h + 2) * DV_A] = ones

    lam = _diff_lambda(dl_ref, lam_init)
    qcos, qslo, qshi = qcos_ref[...], qslo_ref[...], qshi_ref[...]
    q = jnp.concatenate(
        [_rope(q_ref[:, h * hw:(h + 1) * hw], qcos, qslo, qshi) for h in range(H_A)], axis=-1)
    _diff_attention(q, lambda h: kall[:, h * hw:(h + 1) * hw],
                    lambda h: vall[:, 2 * h * DV_A:(2 * h + 2) * DV_A], lam, sg_ref[...],
                    lam_init, o_ref)


def _attention_lat(q, k, v, n_batch, cache_k, cache_v, layer, rope_tabs, dl, sg, lam_init):
    t = q.shape[0]
    seq = t // n_batch
    past = cache_k.shape[2]
    w = H_A * DV_A
    hw = 2 * DQK_A
    nq = seq // TILE
    cos, slo, shi = rope_tabs
    tab_q = pl.BlockSpec((TILE, hw), lambda b, j: (j, 0))
    tab_k = pl.BlockSpec((seq, hw), lambda b, j: (0, 0))
    return pl.pallas_call(
        functools.partial(_attn_lat_kernel, lam_init=lam_init),
        grid=(n_batch, nq),
        in_specs=[
            pl.BlockSpec((TILE, w), lambda b, j: (b * nq + j, 0)),
            pl.BlockSpec((seq, w), lambda b, j: (b, 0)),
            pl.BlockSpec((seq, w), lambda b, j: (b, 0)),
            pl.BlockSpec((None, None, past, w), lambda b, j: (b, layer, 0, 0)),
            pl.BlockSpec((None, None, past, w), lambda b, j: (b, layer, 0, 0)),
            tab_q, tab_q, tab_q, tab_k, tab_k, tab_k,
            pl.BlockSpec(dl.shape, lambda b, j: (0, 0)),
            pl.BlockSpec(sg.shape, lambda b, j: (0, 0)),
        ],
        out_specs=pl.BlockSpec((TILE, w), lambda b, j: (b * nq + j, 0)),
        out_shape=jax.ShapeDtypeStruct((t, w), F32),
        scratch_shapes=[pltpu.VMEM((seq + past, w), BF16), pltpu.VMEM((seq + past, 2 * w), BF16)],
        compiler_params=_params(("arbitrary", "arbitrary")),
        name="attn_lat",
    )(q, k, v, cache_k, cache_v, cos, slo, shi, cos, slo, shi, dl, sg)


def _tile_aux():
    rows = lax.broadcasted_iota(jnp.int32, (TILE, 1), 0)
    lane = lax.broadcasted_iota(jnp.int32, (1, W_B), 1)
    head_masks = [(lane // DK_B) == h for h in range(H_B)]
    first_of_pair = lax.broadcasted_iota(jnp.int32, (1, 2 * DK_B), 1) < DK_B
    xor = (lax.broadcasted_iota(jnp.int32, (HALF, HALF), 0)
           ^ lax.broadcasted_iota(jnp.int32, (HALF, HALF), 1))
    return rows, head_masks, first_of_pair, xor


def _hgrn_lower_bound(lb_ref, layer, direction, depth):
    xs = [lb_ref[2 * j + direction:2 * j + direction + 1, :] for j in range(depth)]
    m = functools.reduce(jnp.maximum, xs)
    es = [jnp.exp(x - m) for x in xs]
    return sum(es[1:layer + 1]) / sum(es)


def _hgrn_gates(fpre, lbd):
    if lbd is None:
        return _log_sigmoid(fpre) * LOG2E, jax.nn.sigmoid(-fpre)
    a = jnp.log(lbd)
    b = jnp.log1p(-lbd) + _log_sigmoid(fpre)
    logf = jnp.maximum(a, b) + jnp.log1p(jnp.exp(-jnp.abs(a - b)))
    return logf * LOG2E, (1.0 - lbd) * jax.nn.sigmoid(-fpre)


def _hgrn_tile(qh, kk, vv, g, reverse, aux, ones_bd, state_t):
    rows, head_masks, first_of_pair, xor = aux
    pair_w = 2 * DK_B
    prefix = g
    total = g
    diag = [[None, None] for _ in range(H_B)]
    off = [None] * H_B
    m = 1
    while m < TILE:
        right = (rows & m) != 0
        if not reverse:
            e = jnp.where(right, prefix, total - prefix)
        else:
            e = jnp.where(right, prefix - g, total - prefix + g)
        x = jnp.exp2(e)
        q_rows = right if not reverse else jnp.logical_not(right)
        qz = jnp.where(q_rows, (qh * x).astype(BF16), 0.0)
        kz = jnp.where(q_rows, 0.0, (kk * x).astype(BF16))
        for h in range(H_B):
            ls = slice((h // 2) * pair_w, (h // 2 + 1) * pair_w)
            qm = jnp.where(first_of_pair, qz[:, ls], 0.0) if h % 2 == 0 else \
                jnp.where(first_of_pair, 0.0, qz[:, ls])
            kh = kz[:, ls]
            if m < HALF:
                for i in range(2):
                    rs = slice(i * HALF, (i + 1) * HALF)
                    pm = _dot_nt(qm[rs], kh[rs])
                    if 2 * m < HALF:
                        pm = jnp.where(xor < 2 * m, pm, 0.0)
                    diag[h][i] = pm if diag[h][i] is None else diag[h][i] + pm
            else:
                lo, hi = slice(0, HALF), slice(HALF, TILE)
                off[h] = _dot_nt(qm[hi], kh[lo]) if not reverse else _dot_nt(qm[lo], kh[hi])
        left_total = pltpu.roll(total, m, 0)
        right_total = pltpu.roll(total, TILE - m, 0)
        prefix = prefix + jnp.where(right, left_total, 0.0)
        total = total + jnp.where(right, left_total, right_total)
        m *= 2
    vb = vv.astype(BF16)
    top = bot = None
    for h in range(H_B):
        vh = jnp.where(head_masks[h], vb, 0.0)
        d0, d1, of = (a.astype(BF16) for a in (diag[h][0], diag[h][1], off[h]))
        if not reverse:
            t = _dot(d0, vh[0:HALF])
            b = _dot(jnp.concatenate([of, d1], axis=1), vh)
        else:
            t = _dot(jnp.concatenate([d0, of], axis=1), vh)
            b = _dot(d1, vh[HALF:])
        top = t if top is None else top + t
        bot = b if bot is None else bot + b
    out = _group_sum(qh * kk, ones_bd) * vv + jnp.concatenate([top, bot], axis=0)
    if not reverse:
        q_decay, k_decay = prefix, total - prefix
    else:
        q_decay, k_decay = total - prefix + g, prefix - g
    if state_t is not None:
        out = out + _dot_nt((qh * jnp.exp2(q_decay)).astype(BF16), state_t.astype(BF16))
    k_hat = (kk * jnp.exp2(k_decay)).astype(BF16)
    return out, k_hat, total[0:1, :]


def _block_diag_mask():
    r = lax.broadcasted_iota(jnp.int32, (W_B, W_B), 0) // DK_B
    c = lax.broadcasted_iota(jnp.int32, (W_B, W_B), 1) // DV_B
    return r == c


def _conv(x, prev_row, next_rows, cw_ref, cb_ref, rows):
    x_m1 = jnp.where(rows == 0, prev_row, pltpu.roll(x, 1, 0))
    x_p1 = jnp.where(rows == TILE - 1, next_rows[0:1], pltpu.roll(x, TILE - 1, 0))
    x_p2 = jnp.where(rows == TILE - 2, next_rows[0:1],
                     jnp.where(rows == TILE - 1, next_rows[1:2], pltpu.roll(x, TILE - 2, 0)))
    y = cb_ref[...] + x_m1 * cw_ref[0:1, :]
    y = y + x * cw_ref[1:2, :]
    y = y + x_p1 * cw_ref[2:3, :]
    return y + x_p2 * cw_ref[3:4, :]


def _rglru_tile(xc, r_pre, i_pre, lam_row, reverse, rows, h_in):
    r = jax.nn.sigmoid(r_pre)
    i = jax.nn.sigmoid(i_pre)
    log_a = -RG_C * r * _softplus(-lam_row)
    a = jnp.exp(log_a)
    u = jnp.sqrt(jnp.tanh(-log_a) * (a * a + 1.0)) * (i * xc)
    k = 1
    while k < TILE:
        if not reverse:
            valid = rows >= k
            a_s, u_s = pltpu.roll(a, k, 0), pltpu.roll(u, k, 0)
        else:
            valid = rows < TILE - k
            a_s, u_s = pltpu.roll(a, TILE - k, 0), pltpu.roll(u, TILE - k, 0)
        u = u + a * jnp.where(valid, u_s, 0.0)
        a = a * jnp.where(valid, a_s, 1.0)
        k *= 2
    if h_in is not None:
        u = u + a * h_in
    return u


def _rec_inputs(refs):
    qb, ff, fb, ib, gb, xcol, gcol = [r[...] for r in refs]
    return _silu(qb), ff, fb, ib, gb, xcol, gcol


def _hgrn_out(o_sum, gb, onorm_gain, ones_bd):
    ms = _group_sum(o_sum * o_sum, ones_bd) * (1.0 / DV_B)
    return o_sum * lax.rsqrt(ms + EPS) * onorm_gain * _silu(gb)


def _rec_ctx_kernel(q_ref, ff_ref, fb_ref, i_ref, g_ref, x_ref, gc_ref, lb_ref, on_ref, cw_ref,
                    cb_ref, wg_ref, bg_ref, lam_ref, ones_ref, ob_ref, oc_ref, st_ref, hs_ref,
                    *, layer, depth):
    aux = _tile_aux()
    rows = aux[0]
    ones_bd = ones_ref[...]
    qh, ff, fb, ib, gb, xcol, gcol = _rec_inputs((q_ref, ff_ref, fb_ref, i_ref, g_ref, x_ref, gc_ref))
    o_sum = None
    for direction, fpre in enumerate((ff, fb)):
        lbd = None if layer == 0 else _hgrn_lower_bound(lb_ref, layer, direction, depth)
        g, kk = _hgrn_gates(fpre, lbd)
        o, k_hat, _ = _hgrn_tile(qh, kk, ib, g, direction == 1, aux, ones_bd, None)
        o_sum = o if o_sum is None else o_sum + o
        full = _dot_tn(k_hat, ib.astype(BF16))
        for h in range(H_B):
            blk = full[h * DK_B:(h + 1) * DK_B, (h // 2) * 2 * DV_B:(h // 2 + 1) * 2 * DV_B]
            if h % 2:
                blk = pltpu.roll(blk, DV_B, 1)
            st_ref[0, direction, h] = blk[:, 0:DV_B]
    ob_ref[...] = _hgrn_out(o_sum, gb, on_ref[...], ones_bd)

    zero_row = jnp.zeros((1, W_C), F32)
    xc = _conv(xcol, zero_row, jnp.zeros((2, W_C), F32), cw_ref, cb_ref, rows)
    gates = _dot(xc.astype(BF16), wg_ref[...]) + bg_ref[...]
    y_f = _rglru_tile(xc, gates[:, 0:W_C], gates[:, W_C:2 * W_C], lam_ref[0:1, :], False, rows, None)
    y_b = _rglru_tile(xc, gates[:, 2 * W_C:3 * W_C], gates[:, 3 * W_C:], lam_ref[1:2, :], True,
                      rows, None)
    oc_ref[...] = (y_f + y_b) * _gelu_tanh(gcol)
    hs_ref[0, 0:1, :] = y_f[TILE - 1:TILE, :]
    hs_ref[0, 1:2, :] = y_b[0:1, :]


def _col_spec(col, row_map):
    return pl.BlockSpec((TILE, W_B), lambda *idx: (row_map(*idx), col))


COL_QB, COL_FF, COL_FB, COL_IB, COL_GB, COL_XC, COL_GC = range(7)
_REC_COLS = (COL_QB, COL_FF, COL_FB, COL_IB, COL_GB, COL_XC, COL_GC)


def _full_spec(a):
    nd = a.ndim
    return pl.BlockSpec(a.shape, lambda *idx: (0,) * nd)


def _recurrence_ctx(proj, layer, depth, consts):
    t = proj.shape[0]
    n = t // TILE
    return pl.pallas_call(
        functools.partial(_rec_ctx_kernel, layer=layer, depth=depth),
        grid=(n,),
        in_specs=[_col_spec(c, lambda b: b) for c in _REC_COLS] + [_full_spec(a) for a in consts],
        out_specs=[
            pl.BlockSpec((TILE, W_B), lambda b: (b, 0)),
            pl.BlockSpec((TILE, W_C), lambda b: (b, 0)),
            pl.BlockSpec((1, 2, H_B, DK_B, DV_B), lambda b: (b, 0, 0, 0, 0)),
            pl.BlockSpec((1, 2, W_C), lambda b: (b, 0, 0)),
        ],
        out_shape=[
            jax.ShapeDtypeStruct((t, W_B), F32),
            jax.ShapeDtypeStruct((t, W_C), F32),
            jax.ShapeDtypeStruct((n, 2, H_B, DK_B, DV_B), F32),
            jax.ShapeDtypeStruct((n, 2, W_C), F32),
        ],
        compiler_params=_params(("arbitrary",)),
        name="rec_ctx",
    )(*([proj] * len(_REC_COLS)), *consts)


def _rec_lat_kernel(q_ref, f_ref, i_ref, g_ref, x_ref, xp_ref, xn_ref, gc_ref, s0_ref, h0_ref,
                    *rest, layer, depth, direction, n_tiles):
    if direction == 0:
        (lb_ref, on_ref, cw_ref, cb_ref, wg_ref, bg_ref, lam_ref, ones_ref,
         o_ref, y_ref, state, hstate) = rest
    else:
        (of_ref, yf_ref, lb_ref, on_ref, cw_ref, cb_ref, wg_ref, bg_ref, lam_ref, ones_ref,
         o_ref, y_ref, state, hstate) = rest
    step = pl.program_id(1)
    tile = step if direction == 0 else n_tiles - 1 - step
    reverse = direction == 1

    @pl.when(step == 0)
    def _():
        state[...] = s0_ref[0, 0]
        hstate[...] = h0_ref[0, 0]

    aux = _tile_aux()
    rows = aux[0]
    ones_bd = ones_ref[...]
    qh = _silu(q_ref[...])
    ib = i_ref[...]
    lbd = None if layer == 0 else _hgrn_lower_bound(lb_ref, layer, direction, depth)
    g, kk = _hgrn_gates(f_ref[...], lbd)
    o, k_hat, total = _hgrn_tile(qh, kk, ib, g, reverse, aux, ones_bd, state[...])
    upd = jnp.where(_block_diag_mask(), _dot_tn(ib.astype(BF16), k_hat), 0.0)
    state[...] = state[...] * jnp.exp2(total) + upd

    prev_row = jnp.where(tile == 0, 0.0, xp_ref[TILE - 1:TILE, :])
    next_rows = jnp.where(tile == n_tiles - 1, 0.0, xn_ref[0:2, :])
    xc = _conv(x_ref[...], prev_row, next_rows, cw_ref, cb_ref, rows)
    gates = _dot(xc.astype(BF16), wg_ref[...]) + bg_ref[...]
    c0 = 2 * W_C * direction
    y = _rglru_tile(xc, gates[:, c0:c0 + W_C], gates[:, c0 + W_C:c0 + 2 * W_C],
                    lam_ref[direction:direction + 1, :], reverse, rows, hstate[...])
    hstate[...] = y[0:1, :] if reverse else y[TILE - 1:TILE, :]

    if direction == 0:
        o_ref[...] = o
        y_ref[...] = y
    else:
        o_ref[...] = _hgrn_out(of_ref[...] + o, g_ref[...], on_ref[...], ones_bd)
        y_ref[...] = (yf_ref[...] + y) * _gelu_tanh(gc_ref[...])


def _recurrence_lat(proj, n_batch, layer, depth, direction, state0, h0, prev, consts):
    t = proj.shape[0]
    n_tiles = t // (TILE * n_batch)

    def tile_of(b, s):
        return s if direction == 0 else n_tiles - 1 - s

    def row(b, s):
        return b * n_tiles + tile_of(b, s)

    def row_prev(b, s):
        return b * n_tiles + jnp.maximum(tile_of(b, s) - 1, 0)

    def row_next(b, s):
        return b * n_tiles + jnp.minimum(tile_of(b, s) + 1, n_tiles - 1)

    f_col = COL_FF if direction == 0 else COL_FB
    in_specs = [_col_spec(COL_QB, row), _col_spec(f_col, row), _col_spec(COL_IB, row),
                _col_spec(COL_GB, row), _col_spec(COL_XC, row), _col_spec(COL_XC, row_prev),
                _col_spec(COL_XC, row_next), _col_spec(COL_GC, row),
                pl.BlockSpec((1, 1, W_B, W_B), lambda b, s: (b, direction, 0, 0)),
                pl.BlockSpec((1, 1, 1, W_C), lambda b, s: (b, direction, 0, 0))]
    args = [proj] * 8 + [state0, h0]
    if direction == 1:
        in_specs += [pl.BlockSpec((TILE, W_B), lambda b, s: (row(b, s), 0)),
                     pl.BlockSpec((TILE, W_C), lambda b, s: (row(b, s), 0))]
        args += list(prev)
    in_specs += [_full_spec(a) for a in consts]
    args += list(consts)
    return pl.pallas_call(
        functools.partial(_rec_lat_kernel, layer=layer, depth=depth, direction=direction,
                          n_tiles=n_tiles),
        grid=(n_batch, n_tiles),
        in_specs=in_specs,
        out_specs=[pl.BlockSpec((TILE, W_B), lambda b, s: (row(b, s), 0)),
                   pl.BlockSpec((TILE, W_C), lambda b, s: (row(b, s), 0))],
        out_shape=[jax.ShapeDtypeStruct((t, W_B), F32), jax.ShapeDtypeStruct((t, W_C), F32)],
        scratch_shapes=[pltpu.VMEM((W_B, W_B), F32), pltpu.VMEM((1, W_C), F32)],
        compiler_params=_params(("arbitrary", "arbitrary")),
        name="rec_lat_fwd" if direction == 0 else "rec_lat_bwd",
    )(*args)


def _out_kernel(x_ref, oa_ref, ob_ref, oc_ref, mod_ref, n2_ref, wo_ref, w1_ref, w2_ref, y_ref):
    x = x_ref[...]
    d = x.shape[-1]
    g1 = mod_ref[0, :, 2 * d:3 * d]
    sh2 = mod_ref[0, :, 3 * d:4 * d]
    sc2 = mod_ref[0, :, 4 * d:5 * d]
    g2 = mod_ref[0, :, 5 * d:6 * d]
    wa = H_A * DV_A
    mix = _dot(oa_ref[...].astype(BF16), wo_ref[0:wa, :])
    mix = mix + _dot(ob_ref[...].astype(BF16), wo_ref[wa:wa + W_B, :])
    mix = mix + _dot(oc_ref[...].astype(BF16), wo_ref[wa + W_B:, :])
    x1 = x + g1 * mix
    h = (_rms_rows(x1, n2_ref[...]) * (1.0 + sc2) + sh2).astype(BF16)
    acc = jnp.zeros_like(x)
    for c in range(w1_ref.shape[1] // D_FF_CHUNK):
        cs = slice(c * D_FF_CHUNK, (c + 1) * D_FF_CHUNK)
        f = jnp.maximum(_dot(h, w1_ref[:, cs]), 0.0)
        acc = acc + _dot((f * f).astype(BF16), w2_ref[cs, :])
    y_ref[...] = x1 + g2 * acc


def _out_mlp(x, oa, ob, oc, mod, tiles_per_mod, n2, w_out, w_ff1, w_ff2, layer):
    t, d = x.shape

    def resident(w):
        return pl.BlockSpec((None,) + w.shape[1:], lambda i: (layer, 0, 0),
                            pipeline_mode=pl.Buffered(1))

    return pl.pallas_call(
        _out_kernel,
        grid=(t // TILE,),
        in_specs=[
            pl.BlockSpec((TILE, d), lambda i: (i, 0)),
            pl.BlockSpec((TILE, oa.shape[1]), lambda i: (i, 0)),
            pl.BlockSpec((TILE, ob.shape[1]), lambda i: (i, 0)),
            pl.BlockSpec((TILE, oc.shape[1]), lambda i: (i, 0)),
            pl.BlockSpec((1, 1, N_MOD * d), lambda i: (i // tiles_per_mod, 0, 0)),
            pl.BlockSpec((1, d), lambda i: (0, 0)),
            resident(w_out),
            resident(w_ff1),
            resident(w_ff2),
        ],
        out_specs=pl.BlockSpec((TILE, d), lambda i: (i, 0)),
        out_shape=jax.ShapeDtypeStruct((t, d), F32),
        compiler_params=_params(("arbitrary",)),
        name="out_mlp",
    )(x, oa, ob, oc, mod, n2, w_out, w_ff1, w_ff2)


def _ones_block_diag(n, group):
    idx = np.arange(n) // group
    return jnp.asarray((idx[:, None] == idx[None, :]).astype(np.float32), dtype=BF16)


def _rope_tables(seq):
    nf = DQK_A // 4
    inv = ROPE_BASE ** (-jnp.arange(nf, dtype=F32) / nf)
    pos = jnp.arange(seq)
    ang_r = (pos // GRID_W).astype(F32)[:, None] * inv
    ang_c = (pos % GRID_W).astype(F32)[:, None] * inv
    z = jnp.zeros_like(ang_r)
    cos = jnp.concatenate([jnp.cos(ang_r)] * 2 + [jnp.cos(ang_c)] * 2, axis=-1)
    sin_lo = jnp.concatenate([-jnp.sin(ang_r), z, -jnp.sin(ang_c), z], axis=-1)
    sin_hi = jnp.concatenate([z, jnp.sin(ang_r), z, jnp.sin(ang_c)], axis=-1)
    return tuple(jnp.tile(a, (1, 2)) for a in (cos, sin_lo, sin_hi))


def _gate_weights(rg_w_l):
    eye = jnp.eye(H_C, dtype=rg_w_l.dtype)
    w = jnp.einsum('dghij,hk->dghikj', rg_w_l, eye)
    w = w.reshape(2, 2, W_C, W_C)
    return jnp.transpose(w, (2, 0, 1, 3)).reshape(W_C, 4 * W_C).astype(BF16)


def _state_block_diag_t(s):
    eye = jnp.eye(H_B, dtype=s.dtype)
    w = jnp.einsum('...hdv,hk->...hvkd', s, eye)
    return w.reshape(s.shape[:-3] + (W_B, W_B))


def kernel(x_prompt, x_sample, cache_k, cache_v, state_hgrn, state_rglru, c, c_ctx, w_mod, b_mod,
           norm1, norm2, w_in, w_out, qk_norm, diff_lambda, subln, hgrn_lb, hgrn_onorm, conv_w,
           conv_b, rg_w, rg_b, rg_lambda, w_ff1, w_ff2):
    batch, seq, d = x_prompt.shape
    dec_batch, dec_seq, _ = x_sample.shape
    depth = w_in.shape[0]
    past = cache_k.shape[2]
    wq = H_A * 2 * DQK_A

    c8 = jnp.concatenate([c_ctx[None], c, jnp.zeros((8 - 1 - dec_batch, d), F32)], axis=0)
    mod_all = _modulation(c8, w_mod, b_mod)

    ones_qk = _ones_block_diag(wq, DQK_A)
    ones_b = _ones_block_diag(W_B, DV_B)
    rope_tabs = _rope_tables(dec_seq)
    ck = cache_k.reshape(dec_batch, depth, past, wq)
    cv = cache_v.reshape(dec_batch, depth, past, H_A * DV_A)
    lb2 = hgrn_lb.reshape(depth * 2, W_B)

    w_in_b, w_out_b, w1_b, w2_b = (w.astype(BF16) for w in (w_in, w_out, w_ff1, w_ff2))

    yp = x_prompt.reshape(batch * seq, d)
    ys = x_sample.reshape(dec_batch * dec_seq, d)
    ks, vs, shs, srs = [], [], [], []
    for l in range(depth):
        lam_init = 0.8 - 0.6 * math.exp(-0.3 * l)
        mod_ctx = mod_all[l, 0:1][:, None, :]
        mod_lat = mod_all[l, 1:1 + dec_batch][:, None, :]
        n1 = norm1[l][None]
        n2 = norm2[l][None]
        gq = jnp.tile(qk_norm[l, 0], H_A * 2)[None] * (DQK_A ** -0.5 * LOG2E)
        gk = jnp.tile(qk_norm[l, 1], H_A * 2)[None]
        dl = diff_lambda[l]
        sg = subln[l][None]
        consts = (lb2, jnp.tile(hgrn_onorm[l], H_B)[None], conv_w[l], conv_b[l][None],
                  _gate_weights(rg_w[l]), rg_b[l].reshape(1, 4 * W_C), rg_lambda[l], ones_b)

        q, k, v, rest = _in_projection(yp, mod_ctx, batch * seq // TILE, n1, w_in_b, l, gq, gk,
                                       ones_qk)
        oa = _attention_ctx(q, k, v, seq, dl, sg, lam_init)
        ob, oc, st, hs = _recurrence_ctx(rest, l, depth, consts)
        yp = _out_mlp(yp, oa, ob, oc, mod_ctx, batch * seq // TILE, n2, w_out_b, w1_b, w2_b, l)
        ks.append(k.reshape(batch, seq, H_A, 2, DQK_A))
        vs.append(v.reshape(batch, seq, H_A, DV_A))
        shs.append(st)
        srs.append(hs)

        q, k, v, rest = _in_projection(ys, mod_lat, dec_seq // TILE, n1, w_in_b, l, gq, gk, ones_qk)
        oa = _attention_lat(q, k, v, dec_batch, ck, cv, l, rope_tabs, dl, sg, lam_init)
        s0 = _state_block_diag_t(state_hgrn[:, l])
        h0 = state_rglru[:, l][:, :, None, :]
        fwd = _recurrence_lat(rest, dec_batch, l, depth, 0, s0, h0, None, consts)
        ob, oc = _recurrence_lat(rest, dec_batch, l, depth, 1, s0, h0, fwd, consts)
        ys = _out_mlp(ys, oa, ob, oc, mod_lat, dec_seq // TILE, n2, w_out_b, w1_b, w2_b, l)

    return (yp.reshape(batch, seq, d), ys.reshape(dec_batch, dec_seq, d),
            jnp.stack(ks, axis=1), jnp.stack(vs, axis=1), jnp.stack(shs, axis=1),
            jnp.stack(srs, axis=1))
```

```python
import functools
import math

import numpy as np
import jax
import jax.numpy as jnp
from jax import lax
from jax.experimental import pallas as pl
from jax.experimental.pallas import tpu as pltpu

F32 = jnp.float32
BF16 = jnp.bfloat16

GRID_W = 64
H_A, DQK_A, DV_A = 4, 64, 128
H_B, DK_B, DV_B = 4, 64, 64
W_B = H_B * DV_B
H_C, W_C = 4, 256
CONV_K = 4
RG_C = 8.0
ROPE_BASE = 10000.0
EPS = 1e-6
N_MOD = 6
D_FF_CHUNK = 1024

TILE = 256
HALF = TILE // 2
SUBLANES = 8
LOG2E = math.log2(math.e)
V7X_VMEM_BYTES = 64 * 1024 * 1024
VMEM_LIMIT = V7X_VMEM_BYTES - 8 * 1024 * 1024


def _dot(a, b):
    return jnp.dot(a, b, preferred_element_type=F32)


def _dot_nt(a, b):
    return lax.dot_general(a, b, (((1,), (1,)), ((), ())), preferred_element_type=F32)


def _dot_tn(a, b):
    return lax.dot_general(a, b, (((0,), (0,)), ((), ())), preferred_element_type=F32)


def _group_sum(x, ones_bd):
    hi = x.astype(BF16)
    lo = (x - hi.astype(F32)).astype(BF16)
    return _dot(hi, ones_bd) + _dot(lo, ones_bd)


def _rms_rows(x, gain):
    return x * lax.rsqrt(jnp.mean(x * x, axis=-1, keepdims=True) + EPS) * gain


def _silu(x):
    return x * jax.nn.sigmoid(x)


def _softplus(x):
    return jnp.maximum(x, 0.0) + jnp.log1p(jnp.exp(-jnp.abs(x)))


def _log_sigmoid(x):
    return jnp.minimum(x, 0.0) - jnp.log1p(jnp.exp(-jnp.abs(x)))


def _gelu_tanh(x):
    return x * (0.5 * (1.0 + jnp.tanh(math.sqrt(2.0 / math.pi) * (x + 0.044715 * (x * x * x)))))


def _params(semantics, flags=None):
    return pltpu.CompilerParams(dimension_semantics=semantics, vmem_limit_bytes=VMEM_LIMIT,
                                flags=flags)


def _mod_kernel(c_ref, w_ref, b_ref, o_ref):
    c = c_ref[...]
    o_ref[0] = _dot(_silu(c).astype(BF16), w_ref[0].astype(BF16)) + b_ref[0]


def _modulation(c8, w_mod, b_mod):
    depth, d, n = w_mod.shape
    tn = d
    return pl.pallas_call(
        _mod_kernel,
        grid=(depth, n // tn),
        in_specs=[
            pl.BlockSpec((8, d), lambda l, j: (0, 0)),
            pl.BlockSpec((1, d, tn), lambda l, j: (l, 0, j)),
            pl.BlockSpec((1, 1, tn), lambda l, j: (l, 0, j)),
        ],
        out_specs=pl.BlockSpec((1, 8, tn), lambda l, j: (l, 0, j)),
        out_shape=jax.ShapeDtypeStruct((depth, 8, n), F32),
        compiler_params=_params(("arbitrary", "arbitrary")),
        name="mod",
    )(c8, w_mod, b_mod.reshape(depth, 1, n))


def _proj_kernel(x_ref, mod_ref, n1_ref, w_ref, gq_ref, gk_ref, ones_ref, q_ref, k_ref, v_ref, r_ref,
                 *kt_ref):
    x = x_ref[...]
    d = x.shape[-1]
    sh = mod_ref[0, :, 0:d]
    sc = mod_ref[0, :, d:2 * d]
    h = _rms_rows(x, n1_ref[...]) * (1.0 + sc) + sh
    p = _dot(h.astype(BF16), w_ref[...])
    ones_bd = ones_ref[...]
    wq = H_A * 2 * DQK_A

    def qk_norm(z, gain):
        z2 = z * z
        nb = ones_bd.shape[0]
        ms = jnp.concatenate([_group_sum(z2[:, c:c + nb], ones_bd) for c in range(0, wq, nb)],
                             axis=1) * (1.0 / DQK_A)
        return z * lax.rsqrt(ms + EPS) * gain

    wv = H_A * DV_A
    q_ref[...] = qk_norm(p[:, 0:wq], gq_ref[...])
    kn = qk_norm(p[:, wq:2 * wq], gk_ref[...])
    k_ref[...] = kn
    if kt_ref:
        row = pl.program_id(0) % SUBLANES
        kt_ref[0][:, pl.ds(row, 1), :] = kn.T[:, None, :]
    for h in range(H_A):
        v_ref[pl.ds(h, TILE, stride=H_A), :] = p[:, 2 * wq + h * DV_A:2 * wq + (h + 1) * DV_A]
    r_ref[...] = p[:, 2 * wq + wv:]


def _in_projection(x, mod, tiles_per_mod, n1, w_in, layer, gq, gk, ones_bd, keys_t=False):
    t, d = x.shape
    n_in = w_in.shape[2]
    wq = gq.shape[1]
    wv = H_A * DV_A
    widths = (wq, wq, wv, n_in - 2 * wq - wv)
    out_specs = [pl.BlockSpec((TILE, w), lambda i: (i, 0)) for w in widths]
    out_shape = [jax.ShapeDtypeStruct((t, w), F32) for w in widths]
    out_specs[2] = pl.BlockSpec((TILE * H_A, DV_A), lambda i: (i, 0))
    out_shape[2] = jax.ShapeDtypeStruct((t * H_A, DV_A), F32)
    if keys_t:
        out_specs.append(pl.BlockSpec((wq, SUBLANES, TILE), lambda i: (0, i // SUBLANES, 0)))
        out_shape.append(jax.ShapeDtypeStruct((wq, t // TILE, TILE), F32))
    return pl.pallas_call(
        _proj_kernel,
        grid=(t // TILE,),
        in_specs=[
            pl.BlockSpec((TILE, d), lambda i: (i, 0)),
            pl.BlockSpec((1, 1, N_MOD * d), lambda i: (i // tiles_per_mod, 0, 0)),
            pl.BlockSpec((1, d), lambda i: (0, 0)),
            pl.BlockSpec((None, d, n_in), lambda i: (layer, 0, 0)),
            pl.BlockSpec((1, wq), lambda i: (0, 0)),
            pl.BlockSpec((1, wq), lambda i: (0, 0)),
            pl.BlockSpec(ones_bd.shape, lambda i: (0, 0)),
        ],
        out_specs=out_specs,
        out_shape=out_shape,
        compiler_params=_params(("arbitrary",)),
        name="proj",
    )(x, mod, n1, w_in, gq, gk, ones_bd)


def _diff_lambda(dl_ref, lam_init):
    lv = dl_ref[...]
    a = jnp.sum(lv[0:1] * lv[1:2], axis=-1, keepdims=True)
    b = jnp.sum(lv[2:3] * lv[3:4], axis=-1, keepdims=True)
    return jnp.exp(a) - jnp.exp(b) + lam_init


def _diff_attention(q, k_of, vext_of, lam, sub_gain, lam_init, o_ref):
    hw = 2 * DQK_A
    first = lax.broadcasted_iota(jnp.int32, (1, hw), 1) < DQK_A
    for h in range(H_A):
        hs = slice(h * hw, (h + 1) * hw)
        qh = q[:, hs]
        kh = k_of(h)
        vext = vext_of(h)
        maps = []
        for qm in (jnp.where(first, qh, 0.0), jnp.where(first, 0.0, qh)):
            s = _dot_nt(qm.astype(BF16), kh)
            e = jnp.exp2(s - jnp.max(s, axis=-1, keepdims=True)).astype(BF16)
            oe = _dot(e, vext)
            maps.append(oe[:, 0:DV_A] / oe[:, DV_A:])
        oh = maps[0] - lam * maps[1]
        o_ref[:, hs] = _rms_rows(oh, sub_gain) * (1.0 - lam_init)


def _attn_ctx_kernel(q_ref, k_ref, v_ref, dl_ref, sg_ref, o_ref, *, lam_init):
    lam = _diff_lambda(dl_ref, lam_init)
    k = k_ref[...].astype(BF16)
    seq = k.shape[0]
    ones = jnp.ones((seq, DV_A), BF16)
    hw = 2 * DQK_A

    def vext(h):
        return jnp.concatenate([v_ref[pl.ds(h, seq, stride=H_A), :].astype(BF16), ones], axis=1)

    _diff_attention(q_ref[...], lambda h: k[:, h * hw:(h + 1) * hw], vext, lam, sg_ref[...],
                    lam_init, o_ref)


def _attention_ctx(q, k, v, seq, dl, sg, lam_init):
    t = q.shape[0]
    w = H_A * DV_A
    return pl.pallas_call(
        functools.partial(_attn_ctx_kernel, lam_init=lam_init),
        grid=(t // seq,),
        in_specs=[
            pl.BlockSpec((seq, w), lambda b: (b, 0)),
            pl.BlockSpec((seq, w), lambda b: (b, 0)),
            pl.BlockSpec((seq * H_A, DV_A), lambda b: (b, 0)),
            pl.BlockSpec(dl.shape, lambda b: (0, 0)),
            pl.BlockSpec(sg.shape, lambda b: (0, 0)),
        ],
        out_specs=pl.BlockSpec((seq, w), lambda b: (b, 0)),
        out_shape=jax.ShapeDtypeStruct((t, w), F32),
        compiler_params=_params(("arbitrary",)),
        name="attn_ctx",
    )(q, k, v, dl, sg)


def _rope(x, cos, sin_lo, sin_hi):
    return (x * cos + pltpu.roll(x, 2 * DQK_A - 16, 1) * sin_lo + pltpu.roll(x, 16, 1) * sin_hi)


def _attn_lat_kernel(q_ref, k_ref, v_ref, ck_ref, cv_ref, qcos_ref, qslo_ref, qshi_ref,
                     kcos_ref, kslo_ref, kshi_ref, dl_ref, sg_ref, o_ref, kall, vall, *, lam_init):
    hw = 2 * DQK_A
    n_new = k_ref.shape[0]

    @pl.when(pl.program_id(1) == 0)
    def _():
        kcos, kslo, kshi = kcos_ref[...], kslo_ref[...], kshi_ref[...]
        for h in range(H_A):
            hs = slice(h * hw, (h + 1) * hw)
            kall[0:n_new, hs] = _rope(k_ref[:, hs], kcos, kslo, kshi).astype(BF16)
        kall[n_new:, :] = ck_ref[...].astype(BF16)
        ones = jnp.ones((vall.shape[0], DV_A), BF16)
        for h in range(H_A):
            vs = slice(h * DV_A, (h + 1) * DV_A)
            vall[0:n_new, 2 * h * DV_A:(2 * h + 1) * DV_A] = (
                v_ref[pl.ds(h, n_new, stride=H_A), :].astype(BF16))
            vall[n_new:, 2 * h * DV_A:(2 * h + 1) * DV_A] = cv_ref[:, vs].astype(BF16)
            vall[:, (2 * h + 1) * DV_A:(2 * h + 2) * DV_A] = ones

    lam = _diff_lambda(dl_ref, lam_init)
    qcos, qslo, qshi = qcos_ref[...], qslo_ref[...], qshi_ref[...]
    q = jnp.concatenate(
        [_rope(q_ref[:, h * hw:(h + 1) * hw], qcos, qslo, qshi) for h in range(H_A)], axis=-1)
    _diff_attention(q, lambda h: kall[:, h * hw:(h + 1) * hw],
                    lambda h: vall[:, 2 * h * DV_A:(2 * h + 2) * DV_A], lam, sg_ref[...],
                    lam_init, o_ref)


def _attention_lat(q, k, v, n_batch, cache_k, cache_v, layer, rope_tabs, dl, sg, lam_init):
    t = q.shape[0]
    seq = t // n_batch
    past = cache_k.shape[2]
    w = H_A * DV_A
    hw = 2 * DQK_A
    nq = seq // TILE
    cos, slo, shi = rope_tabs
    tab_q = pl.BlockSpec((TILE, hw), lambda b, j: (j, 0))
    tab_k = pl.BlockSpec((seq, hw), lambda b, j: (0, 0))
    return pl.pallas_call(
        functools.partial(_attn_lat_kernel, lam_init=lam_init),
        grid=(n_batch, nq),
        in_specs=[
            pl.BlockSpec((TILE, w), lambda b, j: (b * nq + j, 0)),
            pl.BlockSpec((seq, w), lambda b, j: (b, 0)),
            pl.BlockSpec((seq * H_A, DV_A), lambda b, j: (b, 0)),
            pl.BlockSpec((None, None, past, w), lambda b, j: (b, layer, 0, 0)),
            pl.BlockSpec((None, None, past, w), lambda b, j: (b, layer, 0, 0)),
            tab_q, tab_q, tab_q, tab_k, tab_k, tab_k,
            pl.BlockSpec(dl.shape, lambda b, j: (0, 0)),
            pl.BlockSpec(sg.shape, lambda b, j: (0, 0)),
        ],
        out_specs=pl.BlockSpec((TILE, w), lambda b, j: (b * nq + j, 0)),
        out_shape=jax.ShapeDtypeStruct((t, w), F32),
        scratch_shapes=[pltpu.VMEM((seq + past, w), BF16), pltpu.VMEM((seq + past, 2 * w), BF16)],
        compiler_params=_params(("arbitrary", "arbitrary")),
        name="attn_lat",
    )(q, k, v, cache_k, cache_v, cos, slo, shi, cos, slo, shi, dl, sg)


def _tile_aux():
    rows = lax.broadcasted_iota(jnp.int32, (TILE, 1), 0)
    lane = lax.broadcasted_iota(jnp.int32, (1, W_B), 1)
    head_masks = [(lane // DK_B) == h for h in range(H_B)]
    first_of_pair = lax.broadcasted_iota(jnp.int32, (1, 2 * DK_B), 1) < DK_B
    xor = (lax.broadcasted_iota(jnp.int32, (HALF, HALF), 0)
           ^ lax.broadcasted_iota(jnp.int32, (HALF, HALF), 1))
    return rows, head_masks, first_of_pair, xor


def _hgrn_lower_bound(lb_ref, layer, direction, depth):
    xs = [lb_ref[2 * j + direction:2 * j + direction + 1, :] for j in range(depth)]
    m = functools.reduce(jnp.maximum, xs)
    es = [jnp.exp(x - m) for x in xs]
    return sum(es[1:layer + 1]) / sum(es)


def _hgrn_gates(fpre, lbd):
    if lbd is None:
        return _log_sigmoid(fpre) * LOG2E, jax.nn.sigmoid(-fpre)
    a = jnp.log(lbd)
    b = jnp.log1p(-lbd) + _log_sigmoid(fpre)
    logf = jnp.maximum(a, b) + jnp.log1p(jnp.exp(-jnp.abs(a - b)))
    return logf * LOG2E, (1.0 - lbd) * jax.nn.sigmoid(-fpre)


def _hgrn_tile(qh, kk, vv, g, reverse, aux, ones_bd, state_t):
    rows, head_masks, first_of_pair, xor = aux
    pair_w = 2 * DK_B
    prefix = g
    total = g
    diag = [[None, None] for _ in range(H_B)]
    off = [None] * H_B
    m = 1
    while m < TILE:
        right = (rows & m) != 0
        if not reverse:
            e = jnp.where(right, prefix, total - prefix)
        else:
            e = jnp.where(right, prefix - g, total - prefix + g)
        x = jnp.exp2(e)
        q_rows = right if not reverse else jnp.logical_not(right)
        qz = jnp.where(q_rows, (qh * x).astype(BF16), 0.0)
        kz = jnp.where(q_rows, 0.0, (kk * x).astype(BF16))
        for h in range(H_B):
            ls = slice((h // 2) * pair_w, (h // 2 + 1) * pair_w)
            qm = jnp.where(first_of_pair, qz[:, ls], 0.0) if h % 2 == 0 else \
                jnp.where(first_of_pair, 0.0, qz[:, ls])
            kh = kz[:, ls]
            if m < HALF:
                for i in range(2):
                    rs = slice(i * HALF, (i + 1) * HALF)
                    pm = _dot_nt(qm[rs], kh[rs])
                    if 2 * m < HALF:
                        pm = jnp.where(xor < 2 * m, pm, 0.0)
                    diag[h][i] = pm if diag[h][i] is None else diag[h][i] + pm
            else:
                lo, hi = slice(0, HALF), slice(HALF, TILE)
                off[h] = _dot_nt(qm[hi], kh[lo]) if not reverse else _dot_nt(qm[lo], kh[hi])
        left_total = pltpu.roll(total, m, 0)
        right_total = pltpu.roll(total, TILE - m, 0)
        prefix = prefix + jnp.where(right, left_total, 0.0)
        total = total + jnp.where(right, left_total, right_total)
        m *= 2
    vb = vv.astype(BF16)
    top = bot = None
    for h in range(H_B):
        vh = jnp.where(head_masks[h], vb, 0.0)
        d0, d1, of = (a.astype(BF16) for a in (diag[h][0], diag[h][1], off[h]))
        if not reverse:
            t = _dot(d0, vh[0:HALF])
            b = _dot(jnp.concatenate([of, d1], axis=1), vh)
        else:
            t = _dot(jnp.concatenate([d0, of], axis=1), vh)
            b = _dot(d1, vh[HALF:])
        top = t if top is None else top + t
        bot = b if bot is None else bot + b
    out = _group_sum(qh * kk, ones_bd) * vv + jnp.concatenate([top, bot], axis=0)
    if not reverse:
        q_decay, k_decay = prefix, total - prefix
    else:
        q_decay, k_decay = total - prefix + g, prefix - g
    if state_t is not None:
        out = out + _dot_nt((qh * jnp.exp2(q_decay)).astype(BF16), state_t.astype(BF16))
    k_hat = (kk * jnp.exp2(k_decay)).astype(BF16)
    return out, k_hat, total[0:1, :]


def _block_diag_mask():
    r = lax.broadcasted_iota(jnp.int32, (W_B, W_B), 0) // DK_B
    c = lax.broadcasted_iota(jnp.int32, (W_B, W_B), 1) // DV_B
    return r == c


def _conv(x, prev_row, next_rows, cw_ref, cb_ref, rows):
    x_m1 = jnp.where(rows == 0, prev_row, pltpu.roll(x, 1, 0))
    x_p1 = jnp.where(rows == TILE - 1, next_rows[0:1], pltpu.roll(x, TILE - 1, 0))
    x_p2 = jnp.where(rows == TILE - 2, next_rows[0:1],
                     jnp.where(rows == TILE - 1, next_rows[1:2], pltpu.roll(x, TILE - 2, 0)))
    y = cb_ref[...] + x_m1 * cw_ref[0:1, :]
    y = y + x * cw_ref[1:2, :]
    y = y + x_p1 * cw_ref[2:3, :]
    return y + x_p2 * cw_ref[3:4, :]


def _rglru_tile(xc, r_pre, i_pre, lam_row, reverse, rows, h_in):
    r = jax.nn.sigmoid(r_pre)
    i = jax.nn.sigmoid(i_pre)
    log_a = -RG_C * r * _softplus(-lam_row)
    a = jnp.exp(log_a)
    u = jnp.sqrt(jnp.tanh(-log_a) * (a * a + 1.0)) * (i * xc)
    k = 1
    while k < TILE:
        if not reverse:
            valid = rows >= k
            a_s, u_s = pltpu.roll(a, k, 0), pltpu.roll(u, k, 0)
        else:
            valid = rows < TILE - k
            a_s, u_s = pltpu.roll(a, TILE - k, 0), pltpu.roll(u, TILE - k, 0)
        u = u + a * jnp.where(valid, u_s, 0.0)
        a = a * jnp.where(valid, a_s, 1.0)
        k *= 2
    if h_in is not None:
        u = u + a * h_in
    return u


def _rec_inputs(refs):
    qb, ff, fb, ib, gb, xcol, gcol = [r[...] for r in refs]
    return _silu(qb), ff, fb, ib, gb, xcol, gcol


def _hgrn_out(o_sum, gb, onorm_gain, ones_bd):
    ms = _group_sum(o_sum * o_sum, ones_bd) * (1.0 / DV_B)
    return o_sum * lax.rsqrt(ms + EPS) * onorm_gain * _silu(gb)


def _rec_ctx_kernel(q_ref, ff_ref, fb_ref, i_ref, g_ref, x_ref, gc_ref, lb_ref, on_ref, cw_ref,
                    cb_ref, wg_ref, bg_ref, lam_ref, ones_ref, ob_ref, oc_ref, st_ref,
                    hs_ref, *, layer, depth):
    aux = _tile_aux()
    rows = aux[0]
    ones_bd = ones_ref[...]
    qh, ff, fb, ib, gb, xcol, gcol = _rec_inputs((q_ref, ff_ref, fb_ref, i_ref, g_ref, x_ref, gc_ref))
    o_sum = None
    for direction, fpre in enumerate((ff, fb)):
        lbd = None if layer == 0 else _hgrn_lower_bound(lb_ref, layer, direction, depth)
        g, kk = _hgrn_gates(fpre, lbd)
        o, k_hat, _ = _hgrn_tile(qh, kk, ib, g, direction == 1, aux, ones_bd, None)
        o_sum = o if o_sum is None else o_sum + o
        full = _dot_tn(k_hat, ib.astype(BF16))
        for h in range(H_B):
            blk = full[h * DK_B:(h + 1) * DK_B, (h // 2) * 2 * DV_B:(h // 2 + 1) * 2 * DV_B]
            if h % 2:
                blk = pltpu.roll(blk, DV_B, 1)
            st_ref[0, direction, h] = blk[:, 0:DV_B]
    ob_ref[...] = _hgrn_out(o_sum, gb, on_ref[...], ones_bd)

    zero_row = jnp.zeros((1, W_C), F32)
    xc = _conv(xcol, zero_row, jnp.zeros((2, W_C), F32), cw_ref, cb_ref, rows)
    gates = _dot(xc.astype(BF16), wg_ref[...]) + bg_ref[...]
    y_f = _rglru_tile(xc, gates[:, 0:W_C], gates[:, W_C:2 * W_C], lam_ref[0:1, :], False, rows, None)
    y_b = _rglru_tile(xc, gates[:, 2 * W_C:3 * W_C], gates[:, 3 * W_C:], lam_ref[1:2, :], True,
                      rows, None)
    oc_ref[...] = (y_f + y_b) * _gelu_tanh(gcol)
    hs_ref[0, 0:1, :] = y_f[TILE - 1:TILE, :]
    hs_ref[0, 1:2, :] = y_b[0:1, :]


def _col_spec(col, row_map):
    return pl.BlockSpec((TILE, W_B), lambda *idx: (row_map(*idx), col))


COL_QB, COL_FF, COL_FB, COL_IB, COL_GB, COL_XC, COL_GC = range(7)
_REC_COLS = (COL_QB, COL_FF, COL_FB, COL_IB, COL_GB, COL_XC, COL_GC)


def _full_spec(a):
    nd = a.ndim
    return pl.BlockSpec(a.shape, lambda *idx: (0,) * nd)


def _recurrence_ctx(proj, layer, depth, consts):
    t = proj.shape[0]
    n = t // TILE
    return pl.pallas_call(
        functools.partial(_rec_ctx_kernel, layer=layer, depth=depth),
        grid=(n,),
        in_specs=[_col_spec(c, lambda b: b) for c in _REC_COLS] + [_full_spec(a) for a in consts],
        out_specs=[
            pl.BlockSpec((TILE, W_B), lambda b: (b, 0)),
            pl.BlockSpec((TILE, W_C), lambda b: (b, 0)),
            pl.BlockSpec((1, 2, H_B, DK_B, DV_B), lambda b: (b, 0, 0, 0, 0)),
            pl.BlockSpec((1, 2, W_C), lambda b: (b, 0, 0)),
        ],
        out_shape=[
            jax.ShapeDtypeStruct((t, W_B), F32),
            jax.ShapeDtypeStruct((t, W_C), F32),
            jax.ShapeDtypeStruct((n, 2, H_B, DK_B, DV_B), F32),
            jax.ShapeDtypeStruct((n, 2, W_C), F32),
        ],
        compiler_params=_params(("arbitrary",)),
        name="rec_ctx",
    )(*([proj] * len(_REC_COLS)), *consts)


def _rec_lat_kernel(q_ref, f_ref, i_ref, g_ref, x_ref, xp_ref, xn_ref, gc_ref, s0_ref, h0_ref,
                    *rest, layer, depth, direction, n_tiles):
    if direction == 0:
        (lb_ref, on_ref, cw_ref, cb_ref, wg_ref, bg_ref, lam_ref, ones_ref,
         o_ref, y_ref, state, hstate) = rest
    else:
        (of_ref, yf_ref, lb_ref, on_ref, cw_ref, cb_ref, wg_ref, bg_ref, lam_ref, ones_ref,
         o_ref, y_ref, state, hstate) = rest
    step = pl.program_id(1)
    tile = step if direction == 0 else n_tiles - 1 - step
    reverse = direction == 1

    @pl.when(step == 0)
    def _():
        state[...] = s0_ref[0, 0]
        hstate[...] = h0_ref[0, 0]

    aux = _tile_aux()
    rows = aux[0]
    ones_bd = ones_ref[...]
    qh = _silu(q_ref[...])
    ib = i_ref[...]
    lbd = None if layer == 0 else _hgrn_lower_bound(lb_ref, layer, direction, depth)
    g, kk = _hgrn_gates(f_ref[...], lbd)
    o, k_hat, total = _hgrn_tile(qh, kk, ib, g, reverse, aux, ones_bd, state[...])
    upd = jnp.where(_block_diag_mask(), _dot_tn(ib.astype(BF16), k_hat), 0.0)
    state[...] = state[...] * jnp.exp2(total) + upd

    prev_row = jnp.where(tile == 0, 0.0, xp_ref[TILE - 1:TILE, :])
    next_rows = jnp.where(tile == n_tiles - 1, 0.0, xn_ref[0:2, :])
    xc = _conv(x_ref[...], prev_row, next_rows, cw_ref, cb_ref, rows)
    gates = _dot(xc.astype(BF16), wg_ref[...]) + bg_ref[...]
    c0 = 2 * W_C * direction
    y = _rglru_tile(xc, gates[:, c0:c0 + W_C], gates[:, c0 + W_C:c0 + 2 * W_C],
                    lam_ref[direction:direction + 1, :], reverse, rows, hstate[...])
    hstate[...] = y[0:1, :] if reverse else y[TILE - 1:TILE, :]

    if direction == 0:
        o_ref[...] = o
        y_ref[...] = y
    else:
        o_ref[...] = _hgrn_out(of_ref[...] + o, g_ref[...], on_ref[...], ones_bd)
        y_ref[...] = (yf_ref[...] + y) * _gelu_tanh(gc_ref[...])


def _recurrence_lat(proj, n_batch, layer, depth, direction, state0, h0, prev, consts):
    t = proj.shape[0]
    n_tiles = t // (TILE * n_batch)

    def tile_of(b, s):
        return s if direction == 0 else n_tiles - 1 - s

    def row(b, s):
        return b * n_tiles + tile_of(b, s)

    def row_prev(b, s):
        return b * n_tiles + jnp.maximum(tile_of(b, s) - 1, 0)

    def row_next(b, s):
        return b * n_tiles + jnp.minimum(tile_of(b, s) + 1, n_tiles - 1)

    f_col = COL_FF if direction == 0 else COL_FB
    in_specs = [_col_spec(COL_QB, row), _col_spec(f_col, row), _col_spec(COL_IB, row),
                _col_spec(COL_GB, row), _col_spec(COL_XC, row), _col_spec(COL_XC, row_prev),
                _col_spec(COL_XC, row_next), _col_spec(COL_GC, row),
                pl.BlockSpec((1, 1, W_B, W_B), lambda b, s: (b, direction, 0, 0)),
                pl.BlockSpec((1, 1, 1, W_C), lambda b, s: (b, direction, 0, 0))]
    args = [proj] * 8 + [state0, h0]
    if direction == 1:
        in_specs += [pl.BlockSpec((TILE, W_B), lambda b, s: (row(b, s), 0)),
                     pl.BlockSpec((TILE, W_C), lambda b, s: (row(b, s), 0))]
        args += list(prev)
    in_specs += [_full_spec(a) for a in consts]
    args += list(consts)
    return pl.pallas_call(
        functools.partial(_rec_lat_kernel, layer=layer, depth=depth, direction=direction,
                          n_tiles=n_tiles),
        grid=(n_batch, n_tiles),
        in_specs=in_specs,
        out_specs=[pl.BlockSpec((TILE, W_B), lambda b, s: (row(b, s), 0)),
                   pl.BlockSpec((TILE, W_C), lambda b, s: (row(b, s), 0))],
        out_shape=[jax.ShapeDtypeStruct((t, W_B), F32), jax.ShapeDtypeStruct((t, W_C), F32)],
        scratch_shapes=[pltpu.VMEM((W_B, W_B), F32), pltpu.VMEM((1, W_C), F32)],
        compiler_params=_params(("arbitrary", "arbitrary")),
        name="rec_lat_fwd" if direction == 0 else "rec_lat_bwd",
    )(*args)


def _out_kernel(x_ref, oa_ref, ob_ref, oc_ref, mod_ref, n2_ref, wo_ref, w1_ref, w2_ref, y_ref):
    x = x_ref[...]
    d = x.shape[-1]
    g1 = mod_ref[0, :, 2 * d:3 * d]
    sh2 = mod_ref[0, :, 3 * d:4 * d]
    sc2 = mod_ref[0, :, 4 * d:5 * d]
    g2 = mod_ref[0, :, 5 * d:6 * d]
    wa = H_A * DV_A
    mix = _dot(oa_ref[...].astype(BF16), wo_ref[0:wa, :])
    mix = mix + _dot(ob_ref[...].astype(BF16), wo_ref[wa:wa + W_B, :])
    mix = mix + _dot(oc_ref[...].astype(BF16), wo_ref[wa + W_B:, :])
    x1 = x + g1 * mix
    h = (_rms_rows(x1, n2_ref[...]) * (1.0 + sc2) + sh2).astype(BF16)
    acc = jnp.zeros_like(x)
    for c in range(w1_ref.shape[1] // D_FF_CHUNK):
        cs = slice(c * D_FF_CHUNK, (c + 1) * D_FF_CHUNK)
        f = jnp.maximum(_dot(h, w1_ref[:, cs]), 0.0)
        acc = acc + _dot((f * f).astype(BF16), w2_ref[cs, :])
    y_ref[...] = x1 + g2 * acc


def _out_mlp(x, oa, ob, oc, mod, tiles_per_mod, n2, w_out, w_ff1, w_ff2, layer):
    t, d = x.shape

    def resident(w):
        return pl.BlockSpec((None,) + w.shape[1:], lambda i: (layer, 0, 0),
                            pipeline_mode=pl.Buffered(1))

    return pl.pallas_call(
        _out_kernel,
        grid=(t // TILE,),
        in_specs=[
            pl.BlockSpec((TILE, d), lambda i: (i, 0)),
            pl.BlockSpec((TILE, oa.shape[1]), lambda i: (i, 0)),
            pl.BlockSpec((TILE, ob.shape[1]), lambda i: (i, 0)),
            pl.BlockSpec((TILE, oc.shape[1]), lambda i: (i, 0)),
            pl.BlockSpec((1, 1, N_MOD * d), lambda i: (i // tiles_per_mod, 0, 0)),
            pl.BlockSpec((1, d), lambda i: (0, 0)),
            resident(w_out),
            resident(w_ff1),
            resident(w_ff2),
        ],
        out_specs=pl.BlockSpec((TILE, d), lambda i: (i, 0)),
        out_shape=jax.ShapeDtypeStruct((t, d), F32),
        compiler_params=_params(("arbitrary",)),
        name="out_mlp",
    )(x, oa, ob, oc, mod, n2, w_out, w_ff1, w_ff2)


def _ones_block_diag(n, group):
    idx = np.arange(n) // group
    return jnp.asarray((idx[:, None] == idx[None, :]).astype(np.float32), dtype=BF16)


def _rope_tables(seq):
    nf = DQK_A // 4
    inv = ROPE_BASE ** (-jnp.arange(nf, dtype=F32) / nf)
    pos = jnp.arange(seq)
    ang_r = (pos // GRID_W).astype(F32)[:, None] * inv
    ang_c = (pos % GRID_W).astype(F32)[:, None] * inv
    z = jnp.zeros_like(ang_r)
    cos = jnp.concatenate([jnp.cos(ang_r)] * 2 + [jnp.cos(ang_c)] * 2, axis=-1)
    sin_lo = jnp.concatenate([-jnp.sin(ang_r), z, -jnp.sin(ang_c), z], axis=-1)
    sin_hi = jnp.concatenate([z, jnp.sin(ang_r), z, jnp.sin(ang_c)], axis=-1)
    return tuple(jnp.tile(a, (1, 2)) for a in (cos, sin_lo, sin_hi))


def _gate_weights(rg_w_l):
    eye = jnp.eye(H_C, dtype=rg_w_l.dtype)
    w = jnp.einsum('dghij,hk->dghikj', rg_w_l, eye)
    w = w.reshape(2, 2, W_C, W_C)
    return jnp.transpose(w, (2, 0, 1, 3)).reshape(W_C, 4 * W_C).astype(BF16)


def _state_block_diag_t(s):
    eye = jnp.eye(H_B, dtype=s.dtype)
    w = jnp.einsum('...hdv,hk->...hvkd', s, eye)
    return w.reshape(s.shape[:-3] + (W_B, W_B))


def kernel(x_prompt, x_sample, cache_k, cache_v, state_hgrn, state_rglru, c, c_ctx, w_mod, b_mod,
           norm1, norm2, w_in, w_out, qk_norm, diff_lambda, subln, hgrn_lb, hgrn_onorm, conv_w,
           conv_b, rg_w, rg_b, rg_lambda, w_ff1, w_ff2):
    batch, seq, d = x_prompt.shape
    dec_batch, dec_seq, _ = x_sample.shape
    depth = w_in.shape[0]
    past = cache_k.shape[2]
    wq = H_A * 2 * DQK_A

    c8 = jnp.concatenate([c_ctx[None], c, jnp.zeros((8 - 1 - dec_batch, d), F32)], axis=0)
    mod_all = _modulation(c8, w_mod, b_mod)

    assert DQK_A == DV_B and seq == TILE
    ones_b = _ones_block_diag(W_B, DV_B)
    rope_tabs = _rope_tables(dec_seq)
    ck = cache_k.reshape(dec_batch, depth, past, wq)
    cv = cache_v.reshape(dec_batch, depth, past, H_A * DV_A)
    lb2 = hgrn_lb.reshape(depth * 2, W_B)

    w_in_b, w_out_b, w1_b, w2_b = (w.astype(BF16) for w in (w_in, w_out, w_ff1, w_ff2))

    yp = x_prompt.reshape(batch * seq, d)
    ys = x_sample.reshape(dec_batch * dec_seq, d)
    ks, vs, shs, srs = [], [], [], []
    for l in range(depth):
        lam_init = 0.8 - 0.6 * math.exp(-0.3 * l)
        mod_ctx = mod_all[l, 0:1][:, None, :]
        mod_lat = mod_all[l, 1:1 + dec_batch][:, None, :]
        n1 = norm1[l][None]
        n2 = norm2[l][None]
        gq = jnp.tile(qk_norm[l, 0], H_A * 2)[None] * (DQK_A ** -0.5 * LOG2E)
        gk = jnp.tile(qk_norm[l, 1], H_A * 2)[None]
        dl = diff_lambda[l]
        sg = subln[l][None]
        consts = (lb2, jnp.tile(hgrn_onorm[l], H_B)[None], conv_w[l], conv_b[l][None],
                  _gate_weights(rg_w[l]), rg_b[l].reshape(1, 4 * W_C), rg_lambda[l], ones_b)

        q, k, v, rest, kt = _in_projection(yp, mod_ctx, batch * seq // TILE, n1, w_in_b, l, gq, gk,
                                           ones_b, keys_t=True)
        oa = _attention_ctx(q, k, v, seq, dl, sg, lam_init)
        ob, oc, st, hs = _recurrence_ctx(rest, l, depth, consts)
        yp = _out_mlp(yp, oa, ob, oc, mod_ctx, batch * seq // TILE, n2, w_out_b, w1_b, w2_b, l)
        ks.append(kt.reshape(H_A, 2, DQK_A, batch, seq))
        vs.append(v.reshape(batch, seq, H_A, DV_A))
        shs.append(st)
        srs.append(hs)

        q, k, v, rest = _in_projection(ys, mod_lat, dec_seq // TILE, n1, w_in_b, l, gq, gk, ones_b)
        oa = _attention_lat(q, k, v, dec_batch, ck, cv, l, rope_tabs, dl, sg, lam_init)
        s0 = _state_block_diag_t(state_hgrn[:, l])
        h0 = state_rglru[:, l][:, :, None, :]
        fwd = _recurrence_lat(rest, dec_batch, l, depth, 0, s0, h0, None, consts)
        ob, oc = _recurrence_lat(rest, dec_batch, l, depth, 1, s0, h0, fwd, consts)
        ys = _out_mlp(ys, oa, ob, oc, mod_lat, dec_seq // TILE, n2, w_out_b, w1_b, w2_b, l)

    return (yp.reshape(batch, seq, d), ys.reshape(dec_batch, dec_seq, d),
            jnp.transpose(jnp.stack(ks, axis=0), (4, 0, 5, 1, 2, 3)), jnp.stack(vs, axis=1),
            jnp.stack(shs, axis=1),
            jnp.stack(srs, axis=1))
```

```python
import functools
import math

import numpy as np
import jax
import jax.numpy as jnp
from jax import lax
from jax.experimental import pallas as pl
from jax.experimental.pallas import tpu as pltpu

F32 = jnp.float32
BF16 = jnp.bfloat16

GRID_W = 64
H_A, DQK_A, DV_A = 4, 64, 128
H_B, DK_B, DV_B = 4, 64, 64
W_B = H_B * DV_B
H_C, W_C = 4, 256
CONV_K = 4
RG_C = 8.0
ROPE_BASE = 10000.0
EPS = 1e-6
N_MOD = 6
D_FF_CHUNK = 1024

TILE = 256
HALF = TILE // 2
SUBLANES = 8
LANES = 128
CHAIN_LEN = TILE // SUBLANES
LOG2E = math.log2(math.e)
V7X_VMEM_BYTES = 64 * 1024 * 1024
VMEM_LIMIT = V7X_VMEM_BYTES - 8 * 1024 * 1024


def _dot(a, b):
    return jnp.dot(a, b, preferred_element_type=F32)


def _dot_nt(a, b):
    return lax.dot_general(a, b, (((1,), (1,)), ((), ())), preferred_element_type=F32)


def _dot_tn(a, b):
    return lax.dot_general(a, b, (((0,), (0,)), ((), ())), preferred_element_type=F32)


def _group_sum(x, ones_bd):
    hi = x.astype(BF16)
    lo = (x - hi.astype(F32)).astype(BF16)
    return _dot(hi, ones_bd) + _dot(lo, ones_bd)


def _rms_rows(x, gain):
    return x * lax.rsqrt(jnp.mean(x * x, axis=-1, keepdims=True) + EPS) * gain


def _sigmoid(x):
    return 0.5 * jnp.tanh(0.5 * x) + 0.5


def _silu(x):
    return x * _sigmoid(x)


def _softplus(x):
    return jnp.maximum(x, 0.0) + jnp.log1p(jnp.exp(-jnp.abs(x)))


def _log_sigmoid(x):
    return jnp.minimum(x, 0.0) - jnp.log1p(jnp.exp(-jnp.abs(x)))


def _gelu_tanh(x):
    return x * (0.5 * (1.0 + jnp.tanh(math.sqrt(2.0 / math.pi) * (x + 0.044715 * (x * x * x)))))


def _params(semantics, flags=None):
    return pltpu.CompilerParams(dimension_semantics=semantics, vmem_limit_bytes=VMEM_LIMIT,
                                flags=flags)


def _mod_kernel(c_ref, w_ref, b_ref, o_ref):
    c = c_ref[...]
    o_ref[0] = _dot(_silu(c).astype(BF16), w_ref[0].astype(BF16)) + b_ref[0]


def _modulation(c8, w_mod, b_mod):
    depth, d, n = w_mod.shape
    tn = d
    return pl.pallas_call(
        _mod_kernel,
        grid=(depth, n // tn),
        in_specs=[
            pl.BlockSpec((8, d), lambda l, j: (0, 0)),
            pl.BlockSpec((1, d, tn), lambda l, j: (l, 0, j)),
            pl.BlockSpec((1, 1, tn), lambda l, j: (l, 0, j)),
        ],
        out_specs=pl.BlockSpec((1, 8, tn), lambda l, j: (l, 0, j)),
        out_shape=jax.ShapeDtypeStruct((depth, 8, n), F32),
        compiler_params=_params(("arbitrary", "arbitrary")),
        name="mod",
    )(c8, w_mod, b_mod.reshape(depth, 1, n))


def _proj_kernel(x_ref, mod_ref, n1_ref, w_ref, gq_ref, gk_ref, ones_ref, q_ref, k_ref, v_ref, r_ref,
                 *kt_ref):
    x = x_ref[...]
    d = x.shape[-1]
    sh = mod_ref[0, :, 0:d]
    sc = mod_ref[0, :, d:2 * d]
    h = _rms_rows(x, n1_ref[...]) * (1.0 + sc) + sh
    p = _dot(h.astype(BF16), w_ref[...])
    ones_bd = ones_ref[...]
    wq = H_A * 2 * DQK_A

    def qk_norm(z, gain):
        z2 = z * z
        nb = ones_bd.shape[0]
        ms = jnp.concatenate([_group_sum(z2[:, c:c + nb], ones_bd) for c in range(0, wq, nb)],
                             axis=1) * (1.0 / DQK_A)
        return z * lax.rsqrt(ms + EPS) * gain

    wv = H_A * DV_A
    q_ref[...] = qk_norm(p[:, 0:wq], gq_ref[...])
    kn = qk_norm(p[:, wq:2 * wq], gk_ref[...])
    k_ref[...] = kn
    if kt_ref:
        row = pl.program_id(0) % SUBLANES
        kt_ref[0][:, pl.ds(row, 1), :] = kn.T[:, None, :]
    for h in range(H_A):
        v_ref[pl.ds(h, TILE, stride=H_A), :] = p[:, 2 * wq + h * DV_A:2 * wq + (h + 1) * DV_A]
    r_ref[...] = p[:, 2 * wq + wv:]


def _in_projection(x, mod, tiles_per_mod, n1, w_in, layer, gq, gk, ones_bd, keys_t=False):
    t, d = x.shape
    n_in = w_in.shape[2]
    wq = gq.shape[1]
    wv = H_A * DV_A
    widths = (wq, wq, wv, n_in - 2 * wq - wv)
    out_specs = [pl.BlockSpec((TILE, w), lambda i: (i, 0)) for w in widths]
    out_shape = [jax.ShapeDtypeStruct((t, w), F32) for w in widths]
    out_specs[2] = pl.BlockSpec((TILE * H_A, DV_A), lambda i: (i, 0))
    out_shape[2] = jax.ShapeDtypeStruct((t * H_A, DV_A), F32)
    if keys_t:
        out_specs.append(pl.BlockSpec((wq, SUBLANES, TILE), lambda i: (0, i // SUBLANES, 0)))
        out_shape.append(jax.ShapeDtypeStruct((wq, t // TILE, TILE), F32))
    return pl.pallas_call(
        _proj_kernel,
        grid=(t // TILE,),
        in_specs=[
            pl.BlockSpec((TILE, d), lambda i: (i, 0)),
            pl.BlockSpec((1, 1, N_MOD * d), lambda i: (i // tiles_per_mod, 0, 0)),
            pl.BlockSpec((1, d), lambda i: (0, 0)),
            pl.BlockSpec((None, d, n_in), lambda i: (layer, 0, 0)),
            pl.BlockSpec((1, wq), lambda i: (0, 0)),
            pl.BlockSpec((1, wq), lambda i: (0, 0)),
            pl.BlockSpec(ones_bd.shape, lambda i: (0, 0)),
        ],
        out_specs=out_specs,
        out_shape=out_shape,
        compiler_params=_params(("arbitrary",)),
        name="proj",
    )(x, mod, n1, w_in, gq, gk, ones_bd)


def _diff_lambda(dl_ref, lam_init):
    lv = dl_ref[...]
    a = jnp.sum(lv[0:1] * lv[1:2], axis=-1, keepdims=True)
    b = jnp.sum(lv[2:3] * lv[3:4], axis=-1, keepdims=True)
    return jnp.exp(a) - jnp.exp(b) + lam_init


def _diff_attention(q, k_of, vext_of, lam, sub_gain, lam_init, o_ref):
    hw = 2 * DQK_A
    first = lax.broadcasted_iota(jnp.int32, (1, hw), 1) < DQK_A
    for h in range(H_A):
        hs = slice(h * hw, (h + 1) * hw)
        qh = q[:, hs]
        kh = k_of(h)
        vext = vext_of(h)
        maps = []
        for qm in (jnp.where(first, qh, 0.0), jnp.where(first, 0.0, qh)):
            s = _dot_nt(qm.astype(BF16), kh)
            e = jnp.exp2(s - jnp.max(s, axis=-1, keepdims=True)).astype(BF16)
            oe = _dot(e, vext)
            maps.append(oe[:, 0:DV_A] / oe[:, DV_A:])
        oh = maps[0] - lam * maps[1]
        o_ref[:, hs] = _rms_rows(oh, sub_gain) * (1.0 - lam_init)


def _attn_ctx_kernel(q_ref, k_ref, v_ref, dl_ref, sg_ref, o_ref, *, lam_init):
    lam = _diff_lambda(dl_ref, lam_init)
    k = k_ref[...].astype(BF16)
    seq = k.shape[0]
    ones = jnp.ones((seq, DV_A), BF16)
    hw = 2 * DQK_A

    def vext(h):
        return jnp.concatenate([v_ref[pl.ds(h, seq, stride=H_A), :].astype(BF16), ones], axis=1)

    _diff_attention(q_ref[...], lambda h: k[:, h * hw:(h + 1) * hw], vext, lam, sg_ref[...],
                    lam_init, o_ref)


def _attention_ctx(q, k, v, seq, dl, sg, lam_init):
    t = q.shape[0]
    w = H_A * DV_A
    return pl.pallas_call(
        functools.partial(_attn_ctx_kernel, lam_init=lam_init),
        grid=(t // seq,),
        in_specs=[
            pl.BlockSpec((seq, w), lambda b: (b, 0)),
            pl.BlockSpec((seq, w), lambda b: (b, 0)),
            pl.BlockSpec((seq * H_A, DV_A), lambda b: (b, 0)),
            pl.BlockSpec(dl.shape, lambda b: (0, 0)),
            pl.BlockSpec(sg.shape, lambda b: (0, 0)),
        ],
        out_specs=pl.BlockSpec((seq, w), lambda b: (b, 0)),
        out_shape=jax.ShapeDtypeStruct((t, w), F32),
        compiler_params=_params(("arbitrary",)),
        name="attn_ctx",
    )(q, k, v, dl, sg)


def _rope(x, cos, sin_lo, sin_hi):
    return (x * cos + pltpu.roll(x, 2 * DQK_A - 16, 1) * sin_lo + pltpu.roll(x, 16, 1) * sin_hi)


def _attn_lat_kernel(q_ref, k_ref, v_ref, ck_ref, cv_ref, qcos_ref, qslo_ref, qshi_ref,
                     kcos_ref, kslo_ref, kshi_ref, dl_ref, sg_ref, o_ref, kall, vall, *, lam_init):
    hw = 2 * DQK_A
    n_new = k_ref.shape[0]

    @pl.when(pl.program_id(1) == 0)
    def _():
        kcos, kslo, kshi = kcos_ref[...], kslo_ref[...], kshi_ref[...]
        for h in range(H_A):
            hs = slice(h * hw, (h + 1) * hw)
            kall[0:n_new, hs] = _rope(k_ref[:, hs], kcos, kslo, kshi).astype(BF16)
        kall[n_new:, :] = ck_ref[...].astype(BF16)
        ones = jnp.ones((vall.shape[0], DV_A), BF16)
        for h in range(H_A):
            vs = slice(h * DV_A, (h + 1) * DV_A)
            vall[0:n_new, 2 * h * DV_A:(2 * h + 1) * DV_A] = (
                v_ref[pl.ds(h, n_new, stride=H_A), :].astype(BF16))
            vall[n_new:, 2 * h * DV_A:(2 * h + 1) * DV_A] = cv_ref[:, vs].astype(BF16)
            vall[:, (2 * h + 1) * DV_A:(2 * h + 2) * DV_A] = ones

    lam = _diff_lambda(dl_ref, lam_init)
    qcos, qslo, qshi = qcos_ref[...], qslo_ref[...], qshi_ref[...]
    q = jnp.concatenate(
        [_rope(q_ref[:, h * hw:(h + 1) * hw], qcos, qslo, qshi) for h in range(H_A)], axis=-1)
    _diff_attention(q, lambda h: kall[:, h * hw:(h + 1) * hw],
                    lambda h: vall[:, 2 * h * DV_A:(2 * h + 2) * DV_A], lam, sg_ref[...],
                    lam_init, o_ref)


def _attention_lat(q, k, v, n_batch, cache_k, cache_v, layer, rope_tabs, dl, sg, lam_init):
    t = q.shape[0]
    seq = t // n_batch
    past = cache_k.shape[2]
    w = H_A * DV_A
    hw = 2 * DQK_A
    nq = seq // TILE
    cos, slo, shi = rope_tabs
    tab_q = pl.BlockSpec((TILE, hw), lambda b, j: (j, 0))
    tab_k = pl.BlockSpec((seq, hw), lambda b, j: (0, 0))
    return pl.pallas_call(
        functools.partial(_attn_lat_kernel, lam_init=lam_init),
        grid=(n_batch, nq),
        in_specs=[
            pl.BlockSpec((TILE, w), lambda b, j: (b * nq + j, 0)),
            pl.BlockSpec((seq, w), lambda b, j: (b, 0)),
            pl.BlockSpec((seq * H_A, DV_A), lambda b, j: (b, 0)),
            pl.BlockSpec((None, None, past, w), lambda b, j: (b, layer, 0, 0)),
            pl.BlockSpec((None, None, past, w), lambda b, j: (b, layer, 0, 0)),
            tab_q, tab_q, tab_q, tab_k, tab_k, tab_k,
            pl.BlockSpec(dl.shape, lambda b, j: (0, 0)),
            pl.BlockSpec(sg.shape, lambda b, j: (0, 0)),
        ],
        out_specs=pl.BlockSpec((TILE, w), lambda b, j: (b * nq + j, 0)),
        out_shape=jax.ShapeDtypeStruct((t, w), F32),
        scratch_shapes=[pltpu.VMEM((seq + past, w), BF16), pltpu.VMEM((seq + past, 2 * w), BF16)],
        compiler_params=_params(("arbitrary", "arbitrary")),
        name="attn_lat",
    )(q, k, v, cache_k, cache_v, cos, slo, shi, cos, slo, shi, dl, sg)


def _tile_aux():
    rows = lax.broadcasted_iota(jnp.int32, (TILE, 1), 0)
    lane = lax.broadcasted_iota(jnp.int32, (1, W_B), 1)
    head_masks = [(lane // DK_B) == h for h in range(H_B)]
    first_of_pair = lax.broadcasted_iota(jnp.int32, (1, 2 * DK_B), 1) < DK_B
    xor = (lax.broadcasted_iota(jnp.int32, (HALF, HALF), 0)
           ^ lax.broadcasted_iota(jnp.int32, (HALF, HALF), 1))
    return rows, head_masks, first_of_pair, xor


def _hgrn_lower_bound(lb_ref, layer, direction, depth):
    xs = [lb_ref[2 * j + direction:2 * j + direction + 1, :] for j in range(depth)]
    m = functools.reduce(jnp.maximum, xs)
    es = [jnp.exp(x - m) for x in xs]
    return sum(es[1:layer + 1]) / sum(es)


def _hgrn_gates(fpre, lbd):
    t = jnp.exp(-jnp.abs(fpre))
    log_sig = jnp.minimum(fpre, 0.0) - jnp.log(1.0 + t)
    r = 1.0 / (1.0 + t)
    sig_neg = jnp.where(fpre >= 0.0, t * r, r)
    if lbd is None:
        return log_sig * LOG2E, sig_neg
    a = jnp.log(lbd)
    b = jnp.log1p(-lbd) + log_sig
    logf = jnp.maximum(a, b) + jnp.log1p(jnp.exp(-jnp.abs(a - b)))
    return logf * LOG2E, (1.0 - lbd) * sig_neg


def _hgrn_tile(qh, kk, vv, g, reverse, aux, ones_bd, state_t):
    rows, head_masks, first_of_pair, xor = aux
    pair_w = 2 * DK_B
    prefix = g
    total = g
    diag = [[None, None] for _ in range(H_B)]
    off = [None] * H_B
    m = 1
    while m < TILE:
        right = (rows & m) != 0
        if not reverse:
            e = jnp.where(right, prefix, total - prefix)
        else:
            e = jnp.where(right, prefix - g, total - prefix + g)
        x = jnp.exp2(e)
        q_rows = right if not reverse else jnp.logical_not(right)
        qz = jnp.where(q_rows, (qh * x).astype(BF16), 0.0)
        kz = jnp.where(q_rows, 0.0, (kk * x).astype(BF16))
        for h in range(H_B):
            ls = slice((h // 2) * pair_w, (h // 2 + 1) * pair_w)
            qm = jnp.where(first_of_pair, qz[:, ls], 0.0) if h % 2 == 0 else \
                jnp.where(first_of_pair, 0.0, qz[:, ls])
            kh = kz[:, ls]
            if m < HALF:
                for i in range(2):
                    rs = slice(i * HALF, (i + 1) * HALF)
                    pm = _dot_nt(qm[rs], kh[rs])
                    if 2 * m < HALF:
                        pm = jnp.where(xor < 2 * m, pm, 0.0)
                    diag[h][i] = pm if diag[h][i] is None else diag[h][i] + pm
            else:
                lo, hi = slice(0, HALF), slice(HALF, TILE)
                off[h] = _dot_nt(qm[hi], kh[lo]) if not reverse else _dot_nt(qm[lo], kh[hi])
        left_total = pltpu.roll(total, m, 0)
        right_total = pltpu.roll(total, TILE - m, 0)
        prefix = prefix + jnp.where(right, left_total, 0.0)
        total = total + jnp.where(right, left_total, right_total)
        m *= 2
    vb = vv.astype(BF16)
    top = bot = None
    for h in range(H_B):
        vh = jnp.where(head_masks[h], vb, 0.0)
        d0, d1, of = (a.astype(BF16) for a in (diag[h][0], diag[h][1], off[h]))
        if not reverse:
            t = _dot(d0, vh[0:HALF])
            b = _dot(jnp.concatenate([of, d1], axis=1), vh)
        else:
            t = _dot(jnp.concatenate([d0, of], axis=1), vh)
            b = _dot(d1, vh[HALF:])
        top = t if top is None else top + t
        bot = b if bot is None else bot + b
    out = _group_sum(qh * kk, ones_bd) * vv + jnp.concatenate([top, bot], axis=0)
    if not reverse:
        q_decay, k_decay = prefix, total - prefix
    else:
        q_decay, k_decay = total - prefix + g, prefix - g
    if state_t is not None:
        out = out + _dot_nt((qh * jnp.exp2(q_decay)).astype(BF16), state_t.astype(BF16))
    k_hat = (kk * jnp.exp2(k_decay)).astype(BF16)
    return out, k_hat, total[0:1, :]


def _block_diag_mask():
    r = lax.broadcasted_iota(jnp.int32, (W_B, W_B), 0) // DK_B
    c = lax.broadcasted_iota(jnp.int32, (W_B, W_B), 1) // DV_B
    return r == c


def _conv(x, prev_row, next_rows, cw_ref, cb_ref, rows):
    x_m1 = jnp.where(rows == 0, prev_row, pltpu.roll(x, 1, 0))
    x_p1 = jnp.where(rows == TILE - 1, next_rows[0:1], pltpu.roll(x, TILE - 1, 0))
    x_p2 = jnp.where(rows == TILE - 2, next_rows[0:1],
                     jnp.where(rows == TILE - 1, next_rows[1:2], pltpu.roll(x, TILE - 2, 0)))
    y = cb_ref[...] + x_m1 * cw_ref[0:1, :]
    y = y + x * cw_ref[1:2, :]
    y = y + x_p1 * cw_ref[2:3, :]
    return y + x_p2 * cw_ref[3:4, :]


def _chain_address(g):
    per_chain = CHAIN_LEN // SUBLANES
    return SUBLANES * SUBLANES * (g % per_chain) + g // per_chain, SUBLANES


def _to_chains(x, scr):
    slabs = scr.shape[0]
    for s in range(slabs):
        for g in range(TILE // SUBLANES):
            start, stride = _chain_address(g)
            scr[s, pl.ds(start, SUBLANES, stride=stride), :] = (
                x[g * SUBLANES:(g + 1) * SUBLANES, s * LANES:(s + 1) * LANES])
    return [jnp.concatenate([scr[s, j * SUBLANES:(j + 1) * SUBLANES, :] for s in range(slabs)],
                            axis=1) for j in range(CHAIN_LEN)]


def _from_chains(chains, scr):
    slabs = scr.shape[0]
    for s in range(slabs):
        for j, c in enumerate(chains):
            scr[s, j * SUBLANES:(j + 1) * SUBLANES, :] = c[:, s * LANES:(s + 1) * LANES]
    groups = []
    for g in range(TILE // SUBLANES):
        start, stride = _chain_address(g)
        groups.append(jnp.concatenate(
            [scr[s, pl.ds(start, SUBLANES, stride=stride), :] for s in range(slabs)], axis=1))
    return jnp.concatenate(groups, axis=0)


def _rglru_tile(xc, r_pre, i_pre, lam_row, reverse, h_in, scratch):
    a_scr, u_scr, h_scr = scratch
    r = _sigmoid(r_pre)
    i = _sigmoid(i_pre)
    log_a = -RG_C * r * _softplus(-lam_row)
    a = jnp.exp(log_a)
    u = jnp.sqrt(jnp.tanh(-log_a) * (a * a + 1.0)) * (i * xc)
    a_c = _to_chains(a, a_scr)
    u_c = _to_chains(u, u_scr)
    order = list(reversed(range(CHAIN_LEN))) if reverse else list(range(CHAIN_LEN))
    h_c = [None] * CHAIN_LEN
    p_c = [None] * CHAIN_LEN
    h_c[order[0]], p_c[order[0]] = u_c[order[0]], a_c[order[0]]
    for prev, j in zip(order[:-1], order[1:]):
        h_c[j] = a_c[j] * h_c[prev] + u_c[j]
        p_c[j] = a_c[j] * p_c[prev]
    p_tot, h_tot = p_c[order[-1]], h_c[order[-1]]
    sub = lax.broadcasted_iota(jnp.int32, (SUBLANES, 1), 0)
    k = 1
    while k < SUBLANES:
        if not reverse:
            valid = sub >= k
            p_s, h_s = pltpu.roll(p_tot, k, 0), pltpu.roll(h_tot, k, 0)
        else:
            valid = sub < SUBLANES - k
            p_s, h_s = pltpu.roll(p_tot, SUBLANES - k, 0), pltpu.roll(h_tot, SUBLANES - k, 0)
        h_tot = h_tot + p_tot * jnp.where(valid, h_s, 0.0)
        p_tot = p_tot * jnp.where(valid, p_s, 1.0)
        k *= 2
    if h_in is not None:
        h_tot = h_tot + p_tot * h_in
    if not reverse:
        carry = jnp.where(sub == 0, 0.0 if h_in is None else h_in, pltpu.roll(h_tot, 1, 0))
        h_out = h_tot[SUBLANES - 1:SUBLANES, :]
    else:
        carry = jnp.where(sub == SUBLANES - 1, 0.0 if h_in is None else h_in,
                          pltpu.roll(h_tot, SUBLANES - 1, 0))
        h_out = h_tot[0:1, :]
    h_c = [h + p * carry for h, p in zip(h_c, p_c)]
    return _from_chains(h_c, h_scr), h_out


def _rec_inputs(refs):
    qb, ff, fb, ib, gb, xcol, gcol = [r[...] for r in refs]
    return _silu(qb), ff, fb, ib, gb, xcol, gcol


def _hgrn_out(o_sum, gb, onorm_gain, ones_bd):
    ms = _group_sum(o_sum * o_sum, ones_bd) * (1.0 / DV_B)
    return o_sum * lax.rsqrt(ms + EPS) * onorm_gain * _silu(gb)


def _rec_ctx_kernel(q_ref, ff_ref, fb_ref, i_ref, g_ref, x_ref, gc_ref, lb_ref, on_ref, cw_ref,
                    cb_ref, wg_ref, bg_ref, lam_ref, ones_ref, ob_ref, oc_ref, st_ref,
                    hs_ref, *scan_scratch, layer, depth):
    aux = _tile_aux()
    rows = aux[0]
    ones_bd = ones_ref[...]
    qh, ff, fb, ib, gb, xcol, gcol = _rec_inputs((q_ref, ff_ref, fb_ref, i_ref, g_ref, x_ref, gc_ref))
    o_sum = None
    for direction, fpre in enumerate((ff, fb)):
        lbd = None if layer == 0 else _hgrn_lower_bound(lb_ref, layer, direction, depth)
        g, kk = _hgrn_gates(fpre, lbd)
        o, k_hat, _ = _hgrn_tile(qh, kk, ib, g, direction == 1, aux, ones_bd, None)
        o_sum = o if o_sum is None else o_sum + o
        full = _dot_tn(k_hat, ib.astype(BF16))
        for h in range(H_B):
            blk = full[h * DK_B:(h + 1) * DK_B, (h // 2) * 2 * DV_B:(h // 2 + 1) * 2 * DV_B]
            if h % 2:
                blk = pltpu.roll(blk, DV_B, 1)
            st_ref[0, direction, h] = blk[:, 0:DV_B]
    ob_ref[...] = _hgrn_out(o_sum, gb, on_ref[...], ones_bd)

    zero_row = jnp.zeros((1, W_C), F32)
    xc = _conv(xcol, zero_row, jnp.zeros((2, W_C), F32), cw_ref, cb_ref, rows)
    gates = _dot(xc.astype(BF16), wg_ref[...]) + bg_ref[...]
    y_f, h_f = _rglru_tile(xc, gates[:, 0:W_C], gates[:, W_C:2 * W_C], lam_ref[0:1, :], False,
                           None, scan_scratch)
    y_b, h_b = _rglru_tile(xc, gates[:, 2 * W_C:3 * W_C], gates[:, 3 * W_C:], lam_ref[1:2, :], True,
                           None, scan_scratch)
    oc_ref[...] = (y_f + y_b) * _gelu_tanh(gcol)
    hs_ref[0, 0:1, :] = h_f
    hs_ref[0, 1:2, :] = h_b


def _scan_scratch():
    return [pltpu.VMEM((W_C // LANES, TILE, LANES), F32) for _ in range(3)]


def _col_spec(col, row_map):
    return pl.BlockSpec((TILE, W_B), lambda *idx: (row_map(*idx), col))


COL_QB, COL_FF, COL_FB, COL_IB, COL_GB, COL_XC, COL_GC = range(7)
_REC_COLS = (COL_QB, COL_FF, COL_FB, COL_IB, COL_GB, COL_XC, COL_GC)


def _full_spec(a):
    nd = a.ndim
    return pl.BlockSpec(a.shape, lambda *idx: (0,) * nd)


def _recurrence_ctx(proj, layer, depth, consts):
    t = proj.shape[0]
    n = t // TILE
    return pl.pallas_call(
        functools.partial(_rec_ctx_kernel, layer=layer, depth=depth),
        grid=(n,),
        in_specs=[_col_spec(c, lambda b: b) for c in _REC_COLS] + [_full_spec(a) for a in consts],
        out_specs=[
            pl.BlockSpec((TILE, W_B), lambda b: (b, 0)),
            pl.BlockSpec((TILE, W_C), lambda b: (b, 0)),
            pl.BlockSpec((1, 2, H_B, DK_B, DV_B), lambda b: (b, 0, 0, 0, 0)),
            pl.BlockSpec((1, 2, W_C), lambda b: (b, 0, 0)),
        ],
        out_shape=[
            jax.ShapeDtypeStruct((t, W_B), F32),
            jax.ShapeDtypeStruct((t, W_C), F32),
            jax.ShapeDtypeStruct((n, 2, H_B, DK_B, DV_B), F32),
            jax.ShapeDtypeStruct((n, 2, W_C), F32),
        ],
        scratch_shapes=_scan_scratch(),
        compiler_params=_params(("arbitrary",)),
        name="rec_ctx",
    )(*([proj] * len(_REC_COLS)), *consts)


def _rec_lat_kernel(q_ref, f_ref, i_ref, g_ref, x_ref, xp_ref, xn_ref, gc_ref, s0_ref, h0_ref,
                    *rest, layer, depth, direction, n_tiles):
    if direction == 0:
        (lb_ref, on_ref, cw_ref, cb_ref, wg_ref, bg_ref, lam_ref, ones_ref,
         o_ref, y_ref, state, hstate, *scan_scratch) = rest
    else:
        (of_ref, yf_ref, lb_ref, on_ref, cw_ref, cb_ref, wg_ref, bg_ref, lam_ref, ones_ref,
         o_ref, y_ref, state, hstate, *scan_scratch) = rest
    step = pl.program_id(1)
    tile = step if direction == 0 else n_tiles - 1 - step
    reverse = direction == 1

    @pl.when(step == 0)
    def _():
        state[...] = s0_ref[0, 0]
        hstate[...] = h0_ref[0, 0]

    aux = _tile_aux()
    rows = aux[0]
    ones_bd = ones_ref[...]
    qh = _silu(q_ref[...])
    ib = i_ref[...]
    lbd = None if layer == 0 else _hgrn_lower_bound(lb_ref, layer, direction, depth)
    g, kk = _hgrn_gates(f_ref[...], lbd)
    o, k_hat, total = _hgrn_tile(qh, kk, ib, g, reverse, aux, ones_bd, state[...])
    upd = jnp.where(_block_diag_mask(), _dot_tn(ib.astype(BF16), k_hat), 0.0)
    state[...] = state[...] * jnp.exp2(total) + upd

    prev_row = jnp.where(tile == 0, 0.0, xp_ref[TILE - 1:TILE, :])
    next_rows = jnp.where(tile == n_tiles - 1, 0.0, xn_ref[0:2, :])
    xc = _conv(x_ref[...], prev_row, next_rows, cw_ref, cb_ref, rows)
    gates = _dot(xc.astype(BF16), wg_ref[...]) + bg_ref[...]
    c0 = 2 * W_C * direction
    y, h_next = _rglru_tile(xc, gates[:, c0:c0 + W_C], gates[:, c0 + W_C:c0 + 2 * W_C],
                            lam_ref[direction:direction + 1, :], reverse, hstate[...], scan_scratch)
    hstate[...] = h_next

    if direction == 0:
        o_ref[...] = o
        y_ref[...] = y
    else:
        o_ref[...] = _hgrn_out(of_ref[...] + o, g_ref[...], on_ref[...], ones_bd)
        y_ref[...] = (yf_ref[...] + y) * _gelu_tanh(gc_ref[...])


def _recurrence_lat(proj, n_batch, layer, depth, direction, state0, h0, prev, consts):
    t = proj.shape[0]
    n_tiles = t // (TILE * n_batch)

    def tile_of(b, s):
        return s if direction == 0 else n_tiles - 1 - s

    def row(b, s):
        return b * n_tiles + tile_of(b, s)

    def row_prev(b, s):
        return b * n_tiles + jnp.maximum(tile_of(b, s) - 1, 0)

    def row_next(b, s):
        return b * n_tiles + jnp.minimum(tile_of(b, s) + 1, n_tiles - 1)

    f_col = COL_FF if direction == 0 else COL_FB
    in_specs = [_col_spec(COL_QB, row), _col_spec(f_col, row), _col_spec(COL_IB, row),
                _col_spec(COL_GB, row), _col_spec(COL_XC, row), _col_spec(COL_XC, row_prev),
                _col_spec(COL_XC, row_next), _col_spec(COL_GC, row),
                pl.BlockSpec((1, 1, W_B, W_B), lambda b, s: (b, direction, 0, 0)),
                pl.BlockSpec((1, 1, 1, W_C), lambda b, s: (b, direction, 0, 0))]
    args = [proj] * 8 + [state0, h0]
    if direction == 1:
        in_specs += [pl.BlockSpec((TILE, W_B), lambda b, s: (row(b, s), 0)),
                     pl.BlockSpec((TILE, W_C), lambda b, s: (row(b, s), 0))]
        args += list(prev)
    in_specs += [_full_spec(a) for a in consts]
    args += list(consts)
    return pl.pallas_call(
        functools.partial(_rec_lat_kernel, layer=layer, depth=depth, direction=direction,
                          n_tiles=n_tiles),
        grid=(n_batch, n_tiles),
        in_specs=in_specs,
        out_specs=[pl.BlockSpec((TILE, W_B), lambda b, s: (row(b, s), 0)),
                   pl.BlockSpec((TILE, W_C), lambda b, s: (row(b, s), 0))],
        out_shape=[jax.ShapeDtypeStruct((t, W_B), F32), jax.ShapeDtypeStruct((t, W_C), F32)],
        scratch_shapes=[pltpu.VMEM((W_B, W_B), F32), pltpu.VMEM((1, W_C), F32)] + _scan_scratch(),
        compiler_params=_params(("arbitrary", "arbitrary")),
        name="rec_lat_fwd" if direction == 0 else "rec_lat_bwd",
    )(*args)


def _out_kernel(x_ref, oa_ref, ob_ref, oc_ref, mod_ref, n2_ref, wo_ref, w1_ref, w2_ref, y_ref):
    x = x_ref[...]
    d = x.shape[-1]
    g1 = mod_ref[0, :, 2 * d:3 * d]
    sh2 = mod_ref[0, :, 3 * d:4 * d]
    sc2 = mod_ref[0, :, 4 * d:5 * d]
    g2 = mod_ref[0, :, 5 * d:6 * d]
    wa = H_A * DV_A
    mix = _dot(oa_ref[...].astype(BF16), wo_ref[0:wa, :])
    mix = mix + _dot(ob_ref[...].astype(BF16), wo_ref[wa:wa + W_B, :])
    mix = mix + _dot(oc_ref[...].astype(BF16), wo_ref[wa + W_B:, :])
    x1 = x + g1 * mix
    h = (_rms_rows(x1, n2_ref[...]) * (1.0 + sc2) + sh2).astype(BF16)
    acc = jnp.zeros_like(x)
    for c in range(w1_ref.shape[1] // D_FF_CHUNK):
        cs = slice(c * D_FF_CHUNK, (c + 1) * D_FF_CHUNK)
        f = jnp.maximum(_dot(h, w1_ref[:, cs]), 0.0)
        acc = acc + _dot((f * f).astype(BF16), w2_ref[cs, :])
    y_ref[...] = x1 + g2 * acc


def _out_mlp(x, oa, ob, oc, mod, tiles_per_mod, n2, w_out, w_ff1, w_ff2, layer):
    t, d = x.shape

    def resident(w):
        return pl.BlockSpec((None,) + w.shape[1:], lambda i: (layer, 0, 0),
                            pipeline_mode=pl.Buffered(1))

    return pl.pallas_call(
        _out_kernel,
        grid=(t // TILE,),
        in_specs=[
            pl.BlockSpec((TILE, d), lambda i: (i, 0)),
            pl.BlockSpec((TILE, oa.shape[1]), lambda i: (i, 0)),
            pl.BlockSpec((TILE, ob.shape[1]), lambda i: (i, 0)),
            pl.BlockSpec((TILE, oc.shape[1]), lambda i: (i, 0)),
            pl.BlockSpec((1, 1, N_MOD * d), lambda i: (i // tiles_per_mod, 0, 0)),
            pl.BlockSpec((1, d), lambda i: (0, 0)),
            resident(w_out),
            resident(w_ff1),
            resident(w_ff2),
        ],
        out_specs=pl.BlockSpec((TILE, d), lambda i: (i, 0)),
        out_shape=jax.ShapeDtypeStruct((t, d), F32),
        compiler_params=_params(("arbitrary",)),
        name="out_mlp",
    )(x, oa, ob, oc, mod, n2, w_out, w_ff1, w_ff2)


def _ones_block_diag(n, group):
    idx = np.arange(n) // group
    return jnp.asarray((idx[:, None] == idx[None, :]).astype(np.float32), dtype=BF16)


def _rope_tables(seq):
    nf = DQK_A // 4
    inv = ROPE_BASE ** (-jnp.arange(nf, dtype=F32) / nf)
    pos = jnp.arange(seq)
    ang_r = (pos // GRID_W).astype(F32)[:, None] * inv
    ang_c = (pos % GRID_W).astype(F32)[:, None] * inv
    z = jnp.zeros_like(ang_r)
    cos = jnp.concatenate([jnp.cos(ang_r)] * 2 + [jnp.cos(ang_c)] * 2, axis=-1)
    sin_lo = jnp.concatenate([-jnp.sin(ang_r), z, -jnp.sin(ang_c), z], axis=-1)
    sin_hi = jnp.concatenate([z, jnp.sin(ang_r), z, jnp.sin(ang_c)], axis=-1)
    return tuple(jnp.tile(a, (1, 2)) for a in (cos, sin_lo, sin_hi))


def _gate_weights(rg_w_l):
    eye = jnp.eye(H_C, dtype=rg_w_l.dtype)
    w = jnp.einsum('dghij,hk->dghikj', rg_w_l, eye)
    w = w.reshape(2, 2, W_C, W_C)
    return jnp.transpose(w, (2, 0, 1, 3)).reshape(W_C, 4 * W_C).astype(BF16)


def _state_block_diag_t(s):
    eye = jnp.eye(H_B, dtype=s.dtype)
    w = jnp.einsum('...hdv,hk->...hvkd', s, eye)
    return w.reshape(s.shape[:-3] + (W_B, W_B))


def kernel(x_prompt, x_sample, cache_k, cache_v, state_hgrn, state_rglru, c, c_ctx, w_mod, b_mod,
           norm1, norm2, w_in, w_out, qk_norm, diff_lambda, subln, hgrn_lb, hgrn_onorm, conv_w,
           conv_b, rg_w, rg_b, rg_lambda, w_ff1, w_ff2):
    batch, seq, d = x_prompt.shape
    dec_batch, dec_seq, _ = x_sample.shape
    depth = w_in.shape[0]
    past = cache_k.shape[2]
    wq = H_A * 2 * DQK_A

    c8 = jnp.concatenate([c_ctx[None], c, jnp.zeros((8 - 1 - dec_batch, d), F32)], axis=0)
    mod_all = _modulation(c8, w_mod, b_mod)

    assert DQK_A == DV_B and seq == TILE
    ones_b = _ones_block_diag(W_B, DV_B)
    rope_tabs = _rope_tables(dec_seq)
    ck = cache_k.reshape(dec_batch, depth, past, wq)
    cv = cache_v.reshape(dec_batch, depth, past, H_A * DV_A)
    lb2 = hgrn_lb.reshape(depth * 2, W_B)

    w_in_b, w_out_b, w1_b, w2_b = (w.astype(BF16) for w in (w_in, w_out, w_ff1, w_ff2))

    yp = x_prompt.reshape(batch * seq, d)
    ys = x_sample.reshape(dec_batch * dec_seq, d)
    ks, vs, shs, srs = [], [], [], []
    for l in range(depth):
        lam_init = 0.8 - 0.6 * math.exp(-0.3 * l)
        mod_ctx = mod_all[l, 0:1][:, None, :]
        mod_lat = mod_all[l, 1:1 + dec_batch][:, None, :]
        n1 = norm1[l][None]
        n2 = norm2[l][None]
        gq = jnp.tile(qk_norm[l, 0], H_A * 2)[None] * (DQK_A ** -0.5 * LOG2E)
        gk = jnp.tile(qk_norm[l, 1], H_A * 2)[None]
        dl = diff_lambda[l]
        sg = subln[l][None]
        consts = (lb2, jnp.tile(hgrn_onorm[l], H_B)[None], conv_w[l], conv_b[l][None],
                  _gate_weights(rg_w[l]), rg_b[l].reshape(1, 4 * W_C), rg_lambda[l], ones_b)

        q, k, v, rest, kt = _in_projection(yp, mod_ctx, batch * seq // TILE, n1, w_in_b, l, gq, gk,
                                           ones_b, keys_t=True)
        oa = _attention_ctx(q, k, v, seq, dl, sg, lam_init)
        ob, oc, st, hs = _recurrence_ctx(rest, l, depth, consts)
        yp = _out_mlp(yp, oa, ob, oc, mod_ctx, batch * seq // TILE, n2, w_out_b, w1_b, w2_b, l)
        ks.append(kt.reshape(H_A, 2, DQK_A, batch, seq))
        vs.append(v.reshape(batch, seq, H_A, DV_A))
        shs.append(st)
        srs.append(hs)

        q, k, v, rest = _in_projection(ys, mod_lat, dec_seq // TILE, n1, w_in_b, l, gq, gk, ones_b)
        oa = _attention_lat(q, k, v, dec_batch, ck, cv, l, rope_tabs, dl, sg, lam_init)
        s0 = _state_block_diag_t(state_hgrn[:, l])
        h0 = state_rglru[:, l][:, :, None, :]
        fwd = _recurrence_lat(rest, dec_batch, l, depth, 0, s0, h0, None, consts)
        ob, oc = _recurrence_lat(rest, dec_batch, l, depth, 1, s0, h0, fwd, consts)
        ys = _out_mlp(ys, oa, ob, oc, mod_lat, dec_seq // TILE, n2, w_out_b, w1_b, w2_b, l)

    return (yp.reshape(batch, seq, d), ys.reshape(dec_batch, dec_seq, d),
            jnp.transpose(jnp.stack(ks, axis=0), (4, 0, 5, 1, 2, 3)), jnp.stack(vs, axis=1),
            jnp.stack(shs, axis=1),
            jnp.stack(srs, axis=1))
```

```python
import functools
import math

import numpy as np
import jax
import jax.numpy as jnp
from jax import lax
from jax.experimental import pallas as pl
from jax.experimental.pallas import tpu as pltpu

F32 = jnp.float32
BF16 = jnp.bfloat16

GRID_W = 64
H_A, DQK_A, DV_A = 4, 64, 128
H_B, DK_B, DV_B = 4, 64, 64
W_B = H_B * DV_B
H_C, W_C = 4, 256
CONV_K = 4
RG_C = 8.0
ROPE_BASE = 10000.0
EPS = 1e-6
N_MOD = 6
D_FF_CHUNK = 1024

TILE = 256
HALF = TILE // 2
SUBLANES = 8
LANES = 128
CHAIN_LEN = TILE // SUBLANES
LOG2E = math.log2(math.e)
V7X_VMEM_BYTES = 64 * 1024 * 1024
VMEM_LIMIT = V7X_VMEM_BYTES - 8 * 1024 * 1024


def _dot(a, b):
    return jnp.dot(a, b, preferred_element_type=F32)


def _dot_nt(a, b):
    return lax.dot_general(a, b, (((1,), (1,)), ((), ())), preferred_element_type=F32)


def _dot_tn(a, b):
    return lax.dot_general(a, b, (((0,), (0,)), ((), ())), preferred_element_type=F32)


def _group_sum(x, ones_bd):
    hi = x.astype(BF16)
    lo = (x - hi.astype(F32)).astype(BF16)
    return _dot(hi, ones_bd) + _dot(lo, ones_bd)


def _rms_rows(x, gain):
    return x * lax.rsqrt(jnp.mean(x * x, axis=-1, keepdims=True) + EPS) * gain


def _sigmoid(x):
    return 0.5 * jnp.tanh(0.5 * x) + 0.5


def _silu(x):
    return x * _sigmoid(x)


def _softplus(x):
    return jnp.maximum(x, 0.0) + jnp.log1p(jnp.exp(-jnp.abs(x)))


def _log_sigmoid(x):
    return jnp.minimum(x, 0.0) - jnp.log1p(jnp.exp(-jnp.abs(x)))


def _gelu_tanh(x):
    return x * (0.5 * (1.0 + jnp.tanh(math.sqrt(2.0 / math.pi) * (x + 0.044715 * (x * x * x)))))


def _params(semantics, flags=None):
    return pltpu.CompilerParams(dimension_semantics=semantics, vmem_limit_bytes=VMEM_LIMIT,
                                flags=flags)


def _mod_kernel(c_ref, w_ref, b_ref, o_ref):
    c = c_ref[...]
    o_ref[0] = _dot(_silu(c).astype(BF16), w_ref[0].astype(BF16)) + b_ref[0]


def _modulation(c8, w_mod, b_mod):
    depth, d, n = w_mod.shape
    tn = d
    return pl.pallas_call(
        _mod_kernel,
        grid=(depth, n // tn),
        in_specs=[
            pl.BlockSpec((8, d), lambda l, j: (0, 0)),
            pl.BlockSpec((1, d, tn), lambda l, j: (l, 0, j)),
            pl.BlockSpec((1, 1, tn), lambda l, j: (l, 0, j)),
        ],
        out_specs=pl.BlockSpec((1, 8, tn), lambda l, j: (l, 0, j)),
        out_shape=jax.ShapeDtypeStruct((depth, 8, n), F32),
        compiler_params=_params(("arbitrary", "arbitrary")),
        name="mod",
    )(c8, w_mod, b_mod.reshape(depth, 1, n))


def _proj_kernel(x_ref, mod_ref, n1_ref, w_ref, gq_ref, gk_ref, ones_ref, q_ref, k_ref, v_ref, r_ref,
                 *kt_ref):
    x = x_ref[...]
    d = x.shape[-1]
    sh = mod_ref[0, :, 0:d]
    sc = mod_ref[0, :, d:2 * d]
    h = _rms_rows(x, n1_ref[...]) * (1.0 + sc) + sh
    p = _dot(h.astype(BF16), w_ref[...])
    ones_bd = ones_ref[...]
    wq = H_A * 2 * DQK_A

    def qk_norm(z, gain):
        z2 = z * z
        nb = ones_bd.shape[0]
        ms = jnp.concatenate([_group_sum(z2[:, c:c + nb], ones_bd) for c in range(0, wq, nb)],
                             axis=1) * (1.0 / DQK_A)
        return z * lax.rsqrt(ms + EPS) * gain

    wv = H_A * DV_A
    q_ref[...] = qk_norm(p[:, 0:wq], gq_ref[...])
    kn = qk_norm(p[:, wq:2 * wq], gk_ref[...])
    k_ref[...] = kn
    if kt_ref:
        row = pl.program_id(0) % SUBLANES
        kt_ref[0][:, pl.ds(row, 1), :] = kn.T[:, None, :]
    for h in range(H_A):
        v_ref[pl.ds(h, TILE, stride=H_A), :] = p[:, 2 * wq + h * DV_A:2 * wq + (h + 1) * DV_A]
    r_ref[...] = p[:, 2 * wq + wv:]


def _in_projection(x, mod, tiles_per_mod, n1, w_in, layer, gq, gk, ones_bd, keys_t=False):
    t, d = x.shape
    n_in = w_in.shape[2]
    wq = gq.shape[1]
    wv = H_A * DV_A
    widths = (wq, wq, wv, n_in - 2 * wq - wv)
    out_specs = [pl.BlockSpec((TILE, w), lambda i: (i, 0)) for w in widths]
    out_shape = [jax.ShapeDtypeStruct((t, w), F32) for w in widths]
    out_specs[2] = pl.BlockSpec((TILE * H_A, DV_A), lambda i: (i, 0))
    out_shape[2] = jax.ShapeDtypeStruct((t * H_A, DV_A), F32)
    if keys_t:
        out_specs.append(pl.BlockSpec((wq, SUBLANES, TILE), lambda i: (0, i // SUBLANES, 0)))
        out_shape.append(jax.ShapeDtypeStruct((wq, t // TILE, TILE), F32))
    return pl.pallas_call(
        _proj_kernel,
        grid=(t // TILE,),
        in_specs=[
            pl.BlockSpec((TILE, d), lambda i: (i, 0)),
            pl.BlockSpec((1, 1, N_MOD * d), lambda i: (i // tiles_per_mod, 0, 0)),
            pl.BlockSpec((1, d), lambda i: (0, 0)),
            pl.BlockSpec((None, d, n_in), lambda i: (layer, 0, 0)),
            pl.BlockSpec((1, wq), lambda i: (0, 0)),
            pl.BlockSpec((1, wq), lambda i: (0, 0)),
            pl.BlockSpec(ones_bd.shape, lambda i: (0, 0)),
        ],
        out_specs=out_specs,
        out_shape=out_shape,
        compiler_params=_params(("arbitrary",)),
        name="proj",
    )(x, mod, n1, w_in, gq, gk, ones_bd)


def _diff_lambda(dl_ref, lam_init):
    lv = dl_ref[...]
    a = jnp.sum(lv[0:1] * lv[1:2], axis=-1, keepdims=True)
    b = jnp.sum(lv[2:3] * lv[3:4], axis=-1, keepdims=True)
    return jnp.exp(a) - jnp.exp(b) + lam_init


def _diff_attention(q, k_of, vext_of, lam, sub_gain, lam_init, o_ref):
    hw = 2 * DQK_A
    first = lax.broadcasted_iota(jnp.int32, (1, hw), 1) < DQK_A
    for h in range(H_A):
        hs = slice(h * hw, (h + 1) * hw)
        qh = q[:, hs]
        kh = k_of(h)
        vext = vext_of(h)
        maps = []
        for qm in (jnp.where(first, qh, 0.0), jnp.where(first, 0.0, qh)):
            s = _dot_nt(qm.astype(BF16), kh)
            e = jnp.exp2(s - jnp.max(s, axis=-1, keepdims=True)).astype(BF16)
            oe = _dot(e, vext)
            maps.append(oe[:, 0:DV_A] / oe[:, DV_A:])
        oh = maps[0] - lam * maps[1]
        o_ref[:, hs] = _rms_rows(oh, sub_gain) * (1.0 - lam_init)


def _attn_ctx_kernel(q_ref, k_ref, v_ref, dl_ref, sg_ref, o_ref, *, lam_init):
    lam = _diff_lambda(dl_ref, lam_init)
    k = k_ref[...].astype(BF16)
    seq = k.shape[0]
    ones = jnp.ones((seq, DV_A), BF16)
    hw = 2 * DQK_A

    def vext(h):
        return jnp.concatenate([v_ref[pl.ds(h, seq, stride=H_A), :].astype(BF16), ones], axis=1)

    _diff_attention(q_ref[...], lambda h: k[:, h * hw:(h + 1) * hw], vext, lam, sg_ref[...],
                    lam_init, o_ref)


def _attention_ctx(q, k, v, seq, dl, sg, lam_init):
    t = q.shape[0]
    w = H_A * DV_A
    return pl.pallas_call(
        functools.partial(_attn_ctx_kernel, lam_init=lam_init),
        grid=(t // seq,),
        in_specs=[
            pl.BlockSpec((seq, w), lambda b: (b, 0)),
            pl.BlockSpec((seq, w), lambda b: (b, 0)),
            pl.BlockSpec((seq * H_A, DV_A), lambda b: (b, 0)),
            pl.BlockSpec(dl.shape, lambda b: (0, 0)),
            pl.BlockSpec(sg.shape, lambda b: (0, 0)),
        ],
        out_specs=pl.BlockSpec((seq, w), lambda b: (b, 0)),
        out_shape=jax.ShapeDtypeStruct((t, w), F32),
        compiler_params=_params(("arbitrary",)),
        name="attn_ctx",
    )(q, k, v, dl, sg)


def _rope(x, cos, sin_lo, sin_hi):
    return (x * cos + pltpu.roll(x, 2 * DQK_A - 16, 1) * sin_lo + pltpu.roll(x, 16, 1) * sin_hi)


def _attn_lat_kernel(q_ref, k_ref, v_ref, ck_ref, cv_ref, qcos_ref, qslo_ref, qshi_ref,
                     kcos_ref, kslo_ref, kshi_ref, dl_ref, sg_ref, o_ref, kall, vall, *, lam_init):
    hw = 2 * DQK_A
    n_new = k_ref.shape[0]

    @pl.when(pl.program_id(1) == 0)
    def _():
        kcos, kslo, kshi = kcos_ref[...], kslo_ref[...], kshi_ref[...]
        for h in range(H_A):
            hs = slice(h * hw, (h + 1) * hw)
            kall[0:n_new, hs] = _rope(k_ref[:, hs], kcos, kslo, kshi).astype(BF16)
        kall[n_new:, :] = ck_ref[...].astype(BF16)
        ones = jnp.ones((vall.shape[0], DV_A), BF16)
        for h in range(H_A):
            vs = slice(h * DV_A, (h + 1) * DV_A)
            vall[0:n_new, 2 * h * DV_A:(2 * h + 1) * DV_A] = (
                v_ref[pl.ds(h, n_new, stride=H_A), :].astype(BF16))
            vall[n_new:, 2 * h * DV_A:(2 * h + 1) * DV_A] = cv_ref[:, vs].astype(BF16)
            vall[:, (2 * h + 1) * DV_A:(2 * h + 2) * DV_A] = ones

    lam = _diff_lambda(dl_ref, lam_init)
    qcos, qslo, qshi = qcos_ref[...], qslo_ref[...], qshi_ref[...]
    q = jnp.concatenate(
        [_rope(q_ref[:, h * hw:(h + 1) * hw], qcos, qslo, qshi) for h in range(H_A)], axis=-1)
    _diff_attention(q, lambda h: kall[:, h * hw:(h + 1) * hw],
                    lambda h: vall[:, 2 * h * DV_A:(2 * h + 2) * DV_A], lam, sg_ref[...],
                    lam_init, o_ref)


def _attention_lat(q, k, v, n_batch, cache_k, cache_v, layer, rope_tabs, dl, sg, lam_init):
    t = q.shape[0]
    seq = t // n_batch
    past = cache_k.shape[2]
    w = H_A * DV_A
    hw = 2 * DQK_A
    nq = seq // TILE
    cos, slo, shi = rope_tabs
    tab_q = pl.BlockSpec((TILE, hw), lambda b, j: (j, 0))
    tab_k = pl.BlockSpec((seq, hw), lambda b, j: (0, 0))
    return pl.pallas_call(
        functools.partial(_attn_lat_kernel, lam_init=lam_init),
        grid=(n_batch, nq),
        in_specs=[
            pl.BlockSpec((TILE, w), lambda b, j: (b * nq + j, 0)),
            pl.BlockSpec((seq, w), lambda b, j: (b, 0)),
            pl.BlockSpec((seq * H_A, DV_A), lambda b, j: (b, 0)),
            pl.BlockSpec((None, None, past, w), lambda b, j: (b, layer, 0, 0)),
            pl.BlockSpec((None, None, past, w), lambda b, j: (b, layer, 0, 0)),
            tab_q, tab_q, tab_q, tab_k, tab_k, tab_k,
            pl.BlockSpec(dl.shape, lambda b, j: (0, 0)),
            pl.BlockSpec(sg.shape, lambda b, j: (0, 0)),
        ],
        out_specs=pl.BlockSpec((TILE, w), lambda b, j: (b * nq + j, 0)),
        out_shape=jax.ShapeDtypeStruct((t, w), F32),
        scratch_shapes=[pltpu.VMEM((seq + past, w), BF16), pltpu.VMEM((seq + past, 2 * w), BF16)],
        compiler_params=_params(("arbitrary", "arbitrary")),
        name="attn_lat",
    )(q, k, v, cache_k, cache_v, cos, slo, shi, cos, slo, shi, dl, sg)


def _tile_aux():
    rows = lax.broadcasted_iota(jnp.int32, (TILE, 1), 0)
    lane = lax.broadcasted_iota(jnp.int32, (1, W_B), 1)
    head_masks = [(lane // DK_B) == h for h in range(H_B)]
    first_of_pair = lax.broadcasted_iota(jnp.int32, (1, 2 * DK_B), 1) < DK_B
    xor = (lax.broadcasted_iota(jnp.int32, (HALF, HALF), 0)
           ^ lax.broadcasted_iota(jnp.int32, (HALF, HALF), 1))
    return rows, head_masks, first_of_pair, xor


def _hgrn_lower_bound(lb_ref, layer, direction, depth):
    xs = [lb_ref[2 * j + direction:2 * j + direction + 1, :] for j in range(depth)]
    m = functools.reduce(jnp.maximum, xs)
    es = [jnp.exp(x - m) for x in xs]
    return sum(es[1:layer + 1]) / sum(es)


def _hgrn_gates(fpre, lbd):
    t = jnp.exp(-jnp.abs(fpre))
    log_sig = jnp.minimum(fpre, 0.0) - jnp.log(1.0 + t)
    r = 1.0 / (1.0 + t)
    sig_neg = jnp.where(fpre >= 0.0, t * r, r)
    if lbd is None:
        return log_sig * LOG2E, sig_neg
    a = jnp.log(lbd)
    b = jnp.log1p(-lbd) + log_sig
    logf = jnp.maximum(a, b) + jnp.log1p(jnp.exp(-jnp.abs(a - b)))
    return logf * LOG2E, (1.0 - lbd) * sig_neg


def _chain_row(token):
    return SUBLANES * (token % CHAIN_LEN) + token // CHAIN_LEN


def _tile_prefix(g, g_scr, b_scr):
    g_c = _to_chains(g, g_scr)
    run = [g_c[0]]
    for j in range(1, CHAIN_LEN):
        run.append(run[-1] + g_c[j])
    sub = lax.broadcasted_iota(jnp.int32, (SUBLANES, 1), 0)
    incl = run[-1]
    k = 1
    while k < SUBLANES:
        incl = incl + jnp.where(sub >= k, pltpu.roll(incl, k, 0), 0.0)
        k *= 2
    offset = incl - run[-1]
    prefix = _from_chains([r + offset for r in run], b_scr)
    return prefix, incl[SUBLANES - 1:SUBLANES, :]


def _level_exponent(z, g, b_scr, m, reverse, rows):
    if m == 1:
        right = (rows & 1) != 0
        return jnp.where(right, g, 0.0) if not reverse else jnp.where(right, 0.0, g)
    slabs = b_scr.shape[0]

    def ref_rows(token):
        row = _chain_row(token)
        return jnp.concatenate(
            [jnp.broadcast_to(b_scr[s, row:row + 1, :], (SUBLANES, LANES)) for s in range(slabs)],
            axis=1)

    groups = []
    for gq in range(TILE // SUBLANES):
        first = gq * SUBLANES
        zg = z[first:first + SUBLANES]
        if 2 * m >= SUBLANES:
            block = first // (2 * m) * (2 * m)
            ref = ref_rows(block + m - 1)
            if m >= SUBLANES:
                groups.append(zg - ref if first - block >= m else ref - zg)
                continue
        else:
            sub = lax.broadcasted_iota(jnp.int32, (SUBLANES, 1), 0)
            ref = jnp.where(sub < 2 * m, ref_rows(first + m - 1), ref_rows(first + 3 * m - 1))
        groups.append(-jnp.abs(zg - ref))
    return jnp.concatenate(groups, axis=0)


def _hgrn_tile(qh, kk, vv, g, reverse, aux, ones_bd, state_t, scratch):
    rows, head_masks, first_of_pair, xor = aux
    pair_w = 2 * DK_B
    g_scr, b_scr = scratch
    prefix, total = _tile_prefix(g, g_scr, b_scr)
    z = prefix if not reverse else prefix - g
    diag = [[None, None] for _ in range(H_B)]
    off = [None] * H_B
    m = 1
    while m < TILE:
        right = (rows & m) != 0
        x = jnp.exp2(_level_exponent(z, g, b_scr, m, reverse, rows))
        q_rows = right if not reverse else jnp.logical_not(right)
        qz = jnp.where(q_rows, (qh * x).astype(BF16), 0.0)
        kz = jnp.where(q_rows, 0.0, (kk * x).astype(BF16))
        for h in range(H_B):
            ls = slice((h // 2) * pair_w, (h // 2 + 1) * pair_w)
            qm = jnp.where(first_of_pair, qz[:, ls], 0.0) if h % 2 == 0 else \
                jnp.where(first_of_pair, 0.0, qz[:, ls])
            kh = kz[:, ls]
            if m < HALF:
                for i in range(2):
                    rs = slice(i * HALF, (i + 1) * HALF)
                    pm = _dot_nt(qm[rs], kh[rs])
                    if 2 * m < HALF:
                        pm = jnp.where(xor < 2 * m, pm, 0.0)
                    diag[h][i] = pm if diag[h][i] is None else diag[h][i] + pm
            else:
                lo, hi = slice(0, HALF), slice(HALF, TILE)
                off[h] = _dot_nt(qm[hi], kh[lo]) if not reverse else _dot_nt(qm[lo], kh[hi])
        m *= 2
    vb = vv.astype(BF16)
    top = bot = None
    for h in range(H_B):
        vh = jnp.where(head_masks[h], vb, 0.0)
        d0, d1, of = (a.astype(BF16) for a in (diag[h][0], diag[h][1], off[h]))
        if not reverse:
            t = _dot(d0, vh[0:HALF])
            b = _dot(jnp.concatenate([of, d1], axis=1), vh)
        else:
            t = _dot(jnp.concatenate([d0, of], axis=1), vh)
            b = _dot(d1, vh[HALF:])
        top = t if top is None else top + t
        bot = b if bot is None else bot + b
    out = _group_sum(qh * kk, ones_bd) * vv + jnp.concatenate([top, bot], axis=0)
    q_decay, k_decay = (z, total - z) if not reverse else (total - z, z)
    if state_t is not None:
        out = out + _dot_nt((qh * jnp.exp2(q_decay)).astype(BF16), state_t.astype(BF16))
    k_hat = (kk * jnp.exp2(k_decay)).astype(BF16)
    return out, k_hat, total


def _block_diag_mask():
    r = lax.broadcasted_iota(jnp.int32, (W_B, W_B), 0) // DK_B
    c = lax.broadcasted_iota(jnp.int32, (W_B, W_B), 1) // DV_B
    return r == c


def _conv(x, prev_row, next_rows, cw_ref, cb_ref, rows):
    x_m1 = jnp.where(rows == 0, prev_row, pltpu.roll(x, 1, 0))
    x_p1 = jnp.where(rows == TILE - 1, next_rows[0:1], pltpu.roll(x, TILE - 1, 0))
    x_p2 = jnp.where(rows == TILE - 2, next_rows[0:1],
                     jnp.where(rows == TILE - 1, next_rows[1:2], pltpu.roll(x, TILE - 2, 0)))
    y = cb_ref[...] + x_m1 * cw_ref[0:1, :]
    y = y + x * cw_ref[1:2, :]
    y = y + x_p1 * cw_ref[2:3, :]
    return y + x_p2 * cw_ref[3:4, :]


def _chain_address(g):
    per_chain = CHAIN_LEN // SUBLANES
    return SUBLANES * SUBLANES * (g % per_chain) + g // per_chain, SUBLANES


def _to_chains(x, scr):
    slabs = scr.shape[0]
    for s in range(slabs):
        for g in range(TILE // SUBLANES):
            start, stride = _chain_address(g)
            scr[s, pl.ds(start, SUBLANES, stride=stride), :] = (
                x[g * SUBLANES:(g + 1) * SUBLANES, s * LANES:(s + 1) * LANES])
    return [jnp.concatenate([scr[s, j * SUBLANES:(j + 1) * SUBLANES, :] for s in range(slabs)],
                            axis=1) for j in range(CHAIN_LEN)]


def _from_chains(chains, scr):
    slabs = scr.shape[0]
    for s in range(slabs):
        for j, c in enumerate(chains):
            scr[s, j * SUBLANES:(j + 1) * SUBLANES, :] = c[:, s * LANES:(s + 1) * LANES]
    groups = []
    for g in range(TILE // SUBLANES):
        start, stride = _chain_address(g)
        groups.append(jnp.concatenate(
            [scr[s, pl.ds(start, SUBLANES, stride=stride), :] for s in range(slabs)], axis=1))
    return jnp.concatenate(groups, axis=0)


def _rglru_tile(xc, r_pre, i_pre, lam_row, reverse, h_in, scratch):
    a_scr, u_scr, h_scr = scratch
    r = _sigmoid(r_pre)
    i = _sigmoid(i_pre)
    log_a = -RG_C * r * _softplus(-lam_row)
    a = jnp.exp(log_a)
    u = jnp.sqrt(jnp.tanh(-log_a) * (a * a + 1.0)) * (i * xc)
    a_c = _to_chains(a, a_scr)
    u_c = _to_chains(u, u_scr)
    order = list(reversed(range(CHAIN_LEN))) if reverse else list(range(CHAIN_LEN))
    h_c = [None] * CHAIN_LEN
    p_c = [None] * CHAIN_LEN
    h_c[order[0]], p_c[order[0]] = u_c[order[0]], a_c[order[0]]
    for prev, j in zip(order[:-1], order[1:]):
        h_c[j] = a_c[j] * h_c[prev] + u_c[j]
        p_c[j] = a_c[j] * p_c[prev]
    p_tot, h_tot = p_c[order[-1]], h_c[order[-1]]
    sub = lax.broadcasted_iota(jnp.int32, (SUBLANES, 1), 0)
    k = 1
    while k < SUBLANES:
        if not reverse:
            valid = sub >= k
            p_s, h_s = pltpu.roll(p_tot, k, 0), pltpu.roll(h_tot, k, 0)
        else:
            valid = sub < SUBLANES - k
            p_s, h_s = pltpu.roll(p_tot, SUBLANES - k, 0), pltpu.roll(h_tot, SUBLANES - k, 0)
        h_tot = h_tot + p_tot * jnp.where(valid, h_s, 0.0)
        p_tot = p_tot * jnp.where(valid, p_s, 1.0)
        k *= 2
    if h_in is not None:
        h_tot = h_tot + p_tot * h_in
    if not reverse:
        carry = jnp.where(sub == 0, 0.0 if h_in is None else h_in, pltpu.roll(h_tot, 1, 0))
        h_out = h_tot[SUBLANES - 1:SUBLANES, :]
    else:
        carry = jnp.where(sub == SUBLANES - 1, 0.0 if h_in is None else h_in,
                          pltpu.roll(h_tot, SUBLANES - 1, 0))
        h_out = h_tot[0:1, :]
    h_c = [h + p * carry for h, p in zip(h_c, p_c)]
    return _from_chains(h_c, h_scr), h_out


def _rec_inputs(refs):
    qb, ff, fb, ib, gb, xcol, gcol = [r[...] for r in refs]
    return _silu(qb), ff, fb, ib, gb, xcol, gcol


def _hgrn_out(o_sum, gb, onorm_gain, ones_bd):
    ms = _group_sum(o_sum * o_sum, ones_bd) * (1.0 / DV_B)
    return o_sum * lax.rsqrt(ms + EPS) * onorm_gain * _silu(gb)


def _rec_ctx_kernel(q_ref, ff_ref, fb_ref, i_ref, g_ref, x_ref, gc_ref, lb_ref, on_ref, cw_ref,
                    cb_ref, wg_ref, bg_ref, lam_ref, ones_ref, ob_ref, oc_ref, st_ref,
                    hs_ref, *scan_scratch, layer, depth):
    aux = _tile_aux()
    rows = aux[0]
    ones_bd = ones_ref[...]
    qh, ff, fb, ib, gb, xcol, gcol = _rec_inputs((q_ref, ff_ref, fb_ref, i_ref, g_ref, x_ref, gc_ref))
    o_sum = None
    for direction, fpre in enumerate((ff, fb)):
        lbd = None if layer == 0 else _hgrn_lower_bound(lb_ref, layer, direction, depth)
        g, kk = _hgrn_gates(fpre, lbd)
        o, k_hat, _ = _hgrn_tile(qh, kk, ib, g, direction == 1, aux, ones_bd, None,
                                 scan_scratch[3 + 2 * direction:5 + 2 * direction])
        o_sum = o if o_sum is None else o_sum + o
        full = _dot_tn(k_hat, ib.astype(BF16))
        for h in range(H_B):
            blk = full[h * DK_B:(h + 1) * DK_B, (h // 2) * 2 * DV_B:(h // 2 + 1) * 2 * DV_B]
            if h % 2:
                blk = pltpu.roll(blk, DV_B, 1)
            st_ref[0, direction, h] = blk[:, 0:DV_B]
    ob_ref[...] = _hgrn_out(o_sum, gb, on_ref[...], ones_bd)

    zero_row = jnp.zeros((1, W_C), F32)
    xc = _conv(xcol, zero_row, jnp.zeros((2, W_C), F32), cw_ref, cb_ref, rows)
    gates = _dot(xc.astype(BF16), wg_ref[...]) + bg_ref[...]
    y_f, h_f = _rglru_tile(xc, gates[:, 0:W_C], gates[:, W_C:2 * W_C], lam_ref[0:1, :], False,
                           None, scan_scratch[0:3])
    y_b, h_b = _rglru_tile(xc, gates[:, 2 * W_C:3 * W_C], gates[:, 3 * W_C:], lam_ref[1:2, :], True,
                           None, scan_scratch[0:3])
    oc_ref[...] = (y_f + y_b) * _gelu_tanh(gcol)
    hs_ref[0, 0:1, :] = h_f
    hs_ref[0, 1:2, :] = h_b


def _scan_scratch(hgrn_directions):
    return [pltpu.VMEM((W_C // LANES, TILE, LANES), F32) for _ in range(3 + 2 * hgrn_directions)]


def _col_spec(col, row_map):
    return pl.BlockSpec((TILE, W_B), lambda *idx: (row_map(*idx), col))


COL_QB, COL_FF, COL_FB, COL_IB, COL_GB, COL_XC, COL_GC = range(7)
_REC_COLS = (COL_QB, COL_FF, COL_FB, COL_IB, COL_GB, COL_XC, COL_GC)


def _full_spec(a):
    nd = a.ndim
    return pl.BlockSpec(a.shape, lambda *idx: (0,) * nd)


def _recurrence_ctx(proj, layer, depth, consts):
    t = proj.shape[0]
    n = t // TILE
    return pl.pallas_call(
        functools.partial(_rec_ctx_kernel, layer=layer, depth=depth),
        grid=(n,),
        in_specs=[_col_spec(c, lambda b: b) for c in _REC_COLS] + [_full_spec(a) for a in consts],
        out_specs=[
            pl.BlockSpec((TILE, W_B), lambda b: (b, 0)),
            pl.BlockSpec((TILE, W_C), lambda b: (b, 0)),
            pl.BlockSpec((1, 2, H_B, DK_B, DV_B), lambda b: (b, 0, 0, 0, 0)),
            pl.BlockSpec((1, 2, W_C), lambda b: (b, 0, 0)),
        ],
        out_shape=[
            jax.ShapeDtypeStruct((t, W_B), F32),
            jax.ShapeDtypeStruct((t, W_C), F32),
            jax.ShapeDtypeStruct((n, 2, H_B, DK_B, DV_B), F32),
            jax.ShapeDtypeStruct((n, 2, W_C), F32),
        ],
        scratch_shapes=_scan_scratch(2),
        compiler_params=_params(("arbitrary",)),
        name="rec_ctx",
    )(*([proj] * len(_REC_COLS)), *consts)


def _rec_lat_kernel(q_ref, f_ref, i_ref, g_ref, x_ref, xp_ref, xn_ref, gc_ref, s0_ref, h0_ref,
                    *rest, layer, depth, direction, n_tiles):
    if direction == 0:
        (lb_ref, on_ref, cw_ref, cb_ref, wg_ref, bg_ref, lam_ref, ones_ref,
         o_ref, y_ref, state, hstate, *scan_scratch) = rest
    else:
        (of_ref, yf_ref, lb_ref, on_ref, cw_ref, cb_ref, wg_ref, bg_ref, lam_ref, ones_ref,
         o_ref, y_ref, state, hstate, *scan_scratch) = rest
    step = pl.program_id(1)
    tile = step if direction == 0 else n_tiles - 1 - step
    reverse = direction == 1

    @pl.when(step == 0)
    def _():
        state[...] = s0_ref[0, 0]
        hstate[...] = h0_ref[0, 0]

    aux = _tile_aux()
    rows = aux[0]
    ones_bd = ones_ref[...]
    qh = _silu(q_ref[...])
    ib = i_ref[...]
    lbd = None if layer == 0 else _hgrn_lower_bound(lb_ref, layer, direction, depth)
    g, kk = _hgrn_gates(f_ref[...], lbd)
    o, k_hat, total = _hgrn_tile(qh, kk, ib, g, reverse, aux, ones_bd, state[...],
                                 scan_scratch[3:5])
    upd = jnp.where(_block_diag_mask(), _dot_tn(ib.astype(BF16), k_hat), 0.0)
    state[...] = state[...] * jnp.exp2(total) + upd

    prev_row = jnp.where(tile == 0, 0.0, xp_ref[TILE - 1:TILE, :])
    next_rows = jnp.where(tile == n_tiles - 1, 0.0, xn_ref[0:2, :])
    xc = _conv(x_ref[...], prev_row, next_rows, cw_ref, cb_ref, rows)
    gates = _dot(xc.astype(BF16), wg_ref[...]) + bg_ref[...]
    c0 = 2 * W_C * direction
    y, h_next = _rglru_tile(xc, gates[:, c0:c0 + W_C], gates[:, c0 + W_C:c0 + 2 * W_C],
                            lam_ref[direction:direction + 1, :], reverse, hstate[...],
                            scan_scratch[0:3])
    hstate[...] = h_next

    if direction == 0:
        o_ref[...] = o
        y_ref[...] = y
    else:
        o_ref[...] = _hgrn_out(of_ref[...] + o, g_ref[...], on_ref[...], ones_bd)
        y_ref[...] = (yf_ref[...] + y) * _gelu_tanh(gc_ref[...])


def _recurrence_lat(proj, n_batch, layer, depth, direction, state0, h0, prev, consts):
    t = proj.shape[0]
    n_tiles = t // (TILE * n_batch)

    def tile_of(b, s):
        return s if direction == 0 else n_tiles - 1 - s

    def row(b, s):
        return b * n_tiles + tile_of(b, s)

    def row_prev(b, s):
        return b * n_tiles + jnp.maximum(tile_of(b, s) - 1, 0)

    def row_next(b, s):
        return b * n_tiles + jnp.minimum(tile_of(b, s) + 1, n_tiles - 1)

    f_col = COL_FF if direction == 0 else COL_FB
    in_specs = [_col_spec(COL_QB, row), _col_spec(f_col, row), _col_spec(COL_IB, row),
                _col_spec(COL_GB, row), _col_spec(COL_XC, row), _col_spec(COL_XC, row_prev),
                _col_spec(COL_XC, row_next), _col_spec(COL_GC, row),
                pl.BlockSpec((1, 1, W_B, W_B), lambda b, s: (b, direction, 0, 0)),
                pl.BlockSpec((1, 1, 1, W_C), lambda b, s: (b, direction, 0, 0))]
    args = [proj] * 8 + [state0, h0]
    if direction == 1:
        in_specs += [pl.BlockSpec((TILE, W_B), lambda b, s: (row(b, s), 0)),
                     pl.BlockSpec((TILE, W_C), lambda b, s: (row(b, s), 0))]
        args += list(prev)
    in_specs += [_full_spec(a) for a in consts]
    args += list(consts)
    return pl.pallas_call(
        functools.partial(_rec_lat_kernel, layer=layer, depth=depth, direction=direction,
                          n_tiles=n_tiles),
        grid=(n_batch, n_tiles),
        in_specs=in_specs,
        out_specs=[pl.BlockSpec((TILE, W_B), lambda b, s: (row(b, s), 0)),
                   pl.BlockSpec((TILE, W_C), lambda b, s: (row(b, s), 0))],
        out_shape=[jax.ShapeDtypeStruct((t, W_B), F32), jax.ShapeDtypeStruct((t, W_C), F32)],
        scratch_shapes=[pltpu.VMEM((W_B, W_B), F32), pltpu.VMEM((1, W_C), F32)] + _scan_scratch(1),
        compiler_params=_params(("arbitrary", "arbitrary")),
        name="rec_lat_fwd" if direction == 0 else "rec_lat_bwd",
    )(*args)


def _out_kernel(x_ref, oa_ref, ob_ref, oc_ref, mod_ref, n2_ref, wo_ref, w1_ref, w2_ref, y_ref):
    x = x_ref[...]
    d = x.shape[-1]
    g1 = mod_ref[0, :, 2 * d:3 * d]
    sh2 = mod_ref[0, :, 3 * d:4 * d]
    sc2 = mod_ref[0, :, 4 * d:5 * d]
    g2 = mod_ref[0, :, 5 * d:6 * d]
    wa = H_A * DV_A
    mix = _dot(oa_ref[...].astype(BF16), wo_ref[0:wa, :])
    mix = mix + _dot(ob_ref[...].astype(BF16), wo_ref[wa:wa + W_B, :])
    mix = mix + _dot(oc_ref[...].astype(BF16), wo_ref[wa + W_B:, :])
    x1 = x + g1 * mix
    h = (_rms_rows(x1, n2_ref[...]) * (1.0 + sc2) + sh2).astype(BF16)
    acc = jnp.zeros_like(x)
    for c in range(w1_ref.shape[1] // D_FF_CHUNK):
        cs = slice(c * D_FF_CHUNK, (c + 1) * D_FF_CHUNK)
        f = jnp.maximum(_dot(h, w1_ref[:, cs]), 0.0)
        acc = acc + _dot((f * f).astype(BF16), w2_ref[cs, :])
    y_ref[...] = x1 + g2 * acc


def _out_mlp(x, oa, ob, oc, mod, tiles_per_mod, n2, w_out, w_ff1, w_ff2, layer):
    t, d = x.shape

    def resident(w):
        return pl.BlockSpec((None,) + w.shape[1:], lambda i: (layer, 0, 0),
                            pipeline_mode=pl.Buffered(1))

    return pl.pallas_call(
        _out_kernel,
        grid=(t // TILE,),
        in_specs=[
            pl.BlockSpec((TILE, d), lambda i: (i, 0)),
            pl.BlockSpec((TILE, oa.shape[1]), lambda i: (i, 0)),
            pl.BlockSpec((TILE, ob.shape[1]), lambda i: (i, 0)),
            pl.BlockSpec((TILE, oc.shape[1]), lambda i: (i, 0)),
            pl.BlockSpec((1, 1, N_MOD * d), lambda i: (i // tiles_per_mod, 0, 0)),
            pl.BlockSpec((1, d), lambda i: (0, 0)),
            resident(w_out),
            resident(w_ff1),
            resident(w_ff2),
        ],
        out_specs=pl.BlockSpec((TILE, d), lambda i: (i, 0)),
        out_shape=jax.ShapeDtypeStruct((t, d), F32),
        compiler_params=_params(("arbitrary",)),
        name="out_mlp",
    )(x, oa, ob, oc, mod, n2, w_out, w_ff1, w_ff2)


def _ones_block_diag(n, group):
    idx = np.arange(n) // group
    return jnp.asarray((idx[:, None] == idx[None, :]).astype(np.float32), dtype=BF16)


def _rope_tables(seq):
    nf = DQK_A // 4
    inv = ROPE_BASE ** (-jnp.arange(nf, dtype=F32) / nf)
    pos = jnp.arange(seq)
    ang_r = (pos // GRID_W).astype(F32)[:, None] * inv
    ang_c = (pos % GRID_W).astype(F32)[:, None] * inv
    z = jnp.zeros_like(ang_r)
    cos = jnp.concatenate([jnp.cos(ang_r)] * 2 + [jnp.cos(ang_c)] * 2, axis=-1)
    sin_lo = jnp.concatenate([-jnp.sin(ang_r), z, -jnp.sin(ang_c), z], axis=-1)
    sin_hi = jnp.concatenate([z, jnp.sin(ang_r), z, jnp.sin(ang_c)], axis=-1)
    return tuple(jnp.tile(a, (1, 2)) for a in (cos, sin_lo, sin_hi))


def _gate_weights(rg_w_l):
    eye = jnp.eye(H_C, dtype=rg_w_l.dtype)
    w = jnp.einsum('dghij,hk->dghikj', rg_w_l, eye)
    w = w.reshape(2, 2, W_C, W_C)
    return jnp.transpose(w, (2, 0, 1, 3)).reshape(W_C, 4 * W_C).astype(BF16)


def _state_block_diag_t(s):
    eye = jnp.eye(H_B, dtype=s.dtype)
    w = jnp.einsum('...hdv,hk->...hvkd', s, eye)
    return w.reshape(s.shape[:-3] + (W_B, W_B))


def kernel(x_prompt, x_sample, cache_k, cache_v, state_hgrn, state_rglru, c, c_ctx, w_mod, b_mod,
           norm1, norm2, w_in, w_out, qk_norm, diff_lambda, subln, hgrn_lb, hgrn_onorm, conv_w,
           conv_b, rg_w, rg_b, rg_lambda, w_ff1, w_ff2):
    batch, seq, d = x_prompt.shape
    dec_batch, dec_seq, _ = x_sample.shape
    depth = w_in.shape[0]
    past = cache_k.shape[2]
    wq = H_A * 2 * DQK_A

    c8 = jnp.concatenate([c_ctx[None], c, jnp.zeros((8 - 1 - dec_batch, d), F32)], axis=0)
    mod_all = _modulation(c8, w_mod, b_mod)

    assert DQK_A == DV_B and seq == TILE
    ones_b = _ones_block_diag(W_B, DV_B)
    rope_tabs = _rope_tables(dec_seq)
    ck = cache_k.reshape(dec_batch, depth, past, wq)
    cv = cache_v.reshape(dec_batch, depth, past, H_A * DV_A)
    lb2 = hgrn_lb.reshape(depth * 2, W_B)

    w_in_b, w_out_b, w1_b, w2_b = (w.astype(BF16) for w in (w_in, w_out, w_ff1, w_ff2))

    yp = x_prompt.reshape(batch * seq, d)
    ys = x_sample.reshape(dec_batch * dec_seq, d)
    ks, vs, shs, srs = [], [], [], []
    for l in range(depth):
        lam_init = 0.8 - 0.6 * math.exp(-0.3 * l)
        mod_ctx = mod_all[l, 0:1][:, None, :]
        mod_lat = mod_all[l, 1:1 + dec_batch][:, None, :]
        n1 = norm1[l][None]
        n2 = norm2[l][None]
        gq = jnp.tile(qk_norm[l, 0], H_A * 2)[None] * (DQK_A ** -0.5 * LOG2E)
        gk = jnp.tile(qk_norm[l, 1], H_A * 2)[None]
        dl = diff_lambda[l]
        sg = subln[l][None]
        consts = (lb2, jnp.tile(hgrn_onorm[l], H_B)[None], conv_w[l], conv_b[l][None],
                  _gate_weights(rg_w[l]), rg_b[l].reshape(1, 4 * W_C), rg_lambda[l], ones_b)

        q, k, v, rest, kt = _in_projection(yp, mod_ctx, batch * seq // TILE, n1, w_in_b, l, gq, gk,
                                           ones_b, keys_t=True)
        oa = _attention_ctx(q, k, v, seq, dl, sg, lam_init)
        ob, oc, st, hs = _recurrence_ctx(rest, l, depth, consts)
        yp = _out_mlp(yp, oa, ob, oc, mod_ctx, batch * seq // TILE, n2, w_out_b, w1_b, w2_b, l)
        ks.append(kt.reshape(H_A, 2, DQK_A, batch, seq))
        vs.append(v.reshape(batch, seq, H_A, DV_A))
        shs.append(st)
        srs.append(hs)

        q, k, v, rest = _in_projection(ys, mod_lat, dec_seq // TILE, n1, w_in_b, l, gq, gk, ones_b)
        oa = _attention_lat(q, k, v, dec_batch, ck, cv, l, rope_tabs, dl, sg, lam_init)
        s0 = _state_block_diag_t(state_hgrn[:, l])
        h0 = state_rglru[:, l][:, :, None, :]
        fwd = _recurrence_lat(rest, dec_batch, l, depth, 0, s0, h0, None, consts)
        ob, oc = _recurrence_lat(rest, dec_batch, l, depth, 1, s0, h0, fwd, consts)
        ys = _out_mlp(ys, oa, ob, oc, mod_lat, dec_seq // TILE, n2, w_out_b, w1_b, w2_b, l)

    return (yp.reshape(batch, seq, d), ys.reshape(dec_batch, dec_seq, d),
            jnp.transpose(jnp.stack(ks, axis=0), (4, 0, 5, 1, 2, 3)), jnp.stack(vs, axis=1),
            jnp.stack(shs, axis=1),
            jnp.stack(srs, axis=1))
```

```python
import functools
import math

import numpy as np
import jax
import jax.numpy as jnp
from jax import lax
from jax.experimental import pallas as pl
from jax.experimental.pallas import tpu as pltpu

F32 = jnp.float32
BF16 = jnp.bfloat16

GRID_W = 64
H_A, DQK_A, DV_A = 4, 64, 128
H_B, DK_B, DV_B = 4, 64, 64
W_B = H_B * DV_B
H_C, W_C = 4, 256
CONV_K = 4
RG_C = 8.0
ROPE_BASE = 10000.0
EPS = 1e-6
N_MOD = 6
D_FF_CHUNK = 1024

TILE = 256
HALF = TILE // 2
SUBLANES = 8
LANES = 128
CHAIN_LEN = TILE // SUBLANES
LOG2E = math.log2(math.e)
V7X_VMEM_BYTES = 64 * 1024 * 1024
VMEM_LIMIT = V7X_VMEM_BYTES - 8 * 1024 * 1024


def _dot(a, b):
    return jnp.dot(a, b, preferred_element_type=F32)


def _dot_nt(a, b):
    return lax.dot_general(a, b, (((1,), (1,)), ((), ())), preferred_element_type=F32)


def _dot_tn(a, b):
    return lax.dot_general(a, b, (((0,), (0,)), ((), ())), preferred_element_type=F32)


def _group_sum(x, ones_bd):
    hi = x.astype(BF16)
    lo = (x - hi.astype(F32)).astype(BF16)
    return _dot(hi, ones_bd) + _dot(lo, ones_bd)


def _rms_rows(x, gain):
    return x * lax.rsqrt(jnp.mean(x * x, axis=-1, keepdims=True) + EPS) * gain


def _sigmoid(x):
    return 0.5 * jnp.tanh(0.5 * x) + 0.5


def _silu(x):
    return x * _sigmoid(x)


def _softplus(x):
    return jnp.maximum(x, 0.0) + jnp.log1p(jnp.exp(-jnp.abs(x)))


def _log_sigmoid(x):
    return jnp.minimum(x, 0.0) - jnp.log1p(jnp.exp(-jnp.abs(x)))


def _gelu_tanh(x):
    return x * (0.5 * (1.0 + jnp.tanh(math.sqrt(2.0 / math.pi) * (x + 0.044715 * (x * x * x)))))


def _params(semantics, flags=None):
    return pltpu.CompilerParams(dimension_semantics=semantics, vmem_limit_bytes=VMEM_LIMIT,
                                flags=flags)


def _mod_kernel(c_ref, w_ref, b_ref, o_ref):
    c = c_ref[...]
    o_ref[0] = _dot(_silu(c).astype(BF16), w_ref[0].astype(BF16)) + b_ref[0]


def _modulation(c8, w_mod, b_mod):
    depth, d, n = w_mod.shape
    tn = d
    return pl.pallas_call(
        _mod_kernel,
        grid=(depth, n // tn),
        in_specs=[
            pl.BlockSpec((8, d), lambda l, j: (0, 0)),
            pl.BlockSpec((1, d, tn), lambda l, j: (l, 0, j)),
            pl.BlockSpec((1, 1, tn), lambda l, j: (l, 0, j)),
        ],
        out_specs=pl.BlockSpec((1, 8, tn), lambda l, j: (l, 0, j)),
        out_shape=jax.ShapeDtypeStruct((depth, 8, n), F32),
        compiler_params=_params(("arbitrary", "arbitrary")),
        name="mod",
    )(c8, w_mod, b_mod.reshape(depth, 1, n))


def _proj_kernel(x_ref, mod_ref, n1_ref, w_ref, gq_ref, gk_ref, ones_ref, *refs, layers_before):
    prev = refs[:2] if layers_before else ()
    q_ref, k_ref, v_ref, r_ref, *kt_ref = refs[len(prev):]
    x = x_ref[...]
    d = x.shape[-1]
    sh = mod_ref[0, :, 0:d]
    sc = mod_ref[0, :, d:2 * d]
    h = _rms_rows(x, n1_ref[...]) * (1.0 + sc) + sh
    p = _dot(h.astype(BF16), w_ref[...])
    ones_bd = ones_ref[...]
    wq = H_A * 2 * DQK_A

    def qk_norm(z, gain):
        z2 = z * z
        nb = ones_bd.shape[0]
        ms = jnp.concatenate([_group_sum(z2[:, c:c + nb], ones_bd) for c in range(0, wq, nb)],
                             axis=1) * (1.0 / DQK_A)
        return z * lax.rsqrt(ms + EPS) * gain

    wv = H_A * DV_A
    q_ref[...] = qk_norm(p[:, 0:wq], gq_ref[...])
    kn = qk_norm(p[:, wq:2 * wq], gk_ref[...])
    k_ref[...] = kn
    v_dst = v_ref
    if kt_ref:
        row = pl.program_id(0) % SUBLANES
        kt_ref[0][layers_before, :, pl.ds(row, 1), :] = kn.T[:, None, :]
        v_dst = v_ref.at[0, layers_before]
        if prev:
            ktp_ref, vp_ref = prev
            v_ref[0, 0:layers_before] = vp_ref[0]

            @pl.when(row == 0)
            def _():
                kt_ref[0][0:layers_before] = ktp_ref[...]
    for h in range(H_A):
        v_dst[pl.ds(h, TILE, stride=H_A), :] = p[:, 2 * wq + h * DV_A:2 * wq + (h + 1) * DV_A]
    r_ref[...] = p[:, 2 * wq + wv:]


def _in_projection(x, mod, tiles_per_mod, n1, w_in, layer, gq, gk, ones_bd, caches=None):
    t, d = x.shape
    layers_before = caches[0].shape[0] if caches else 0
    n_in = w_in.shape[2]
    wq = gq.shape[1]
    wv = H_A * DV_A
    widths = (wq, wq, wv, n_in - 2 * wq - wv)
    out_specs = [pl.BlockSpec((TILE, w), lambda i: (i, 0)) for w in widths]
    out_shape = [jax.ShapeDtypeStruct((t, w), F32) for w in widths]
    out_specs[2] = pl.BlockSpec((TILE * H_A, DV_A), lambda i: (i, 0))
    out_shape[2] = jax.ShapeDtypeStruct((t * H_A, DV_A), F32)
    cache_specs = []
    if caches is not None:
        n_seq, nl = t // TILE, layers_before + 1
        assert n_seq % SUBLANES == 0
        out_specs[2] = pl.BlockSpec((1, nl, TILE * H_A, DV_A), lambda i: (i, 0, 0, 0))
        out_shape[2] = jax.ShapeDtypeStruct((n_seq, nl, TILE * H_A, DV_A), F32)
        out_specs.append(pl.BlockSpec((nl, wq, SUBLANES, TILE), lambda i: (0, 0, i // SUBLANES, 0)))
        out_shape.append(jax.ShapeDtypeStruct((nl, wq, n_seq, TILE), F32))
        if caches:
            cache_specs = [
                pl.BlockSpec((layers_before, wq, SUBLANES, TILE), lambda i: (0, 0, i // SUBLANES, 0)),
                pl.BlockSpec((1, layers_before, TILE * H_A, DV_A), lambda i: (i, 0, 0, 0))]
    return pl.pallas_call(
        functools.partial(_proj_kernel, layers_before=layers_before),
        grid=(t // TILE,),
        in_specs=[
            pl.BlockSpec((TILE, d), lambda i: (i, 0)),
            pl.BlockSpec((1, 1, N_MOD * d), lambda i: (i // tiles_per_mod, 0, 0)),
            pl.BlockSpec((1, d), lambda i: (0, 0)),
            pl.BlockSpec((None, d, n_in), lambda i: (layer, 0, 0)),
            pl.BlockSpec((1, wq), lambda i: (0, 0)),
            pl.BlockSpec((1, wq), lambda i: (0, 0)),
            pl.BlockSpec(ones_bd.shape, lambda i: (0, 0)),
        ] + cache_specs,
        out_specs=out_specs,
        out_shape=out_shape,
        compiler_params=_params(("arbitrary",)),
        name="proj",
    )(x, mod, n1, w_in, gq, gk, ones_bd, *(caches or ()))


def _diff_lambda(dl_ref, lam_init):
    lv = dl_ref[...]
    a = jnp.sum(lv[0:1] * lv[1:2], axis=-1, keepdims=True)
    b = jnp.sum(lv[2:3] * lv[3:4], axis=-1, keepdims=True)
    return jnp.exp(a) - jnp.exp(b) + lam_init


def _diff_attention(q, k_of, vext_of, lam, sub_gain, lam_init, o_ref):
    hw = 2 * DQK_A
    first = lax.broadcasted_iota(jnp.int32, (1, hw), 1) < DQK_A
    for h in range(H_A):
        hs = slice(h * hw, (h + 1) * hw)
        qh = q[:, hs]
        kh = k_of(h)
        vext = vext_of(h)
        maps = []
        for qm in (jnp.where(first, qh, 0.0), jnp.where(first, 0.0, qh)):
            s = _dot_nt(qm.astype(BF16), kh)
            e = jnp.exp2(s - jnp.max(s, axis=-1, keepdims=True)).astype(BF16)
            oe = _dot(e, vext)
            maps.append(oe[:, 0:DV_A] / oe[:, DV_A:])
        oh = maps[0] - lam * maps[1]
        o_ref[:, hs] = _rms_rows(oh, sub_gain) * (1.0 - lam_init)


def _attn_ctx_kernel(q_ref, k_ref, v_ref, dl_ref, sg_ref, o_ref, *, lam_init):
    lam = _diff_lambda(dl_ref, lam_init)
    k = k_ref[...].astype(BF16)
    seq = k.shape[0]
    ones = jnp.ones((seq, DV_A), BF16)
    hw = 2 * DQK_A

    def vext(h):
        return jnp.concatenate([v_ref[pl.ds(h, seq, stride=H_A), :].astype(BF16), ones], axis=1)

    _diff_attention(q_ref[...], lambda h: k[:, h * hw:(h + 1) * hw], vext, lam, sg_ref[...],
                    lam_init, o_ref)


def _attention_ctx(q, k, v_all, layer, seq, dl, sg, lam_init):
    t = q.shape[0]
    w = H_A * DV_A
    return pl.pallas_call(
        functools.partial(_attn_ctx_kernel, lam_init=lam_init),
        grid=(t // seq,),
        in_specs=[
            pl.BlockSpec((seq, w), lambda b: (b, 0)),
            pl.BlockSpec((seq, w), lambda b: (b, 0)),
            pl.BlockSpec((None, None, seq * H_A, DV_A), lambda b: (b, layer, 0, 0)),
            pl.BlockSpec(dl.shape, lambda b: (0, 0)),
            pl.BlockSpec(sg.shape, lambda b: (0, 0)),
        ],
        out_specs=pl.BlockSpec((seq, w), lambda b: (b, 0)),
        out_shape=jax.ShapeDtypeStruct((t, w), F32),
        compiler_params=_params(("arbitrary",)),
        name="attn_ctx",
    )(q, k, v_all, dl, sg)


def _rope(x, cos, sin_lo, sin_hi):
    return (x * cos + pltpu.roll(x, 2 * DQK_A - 16, 1) * sin_lo + pltpu.roll(x, 16, 1) * sin_hi)


def _attn_lat_kernel(q_ref, k_ref, v_ref, ck_ref, cv_ref, qcos_ref, qslo_ref, qshi_ref,
                     kcos_ref, kslo_ref, kshi_ref, dl_ref, sg_ref, o_ref, kall, vall, *, lam_init):
    hw = 2 * DQK_A
    n_new = k_ref.shape[0]

    @pl.when(pl.program_id(1) == 0)
    def _():
        kcos, kslo, kshi = kcos_ref[...], kslo_ref[...], kshi_ref[...]
        for h in range(H_A):
            hs = slice(h * hw, (h + 1) * hw)
            kall[0:n_new, hs] = _rope(k_ref[:, hs], kcos, kslo, kshi).astype(BF16)
        kall[n_new:, :] = ck_ref[...].astype(BF16)
        ones = jnp.ones((vall.shape[0], DV_A), BF16)
        for h in range(H_A):
            vs = slice(h * DV_A, (h + 1) * DV_A)
            vall[0:n_new, 2 * h * DV_A:(2 * h + 1) * DV_A] = (
                v_ref[pl.ds(h, n_new, stride=H_A), :].astype(BF16))
            vall[n_new:, 2 * h * DV_A:(2 * h + 1) * DV_A] = cv_ref[:, vs].astype(BF16)
            vall[:, (2 * h + 1) * DV_A:(2 * h + 2) * DV_A] = ones

    lam = _diff_lambda(dl_ref, lam_init)
    qcos, qslo, qshi = qcos_ref[...], qslo_ref[...], qshi_ref[...]
    q = jnp.concatenate(
        [_rope(q_ref[:, h * hw:(h + 1) * hw], qcos, qslo, qshi) for h in range(H_A)], axis=-1)
    _diff_attention(q, lambda h: kall[:, h * hw:(h + 1) * hw],
                    lambda h: vall[:, 2 * h * DV_A:(2 * h + 2) * DV_A], lam, sg_ref[...],
                    lam_init, o_ref)


def _attention_lat(q, k, v, n_batch, cache_k, cache_v, layer, rope_tabs, dl, sg, lam_init):
    t = q.shape[0]
    seq = t // n_batch
    past = cache_k.shape[2]
    w = H_A * DV_A
    hw = 2 * DQK_A
    nq = seq // TILE
    cos, slo, shi = rope_tabs
    tab_q = pl.BlockSpec((TILE, hw), lambda b, j: (j, 0))
    tab_k = pl.BlockSpec((seq, hw), lambda b, j: (0, 0))
    return pl.pallas_call(
        functools.partial(_attn_lat_kernel, lam_init=lam_init),
        grid=(n_batch, nq),
        in_specs=[
            pl.BlockSpec((TILE, w), lambda b, j: (b * nq + j, 0)),
            pl.BlockSpec((seq, w), lambda b, j: (b, 0)),
            pl.BlockSpec((seq * H_A, DV_A), lambda b, j: (b, 0)),
            pl.BlockSpec((None, None, past, w), lambda b, j: (b, layer, 0, 0)),
            pl.BlockSpec((None, None, past, w), lambda b, j: (b, layer, 0, 0)),
            tab_q, tab_q, tab_q, tab_k, tab_k, tab_k,
            pl.BlockSpec(dl.shape, lambda b, j: (0, 0)),
            pl.BlockSpec(sg.shape, lambda b, j: (0, 0)),
        ],
        out_specs=pl.BlockSpec((TILE, w), lambda b, j: (b * nq + j, 0)),
        out_shape=jax.ShapeDtypeStruct((t, w), F32),
        scratch_shapes=[pltpu.VMEM((seq + past, w), BF16), pltpu.VMEM((seq + past, 2 * w), BF16)],
        compiler_params=_params(("arbitrary", "arbitrary")),
        name="attn_lat",
    )(q, k, v, cache_k, cache_v, cos, slo, shi, cos, slo, shi, dl, sg)


def _tile_aux():
    rows = lax.broadcasted_iota(jnp.int32, (TILE, 1), 0)
    lane = lax.broadcasted_iota(jnp.int32, (1, W_B), 1)
    head_masks = [(lane // DK_B) == h for h in range(H_B)]
    first_of_pair = lax.broadcasted_iota(jnp.int32, (1, 2 * DK_B), 1) < DK_B
    xor = (lax.broadcasted_iota(jnp.int32, (HALF, HALF), 0)
           ^ lax.broadcasted_iota(jnp.int32, (HALF, HALF), 1))
    levels = [1 << i for i in range(TILE.bit_length() - 1)]
    right = {m: (rows & m) != 0 for m in levels}
    same_block = {m: xor < 2 * m for m in levels if 2 * m < HALF}
    return rows, head_masks, first_of_pair, right, same_block


def _hgrn_lower_bound(lb_ref, layer, direction, depth):
    xs = [lb_ref[2 * j + direction:2 * j + direction + 1, :] for j in range(depth)]
    m = functools.reduce(jnp.maximum, xs)
    es = [jnp.exp(x - m) for x in xs]
    return sum(es[1:layer + 1]) / sum(es)


def _hgrn_gates(fpre, lbd):
    t = jnp.exp(-jnp.abs(fpre))
    log_sig = jnp.minimum(fpre, 0.0) - jnp.log(1.0 + t)
    r = 1.0 / (1.0 + t)
    sig_neg = jnp.where(fpre >= 0.0, t * r, r)
    if lbd is None:
        return log_sig * LOG2E, sig_neg
    a = jnp.log(lbd)
    b = jnp.log1p(-lbd) + log_sig
    logf = jnp.maximum(a, b) + jnp.log1p(jnp.exp(-jnp.abs(a - b)))
    return logf * LOG2E, (1.0 - lbd) * sig_neg


def _chain_row(token):
    return SUBLANES * (token % CHAIN_LEN) + token // CHAIN_LEN


def _tile_prefix(g, g_scr, b_scr):
    g_c = _to_chains(g, g_scr)
    run = [g_c[0]]
    for j in range(1, CHAIN_LEN):
        run.append(run[-1] + g_c[j])
    sub = lax.broadcasted_iota(jnp.int32, (SUBLANES, 1), 0)
    incl = run[-1]
    k = 1
    while k < SUBLANES:
        incl = incl + jnp.where(sub >= k, pltpu.roll(incl, k, 0), 0.0)
        k *= 2
    offset = incl - run[-1]
    prefix = _from_chains([r + offset for r in run], b_scr)
    return prefix, incl[SUBLANES - 1:SUBLANES, :]


def _level_exponent(z, g, b_scr, m, reverse, right):
    if m == 1:
        return jnp.where(right, g, 0.0) if not reverse else jnp.where(right, 0.0, g)
    slabs = b_scr.shape[0]

    def ref_rows(token):
        row = _chain_row(token)
        return jnp.concatenate(
            [jnp.broadcast_to(b_scr[s, row:row + 1, :], (SUBLANES, LANES)) for s in range(slabs)],
            axis=1)

    groups = []
    for gq in range(TILE // SUBLANES):
        first = gq * SUBLANES
        zg = z[first:first + SUBLANES]
        if 2 * m >= SUBLANES:
            block = first // (2 * m) * (2 * m)
            ref = ref_rows(block + m - 1)
            if m >= SUBLANES:
                groups.append(zg - ref if first - block >= m else ref - zg)
                continue
        else:
            sub = lax.broadcasted_iota(jnp.int32, (SUBLANES, 1), 0)
            ref = jnp.where(sub < 2 * m, ref_rows(first + m - 1), ref_rows(first + 3 * m - 1))
        groups.append(-jnp.abs(zg - ref))
    return jnp.concatenate(groups, axis=0)


def _hgrn_tile(qh, kk, vv, g, reverse, aux, ones_bd, state_t, scratch):
    rows, head_masks, first_of_pair, right_rows, same_block = aux
    pair_w = 2 * DK_B
    g_scr, b_scr = scratch
    prefix, total = _tile_prefix(g, g_scr, b_scr)
    z = prefix if not reverse else prefix - g
    diag = [[None, None] for _ in range(H_B)]
    off = [None] * H_B
    m = 1
    while m < TILE:
        right = right_rows[m]
        x = jnp.exp2(_level_exponent(z, g, b_scr, m, reverse, right))
        q_rows = right if not reverse else jnp.logical_not(right)
        qz = jnp.where(q_rows, (qh * x).astype(BF16), 0.0)
        kz = jnp.where(q_rows, 0.0, (kk * x).astype(BF16))
        for h in range(H_B):
            ls = slice((h // 2) * pair_w, (h // 2 + 1) * pair_w)
            qm = jnp.where(first_of_pair, qz[:, ls], 0.0) if h % 2 == 0 else \
                jnp.where(first_of_pair, 0.0, qz[:, ls])
            kh = kz[:, ls]
            if m < HALF:
                for i in range(2):
                    rs = slice(i * HALF, (i + 1) * HALF)
                    pm = _dot_nt(qm[rs], kh[rs])
                    if m in same_block:
                        pm = jnp.where(same_block[m], pm, 0.0)
                    diag[h][i] = pm if diag[h][i] is None else diag[h][i] + pm
            else:
                lo, hi = slice(0, HALF), slice(HALF, TILE)
                off[h] = _dot_nt(qm[hi], kh[lo]) if not reverse else _dot_nt(qm[lo], kh[hi])
        m *= 2
    vb = vv.astype(BF16)
    top = bot = None
    for h in range(H_B):
        vh = jnp.where(head_masks[h], vb, 0.0)
        d0, d1, of = (a.astype(BF16) for a in (diag[h][0], diag[h][1], off[h]))
        if not reverse:
            t = _dot(d0, vh[0:HALF])
            b = _dot(jnp.concatenate([of, d1], axis=1), vh)
        else:
            t = _dot(jnp.concatenate([d0, of], axis=1), vh)
            b = _dot(d1, vh[HALF:])
        top = t if top is None else top + t
        bot = b if bot is None else bot + b
    out = _group_sum(qh * kk, ones_bd) * vv + jnp.concatenate([top, bot], axis=0)
    q_decay, k_decay = (z, total - z) if not reverse else (total - z, z)
    if state_t is not None:
        out = out + _dot_nt((qh * jnp.exp2(q_decay)).astype(BF16), state_t.astype(BF16))
    k_hat = (kk * jnp.exp2(k_decay)).astype(BF16)
    return out, k_hat, total


def _block_diag_mask():
    r = lax.broadcasted_iota(jnp.int32, (W_B, W_B), 0) // DK_B
    c = lax.broadcasted_iota(jnp.int32, (W_B, W_B), 1) // DV_B
    return r == c


def _conv(x, prev_row, next_rows, cw_ref, cb_ref, rows):
    x_m1 = jnp.where(rows == 0, prev_row, pltpu.roll(x, 1, 0))
    x_p1 = jnp.where(rows == TILE - 1, next_rows[0:1], pltpu.roll(x, TILE - 1, 0))
    x_p2 = jnp.where(rows == TILE - 2, next_rows[0:1],
                     jnp.where(rows == TILE - 1, next_rows[1:2], pltpu.roll(x, TILE - 2, 0)))
    y = cb_ref[...] + x_m1 * cw_ref[0:1, :]
    y = y + x * cw_ref[1:2, :]
    y = y + x_p1 * cw_ref[2:3, :]
    return y + x_p2 * cw_ref[3:4, :]


def _chain_address(g):
    per_chain = CHAIN_LEN // SUBLANES
    return SUBLANES * SUBLANES * (g % per_chain) + g // per_chain, SUBLANES


def _to_chains(x, scr):
    slabs = scr.shape[0]
    for s in range(slabs):
        for g in range(TILE // SUBLANES):
            start, stride = _chain_address(g)
            scr[s, pl.ds(start, SUBLANES, stride=stride), :] = (
                x[g * SUBLANES:(g + 1) * SUBLANES, s * LANES:(s + 1) * LANES])
    return [jnp.concatenate([scr[s, j * SUBLANES:(j + 1) * SUBLANES, :] for s in range(slabs)],
                            axis=1) for j in range(CHAIN_LEN)]


def _from_chains(chains, scr):
    slabs = scr.shape[0]
    for s in range(slabs):
        for j, c in enumerate(chains):
            scr[s, j * SUBLANES:(j + 1) * SUBLANES, :] = c[:, s * LANES:(s + 1) * LANES]
    groups = []
    for g in range(TILE // SUBLANES):
        start, stride = _chain_address(g)
        groups.append(jnp.concatenate(
            [scr[s, pl.ds(start, SUBLANES, stride=stride), :] for s in range(slabs)], axis=1))
    return jnp.concatenate(groups, axis=0)


def _rglru_tile(xc, r_pre, i_pre, lam_row, reverse, h_in, scratch):
    a_scr, u_scr, h_scr = scratch
    r = _sigmoid(r_pre)
    i = _sigmoid(i_pre)
    log_a = -RG_C * r * _softplus(-lam_row)
    a = jnp.exp(log_a)
    u = jnp.sqrt(jnp.tanh(-log_a) * (a * a + 1.0)) * (i * xc)
    a_c = _to_chains(a, a_scr)
    u_c = _to_chains(u, u_scr)
    order = list(reversed(range(CHAIN_LEN))) if reverse else list(range(CHAIN_LEN))
    h_c = [None] * CHAIN_LEN
    p_c = [None] * CHAIN_LEN
    h_c[order[0]], p_c[order[0]] = u_c[order[0]], a_c[order[0]]
    for prev, j in zip(order[:-1], order[1:]):
        h_c[j] = a_c[j] * h_c[prev] + u_c[j]
        p_c[j] = a_c[j] * p_c[prev]
    p_tot, h_tot = p_c[order[-1]], h_c[order[-1]]
    sub = lax.broadcasted_iota(jnp.int32, (SUBLANES, 1), 0)
    k = 1
    while k < SUBLANES:
        if not reverse:
            valid = sub >= k
            p_s, h_s = pltpu.roll(p_tot, k, 0), pltpu.roll(h_tot, k, 0)
        else:
            valid = sub < SUBLANES - k
            p_s, h_s = pltpu.roll(p_tot, SUBLANES - k, 0), pltpu.roll(h_tot, SUBLANES - k, 0)
        h_tot = h_tot + p_tot * jnp.where(valid, h_s, 0.0)
        p_tot = p_tot * jnp.where(valid, p_s, 1.0)
        k *= 2
    if h_in is not None:
        h_tot = h_tot + p_tot * h_in
    if not reverse:
        carry = jnp.where(sub == 0, 0.0 if h_in is None else h_in, pltpu.roll(h_tot, 1, 0))
        h_out = h_tot[SUBLANES - 1:SUBLANES, :]
    else:
        carry = jnp.where(sub == SUBLANES - 1, 0.0 if h_in is None else h_in,
                          pltpu.roll(h_tot, SUBLANES - 1, 0))
        h_out = h_tot[0:1, :]
    h_c = [h + p * carry for h, p in zip(h_c, p_c)]
    return _from_chains(h_c, h_scr), h_out


def _rec_inputs(refs):
    qb, ff, fb, ib, gb, xcol, gcol = [r[...] for r in refs]
    return _silu(qb), ff, fb, ib, gb, xcol, gcol


def _hgrn_out(o_sum, gb, onorm_gain, ones_bd):
    ms = _group_sum(o_sum * o_sum, ones_bd) * (1.0 / DV_B)
    return o_sum * lax.rsqrt(ms + EPS) * onorm_gain * _silu(gb)


def _rec_ctx_kernel(q_ref, ff_ref, fb_ref, i_ref, g_ref, x_ref, gc_ref, lb_ref, on_ref, cw_ref,
                    cb_ref, wg_ref, bg_ref, lam_ref, ones_ref, ob_ref, oc_ref, st_ref,
                    hs_ref, *scan_scratch, layer, depth):
    aux = _tile_aux()
    rows = aux[0]
    ones_bd = ones_ref[...]
    qh, ff, fb, ib, gb, xcol, gcol = _rec_inputs((q_ref, ff_ref, fb_ref, i_ref, g_ref, x_ref, gc_ref))
    o_sum = None
    for direction, fpre in enumerate((ff, fb)):
        lbd = None if layer == 0 else _hgrn_lower_bound(lb_ref, layer, direction, depth)
        g, kk = _hgrn_gates(fpre, lbd)
        o, k_hat, _ = _hgrn_tile(qh, kk, ib, g, direction == 1, aux, ones_bd, None,
                                 scan_scratch[3 + 2 * direction:5 + 2 * direction])
        o_sum = o if o_sum is None else o_sum + o
        full = _dot_tn(k_hat, ib.astype(BF16))
        for h in range(H_B):
            blk = full[h * DK_B:(h + 1) * DK_B, (h // 2) * 2 * DV_B:(h // 2 + 1) * 2 * DV_B]
            if h % 2:
                blk = pltpu.roll(blk, DV_B, 1)
            st_ref[0, direction, h] = blk[:, 0:DV_B]
    ob_ref[...] = _hgrn_out(o_sum, gb, on_ref[...], ones_bd)

    zero_row = jnp.zeros((1, W_C), F32)
    xc = _conv(xcol, zero_row, jnp.zeros((2, W_C), F32), cw_ref, cb_ref, rows)
    gates = _dot(xc.astype(BF16), wg_ref[...]) + bg_ref[...]
    y_f, h_f = _rglru_tile(xc, gates[:, 0:W_C], gates[:, W_C:2 * W_C], lam_ref[0:1, :], False,
                           None, scan_scratch[0:3])
    y_b, h_b = _rglru_tile(xc, gates[:, 2 * W_C:3 * W_C], gates[:, 3 * W_C:], lam_ref[1:2, :], True,
                           None, scan_scratch[0:3])
    oc_ref[...] = (y_f + y_b) * _gelu_tanh(gcol)
    hs_ref[0, 0:1, :] = h_f
    hs_ref[0, 1:2, :] = h_b


def _scan_scratch(hgrn_directions):
    return [pltpu.VMEM((W_C // LANES, TILE, LANES), F32) for _ in range(3 + 2 * hgrn_directions)]


def _col_spec(col, row_map):
    return pl.BlockSpec((TILE, W_B), lambda *idx: (row_map(*idx), col))


COL_QB, COL_FF, COL_FB, COL_IB, COL_GB, COL_XC, COL_GC = range(7)
_REC_COLS = (COL_QB, COL_FF, COL_FB, COL_IB, COL_GB, COL_XC, COL_GC)


def _full_spec(a):
    nd = a.ndim
    return pl.BlockSpec(a.shape, lambda *idx: (0,) * nd)


def _recurrence_ctx(proj, layer, depth, consts):
    t = proj.shape[0]
    n = t // TILE
    return pl.pallas_call(
        functools.partial(_rec_ctx_kernel, layer=layer, depth=depth),
        grid=(n,),
        in_specs=[_col_spec(c, lambda b: b) for c in _REC_COLS] + [_full_spec(a) for a in consts],
        out_specs=[
            pl.BlockSpec((TILE, W_B), lambda b: (b, 0)),
            pl.BlockSpec((TILE, W_C), lambda b: (b, 0)),
            pl.BlockSpec((1, 2, H_B, DK_B, DV_B), lambda b: (b, 0, 0, 0, 0)),
            pl.BlockSpec((1, 2, W_C), lambda b: (b, 0, 0)),
        ],
        out_shape=[
            jax.ShapeDtypeStruct((t, W_B), F32),
            jax.ShapeDtypeStruct((t, W_C), F32),
            jax.ShapeDtypeStruct((n, 2, H_B, DK_B, DV_B), F32),
            jax.ShapeDtypeStruct((n, 2, W_C), F32),
        ],
        scratch_shapes=_scan_scratch(2),
        compiler_params=_params(("arbitrary",)),
        name="rec_ctx",
    )(*([proj] * len(_REC_COLS)), *consts)


def _rec_lat_kernel(q_ref, f_ref, i_ref, g_ref, x_ref, xp_ref, xn_ref, gc_ref, s0_ref, h0_ref,
                    *rest, layer, depth, direction, n_tiles):
    if direction == 0:
        (lb_ref, on_ref, cw_ref, cb_ref, wg_ref, bg_ref, lam_ref, ones_ref,
         o_ref, y_ref, state, hstate, *scan_scratch) = rest
    else:
        (of_ref, yf_ref, lb_ref, on_ref, cw_ref, cb_ref, wg_ref, bg_ref, lam_ref, ones_ref,
         o_ref, y_ref, state, hstate, *scan_scratch) = rest
    step = pl.program_id(1)
    tile = step if direction == 0 else n_tiles - 1 - step
    reverse = direction == 1

    @pl.when(step == 0)
    def _():
        state[...] = s0_ref[0, 0]
        hstate[...] = h0_ref[0, 0]

    aux = _tile_aux()
    rows = aux[0]
    ones_bd = ones_ref[...]
    qh = _silu(q_ref[...])
    ib = i_ref[...]
    lbd = None if layer == 0 else _hgrn_lower_bound(lb_ref, layer, direction, depth)
    g, kk = _hgrn_gates(f_ref[...], lbd)
    o, k_hat, total = _hgrn_tile(qh, kk, ib, g, reverse, aux, ones_bd, state[...],
                                 scan_scratch[3:5])
    upd = jnp.where(_block_diag_mask(), _dot_tn(ib.astype(BF16), k_hat), 0.0)
    state[...] = state[...] * jnp.exp2(total) + upd

    prev_row = jnp.where(tile == 0, 0.0, xp_ref[TILE - 1:TILE, :])
    next_rows = jnp.where(tile == n_tiles - 1, 0.0, xn_ref[0:2, :])
    xc = _conv(x_ref[...], prev_row, next_rows, cw_ref, cb_ref, rows)
    gates = _dot(xc.astype(BF16), wg_ref[...]) + bg_ref[...]
    c0 = 2 * W_C * direction
    y, h_next = _rglru_tile(xc, gates[:, c0:c0 + W_C], gates[:, c0 + W_C:c0 + 2 * W_C],
                            lam_ref[direction:direction + 1, :], reverse, hstate[...],
                            scan_scratch[0:3])
    hstate[...] = h_next

    if direction == 0:
        o_ref[...] = o
        y_ref[...] = y
    else:
        o_ref[...] = _hgrn_out(of_ref[...] + o, g_ref[...], on_ref[...], ones_bd)
        y_ref[...] = (yf_ref[...] + y) * _gelu_tanh(gc_ref[...])


def _recurrence_lat(proj, n_batch, layer, depth, direction, state0, h0, prev, consts):
    t = proj.shape[0]
    n_tiles = t // (TILE * n_batch)

    def tile_of(b, s):
        return s if direction == 0 else n_tiles - 1 - s

    def row(b, s):
        return b * n_tiles + tile_of(b, s)

    def row_prev(b, s):
        return b * n_tiles + jnp.maximum(tile_of(b, s) - 1, 0)

    def row_next(b, s):
        return b * n_tiles + jnp.minimum(tile_of(b, s) + 1, n_tiles - 1)

    f_col = COL_FF if direction == 0 else COL_FB
    in_specs = [_col_spec(COL_QB, row), _col_spec(f_col, row), _col_spec(COL_IB, row),
                _col_spec(COL_GB, row), _col_spec(COL_XC, row), _col_spec(COL_XC, row_prev),
                _col_spec(COL_XC, row_next), _col_spec(COL_GC, row),
                pl.BlockSpec((1, 1, W_B, W_B), lambda b, s: (b, direction, 0, 0)),
                pl.BlockSpec((1, 1, 1, W_C), lambda b, s: (b, direction, 0, 0))]
    args = [proj] * 8 + [state0, h0]
    if direction == 1:
        in_specs += [pl.BlockSpec((TILE, W_B), lambda b, s: (row(b, s), 0)),
                     pl.BlockSpec((TILE, W_C), lambda b, s: (row(b, s), 0))]
        args += list(prev)
    in_specs += [_full_spec(a) for a in consts]
    args += list(consts)
    return pl.pallas_call(
        functools.partial(_rec_lat_kernel, layer=layer, depth=depth, direction=direction,
                          n_tiles=n_tiles),
        grid=(n_batch, n_tiles),
        in_specs=in_specs,
        out_specs=[pl.BlockSpec((TILE, W_B), lambda b, s: (row(b, s), 0)),
                   pl.BlockSpec((TILE, W_C), lambda b, s: (row(b, s), 0))],
        out_shape=[jax.ShapeDtypeStruct((t, W_B), F32), jax.ShapeDtypeStruct((t, W_C), F32)],
        scratch_shapes=[pltpu.VMEM((W_B, W_B), F32), pltpu.VMEM((1, W_C), F32)] + _scan_scratch(1),
        compiler_params=_params(("arbitrary", "arbitrary")),
        name="rec_lat_fwd" if direction == 0 else "rec_lat_bwd",
    )(*args)


def _out_kernel(x_ref, oa_ref, ob_ref, oc_ref, mod_ref, n2_ref, wo_ref, w1_ref, w2_ref, y_ref):
    x = x_ref[...]
    d = x.shape[-1]
    g1 = mod_ref[0, :, 2 * d:3 * d]
    sh2 = mod_ref[0, :, 3 * d:4 * d]
    sc2 = mod_ref[0, :, 4 * d:5 * d]
    g2 = mod_ref[0, :, 5 * d:6 * d]
    wa = H_A * DV_A
    mix = _dot(oa_ref[...].astype(BF16), wo_ref[0:wa, :])
    mix = mix + _dot(ob_ref[...].astype(BF16), wo_ref[wa:wa + W_B, :])
    mix = mix + _dot(oc_ref[...].astype(BF16), wo_ref[wa + W_B:, :])
    x1 = x + g1 * mix
    h = (_rms_rows(x1, n2_ref[...]) * (1.0 + sc2) + sh2).astype(BF16)
    acc = jnp.zeros_like(x)
    for c in range(w1_ref.shape[1] // D_FF_CHUNK):
        cs = slice(c * D_FF_CHUNK, (c + 1) * D_FF_CHUNK)
        f = jnp.maximum(_dot(h, w1_ref[:, cs]), 0.0)
        acc = acc + _dot((f * f).astype(BF16), w2_ref[cs, :])
    y_ref[...] = x1 + g2 * acc


def _out_mlp(x, oa, ob, oc, mod, tiles_per_mod, n2, w_out, w_ff1, w_ff2, layer):
    t, d = x.shape

    def resident(w):
        return pl.BlockSpec((None,) + w.shape[1:], lambda i: (layer, 0, 0),
                            pipeline_mode=pl.Buffered(1))

    return pl.pallas_call(
        _out_kernel,
        grid=(t // TILE,),
        in_specs=[
            pl.BlockSpec((TILE, d), lambda i: (i, 0)),
            pl.BlockSpec((TILE, oa.shape[1]), lambda i: (i, 0)),
            pl.BlockSpec((TILE, ob.shape[1]), lambda i: (i, 0)),
            pl.BlockSpec((TILE, oc.shape[1]), lambda i: (i, 0)),
            pl.BlockSpec((1, 1, N_MOD * d), lambda i: (i // tiles_per_mod, 0, 0)),
            pl.BlockSpec((1, d), lambda i: (0, 0)),
            resident(w_out),
            resident(w_ff1),
            resident(w_ff2),
        ],
        out_specs=pl.BlockSpec((TILE, d), lambda i: (i, 0)),
        out_shape=jax.ShapeDtypeStruct((t, d), F32),
        compiler_params=_params(("arbitrary",)),
        name="out_mlp",
    )(x, oa, ob, oc, mod, n2, w_out, w_ff1, w_ff2)


def _ones_block_diag(n, group):
    idx = np.arange(n) // group
    return jnp.asarray((idx[:, None] == idx[None, :]).astype(np.float32), dtype=BF16)


def _rope_tables(seq):
    nf = DQK_A // 4
    inv = ROPE_BASE ** (-jnp.arange(nf, dtype=F32) / nf)
    pos = jnp.arange(seq)
    ang_r = (pos // GRID_W).astype(F32)[:, None] * inv
    ang_c = (pos % GRID_W).astype(F32)[:, None] * inv
    z = jnp.zeros_like(ang_r)
    cos = jnp.concatenate([jnp.cos(ang_r)] * 2 + [jnp.cos(ang_c)] * 2, axis=-1)
    sin_lo = jnp.concatenate([-jnp.sin(ang_r), z, -jnp.sin(ang_c), z], axis=-1)
    sin_hi = jnp.concatenate([z, jnp.sin(ang_r), z, jnp.sin(ang_c)], axis=-1)
    return tuple(jnp.tile(a, (1, 2)) for a in (cos, sin_lo, sin_hi))


def _gate_weights(rg_w_l):
    eye = jnp.eye(H_C, dtype=rg_w_l.dtype)
    w = jnp.einsum('dghij,hk->dghikj', rg_w_l, eye)
    w = w.reshape(2, 2, W_C, W_C)
    return jnp.transpose(w, (2, 0, 1, 3)).reshape(W_C, 4 * W_C).astype(BF16)


def _state_block_diag_t(s):
    eye = jnp.eye(H_B, dtype=s.dtype)
    w = jnp.einsum('...hdv,hk->...hvkd', s, eye)
    return w.reshape(s.shape[:-3] + (W_B, W_B))


def kernel(x_prompt, x_sample, cache_k, cache_v, state_hgrn, state_rglru, c, c_ctx, w_mod, b_mod,
           norm1, norm2, w_in, w_out, qk_norm, diff_lambda, subln, hgrn_lb, hgrn_onorm, conv_w,
           conv_b, rg_w, rg_b, rg_lambda, w_ff1, w_ff2):
    batch, seq, d = x_prompt.shape
    dec_batch, dec_seq, _ = x_sample.shape
    depth = w_in.shape[0]
    past = cache_k.shape[2]
    wq = H_A * 2 * DQK_A

    c8 = jnp.concatenate([c_ctx[None], c, jnp.zeros((8 - 1 - dec_batch, d), F32)], axis=0)
    mod_all = _modulation(c8, w_mod, b_mod)

    assert DQK_A == DV_B and seq == TILE
    ones_b = _ones_block_diag(W_B, DV_B)
    rope_tabs = _rope_tables(dec_seq)
    ck = cache_k.reshape(dec_batch, depth, past, wq)
    cv = cache_v.reshape(dec_batch, depth, past, H_A * DV_A)
    lb2 = hgrn_lb.reshape(depth * 2, W_B)

    w_in_b, w_out_b, w1_b, w2_b = (w.astype(BF16) for w in (w_in, w_out, w_ff1, w_ff2))

    yp = x_prompt.reshape(batch * seq, d)
    ys = x_sample.reshape(dec_batch * dec_seq, d)
    caches, shs, srs = (), [], []
    for l in range(depth):
        lam_init = 0.8 - 0.6 * math.exp(-0.3 * l)
        mod_ctx = mod_all[l, 0:1][:, None, :]
        mod_lat = mod_all[l, 1:1 + dec_batch][:, None, :]
        n1 = norm1[l][None]
        n2 = norm2[l][None]
        gq = jnp.tile(qk_norm[l, 0], H_A * 2)[None] * (DQK_A ** -0.5 * LOG2E)
        gk = jnp.tile(qk_norm[l, 1], H_A * 2)[None]
        dl = diff_lambda[l]
        sg = subln[l][None]
        consts = (lb2, jnp.tile(hgrn_onorm[l], H_B)[None], conv_w[l], conv_b[l][None],
                  _gate_weights(rg_w[l]), rg_b[l].reshape(1, 4 * W_C), rg_lambda[l], ones_b)

        q, k, v_all, rest, kt_all = _in_projection(yp, mod_ctx, batch * seq // TILE, n1, w_in_b, l,
                                                   gq, gk, ones_b, caches=caches)
        caches = (kt_all, v_all)
        oa = _attention_ctx(q, k, v_all, l, seq, dl, sg, lam_init)
        ob, oc, st, hs = _recurrence_ctx(rest, l, depth, consts)
        shs.append(st)
        srs.append(hs)
        yp = _out_mlp(yp, oa, ob, oc, mod_ctx, batch * seq // TILE, n2, w_out_b, w1_b, w2_b, l)

        q, k, v, rest = _in_projection(ys, mod_lat, dec_seq // TILE, n1, w_in_b, l, gq, gk, ones_b)
        oa = _attention_lat(q, k, v, dec_batch, ck, cv, l, rope_tabs, dl, sg, lam_init)
        s0 = _state_block_diag_t(state_hgrn[:, l])
        h0 = state_rglru[:, l][:, :, None, :]
        fwd = _recurrence_lat(rest, dec_batch, l, depth, 0, s0, h0, None, consts)
        ob, oc = _recurrence_lat(rest, dec_batch, l, depth, 1, s0, h0, fwd, consts)
        ys = _out_mlp(ys, oa, ob, oc, mod_lat, dec_seq // TILE, n2, w_out_b, w1_b, w2_b, l)

    kt_all, v_all = caches
    new_k = jnp.transpose(kt_all.reshape(depth, H_A, 2, DQK_A, batch, seq), (4, 0, 5, 1, 2, 3))
    new_v = v_all.reshape(batch, depth, seq, H_A, DV_A)
    return (yp.reshape(batch, seq, d), ys.reshape(dec_batch, dec_seq, d), new_k, new_v,
            jnp.stack(shs, axis=1), jnp.stack(srs, axis=1))
```

```python
import functools
import math

import numpy as np
import jax
import jax.numpy as jnp
from jax import lax
from jax.experimental import pallas as pl
from jax.experimental.pallas import tpu as pltpu

F32 = jnp.float32
BF16 = jnp.bfloat16

GRID_W = 64
H_A, DQK_A, DV_A = 4, 64, 128
H_B, DK_B, DV_B = 4, 64, 64
W_B = H_B * DV_B
H_C, W_C = 4, 256
CONV_K = 4
RG_C = 8.0
ROPE_BASE = 10000.0
EPS = 1e-6
N_MOD = 6
D_FF_CHUNK = 1024

TILE = 256
HALF = TILE // 2
SUBLANES = 8
LANES = 128
CHAIN_LEN = TILE // SUBLANES
LOG2E = math.log2(math.e)
V7X_VMEM_BYTES = 64 * 1024 * 1024
VMEM_LIMIT = V7X_VMEM_BYTES - 8 * 1024 * 1024


def _dot(a, b):
    return jnp.dot(a, b, preferred_element_type=F32)


def _dot_nt(a, b):
    return lax.dot_general(a, b, (((1,), (1,)), ((), ())), preferred_element_type=F32)


def _dot_tn(a, b):
    return lax.dot_general(a, b, (((0,), (0,)), ((), ())), preferred_element_type=F32)


def _group_sum(x, ones_bd):
    hi = x.astype(BF16)
    lo = (x - hi.astype(F32)).astype(BF16)
    return _dot(hi, ones_bd) + _dot(lo, ones_bd)


def _rms_rows(x, gain):
    return x * lax.rsqrt(jnp.mean(x * x, axis=-1, keepdims=True) + EPS) * gain


def _sigmoid(x):
    return 0.5 * jnp.tanh(0.5 * x) + 0.5


def _silu(x):
    return x * _sigmoid(x)


def _softplus(x):
    return jnp.maximum(x, 0.0) + jnp.log1p(jnp.exp(-jnp.abs(x)))


def _log_sigmoid(x):
    return jnp.minimum(x, 0.0) - jnp.log1p(jnp.exp(-jnp.abs(x)))


def _gelu_tanh(x):
    return x * (0.5 * (1.0 + jnp.tanh(math.sqrt(2.0 / math.pi) * (x + 0.044715 * (x * x * x)))))


def _params(semantics, flags=None):
    return pltpu.CompilerParams(dimension_semantics=semantics, vmem_limit_bytes=VMEM_LIMIT,
                                flags=flags)


def _mod_kernel(c_ref, w_ref, b_ref, o_ref):
    c = c_ref[...]
    o_ref[0] = _dot(_silu(c).astype(BF16), w_ref[0].astype(BF16)) + b_ref[0]


def _modulation(c8, w_mod, b_mod):
    depth, d, n = w_mod.shape
    tn = d
    return pl.pallas_call(
        _mod_kernel,
        grid=(depth, n // tn),
        in_specs=[
            pl.BlockSpec((8, d), lambda l, j: (0, 0)),
            pl.BlockSpec((1, d, tn), lambda l, j: (l, 0, j)),
            pl.BlockSpec((1, 1, tn), lambda l, j: (l, 0, j)),
        ],
        out_specs=pl.BlockSpec((1, 8, tn), lambda l, j: (l, 0, j)),
        out_shape=jax.ShapeDtypeStruct((depth, 8, n), F32),
        compiler_params=_params(("arbitrary", "arbitrary")),
        name="mod",
    )(c8, w_mod, b_mod.reshape(depth, 1, n))


def _proj_kernel(x_ref, mod_ref, n1_ref, w_ref, gq_ref, gk_ref, ones_ref, *refs, layers_before):
    prev = refs[:2] if layers_before else ()
    q_ref, k_ref, v_ref, r_ref, *kt_ref = refs[len(prev):]
    x = x_ref[...]
    d = x.shape[-1]
    sh = mod_ref[0, :, 0:d]
    sc = mod_ref[0, :, d:2 * d]
    h = _rms_rows(x, n1_ref[...]) * (1.0 + sc) + sh
    p = _dot(h.astype(BF16), w_ref[...])
    ones_bd = ones_ref[...]
    wq = H_A * 2 * DQK_A

    def qk_norm(z, gain):
        z2 = z * z
        nb = ones_bd.shape[0]
        ms = jnp.concatenate([_group_sum(z2[:, c:c + nb], ones_bd) for c in range(0, wq, nb)],
                             axis=1) * (1.0 / DQK_A)
        return z * lax.rsqrt(ms + EPS) * gain

    wv = H_A * DV_A
    q_ref[...] = qk_norm(p[:, 0:wq], gq_ref[...])
    kn = qk_norm(p[:, wq:2 * wq], gk_ref[...])
    k_ref[...] = kn
    v_dst = v_ref
    if kt_ref:
        kt_ref[0][0, layers_before] = kn.T
        v_dst = v_ref.at[0, layers_before]
        if prev:
            ktp_ref, vp_ref = prev
            kt_ref[0][0, 0:layers_before] = ktp_ref[0]
            v_ref[0, 0:layers_before] = vp_ref[0]
    for h in range(H_A):
        v_dst[pl.ds(h, TILE, stride=H_A), :] = p[:, 2 * wq + h * DV_A:2 * wq + (h + 1) * DV_A]
    r_ref[...] = p[:, 2 * wq + wv:]


def _in_projection(x, mod, tiles_per_mod, n1, w_in, layer, gq, gk, ones_bd, caches=None):
    t, d = x.shape
    layers_before = caches[0].shape[1] if caches else 0
    n_in = w_in.shape[2]
    wq = gq.shape[1]
    wv = H_A * DV_A
    widths = (wq, wq, wv, n_in - 2 * wq - wv)
    out_specs = [pl.BlockSpec((TILE, w), lambda i: (i, 0)) for w in widths]
    out_shape = [jax.ShapeDtypeStruct((t, w), F32) for w in widths]
    out_specs[2] = pl.BlockSpec((TILE * H_A, DV_A), lambda i: (i, 0))
    out_shape[2] = jax.ShapeDtypeStruct((t * H_A, DV_A), F32)
    cache_specs = []
    if caches is not None:
        n_seq, nl = t // TILE, layers_before + 1
        out_specs[2] = pl.BlockSpec((1, nl, TILE * H_A, DV_A), lambda i: (i, 0, 0, 0))
        out_shape[2] = jax.ShapeDtypeStruct((n_seq, nl, TILE * H_A, DV_A), F32)
        out_specs.append(pl.BlockSpec((1, nl, wq, TILE), lambda i: (i, 0, 0, 0)))
        out_shape.append(jax.ShapeDtypeStruct((n_seq, nl, wq, TILE), F32))
        if caches:
            cache_specs = [
                pl.BlockSpec((1, layers_before, wq, TILE), lambda i: (i, 0, 0, 0)),
                pl.BlockSpec((1, layers_before, TILE * H_A, DV_A), lambda i: (i, 0, 0, 0))]
    return pl.pallas_call(
        functools.partial(_proj_kernel, layers_before=layers_before),
        grid=(t // TILE,),
        in_specs=[
            pl.BlockSpec((TILE, d), lambda i: (i, 0)),
            pl.BlockSpec((1, 1, N_MOD * d), lambda i: (i // tiles_per_mod, 0, 0)),
            pl.BlockSpec((1, d), lambda i: (0, 0)),
            pl.BlockSpec((None, d, n_in), lambda i: (layer, 0, 0)),
            pl.BlockSpec((1, wq), lambda i: (0, 0)),
            pl.BlockSpec((1, wq), lambda i: (0, 0)),
            pl.BlockSpec(ones_bd.shape, lambda i: (0, 0)),
        ] + cache_specs,
        out_specs=out_specs,
        out_shape=out_shape,
        compiler_params=_params(("arbitrary",)),
        name="proj",
    )(x, mod, n1, w_in, gq, gk, ones_bd, *(caches or ()))


def _diff_lambda(dl_ref, lam_init):
    lv = dl_ref[...]
    a = jnp.sum(lv[0:1] * lv[1:2], axis=-1, keepdims=True)
    b = jnp.sum(lv[2:3] * lv[3:4], axis=-1, keepdims=True)
    return jnp.exp(a) - jnp.exp(b) + lam_init


def _diff_attention(q, k_of, vext_of, lam, sub_gain, lam_init, o_ref):
    hw = 2 * DQK_A
    first = lax.broadcasted_iota(jnp.int32, (1, hw), 1) < DQK_A
    for h in range(H_A):
        hs = slice(h * hw, (h + 1) * hw)
        qh = q[:, hs]
        kh = k_of(h)
        vext = vext_of(h)
        maps = []
        for qm in (jnp.where(first, qh, 0.0), jnp.where(first, 0.0, qh)):
            s = _dot_nt(qm.astype(BF16), kh)
            e = jnp.exp2(s - jnp.max(s, axis=-1, keepdims=True)).astype(BF16)
            oe = _dot(e, vext)
            maps.append(oe[:, 0:DV_A] / oe[:, DV_A:])
        oh = maps[0] - lam * maps[1]
        o_ref[:, hs] = _rms_rows(oh, sub_gain) * (1.0 - lam_init)


def _attn_ctx_kernel(q_ref, k_ref, v_ref, dl_ref, sg_ref, o_ref, *, lam_init):
    lam = _diff_lambda(dl_ref, lam_init)
    k = k_ref[...].astype(BF16)
    seq = k.shape[0]
    ones = jnp.ones((seq, DV_A), BF16)
    hw = 2 * DQK_A

    def vext(h):
        return jnp.concatenate([v_ref[pl.ds(h, seq, stride=H_A), :].astype(BF16), ones], axis=1)

    _diff_attention(q_ref[...], lambda h: k[:, h * hw:(h + 1) * hw], vext, lam, sg_ref[...],
                    lam_init, o_ref)


def _attention_ctx(q, k, v_all, layer, seq, dl, sg, lam_init):
    t = q.shape[0]
    w = H_A * DV_A
    return pl.pallas_call(
        functools.partial(_attn_ctx_kernel, lam_init=lam_init),
        grid=(t // seq,),
        in_specs=[
            pl.BlockSpec((seq, w), lambda b: (b, 0)),
            pl.BlockSpec((seq, w), lambda b: (b, 0)),
            pl.BlockSpec((None, None, seq * H_A, DV_A), lambda b: (b, layer, 0, 0)),
            pl.BlockSpec(dl.shape, lambda b: (0, 0)),
            pl.BlockSpec(sg.shape, lambda b: (0, 0)),
        ],
        out_specs=pl.BlockSpec((seq, w), lambda b: (b, 0)),
        out_shape=jax.ShapeDtypeStruct((t, w), F32),
        compiler_params=_params(("arbitrary",)),
        name="attn_ctx",
    )(q, k, v_all, dl, sg)


def _rope(x, cos, sin_lo, sin_hi):
    return (x * cos + pltpu.roll(x, 2 * DQK_A - 16, 1) * sin_lo + pltpu.roll(x, 16, 1) * sin_hi)


def _attn_lat_kernel(q_ref, k_ref, v_ref, ck_ref, cv_ref, qcos_ref, qslo_ref, qshi_ref,
                     kcos_ref, kslo_ref, kshi_ref, dl_ref, sg_ref, o_ref, kall, vall, *, lam_init):
    hw = 2 * DQK_A
    n_new = k_ref.shape[0]

    @pl.when(pl.program_id(1) == 0)
    def _():
        kcos, kslo, kshi = kcos_ref[...], kslo_ref[...], kshi_ref[...]
        for h in range(H_A):
            hs = slice(h * hw, (h + 1) * hw)
            kall[0:n_new, hs] = _rope(k_ref[:, hs], kcos, kslo, kshi).astype(BF16)
        kall[n_new:, :] = ck_ref[...].astype(BF16)
        ones = jnp.ones((vall.shape[0], DV_A), BF16)
        for h in range(H_A):
            vs = slice(h * DV_A, (h + 1) * DV_A)
            vall[0:n_new, 2 * h * DV_A:(2 * h + 1) * DV_A] = (
                v_ref[pl.ds(h, n_new, stride=H_A), :].astype(BF16))
            vall[n_new:, 2 * h * DV_A:(2 * h + 1) * DV_A] = cv_ref[:, vs].astype(BF16)
            vall[:, (2 * h + 1) * DV_A:(2 * h + 2) * DV_A] = ones

    lam = _diff_lambda(dl_ref, lam_init)
    qcos, qslo, qshi = qcos_ref[...], qslo_ref[...], qshi_ref[...]
    q = jnp.concatenate(
        [_rope(q_ref[:, h * hw:(h + 1) * hw], qcos, qslo, qshi) for h in range(H_A)], axis=-1)
    _diff_attention(q, lambda h: kall[:, h * hw:(h + 1) * hw],
                    lambda h: vall[:, 2 * h * DV_A:(2 * h + 2) * DV_A], lam, sg_ref[...],
                    lam_init, o_ref)


def _attention_lat(q, k, v, n_batch, cache_k, cache_v, layer, rope_tabs, dl, sg, lam_init):
    t = q.shape[0]
    seq = t // n_batch
    past = cache_k.shape[2]
    w = H_A * DV_A
    hw = 2 * DQK_A
    nq = seq // TILE
    cos, slo, shi = rope_tabs
    tab_q = pl.BlockSpec((TILE, hw), lambda b, j: (j, 0))
    tab_k = pl.BlockSpec((seq, hw), lambda b, j: (0, 0))
    return pl.pallas_call(
        functools.partial(_attn_lat_kernel, lam_init=lam_init),
        grid=(n_batch, nq),
        in_specs=[
            pl.BlockSpec((TILE, w), lambda b, j: (b * nq + j, 0)),
            pl.BlockSpec((seq, w), lambda b, j: (b, 0)),
            pl.BlockSpec((seq * H_A, DV_A), lambda b, j: (b, 0)),
            pl.BlockSpec((None, None, past, w), lambda b, j: (b, layer, 0, 0)),
            pl.BlockSpec((None, None, past, w), lambda b, j: (b, layer, 0, 0)),
            tab_q, tab_q, tab_q, tab_k, tab_k, tab_k,
            pl.BlockSpec(dl.shape, lambda b, j: (0, 0)),
            pl.BlockSpec(sg.shape, lambda b, j: (0, 0)),
        ],
        out_specs=pl.BlockSpec((TILE, w), lambda b, j: (b * nq + j, 0)),
        out_shape=jax.ShapeDtypeStruct((t, w), F32),
        scratch_shapes=[pltpu.VMEM((seq + past, w), BF16), pltpu.VMEM((seq + past, 2 * w), BF16)],
        compiler_params=_params(("arbitrary", "arbitrary")),
        name="attn_lat",
    )(q, k, v, cache_k, cache_v, cos, slo, shi, cos, slo, shi, dl, sg)


def _tile_aux():
    rows = lax.broadcasted_iota(jnp.int32, (TILE, 1), 0)
    lane = lax.broadcasted_iota(jnp.int32, (1, W_B), 1)
    head_masks = [(lane // DK_B) == h for h in range(H_B)]
    first_of_pair = lax.broadcasted_iota(jnp.int32, (1, 2 * DK_B), 1) < DK_B
    xor = (lax.broadcasted_iota(jnp.int32, (HALF, HALF), 0)
           ^ lax.broadcasted_iota(jnp.int32, (HALF, HALF), 1))
    levels = [1 << i for i in range(TILE.bit_length() - 1)]
    right = {m: (rows & m) != 0 for m in levels}
    same_block = {m: xor < 2 * m for m in levels if 2 * m < HALF}
    return rows, head_masks, first_of_pair, right, same_block


def _hgrn_lower_bound(lb_ref, layer, direction, depth):
    xs = [lb_ref[2 * j + direction:2 * j + direction + 1, :] for j in range(depth)]
    m = functools.reduce(jnp.maximum, xs)
    es = [jnp.exp(x - m) for x in xs]
    return sum(es[1:layer + 1]) / sum(es)


def _hgrn_gates(fpre, lbd):
    t = jnp.exp(-jnp.abs(fpre))
    log_sig = jnp.minimum(fpre, 0.0) - jnp.log(1.0 + t)
    r = 1.0 / (1.0 + t)
    sig_neg = jnp.where(fpre >= 0.0, t * r, r)
    if lbd is None:
        return log_sig * LOG2E, sig_neg
    a = jnp.log(lbd)
    b = jnp.log1p(-lbd) + log_sig
    logf = jnp.maximum(a, b) + jnp.log1p(jnp.exp(-jnp.abs(a - b)))
    return logf * LOG2E, (1.0 - lbd) * sig_neg


def _chain_row(token):
    return SUBLANES * (token % CHAIN_LEN) + token // CHAIN_LEN


def _tile_prefix(g, g_scr, b_scr):
    g_c = _to_chains(g, g_scr)
    run = [g_c[0]]
    for j in range(1, CHAIN_LEN):
        run.append(run[-1] + g_c[j])
    sub = lax.broadcasted_iota(jnp.int32, (SUBLANES, 1), 0)
    incl = run[-1]
    k = 1
    while k < SUBLANES:
        incl = incl + jnp.where(sub >= k, pltpu.roll(incl, k, 0), 0.0)
        k *= 2
    offset = incl - run[-1]
    prefix = _from_chains([r + offset for r in run], b_scr)
    return prefix, incl[SUBLANES - 1:SUBLANES, :]


def _level_exponent(z, g, b_scr, m, reverse, right):
    if m == 1:
        return jnp.where(right, g, 0.0) if not reverse else jnp.where(right, 0.0, g)
    slabs = b_scr.shape[0]

    def ref_rows(token):
        row = _chain_row(token)
        return jnp.concatenate(
            [jnp.broadcast_to(b_scr[s, row:row + 1, :], (SUBLANES, LANES)) for s in range(slabs)],
            axis=1)

    groups = []
    for gq in range(TILE // SUBLANES):
        first = gq * SUBLANES
        zg = z[first:first + SUBLANES]
        if 2 * m >= SUBLANES:
            block = first // (2 * m) * (2 * m)
            ref = ref_rows(block + m - 1)
            if m >= SUBLANES:
                groups.append(zg - ref if first - block >= m else ref - zg)
                continue
        else:
            sub = lax.broadcasted_iota(jnp.int32, (SUBLANES, 1), 0)
            ref = jnp.where(sub < 2 * m, ref_rows(first + m - 1), ref_rows(first + 3 * m - 1))
        groups.append(-jnp.abs(zg - ref))
    return jnp.concatenate(groups, axis=0)


def _hgrn_tile(qh, kk, vv, g, reverse, aux, ones_bd, state_t, scratch):
    rows, head_masks, first_of_pair, right_rows, same_block = aux
    pair_w = 2 * DK_B
    g_scr, b_scr = scratch
    prefix, total = _tile_prefix(g, g_scr, b_scr)
    z = prefix if not reverse else prefix - g
    diag = [[None, None] for _ in range(H_B)]
    off = [None] * H_B
    m = 1
    while m < TILE:
        right = right_rows[m]
        x = jnp.exp2(_level_exponent(z, g, b_scr, m, reverse, right))
        q_rows = right if not reverse else jnp.logical_not(right)
        qz = jnp.where(q_rows, (qh * x).astype(BF16), 0.0)
        kz = jnp.where(q_rows, 0.0, (kk * x).astype(BF16))
        for h in range(H_B):
            ls = slice((h // 2) * pair_w, (h // 2 + 1) * pair_w)
            qm = jnp.where(first_of_pair, qz[:, ls], 0.0) if h % 2 == 0 else \
                jnp.where(first_of_pair, 0.0, qz[:, ls])
            kh = kz[:, ls]
            if m < HALF:
                for i in range(2):
                    rs = slice(i * HALF, (i + 1) * HALF)
                    pm = _dot_nt(qm[rs], kh[rs])
                    if m in same_block:
                        pm = jnp.where(same_block[m], pm, 0.0)
                    diag[h][i] = pm if diag[h][i] is None else diag[h][i] + pm
            else:
                lo, hi = slice(0, HALF), slice(HALF, TILE)
                off[h] = _dot_nt(qm[hi], kh[lo]) if not reverse else _dot_nt(qm[lo], kh[hi])
        m *= 2
    vb = vv.astype(BF16)
    top = bot = None
    for h in range(H_B):
        vh = jnp.where(head_masks[h], vb, 0.0)
        d0, d1, of = (a.astype(BF16) for a in (diag[h][0], diag[h][1], off[h]))
        if not reverse:
            t = _dot(d0, vh[0:HALF])
            b = _dot(jnp.concatenate([of, d1], axis=1), vh)
        else:
            t = _dot(jnp.concatenate([d0, of], axis=1), vh)
            b = _dot(d1, vh[HALF:])
        top = t if top is None else top + t
        bot = b if bot is None else bot + b
    out = _group_sum(qh * kk, ones_bd) * vv + jnp.concatenate([top, bot], axis=0)
    q_decay, k_decay = (z, total - z) if not reverse else (total - z, z)
    if state_t is not None:
        out = out + _dot_nt((qh * jnp.exp2(q_decay)).astype(BF16), state_t.astype(BF16))
    k_hat = (kk * jnp.exp2(k_decay)).astype(BF16)
    return out, k_hat, total


def _block_diag_mask():
    r = lax.broadcasted_iota(jnp.int32, (W_B, W_B), 0) // DK_B
    c = lax.broadcasted_iota(jnp.int32, (W_B, W_B), 1) // DV_B
    return r == c


def _conv(x, prev_row, next_rows, cw_ref, cb_ref, rows):
    x_m1 = jnp.where(rows == 0, prev_row, pltpu.roll(x, 1, 0))
    x_p1 = jnp.where(rows == TILE - 1, next_rows[0:1], pltpu.roll(x, TILE - 1, 0))
    x_p2 = jnp.where(rows == TILE - 2, next_rows[0:1],
                     jnp.where(rows == TILE - 1, next_rows[1:2], pltpu.roll(x, TILE - 2, 0)))
    y = cb_ref[...] + x_m1 * cw_ref[0:1, :]
    y = y + x * cw_ref[1:2, :]
    y = y + x_p1 * cw_ref[2:3, :]
    return y + x_p2 * cw_ref[3:4, :]


def _chain_address(g):
    per_chain = CHAIN_LEN // SUBLANES
    return SUBLANES * SUBLANES * (g % per_chain) + g // per_chain, SUBLANES


def _to_chains(x, scr):
    slabs = scr.shape[0]
    for s in range(slabs):
        for g in range(TILE // SUBLANES):
            start, stride = _chain_address(g)
            scr[s, pl.ds(start, SUBLANES, stride=stride), :] = (
                x[g * SUBLANES:(g + 1) * SUBLANES, s * LANES:(s + 1) * LANES])
    return [jnp.concatenate([scr[s, j * SUBLANES:(j + 1) * SUBLANES, :] for s in range(slabs)],
                            axis=1) for j in range(CHAIN_LEN)]


def _from_chains(chains, scr):
    slabs = scr.shape[0]
    for s in range(slabs):
        for j, c in enumerate(chains):
            scr[s, j * SUBLANES:(j + 1) * SUBLANES, :] = c[:, s * LANES:(s + 1) * LANES]
    groups = []
    for g in range(TILE // SUBLANES):
        start, stride = _chain_address(g)
        groups.append(jnp.concatenate(
            [scr[s, pl.ds(start, SUBLANES, stride=stride), :] for s in range(slabs)], axis=1))
    return jnp.concatenate(groups, axis=0)


def _rglru_tile(xc, r_pre, i_pre, lam_row, reverse, h_in, scratch):
    a_scr, u_scr, h_scr = scratch
    r = _sigmoid(r_pre)
    i = _sigmoid(i_pre)
    log_a = -RG_C * r * _softplus(-lam_row)
    a = jnp.exp(log_a)
    u = jnp.sqrt(jnp.tanh(-log_a) * (a * a + 1.0)) * (i * xc)
    a_c = _to_chains(a, a_scr)
    u_c = _to_chains(u, u_scr)
    order = list(reversed(range(CHAIN_LEN))) if reverse else list(range(CHAIN_LEN))
    h_c = [None] * CHAIN_LEN
    p_c = [None] * CHAIN_LEN
    h_c[order[0]], p_c[order[0]] = u_c[order[0]], a_c[order[0]]
    for prev, j in zip(order[:-1], order[1:]):
        h_c[j] = a_c[j] * h_c[prev] + u_c[j]
        p_c[j] = a_c[j] * p_c[prev]
    p_tot, h_tot = p_c[order[-1]], h_c[order[-1]]
    sub = lax.broadcasted_iota(jnp.int32, (SUBLANES, 1), 0)
    k = 1
    while k < SUBLANES:
        if not reverse:
            valid = sub >= k
            p_s, h_s = pltpu.roll(p_tot, k, 0), pltpu.roll(h_tot, k, 0)
        else:
            valid = sub < SUBLANES - k
            p_s, h_s = pltpu.roll(p_tot, SUBLANES - k, 0), pltpu.roll(h_tot, SUBLANES - k, 0)
        h_tot = h_tot + p_tot * jnp.where(valid, h_s, 0.0)
        p_tot = p_tot * jnp.where(valid, p_s, 1.0)
        k *= 2
    if h_in is not None:
        h_tot = h_tot + p_tot * h_in
    if not reverse:
        carry = jnp.where(sub == 0, 0.0 if h_in is None else h_in, pltpu.roll(h_tot, 1, 0))
        h_out = h_tot[SUBLANES - 1:SUBLANES, :]
    else:
        carry = jnp.where(sub == SUBLANES - 1, 0.0 if h_in is None else h_in,
                          pltpu.roll(h_tot, SUBLANES - 1, 0))
        h_out = h_tot[0:1, :]
    h_c = [h + p * carry for h, p in zip(h_c, p_c)]
    return _from_chains(h_c, h_scr), h_out


def _rec_inputs(refs):
    qb, ff, fb, ib, gb, xcol, gcol = [r[...] for r in refs]
    return _silu(qb), ff, fb, ib, gb, xcol, gcol


def _hgrn_out(o_sum, gb, onorm_gain, ones_bd):
    ms = _group_sum(o_sum * o_sum, ones_bd) * (1.0 / DV_B)
    return o_sum * lax.rsqrt(ms + EPS) * onorm_gain * _silu(gb)


def _rec_ctx_kernel(q_ref, ff_ref, fb_ref, i_ref, g_ref, x_ref, gc_ref, lb_ref, on_ref, cw_ref,
                    cb_ref, wg_ref, bg_ref, lam_ref, ones_ref, ob_ref, oc_ref, st_ref,
                    hs_ref, *scan_scratch, layer, depth):
    aux = _tile_aux()
    rows = aux[0]
    ones_bd = ones_ref[...]
    qh, ff, fb, ib, gb, xcol, gcol = _rec_inputs((q_ref, ff_ref, fb_ref, i_ref, g_ref, x_ref, gc_ref))
    o_sum = None
    for direction, fpre in enumerate((ff, fb)):
        lbd = None if layer == 0 else _hgrn_lower_bound(lb_ref, layer, direction, depth)
        g, kk = _hgrn_gates(fpre, lbd)
        o, k_hat, _ = _hgrn_tile(qh, kk, ib, g, direction == 1, aux, ones_bd, None,
                                 scan_scratch[3 + 2 * direction:5 + 2 * direction])
        o_sum = o if o_sum is None else o_sum + o
        full = _dot_tn(k_hat, ib.astype(BF16))
        for h in range(H_B):
            blk = full[h * DK_B:(h + 1) * DK_B, (h // 2) * 2 * DV_B:(h // 2 + 1) * 2 * DV_B]
            if h % 2:
                blk = pltpu.roll(blk, DV_B, 1)
            st_ref[0, direction, h] = blk[:, 0:DV_B]
    ob_ref[...] = _hgrn_out(o_sum, gb, on_ref[...], ones_bd)

    zero_row = jnp.zeros((1, W_C), F32)
    xc = _conv(xcol, zero_row, jnp.zeros((2, W_C), F32), cw_ref, cb_ref, rows)
    gates = _dot(xc.astype(BF16), wg_ref[...]) + bg_ref[...]
    y_f, h_f = _rglru_tile(xc, gates[:, 0:W_C], gates[:, W_C:2 * W_C], lam_ref[0:1, :], False,
                           None, scan_scratch[0:3])
    y_b, h_b = _rglru_tile(xc, gates[:, 2 * W_C:3 * W_C], gates[:, 3 * W_C:], lam_ref[1:2, :], True,
                           None, scan_scratch[0:3])
    oc_ref[...] = (y_f + y_b) * _gelu_tanh(gcol)
    hs_ref[0, 0:1, :] = h_f
    hs_ref[0, 1:2, :] = h_b


def _scan_scratch(hgrn_directions):
    return [pltpu.VMEM((W_C // LANES, TILE, LANES), F32) for _ in range(3 + 2 * hgrn_directions)]


def _col_spec(col, row_map):
    return pl.BlockSpec((TILE, W_B), lambda *idx: (row_map(*idx), col))


COL_QB, COL_FF, COL_FB, COL_IB, COL_GB, COL_XC, COL_GC = range(7)
_REC_COLS = (COL_QB, COL_FF, COL_FB, COL_IB, COL_GB, COL_XC, COL_GC)


def _full_spec(a):
    nd = a.ndim
    return pl.BlockSpec(a.shape, lambda *idx: (0,) * nd)


def _recurrence_ctx(proj, layer, depth, consts):
    t = proj.shape[0]
    n = t // TILE
    return pl.pallas_call(
        functools.partial(_rec_ctx_kernel, layer=layer, depth=depth),
        grid=(n,),
        in_specs=[_col_spec(c, lambda b: b) for c in _REC_COLS] + [_full_spec(a) for a in consts],
        out_specs=[
            pl.BlockSpec((TILE, W_B), lambda b: (b, 0)),
            pl.BlockSpec((TILE, W_C), lambda b: (b, 0)),
            pl.BlockSpec((1, 2, H_B, DK_B, DV_B), lambda b: (b, 0, 0, 0, 0)),
            pl.BlockSpec((1, 2, W_C), lambda b: (b, 0, 0)),
        ],
        out_shape=[
            jax.ShapeDtypeStruct((t, W_B), F32),
            jax.ShapeDtypeStruct((t, W_C), F32),
            jax.ShapeDtypeStruct((n, 2, H_B, DK_B, DV_B), F32),
            jax.ShapeDtypeStruct((n, 2, W_C), F32),
        ],
        scratch_shapes=_scan_scratch(2),
        compiler_params=_params(("arbitrary",)),
        name="rec_ctx",
    )(*([proj] * len(_REC_COLS)), *consts)


def _rec_lat_kernel(q_ref, f_ref, i_ref, g_ref, x_ref, xp_ref, xn_ref, gc_ref, s0_ref, h0_ref,
                    *rest, layer, depth, direction, n_tiles):
    if direction == 0:
        (lb_ref, on_ref, cw_ref, cb_ref, wg_ref, bg_ref, lam_ref, ones_ref,
         o_ref, y_ref, state, hstate, *scan_scratch) = rest
    else:
        (of_ref, yf_ref, lb_ref, on_ref, cw_ref, cb_ref, wg_ref, bg_ref, lam_ref, ones_ref,
         o_ref, y_ref, state, hstate, *scan_scratch) = rest
    step = pl.program_id(1)
    tile = step if direction == 0 else n_tiles - 1 - step
    reverse = direction == 1

    @pl.when(step == 0)
    def _():
        state[...] = s0_ref[0, 0]
        hstate[...] = h0_ref[0, 0]

    aux = _tile_aux()
    rows = aux[0]
    ones_bd = ones_ref[...]
    qh = _silu(q_ref[...])
    ib = i_ref[...]
    lbd = None if layer == 0 else _hgrn_lower_bound(lb_ref, layer, direction, depth)
    g, kk = _hgrn_gates(f_ref[...], lbd)
    o, k_hat, total = _hgrn_tile(qh, kk, ib, g, reverse, aux, ones_bd, state[...],
                                 scan_scratch[3:5])
    upd = jnp.where(_block_diag_mask(), _dot_tn(ib.astype(BF16), k_hat), 0.0)
    state[...] = state[...] * jnp.exp2(total) + upd

    prev_row = jnp.where(tile == 0, 0.0, xp_ref[TILE - 1:TILE, :])
    next_rows = jnp.where(tile == n_tiles - 1, 0.0, xn_ref[0:2, :])
    xc = _conv(x_ref[...], prev_row, next_rows, cw_ref, cb_ref, rows)
    gates = _dot(xc.astype(BF16), wg_ref[...]) + bg_ref[...]
    c0 = 2 * W_C * direction
    y, h_next = _rglru_tile(xc, gates[:, c0:c0 + W_C], gates[:, c0 + W_C:c0 + 2 * W_C],
                            lam_ref[direction:direction + 1, :], reverse, hstate[...],
                            scan_scratch[0:3])
    hstate[...] = h_next

    if direction == 0:
        o_ref[...] = o
        y_ref[...] = y
    else:
        o_ref[...] = _hgrn_out(of_ref[...] + o, g_ref[...], on_ref[...], ones_bd)
        y_ref[...] = (yf_ref[...] + y) * _gelu_tanh(gc_ref[...])


def _recurrence_lat(proj, n_batch, layer, depth, direction, state0, h0, prev, consts):
    t = proj.shape[0]
    n_tiles = t // (TILE * n_batch)

    def tile_of(b, s):
        return s if direction == 0 else n_tiles - 1 - s

    def row(b, s):
        return b * n_tiles + tile_of(b, s)

    def row_prev(b, s):
        return b * n_tiles + jnp.maximum(tile_of(b, s) - 1, 0)

    def row_next(b, s):
        return b * n_tiles + jnp.minimum(tile_of(b, s) + 1, n_tiles - 1)

    f_col = COL_FF if direction == 0 else COL_FB
    in_specs = [_col_spec(COL_QB, row), _col_spec(f_col, row), _col_spec(COL_IB, row),
                _col_spec(COL_GB, row), _col_spec(COL_XC, row), _col_spec(COL_XC, row_prev),
                _col_spec(COL_XC, row_next), _col_spec(COL_GC, row),
                pl.BlockSpec((1, 1, W_B, W_B), lambda b, s: (b, direction, 0, 0)),
                pl.BlockSpec((1, 1, 1, W_C), lambda b, s: (b, direction, 0, 0))]
    args = [proj] * 8 + [state0, h0]
    if direction == 1:
        in_specs += [pl.BlockSpec((TILE, W_B), lambda b, s: (row(b, s), 0)),
                     pl.BlockSpec((TILE, W_C), lambda b, s: (row(b, s), 0))]
        args += list(prev)
    in_specs += [_full_spec(a) for a in consts]
    args += list(consts)
    return pl.pallas_call(
        functools.partial(_rec_lat_kernel, layer=layer, depth=depth, direction=direction,
                          n_tiles=n_tiles),
        grid=(n_batch, n_tiles),
        in_specs=in_specs,
        out_specs=[pl.BlockSpec((TILE, W_B), lambda b, s: (row(b, s), 0)),
                   pl.BlockSpec((TILE, W_C), lambda b, s: (row(b, s), 0))],
        out_shape=[jax.ShapeDtypeStruct((t, W_B), F32), jax.ShapeDtypeStruct((t, W_C), F32)],
        scratch_shapes=[pltpu.VMEM((W_B, W_B), F32), pltpu.VMEM((1, W_C), F32)] + _scan_scratch(1),
        compiler_params=_params(("arbitrary", "arbitrary")),
        name="rec_lat_fwd" if direction == 0 else "rec_lat_bwd",
    )(*args)


def _out_kernel(x_ref, oa_ref, ob_ref, oc_ref, mod_ref, n2_ref, wo_ref, w1_ref, w2_ref, y_ref):
    x = x_ref[...]
    d = x.shape[-1]
    g1 = mod_ref[0, :, 2 * d:3 * d]
    sh2 = mod_ref[0, :, 3 * d:4 * d]
    sc2 = mod_ref[0, :, 4 * d:5 * d]
    g2 = mod_ref[0, :, 5 * d:6 * d]
    wa = H_A * DV_A
    mix = _dot(oa_ref[...].astype(BF16), wo_ref[0:wa, :])
    mix = mix + _dot(ob_ref[...].astype(BF16), wo_ref[wa:wa + W_B, :])
    mix = mix + _dot(oc_ref[...].astype(BF16), wo_ref[wa + W_B:, :])
    x1 = x + g1 * mix
    h = (_rms_rows(x1, n2_ref[...]) * (1.0 + sc2) + sh2).astype(BF16)
    acc = jnp.zeros_like(x)
    for c in range(w1_ref.shape[1] // D_FF_CHUNK):
        cs = slice(c * D_FF_CHUNK, (c + 1) * D_FF_CHUNK)
        f = jnp.maximum(_dot(h, w1_ref[:, cs]), 0.0)
        acc = acc + _dot((f * f).astype(BF16), w2_ref[cs, :])
    y_ref[...] = x1 + g2 * acc


def _out_mlp(x, oa, ob, oc, mod, tiles_per_mod, n2, w_out, w_ff1, w_ff2, layer):
    t, d = x.shape

    def resident(w):
        return pl.BlockSpec((None,) + w.shape[1:], lambda i: (layer, 0, 0),
                            pipeline_mode=pl.Buffered(1))

    return pl.pallas_call(
        _out_kernel,
        grid=(t // TILE,),
        in_specs=[
            pl.BlockSpec((TILE, d), lambda i: (i, 0)),
            pl.BlockSpec((TILE, oa.shape[1]), lambda i: (i, 0)),
            pl.BlockSpec((TILE, ob.shape[1]), lambda i: (i, 0)),
            pl.BlockSpec((TILE, oc.shape[1]), lambda i: (i, 0)),
            pl.BlockSpec((1, 1, N_MOD * d), lambda i: (i // tiles_per_mod, 0, 0)),
            pl.BlockSpec((1, d), lambda i: (0, 0)),
            resident(w_out),
            resident(w_ff1),
            resident(w_ff2),
        ],
        out_specs=pl.BlockSpec((TILE, d), lambda i: (i, 0)),
        out_shape=jax.ShapeDtypeStruct((t, d), F32),
        compiler_params=_params(("arbitrary",)),
        name="out_mlp",
    )(x, oa, ob, oc, mod, n2, w_out, w_ff1, w_ff2)


def _ones_block_diag(n, group):
    idx = np.arange(n) // group
    return jnp.asarray((idx[:, None] == idx[None, :]).astype(np.float32), dtype=BF16)


def _rope_tables(seq):
    nf = DQK_A // 4
    inv = ROPE_BASE ** (-jnp.arange(nf, dtype=F32) / nf)
    pos = jnp.arange(seq)
    ang_r = (pos // GRID_W).astype(F32)[:, None] * inv
    ang_c = (pos % GRID_W).astype(F32)[:, None] * inv
    z = jnp.zeros_like(ang_r)
    cos = jnp.concatenate([jnp.cos(ang_r)] * 2 + [jnp.cos(ang_c)] * 2, axis=-1)
    sin_lo = jnp.concatenate([-jnp.sin(ang_r), z, -jnp.sin(ang_c), z], axis=-1)
    sin_hi = jnp.concatenate([z, jnp.sin(ang_r), z, jnp.sin(ang_c)], axis=-1)
    return tuple(jnp.tile(a, (1, 2)) for a in (cos, sin_lo, sin_hi))


def _gate_weights(rg_w_l):
    eye = jnp.eye(H_C, dtype=rg_w_l.dtype)
    w = jnp.einsum('dghij,hk->dghikj', rg_w_l, eye)
    w = w.reshape(2, 2, W_C, W_C)
    return jnp.transpose(w, (2, 0, 1, 3)).reshape(W_C, 4 * W_C).astype(BF16)


def _state_block_diag_t(s):
    eye = jnp.eye(H_B, dtype=s.dtype)
    w = jnp.einsum('...hdv,hk->...hvkd', s, eye)
    return w.reshape(s.shape[:-3] + (W_B, W_B))


def kernel(x_prompt, x_sample, cache_k, cache_v, state_hgrn, state_rglru, c, c_ctx, w_mod, b_mod,
           norm1, norm2, w_in, w_out, qk_norm, diff_lambda, subln, hgrn_lb, hgrn_onorm, conv_w,
           conv_b, rg_w, rg_b, rg_lambda, w_ff1, w_ff2):
    batch, seq, d = x_prompt.shape
    dec_batch, dec_seq, _ = x_sample.shape
    depth = w_in.shape[0]
    past = cache_k.shape[2]
    wq = H_A * 2 * DQK_A

    c8 = jnp.concatenate([c_ctx[None], c, jnp.zeros((8 - 1 - dec_batch, d), F32)], axis=0)
    mod_all = _modulation(c8, w_mod, b_mod)

    assert DQK_A == DV_B and seq == TILE
    ones_b = _ones_block_diag(W_B, DV_B)
    rope_tabs = _rope_tables(dec_seq)
    ck = cache_k.reshape(dec_batch, depth, past, wq)
    cv = cache_v.reshape(dec_batch, depth, past, H_A * DV_A)
    lb2 = hgrn_lb.reshape(depth * 2, W_B)

    w_in_b, w_out_b, w1_b, w2_b = (w.astype(BF16) for w in (w_in, w_out, w_ff1, w_ff2))

    yp = x_prompt.reshape(batch * seq, d)
    ys = x_sample.reshape(dec_batch * dec_seq, d)
    caches, shs, srs = (), [], []
    for l in range(depth):
        lam_init = 0.8 - 0.6 * math.exp(-0.3 * l)
        mod_ctx = mod_all[l, 0:1][:, None, :]
        mod_lat = mod_all[l, 1:1 + dec_batch][:, None, :]
        n1 = norm1[l][None]
        n2 = norm2[l][None]
        gq = jnp.tile(qk_norm[l, 0], H_A * 2)[None] * (DQK_A ** -0.5 * LOG2E)
        gk = jnp.tile(qk_norm[l, 1], H_A * 2)[None]
        dl = diff_lambda[l]
        sg = subln[l][None]
        consts = (lb2, jnp.tile(hgrn_onorm[l], H_B)[None], conv_w[l], conv_b[l][None],
                  _gate_weights(rg_w[l]), rg_b[l].reshape(1, 4 * W_C), rg_lambda[l], ones_b)

        q, k, v_all, rest, kt_all = _in_projection(yp, mod_ctx, batch * seq // TILE, n1, w_in_b, l,
                                                   gq, gk, ones_b, caches=caches)
        caches = (kt_all, v_all)
        oa = _attention_ctx(q, k, v_all, l, seq, dl, sg, lam_init)
        ob, oc, st, hs = _recurrence_ctx(rest, l, depth, consts)
        shs.append(st)
        srs.append(hs)
        yp = _out_mlp(yp, oa, ob, oc, mod_ctx, batch * seq // TILE, n2, w_out_b, w1_b, w2_b, l)

        q, k, v, rest = _in_projection(ys, mod_lat, dec_seq // TILE, n1, w_in_b, l, gq, gk, ones_b)
        oa = _attention_lat(q, k, v, dec_batch, ck, cv, l, rope_tabs, dl, sg, lam_init)
        s0 = _state_block_diag_t(state_hgrn[:, l])
        h0 = state_rglru[:, l][:, :, None, :]
        fwd = _recurrence_lat(rest, dec_batch, l, depth, 0, s0, h0, None, consts)
        ob, oc = _recurrence_lat(rest, dec_batch, l, depth, 1, s0, h0, fwd, consts)
        ys = _out_mlp(ys, oa, ob, oc, mod_lat, dec_seq // TILE, n2, w_out_b, w1_b, w2_b, l)

    kt_all, v_all = caches
    new_k = jnp.transpose(kt_all.reshape(batch, depth, H_A, 2, DQK_A, seq), (0, 1, 5, 2, 3, 4))
    new_v = v_all.reshape(batch, depth, seq, H_A, DV_A)
    return (yp.reshape(batch, seq, d), ys.reshape(dec_batch, dec_seq, d), new_k, new_v,
            jnp.stack(shs, axis=1), jnp.stack(srs, axis=1))
```

```python
import functools
import math

import numpy as np
import jax
import jax.numpy as jnp
from jax import lax
from jax.experimental import pallas as pl
from jax.experimental.pallas import tpu as pltpu

F32 = jnp.float32
BF16 = jnp.bfloat16

GRID_W = 64
H_A, DQK_A, DV_A = 4, 64, 128
H_B, DK_B, DV_B = 4, 64, 64
W_B = H_B * DV_B
H_C, W_C = 4, 256
CONV_K = 4
RG_C = 8.0
ROPE_BASE = 10000.0
EPS = 1e-6
N_MOD = 6
D_FF_CHUNK = 1024

TILE = 256
HALF = TILE // 2
ROW_TILE = 512
ATTN_SEQS_PER_STEP = 4
SUBLANES = 8
LANES = 128
CHAIN_LEN = TILE // SUBLANES
LOG2E = math.log2(math.e)
V7X_VMEM_BYTES = 64 * 1024 * 1024
VMEM_LIMIT = V7X_VMEM_BYTES - 8 * 1024 * 1024


def _dot(a, b):
    return jnp.dot(a, b, preferred_element_type=F32)


def _dot_nt(a, b):
    return lax.dot_general(a, b, (((1,), (1,)), ((), ())), preferred_element_type=F32)


def _dot_tn(a, b):
    return lax.dot_general(a, b, (((0,), (0,)), ((), ())), preferred_element_type=F32)


def _group_sum(x, ones_bd):
    hi = x.astype(BF16)
    lo = (x - hi.astype(F32)).astype(BF16)
    return _dot(hi, ones_bd) + _dot(lo, ones_bd)


def _rms_rows(x, gain):
    return x * lax.rsqrt(jnp.mean(x * x, axis=-1, keepdims=True) + EPS) * gain


def _sigmoid(x):
    return 0.5 * jnp.tanh(0.5 * x) + 0.5


def _silu(x):
    return x * _sigmoid(x)


def _softplus(x):
    return jnp.maximum(x, 0.0) + jnp.log1p(jnp.exp(-jnp.abs(x)))


def _log_sigmoid(x):
    return jnp.minimum(x, 0.0) - jnp.log1p(jnp.exp(-jnp.abs(x)))


def _gelu_tanh(x):
    return x * (0.5 * (1.0 + jnp.tanh(math.sqrt(2.0 / math.pi) * (x + 0.044715 * (x * x * x)))))


def _params(semantics, flags=None):
    return pltpu.CompilerParams(dimension_semantics=semantics, vmem_limit_bytes=VMEM_LIMIT,
                                flags=flags)


def _mod_kernel(c_ref, w_ref, b_ref, o_ref):
    c = c_ref[...]
    o_ref[0] = _dot(_silu(c).astype(BF16), w_ref[0].astype(BF16)) + b_ref[0]


def _modulation(c8, w_mod, b_mod):
    depth, d, n = w_mod.shape
    tn = d
    return pl.pallas_call(
        _mod_kernel,
        grid=(depth, n // tn),
        in_specs=[
            pl.BlockSpec((8, d), lambda l, j: (0, 0)),
            pl.BlockSpec((1, d, tn), lambda l, j: (l, 0, j)),
            pl.BlockSpec((1, 1, tn), lambda l, j: (l, 0, j)),
        ],
        out_specs=pl.BlockSpec((1, 8, tn), lambda l, j: (l, 0, j)),
        out_shape=jax.ShapeDtypeStruct((depth, 8, n), F32),
        compiler_params=_params(("arbitrary", "arbitrary")),
        name="mod",
    )(c8, w_mod, b_mod.reshape(depth, 1, n))


def _proj_kernel(x_ref, mod_ref, n1_ref, w_ref, gq_ref, gk_ref, ones_ref, *refs, layers_before):
    prev = refs[:2] if layers_before else ()
    q_ref, k_ref, v_ref, r_ref, *kt_ref = refs[len(prev):]
    x = x_ref[...]
    d = x.shape[-1]
    sh = mod_ref[0, :, 0:d]
    sc = mod_ref[0, :, d:2 * d]
    h = _rms_rows(x, n1_ref[...]) * (1.0 + sc) + sh
    p = _dot(h.astype(BF16), w_ref[...])
    ones_bd = ones_ref[...]
    wq = H_A * 2 * DQK_A

    def qk_norm(z, gain):
        z2 = z * z
        nb = ones_bd.shape[0]
        ms = jnp.concatenate([_group_sum(z2[:, c:c + nb], ones_bd) for c in range(0, wq, nb)],
                             axis=1) * (1.0 / DQK_A)
        return z * lax.rsqrt(ms + EPS) * gain

    wv = H_A * DV_A
    q_ref[...] = qk_norm(p[:, 0:wq], gq_ref[...])
    kn = qk_norm(p[:, wq:2 * wq], gk_ref[...])
    k_ref[...] = kn
    if kt_ref:
        for s in range(ROW_TILE // TILE):
            kt_ref[0][s, layers_before] = kn[s * TILE:(s + 1) * TILE].T
        if prev:
            ktp_ref, vp_ref = prev
            kt_ref[0][:, 0:layers_before] = ktp_ref[...]
            v_ref[:, 0:layers_before] = vp_ref[...]
    for s in range(ROW_TILE // TILE):
        v_dst = v_ref.at[s, layers_before] if kt_ref else v_ref.at[pl.ds(s * TILE * H_A, TILE * H_A)]
        for h in range(H_A):
            v_dst[pl.ds(h, TILE, stride=H_A), :] = (
                p[s * TILE:(s + 1) * TILE, 2 * wq + h * DV_A:2 * wq + (h + 1) * DV_A])
    r_ref[...] = p[:, 2 * wq + wv:]


def _in_projection(x, mod, tokens_per_mod, n1, w_in, layer, gq, gk, ones_bd, caches=None):
    t, d = x.shape
    layers_before = caches[0].shape[1] if caches else 0
    n_in = w_in.shape[2]
    wq = gq.shape[1]
    wv = H_A * DV_A
    widths = (wq, wq, wv, n_in - 2 * wq - wv)
    seqs = ROW_TILE // TILE
    out_specs = [pl.BlockSpec((ROW_TILE, w), lambda i: (i, 0)) for w in widths]
    out_shape = [jax.ShapeDtypeStruct((t, w), F32) for w in widths]
    out_specs[2] = pl.BlockSpec((ROW_TILE * H_A, DV_A), lambda i: (i, 0))
    out_shape[2] = jax.ShapeDtypeStruct((t * H_A, DV_A), F32)
    cache_specs = []
    if caches is not None:
        n_seq, nl = t // TILE, layers_before + 1
        out_specs[2] = pl.BlockSpec((seqs, nl, TILE * H_A, DV_A), lambda i: (i, 0, 0, 0))
        out_shape[2] = jax.ShapeDtypeStruct((n_seq, nl, TILE * H_A, DV_A), F32)
        out_specs.append(pl.BlockSpec((seqs, nl, wq, TILE), lambda i: (i, 0, 0, 0)))
        out_shape.append(jax.ShapeDtypeStruct((n_seq, nl, wq, TILE), F32))
        if caches:
            cache_specs = [
                pl.BlockSpec((seqs, layers_before, wq, TILE), lambda i: (i, 0, 0, 0)),
                pl.BlockSpec((seqs, layers_before, TILE * H_A, DV_A), lambda i: (i, 0, 0, 0))]
    return pl.pallas_call(
        functools.partial(_proj_kernel, layers_before=layers_before),
        grid=(t // ROW_TILE,),
        in_specs=[
            pl.BlockSpec((ROW_TILE, d), lambda i: (i, 0)),
            pl.BlockSpec((1, 1, N_MOD * d), lambda i: (i * ROW_TILE // tokens_per_mod, 0, 0)),
            pl.BlockSpec((1, d), lambda i: (0, 0)),
            pl.BlockSpec((None, d, n_in), lambda i: (layer, 0, 0)),
            pl.BlockSpec((1, wq), lambda i: (0, 0)),
            pl.BlockSpec((1, wq), lambda i: (0, 0)),
            pl.BlockSpec(ones_bd.shape, lambda i: (0, 0)),
        ] + cache_specs,
        out_specs=out_specs,
        out_shape=out_shape,
        compiler_params=_params(("arbitrary",)),
        name="proj",
    )(x, mod, n1, w_in, gq, gk, ones_bd, *(caches or ()))


def _diff_lambda(dl_ref, lam_init):
    lv = dl_ref[...]
    a = jnp.sum(lv[0:1] * lv[1:2], axis=-1, keepdims=True)
    b = jnp.sum(lv[2:3] * lv[3:4], axis=-1, keepdims=True)
    return jnp.exp(a) - jnp.exp(b) + lam_init


def _diff_attention(q, k_of, vext_of, lam, sub_gain, lam_init, o_ref):
    hw = 2 * DQK_A
    first = lax.broadcasted_iota(jnp.int32, (1, hw), 1) < DQK_A
    for h in range(H_A):
        hs = slice(h * hw, (h + 1) * hw)
        qh = q[:, hs]
        kh = k_of(h)
        vext = vext_of(h)
        maps = []
        for qm in (jnp.where(first, qh, 0.0), jnp.where(first, 0.0, qh)):
            s = _dot_nt(qm.astype(BF16), kh)
            e = jnp.exp2(s - jnp.max(s, axis=-1, keepdims=True)).astype(BF16)
            oe = _dot(e, vext)
            maps.append(oe[:, 0:DV_A] / oe[:, DV_A:])
        oh = maps[0] - lam * maps[1]
        o_ref[:, hs] = _rms_rows(oh, sub_gain) * (1.0 - lam_init)


def _attn_ctx_kernel(q_ref, k_ref, v_ref, dl_ref, sg_ref, o_ref, *, lam_init, seq):
    lam = _diff_lambda(dl_ref, lam_init)
    ones = jnp.ones((seq, DV_A), BF16)
    hw = 2 * DQK_A
    for s in range(v_ref.shape[0]):
        rows = pl.ds(s * seq, seq)
        k = k_ref[rows, :].astype(BF16)

        def vext(h, s=s):
            return jnp.concatenate(
                [v_ref[s, pl.ds(h, seq, stride=H_A), :].astype(BF16), ones], axis=1)

        _diff_attention(q_ref[rows, :], lambda h, k=k: k[:, h * hw:(h + 1) * hw], vext, lam,
                        sg_ref[...], lam_init, o_ref.at[rows])


def _attention_ctx(q, k, v_all, layer, seq, dl, sg, lam_init):
    t = q.shape[0]
    w = H_A * DV_A
    per_step = ATTN_SEQS_PER_STEP
    return pl.pallas_call(
        functools.partial(_attn_ctx_kernel, lam_init=lam_init, seq=seq),
        grid=(t // (seq * per_step),),
        in_specs=[
            pl.BlockSpec((per_step * seq, w), lambda b: (b, 0)),
            pl.BlockSpec((per_step * seq, w), lambda b: (b, 0)),
            pl.BlockSpec((per_step, None, seq * H_A, DV_A), lambda b: (b, layer, 0, 0)),
            pl.BlockSpec(dl.shape, lambda b: (0, 0)),
            pl.BlockSpec(sg.shape, lambda b: (0, 0)),
        ],
        out_specs=pl.BlockSpec((per_step * seq, w), lambda b: (b, 0)),
        out_shape=jax.ShapeDtypeStruct((t, w), F32),
        compiler_params=_params(("arbitrary",)),
        name="attn_ctx",
    )(q, k, v_all, dl, sg)


def _rope(x, cos, sin_lo, sin_hi):
    return (x * cos + pltpu.roll(x, 2 * DQK_A - 16, 1) * sin_lo + pltpu.roll(x, 16, 1) * sin_hi)


def _attn_lat_kernel(q_ref, k_ref, v_ref, ck_ref, cv_ref, qcos_ref, qslo_ref, qshi_ref,
                     kcos_ref, kslo_ref, kshi_ref, dl_ref, sg_ref, o_ref, kall, vall, *, lam_init):
    hw = 2 * DQK_A
    n_new = k_ref.shape[0]

    @pl.when(pl.program_id(1) == 0)
    def _():
        kcos, kslo, kshi = kcos_ref[...], kslo_ref[...], kshi_ref[...]
        for h in range(H_A):
            hs = slice(h * hw, (h + 1) * hw)
            kall[0:n_new, hs] = _rope(k_ref[:, hs], kcos, kslo, kshi).astype(BF16)
        kall[n_new:, :] = ck_ref[...].astype(BF16)
        ones = jnp.ones((vall.shape[0], DV_A), BF16)
        for h in range(H_A):
            vs = slice(h * DV_A, (h + 1) * DV_A)
            vall[0:n_new, 2 * h * DV_A:(2 * h + 1) * DV_A] = (
                v_ref[pl.ds(h, n_new, stride=H_A), :].astype(BF16))
            vall[n_new:, 2 * h * DV_A:(2 * h + 1) * DV_A] = cv_ref[:, vs].astype(BF16)
            vall[:, (2 * h + 1) * DV_A:(2 * h + 2) * DV_A] = ones

    lam = _diff_lambda(dl_ref, lam_init)
    qcos, qslo, qshi = qcos_ref[...], qslo_ref[...], qshi_ref[...]
    q = jnp.concatenate(
        [_rope(q_ref[:, h * hw:(h + 1) * hw], qcos, qslo, qshi) for h in range(H_A)], axis=-1)
    _diff_attention(q, lambda h: kall[:, h * hw:(h + 1) * hw],
                    lambda h: vall[:, 2 * h * DV_A:(2 * h + 2) * DV_A], lam, sg_ref[...],
                    lam_init, o_ref)


def _attention_lat(q, k, v, n_batch, cache_k, cache_v, layer, rope_tabs, dl, sg, lam_init):
    t = q.shape[0]
    seq = t // n_batch
    past = cache_k.shape[2]
    w = H_A * DV_A
    hw = 2 * DQK_A
    nq = seq // TILE
    cos, slo, shi = rope_tabs
    tab_q = pl.BlockSpec((TILE, hw), lambda b, j: (j, 0))
    tab_k = pl.BlockSpec((seq, hw), lambda b, j: (0, 0))
    return pl.pallas_call(
        functools.partial(_attn_lat_kernel, lam_init=lam_init),
        grid=(n_batch, nq),
        in_specs=[
            pl.BlockSpec((TILE, w), lambda b, j: (b * nq + j, 0)),
            pl.BlockSpec((seq, w), lambda b, j: (b, 0)),
            pl.BlockSpec((seq * H_A, DV_A), lambda b, j: (b, 0)),
            pl.BlockSpec((None, None, past, w), lambda b, j: (b, layer, 0, 0)),
            pl.BlockSpec((None, None, past, w), lambda b, j: (b, layer, 0, 0)),
            tab_q, tab_q, tab_q, tab_k, tab_k, tab_k,
            pl.BlockSpec(dl.shape, lambda b, j: (0, 0)),
            pl.BlockSpec(sg.shape, lambda b, j: (0, 0)),
        ],
        out_specs=pl.BlockSpec((TILE, w), lambda b, j: (b * nq + j, 0)),
        out_shape=jax.ShapeDtypeStruct((t, w), F32),
        scratch_shapes=[pltpu.VMEM((seq + past, w), BF16), pltpu.VMEM((seq + past, 2 * w), BF16)],
        compiler_params=_params(("arbitrary", "arbitrary")),
        name="attn_lat",
    )(q, k, v, cache_k, cache_v, cos, slo, shi, cos, slo, shi, dl, sg)


def _tile_aux():
    rows = lax.broadcasted_iota(jnp.int32, (TILE, 1), 0)
    lane = lax.broadcasted_iota(jnp.int32, (1, W_B), 1)
    head_masks = [(lane // DK_B) == h for h in range(H_B)]
    first_of_pair = lax.broadcasted_iota(jnp.int32, (1, 2 * DK_B), 1) < DK_B
    xor = (lax.broadcasted_iota(jnp.int32, (HALF, HALF), 0)
           ^ lax.broadcasted_iota(jnp.int32, (HALF, HALF), 1))
    levels = [1 << i for i in range(TILE.bit_length() - 1)]
    right = {m: (rows & m) != 0 for m in levels}
    same_block = {m: xor < 2 * m for m in levels if 2 * m < HALF}
    return rows, head_masks, first_of_pair, right, same_block


def _hgrn_lower_bound(lb_ref, layer, direction, depth):
    xs = [lb_ref[2 * j + direction:2 * j + direction + 1, :] for j in range(depth)]
    m = functools.reduce(jnp.maximum, xs)
    es = [jnp.exp(x - m) for x in xs]
    return sum(es[1:layer + 1]) / sum(es)


def _hgrn_gates(fpre, lbd):
    t = jnp.exp(-jnp.abs(fpre))
    log_sig = jnp.minimum(fpre, 0.0) - jnp.log(1.0 + t)
    r = 1.0 / (1.0 + t)
    sig_neg = jnp.where(fpre >= 0.0, t * r, r)
    if lbd is None:
        return log_sig * LOG2E, sig_neg
    a = jnp.log(lbd)
    b = jnp.log1p(-lbd) + log_sig
    logf = jnp.maximum(a, b) + jnp.log1p(jnp.exp(-jnp.abs(a - b)))
    return logf * LOG2E, (1.0 - lbd) * sig_neg


def _chain_row(token):
    return SUBLANES * (token % CHAIN_LEN) + token // CHAIN_LEN


def _tile_prefix(g, g_scr, b_scr):
    g_c = _to_chains(g, g_scr)
    run = [g_c[0]]
    for j in range(1, CHAIN_LEN):
        run.append(run[-1] + g_c[j])
    sub = lax.broadcasted_iota(jnp.int32, (SUBLANES, 1), 0)
    incl = run[-1]
    k = 1
    while k < SUBLANES:
        incl = incl + jnp.where(sub >= k, pltpu.roll(incl, k, 0), 0.0)
        k *= 2
    offset = incl - run[-1]
    prefix = _from_chains([r + offset for r in run], b_scr)
    return prefix, incl[SUBLANES - 1:SUBLANES, :]


def _level_exponent(z, g, b_scr, m, reverse, right):
    if m == 1:
        return jnp.where(right, g, 0.0) if not reverse else jnp.where(right, 0.0, g)
    slabs = b_scr.shape[0]

    def ref_rows(token):
        row = _chain_row(token)
        return jnp.concatenate(
            [jnp.broadcast_to(b_scr[s, row:row + 1, :], (SUBLANES, LANES)) for s in range(slabs)],
            axis=1)

    groups = []
    for gq in range(TILE // SUBLANES):
        first = gq * SUBLANES
        zg = z[first:first + SUBLANES]
        if 2 * m >= SUBLANES:
            block = first // (2 * m) * (2 * m)
            ref = ref_rows(block + m - 1)
            if m >= SUBLANES:
                groups.append(zg - ref if first - block >= m else ref - zg)
                continue
        else:
            sub = lax.broadcasted_iota(jnp.int32, (SUBLANES, 1), 0)
            ref = jnp.where(sub < 2 * m, ref_rows(first + m - 1), ref_rows(first + 3 * m - 1))
        groups.append(-jnp.abs(zg - ref))
    return jnp.concatenate(groups, axis=0)


def _hgrn_tile(qh, kk, vv, g, reverse, aux, ones_bd, state_t, scratch):
    rows, head_masks, first_of_pair, right_rows, same_block = aux
    pair_w = 2 * DK_B
    g_scr, b_scr = scratch
    prefix, total = _tile_prefix(g, g_scr, b_scr)
    z = prefix if not reverse else prefix - g
    diag = [[None, None] for _ in range(H_B)]
    off = [None] * H_B
    m = 1
    while m < TILE:
        right = right_rows[m]
        x = jnp.exp2(_level_exponent(z, g, b_scr, m, reverse, right))
        q_rows = right if not reverse else jnp.logical_not(right)
        qz = jnp.where(q_rows, (qh * x).astype(BF16), 0.0)
        kz = jnp.where(q_rows, 0.0, (kk * x).astype(BF16))
        for h in range(H_B):
            ls = slice((h // 2) * pair_w, (h // 2 + 1) * pair_w)
            qm = jnp.where(first_of_pair, qz[:, ls], 0.0) if h % 2 == 0 else \
                jnp.where(first_of_pair, 0.0, qz[:, ls])
            kh = kz[:, ls]
            if m < HALF:
                for i in range(2):
                    rs = slice(i * HALF, (i + 1) * HALF)
                    pm = _dot_nt(qm[rs], kh[rs])
                    if m in same_block:
                        pm = jnp.where(same_block[m], pm, 0.0)
                    diag[h][i] = pm if diag[h][i] is None else diag[h][i] + pm
            else:
                lo, hi = slice(0, HALF), slice(HALF, TILE)
                off[h] = _dot_nt(qm[hi], kh[lo]) if not reverse else _dot_nt(qm[lo], kh[hi])
        m *= 2
    vb = vv.astype(BF16)
    top = bot = None
    for h in range(H_B):
        vh = jnp.where(head_masks[h], vb, 0.0)
        d0, d1, of = (a.astype(BF16) for a in (diag[h][0], diag[h][1], off[h]))
        if not reverse:
            t = _dot(d0, vh[0:HALF])
            b = _dot(jnp.concatenate([of, d1], axis=1), vh)
        else:
            t = _dot(jnp.concatenate([d0, of], axis=1), vh)
            b = _dot(d1, vh[HALF:])
        top = t if top is None else top + t
        bot = b if bot is None else bot + b
    out = _group_sum(qh * kk, ones_bd) * vv + jnp.concatenate([top, bot], axis=0)
    q_decay, k_decay = (z, total - z) if not reverse else (total - z, z)
    if state_t is not None:
        out = out + _dot_nt((qh * jnp.exp2(q_decay)).astype(BF16), state_t.astype(BF16))
    k_hat = (kk * jnp.exp2(k_decay)).astype(BF16)
    return out, k_hat, total


def _block_diag_mask():
    r = lax.broadcasted_iota(jnp.int32, (W_B, W_B), 0) // DK_B
    c = lax.broadcasted_iota(jnp.int32, (W_B, W_B), 1) // DV_B
    return r == c


def _conv(x, prev_row, next_rows, cw_ref, cb_ref, rows):
    x_m1 = jnp.where(rows == 0, prev_row, pltpu.roll(x, 1, 0))
    x_p1 = jnp.where(rows == TILE - 1, next_rows[0:1], pltpu.roll(x, TILE - 1, 0))
    x_p2 = jnp.where(rows == TILE - 2, next_rows[0:1],
                     jnp.where(rows == TILE - 1, next_rows[1:2], pltpu.roll(x, TILE - 2, 0)))
    y = cb_ref[...] + x_m1 * cw_ref[0:1, :]
    y = y + x * cw_ref[1:2, :]
    y = y + x_p1 * cw_ref[2:3, :]
    return y + x_p2 * cw_ref[3:4, :]


def _chain_address(g):
    per_chain = CHAIN_LEN // SUBLANES
    return SUBLANES * SUBLANES * (g % per_chain) + g // per_chain, SUBLANES


def _to_chains(x, scr):
    slabs = scr.shape[0]
    for s in range(slabs):
        for g in range(TILE // SUBLANES):
            start, stride = _chain_address(g)
            scr[s, pl.ds(start, SUBLANES, stride=stride), :] = (
                x[g * SUBLANES:(g + 1) * SUBLANES, s * LANES:(s + 1) * LANES])
    return [jnp.concatenate([scr[s, j * SUBLANES:(j + 1) * SUBLANES, :] for s in range(slabs)],
                            axis=1) for j in range(CHAIN_LEN)]


def _from_chains(chains, scr):
    slabs = scr.shape[0]
    for s in range(slabs):
        for j, c in enumerate(chains):
            scr[s, j * SUBLANES:(j + 1) * SUBLANES, :] = c[:, s * LANES:(s + 1) * LANES]
    groups = []
    for g in range(TILE // SUBLANES):
        start, stride = _chain_address(g)
        groups.append(jnp.concatenate(
            [scr[s, pl.ds(start, SUBLANES, stride=stride), :] for s in range(slabs)], axis=1))
    return jnp.concatenate(groups, axis=0)


def _rglru_tile(xc, r_pre, i_pre, lam_row, reverse, h_in, scratch):
    a_scr, u_scr, h_scr = scratch
    r = _sigmoid(r_pre)
    i = _sigmoid(i_pre)
    log_a = -RG_C * r * _softplus(-lam_row)
    a = jnp.exp(log_a)
    u = jnp.sqrt(jnp.tanh(-log_a) * (a * a + 1.0)) * (i * xc)
    a_c = _to_chains(a, a_scr)
    u_c = _to_chains(u, u_scr)
    order = list(reversed(range(CHAIN_LEN))) if reverse else list(range(CHAIN_LEN))
    h_c = [None] * CHAIN_LEN
    p_c = [None] * CHAIN_LEN
    h_c[order[0]], p_c[order[0]] = u_c[order[0]], a_c[order[0]]
    for prev, j in zip(order[:-1], order[1:]):
        h_c[j] = a_c[j] * h_c[prev] + u_c[j]
        p_c[j] = a_c[j] * p_c[prev]
    p_tot, h_tot = p_c[order[-1]], h_c[order[-1]]
    sub = lax.broadcasted_iota(jnp.int32, (SUBLANES, 1), 0)
    k = 1
    while k < SUBLANES:
        if not reverse:
            valid = sub >= k
            p_s, h_s = pltpu.roll(p_tot, k, 0), pltpu.roll(h_tot, k, 0)
        else:
            valid = sub < SUBLANES - k
            p_s, h_s = pltpu.roll(p_tot, SUBLANES - k, 0), pltpu.roll(h_tot, SUBLANES - k, 0)
        h_tot = h_tot + p_tot * jnp.where(valid, h_s, 0.0)
        p_tot = p_tot * jnp.where(valid, p_s, 1.0)
        k *= 2
    if h_in is not None:
        h_tot = h_tot + p_tot * h_in
    if not reverse:
        carry = jnp.where(sub == 0, 0.0 if h_in is None else h_in, pltpu.roll(h_tot, 1, 0))
        h_out = h_tot[SUBLANES - 1:SUBLANES, :]
    else:
        carry = jnp.where(sub == SUBLANES - 1, 0.0 if h_in is None else h_in,
                          pltpu.roll(h_tot, SUBLANES - 1, 0))
        h_out = h_tot[0:1, :]
    h_c = [h + p * carry for h, p in zip(h_c, p_c)]
    return _from_chains(h_c, h_scr), h_out


def _rec_inputs(refs):
    qb, ff, fb, ib, gb, xcol, gcol = [r[...] for r in refs]
    return _silu(qb), ff, fb, ib, gb, xcol, gcol


def _hgrn_out(o_sum, gb, onorm_gain, ones_bd):
    ms = _group_sum(o_sum * o_sum, ones_bd) * (1.0 / DV_B)
    return o_sum * lax.rsqrt(ms + EPS) * onorm_gain * _silu(gb)


def _rec_ctx_kernel(q_ref, ff_ref, fb_ref, i_ref, g_ref, x_ref, gc_ref, lb_ref, on_ref, cw_ref,
                    cb_ref, wg_ref, bg_ref, lam_ref, ones_ref, ob_ref, oc_ref, st_ref,
                    hs_ref, *scan_scratch, layer, depth):
    aux = _tile_aux()
    rows = aux[0]
    ones_bd = ones_ref[...]
    qh, ff, fb, ib, gb, xcol, gcol = _rec_inputs((q_ref, ff_ref, fb_ref, i_ref, g_ref, x_ref, gc_ref))
    o_sum = None
    for direction, fpre in enumerate((ff, fb)):
        lbd = None if layer == 0 else _hgrn_lower_bound(lb_ref, layer, direction, depth)
        g, kk = _hgrn_gates(fpre, lbd)
        o, k_hat, _ = _hgrn_tile(qh, kk, ib, g, direction == 1, aux, ones_bd, None,
                                 scan_scratch[3 + 2 * direction:5 + 2 * direction])
        o_sum = o if o_sum is None else o_sum + o
        full = _dot_tn(k_hat, ib.astype(BF16))
        for h in range(H_B):
            blk = full[h * DK_B:(h + 1) * DK_B, (h // 2) * 2 * DV_B:(h // 2 + 1) * 2 * DV_B]
            if h % 2:
                blk = pltpu.roll(blk, DV_B, 1)
            st_ref[0, direction, h] = blk[:, 0:DV_B]
    ob_ref[...] = _hgrn_out(o_sum, gb, on_ref[...], ones_bd)

    zero_row = jnp.zeros((1, W_C), F32)
    xc = _conv(xcol, zero_row, jnp.zeros((2, W_C), F32), cw_ref, cb_ref, rows)
    gates = _dot(xc.astype(BF16), wg_ref[...]) + bg_ref[...]
    y_f, h_f = _rglru_tile(xc, gates[:, 0:W_C], gates[:, W_C:2 * W_C], lam_ref[0:1, :], False,
                           None, scan_scratch[0:3])
    y_b, h_b = _rglru_tile(xc, gates[:, 2 * W_C:3 * W_C], gates[:, 3 * W_C:], lam_ref[1:2, :], True,
                           None, scan_scratch[0:3])
    oc_ref[...] = (y_f + y_b) * _gelu_tanh(gcol)
    hs_ref[0, 0:1, :] = h_f
    hs_ref[0, 1:2, :] = h_b


def _scan_scratch(hgrn_directions):
    return [pltpu.VMEM((W_C // LANES, TILE, LANES), F32) for _ in range(3 + 2 * hgrn_directions)]


def _col_spec(col, row_map):
    return pl.BlockSpec((TILE, W_B), lambda *idx: (row_map(*idx), col))


COL_QB, COL_FF, COL_FB, COL_IB, COL_GB, COL_XC, COL_GC = range(7)
_REC_COLS = (COL_QB, COL_FF, COL_FB, COL_IB, COL_GB, COL_XC, COL_GC)


def _full_spec(a):
    nd = a.ndim
    return pl.BlockSpec(a.shape, lambda *idx: (0,) * nd)


def _recurrence_ctx(proj, layer, depth, consts):
    t = proj.shape[0]
    n = t // TILE
    return pl.pallas_call(
        functools.partial(_rec_ctx_kernel, layer=layer, depth=depth),
        grid=(n,),
        in_specs=[_col_spec(c, lambda b: b) for c in _REC_COLS] + [_full_spec(a) for a in consts],
        out_specs=[
            pl.BlockSpec((TILE, W_B), lambda b: (b, 0)),
            pl.BlockSpec((TILE, W_C), lambda b: (b, 0)),
            pl.BlockSpec((1, 2, H_B, DK_B, DV_B), lambda b: (b, 0, 0, 0, 0)),
            pl.BlockSpec((1, 2, W_C), lambda b: (b, 0, 0)),
        ],
        out_shape=[
            jax.ShapeDtypeStruct((t, W_B), F32),
            jax.ShapeDtypeStruct((t, W_C), F32),
            jax.ShapeDtypeStruct((n, 2, H_B, DK_B, DV_B), F32),
            jax.ShapeDtypeStruct((n, 2, W_C), F32),
        ],
        scratch_shapes=_scan_scratch(2),
        compiler_params=_params(("arbitrary",)),
        name="rec_ctx",
    )(*([proj] * len(_REC_COLS)), *consts)


def _rec_lat_kernel(q_ref, f_ref, i_ref, g_ref, x_ref, xp_ref, xn_ref, gc_ref, s0_ref, h0_ref,
                    *rest, layer, depth, direction, n_tiles):
    if direction == 0:
        (lb_ref, on_ref, cw_ref, cb_ref, wg_ref, bg_ref, lam_ref, ones_ref,
         o_ref, y_ref, state, hstate, *scan_scratch) = rest
    else:
        (of_ref, yf_ref, lb_ref, on_ref, cw_ref, cb_ref, wg_ref, bg_ref, lam_ref, ones_ref,
         o_ref, y_ref, state, hstate, *scan_scratch) = rest
    step = pl.program_id(1)
    tile = step if direction == 0 else n_tiles - 1 - step
    reverse = direction == 1

    @pl.when(step == 0)
    def _():
        state[...] = s0_ref[0, 0]
        hstate[...] = h0_ref[0, 0]

    aux = _tile_aux()
    rows = aux[0]
    ones_bd = ones_ref[...]
    qh = _silu(q_ref[...])
    ib = i_ref[...]
    lbd = None if layer == 0 else _hgrn_lower_bound(lb_ref, layer, direction, depth)
    g, kk = _hgrn_gates(f_ref[...], lbd)
    o, k_hat, total = _hgrn_tile(qh, kk, ib, g, reverse, aux, ones_bd, state[...],
                                 scan_scratch[3:5])
    upd = jnp.where(_block_diag_mask(), _dot_tn(ib.astype(BF16), k_hat), 0.0)
    state[...] = state[...] * jnp.exp2(total) + upd

    prev_row = jnp.where(tile == 0, 0.0, xp_ref[TILE - 1:TILE, :])
    next_rows = jnp.where(tile == n_tiles - 1, 0.0, xn_ref[0:2, :])
    xc = _conv(x_ref[...], prev_row, next_rows, cw_ref, cb_ref, rows)
    gates = _dot(xc.astype(BF16), wg_ref[...]) + bg_ref[...]
    c0 = 2 * W_C * direction
    y, h_next = _rglru_tile(xc, gates[:, c0:c0 + W_C], gates[:, c0 + W_C:c0 + 2 * W_C],
                            lam_ref[direction:direction + 1, :], reverse, hstate[...],
                            scan_scratch[0:3])
    hstate[...] = h_next

    if direction == 0:
        o_ref[...] = o
        y_ref[...] = y
    else:
        o_ref[...] = _hgrn_out(of_ref[...] + o, g_ref[...], on_ref[...], ones_bd)
        y_ref[...] = (yf_ref[...] + y) * _gelu_tanh(gc_ref[...])


def _recurrence_lat(proj, n_batch, layer, depth, direction, state0, h0, prev, consts):
    t = proj.shape[0]
    n_tiles = t // (TILE * n_batch)

    def tile_of(b, s):
        return s if direction == 0 else n_tiles - 1 - s

    def row(b, s):
        return b * n_tiles + tile_of(b, s)

    def row_prev(b, s):
        return b * n_tiles + jnp.maximum(tile_of(b, s) - 1, 0)

    def row_next(b, s):
        return b * n_tiles + jnp.minimum(tile_of(b, s) + 1, n_tiles - 1)

    f_col = COL_FF if direction == 0 else COL_FB
    in_specs = [_col_spec(COL_QB, row), _col_spec(f_col, row), _col_spec(COL_IB, row),
                _col_spec(COL_GB, row), _col_spec(COL_XC, row), _col_spec(COL_XC, row_prev),
                _col_spec(COL_XC, row_next), _col_spec(COL_GC, row),
                pl.BlockSpec((1, 1, W_B, W_B), lambda b, s: (b, direction, 0, 0)),
                pl.BlockSpec((1, 1, 1, W_C), lambda b, s: (b, direction, 0, 0))]
    args = [proj] * 8 + [state0, h0]
    if direction == 1:
        in_specs += [pl.BlockSpec((TILE, W_B), lambda b, s: (row(b, s), 0)),
                     pl.BlockSpec((TILE, W_C), lambda b, s: (row(b, s), 0))]
        args += list(prev)
    in_specs += [_full_spec(a) for a in consts]
    args += list(consts)
    return pl.pallas_call(
        functools.partial(_rec_lat_kernel, layer=layer, depth=depth, direction=direction,
                          n_tiles=n_tiles),
        grid=(n_batch, n_tiles),
        in_specs=in_specs,
        out_specs=[pl.BlockSpec((TILE, W_B), lambda b, s: (row(b, s), 0)),
                   pl.BlockSpec((TILE, W_C), lambda b, s: (row(b, s), 0))],
        out_shape=[jax.ShapeDtypeStruct((t, W_B), F32), jax.ShapeDtypeStruct((t, W_C), F32)],
        scratch_shapes=[pltpu.VMEM((W_B, W_B), F32), pltpu.VMEM((1, W_C), F32)] + _scan_scratch(1),
        compiler_params=_params(("arbitrary", "arbitrary")),
        name="rec_lat_fwd" if direction == 0 else "rec_lat_bwd",
    )(*args)


def _out_kernel(x_ref, oa_ref, ob_ref, oc_ref, mod_ref, n2_ref, wo_ref, w1_ref, w2_ref, y_ref):
    x = x_ref[...]
    d = x.shape[-1]
    g1 = mod_ref[0, :, 2 * d:3 * d]
    sh2 = mod_ref[0, :, 3 * d:4 * d]
    sc2 = mod_ref[0, :, 4 * d:5 * d]
    g2 = mod_ref[0, :, 5 * d:6 * d]
    wa = H_A * DV_A
    mix = _dot(oa_ref[...].astype(BF16), wo_ref[0:wa, :])
    mix = mix + _dot(ob_ref[...].astype(BF16), wo_ref[wa:wa + W_B, :])
    mix = mix + _dot(oc_ref[...].astype(BF16), wo_ref[wa + W_B:, :])
    x1 = x + g1 * mix
    h = (_rms_rows(x1, n2_ref[...]) * (1.0 + sc2) + sh2).astype(BF16)
    acc = jnp.zeros_like(x)
    for c in range(w1_ref.shape[1] // D_FF_CHUNK):
        cs = slice(c * D_FF_CHUNK, (c + 1) * D_FF_CHUNK)
        f = jnp.maximum(_dot(h, w1_ref[:, cs]), 0.0)
        acc = acc + _dot((f * f).astype(BF16), w2_ref[cs, :])
    y_ref[...] = x1 + g2 * acc


def _out_mlp(x, oa, ob, oc, mod, tokens_per_mod, n2, w_out, w_ff1, w_ff2, layer):
    t, d = x.shape

    def resident(w):
        return pl.BlockSpec((None,) + w.shape[1:], lambda i: (layer, 0, 0),
                            pipeline_mode=pl.Buffered(1))

    return pl.pallas_call(
        _out_kernel,
        grid=(t // ROW_TILE,),
        in_specs=[
            pl.BlockSpec((ROW_TILE, d), lambda i: (i, 0)),
            pl.BlockSpec((ROW_TILE, oa.shape[1]), lambda i: (i, 0)),
            pl.BlockSpec((ROW_TILE, ob.shape[1]), lambda i: (i, 0)),
            pl.BlockSpec((ROW_TILE, oc.shape[1]), lambda i: (i, 0)),
            pl.BlockSpec((1, 1, N_MOD * d), lambda i: (i * ROW_TILE // tokens_per_mod, 0, 0)),
            pl.BlockSpec((1, d), lambda i: (0, 0)),
            resident(w_out),
            resident(w_ff1),
            resident(w_ff2),
        ],
        out_specs=pl.BlockSpec((ROW_TILE, d), lambda i: (i, 0)),
        out_shape=jax.ShapeDtypeStruct((t, d), F32),
        compiler_params=_params(("arbitrary",)),
        name="out_mlp",
    )(x, oa, ob, oc, mod, n2, w_out, w_ff1, w_ff2)


def _ones_block_diag(n, group):
    idx = np.arange(n) // group
    return jnp.asarray((idx[:, None] == idx[None, :]).astype(np.float32), dtype=BF16)


def _rope_tables(seq):
    nf = DQK_A // 4
    inv = ROPE_BASE ** (-jnp.arange(nf, dtype=F32) / nf)
    pos = jnp.arange(seq)
    ang_r = (pos // GRID_W).astype(F32)[:, None] * inv
    ang_c = (pos % GRID_W).astype(F32)[:, None] * inv
    z = jnp.zeros_like(ang_r)
    cos = jnp.concatenate([jnp.cos(ang_r)] * 2 + [jnp.cos(ang_c)] * 2, axis=-1)
    sin_lo = jnp.concatenate([-jnp.sin(ang_r), z, -jnp.sin(ang_c), z], axis=-1)
    sin_hi = jnp.concatenate([z, jnp.sin(ang_r), z, jnp.sin(ang_c)], axis=-1)
    return tuple(jnp.tile(a, (1, 2)) for a in (cos, sin_lo, sin_hi))


def _gate_weights(rg_w_l):
    eye = jnp.eye(H_C, dtype=rg_w_l.dtype)
    w = jnp.einsum('dghij,hk->dghikj', rg_w_l, eye)
    w = w.reshape(2, 2, W_C, W_C)
    return jnp.transpose(w, (2, 0, 1, 3)).reshape(W_C, 4 * W_C).astype(BF16)


def _state_block_diag_t(s):
    eye = jnp.eye(H_B, dtype=s.dtype)
    w = jnp.einsum('...hdv,hk->...hvkd', s, eye)
    return w.reshape(s.shape[:-3] + (W_B, W_B))


def kernel(x_prompt, x_sample, cache_k, cache_v, state_hgrn, state_rglru, c, c_ctx, w_mod, b_mod,
           norm1, norm2, w_in, w_out, qk_norm, diff_lambda, subln, hgrn_lb, hgrn_onorm, conv_w,
           conv_b, rg_w, rg_b, rg_lambda, w_ff1, w_ff2):
    batch, seq, d = x_prompt.shape
    dec_batch, dec_seq, _ = x_sample.shape
    depth = w_in.shape[0]
    past = cache_k.shape[2]
    wq = H_A * 2 * DQK_A

    c8 = jnp.concatenate([c_ctx[None], c, jnp.zeros((8 - 1 - dec_batch, d), F32)], axis=0)
    mod_all = _modulation(c8, w_mod, b_mod)

    assert DQK_A == DV_B and seq == TILE
    ones_b = _ones_block_diag(W_B, DV_B)
    rope_tabs = _rope_tables(dec_seq)
    ck = cache_k.reshape(dec_batch, depth, past, wq)
    cv = cache_v.reshape(dec_batch, depth, past, H_A * DV_A)
    lb2 = hgrn_lb.reshape(depth * 2, W_B)

    w_in_b, w_out_b, w1_b, w2_b = (w.astype(BF16) for w in (w_in, w_out, w_ff1, w_ff2))

    yp = x_prompt.reshape(batch * seq, d)
    ys = x_sample.reshape(dec_batch * dec_seq, d)
    caches, shs, srs = (), [], []
    for l in range(depth):
        lam_init = 0.8 - 0.6 * math.exp(-0.3 * l)
        mod_ctx = mod_all[l, 0:1][:, None, :]
        mod_lat = mod_all[l, 1:1 + dec_batch][:, None, :]
        n1 = norm1[l][None]
        n2 = norm2[l][None]
        gq = jnp.tile(qk_norm[l, 0], H_A * 2)[None] * (DQK_A ** -0.5 * LOG2E)
        gk = jnp.tile(qk_norm[l, 1], H_A * 2)[None]
        dl = diff_lambda[l]
        sg = subln[l][None]
        consts = (lb2, jnp.tile(hgrn_onorm[l], H_B)[None], conv_w[l], conv_b[l][None],
                  _gate_weights(rg_w[l]), rg_b[l].reshape(1, 4 * W_C), rg_lambda[l], ones_b)

        q, k, v_all, rest, kt_all = _in_projection(yp, mod_ctx, batch * seq, n1, w_in_b, l,
                                                   gq, gk, ones_b, caches=caches)
        caches = (kt_all, v_all)
        oa = _attention_ctx(q, k, v_all, l, seq, dl, sg, lam_init)
        ob, oc, st, hs = _recurrence_ctx(rest, l, depth, consts)
        shs.append(st)
        srs.append(hs)
        yp = _out_mlp(yp, oa, ob, oc, mod_ctx, batch * seq, n2, w_out_b, w1_b, w2_b, l)

        q, k, v, rest = _in_projection(ys, mod_lat, dec_seq, n1, w_in_b, l, gq, gk, ones_b)
        oa = _attention_lat(q, k, v, dec_batch, ck, cv, l, rope_tabs, dl, sg, lam_init)
        s0 = _state_block_diag_t(state_hgrn[:, l])
        h0 = state_rglru[:, l][:, :, None, :]
        fwd = _recurrence_lat(rest, dec_batch, l, depth, 0, s0, h0, None, consts)
        ob, oc = _recurrence_lat(rest, dec_batch, l, depth, 1, s0, h0, fwd, consts)
        ys = _out_mlp(ys, oa, ob, oc, mod_lat, dec_seq, n2, w_out_b, w1_b, w2_b, l)

    kt_all, v_all = caches
    new_k = jnp.transpose(kt_all.reshape(batch, depth, H_A, 2, DQK_A, seq), (0, 1, 5, 2, 3, 4))
    new_v = v_all.reshape(batch, depth, seq, H_A, DV_A)
    return (yp.reshape(batch, seq, d), ys.reshape(dec_batch, dec_seq, d), new_k, new_v,
            jnp.stack(shs, axis=1), jnp.stack(srs, axis=1))
```

```python
import functools
import math

import numpy as np
import jax
import jax.numpy as jnp
from jax import lax
from jax.experimental import pallas as pl
from jax.experimental.pallas import tpu as pltpu

F32 = jnp.float32
BF16 = jnp.bfloat16

GRID_W = 64
H_A, DQK_A, DV_A = 4, 64, 128
H_B, DK_B, DV_B = 4, 64, 64
W_B = H_B * DV_B
H_C, W_C = 4, 256
CONV_K = 4
RG_C = 8.0
ROPE_BASE = 10000.0
EPS = 1e-6
N_MOD = 6
D_FF_CHUNK = 1024

TILE = 256
HALF = TILE // 2
ROW_TILE = 512
ATTN_SEQS_PER_STEP = 4
SUBLANES = 8
LANES = 128
CHAIN_LEN = TILE // SUBLANES
LOG2E = math.log2(math.e)
V7X_VMEM_BYTES = 64 * 1024 * 1024
VMEM_LIMIT = V7X_VMEM_BYTES - 8 * 1024 * 1024


def _dot(a, b):
    return jnp.dot(a, b, preferred_element_type=F32)


def _dot_nt(a, b):
    return lax.dot_general(a, b, (((1,), (1,)), ((), ())), preferred_element_type=F32)


def _dot_tn(a, b):
    return lax.dot_general(a, b, (((0,), (0,)), ((), ())), preferred_element_type=F32)


def _group_sum(x, ones_bd):
    hi = x.astype(BF16)
    lo = (x - hi.astype(F32)).astype(BF16)
    return _dot(hi, ones_bd) + _dot(lo, ones_bd)


def _rms_rows(x, gain):
    return x * lax.rsqrt(jnp.mean(x * x, axis=-1, keepdims=True) + EPS) * gain


def _sigmoid(x):
    return 0.5 * jnp.tanh(0.5 * x) + 0.5


def _silu(x):
    return x * _sigmoid(x)


def _softplus(x):
    return jnp.maximum(x, 0.0) + jnp.log1p(jnp.exp(-jnp.abs(x)))


def _log_sigmoid(x):
    return jnp.minimum(x, 0.0) - jnp.log1p(jnp.exp(-jnp.abs(x)))


def _gelu_tanh(x):
    return x * (0.5 * (1.0 + jnp.tanh(math.sqrt(2.0 / math.pi) * (x + 0.044715 * (x * x * x)))))


def _params(semantics, flags=None):
    return pltpu.CompilerParams(dimension_semantics=semantics, vmem_limit_bytes=VMEM_LIMIT,
                                flags=flags)


def _mod_kernel(c_ref, w_ref, b_ref, o_ref):
    c = c_ref[...]
    o_ref[0] = _dot(_silu(c).astype(BF16), w_ref[0].astype(BF16)) + b_ref[0]


def _modulation(c8, w_mod, b_mod):
    depth, d, n = w_mod.shape
    tn = d
    return pl.pallas_call(
        _mod_kernel,
        grid=(depth, n // tn),
        in_specs=[
            pl.BlockSpec((8, d), lambda l, j: (0, 0)),
            pl.BlockSpec((1, d, tn), lambda l, j: (l, 0, j)),
            pl.BlockSpec((1, 1, tn), lambda l, j: (l, 0, j)),
        ],
        out_specs=pl.BlockSpec((1, 8, tn), lambda l, j: (l, 0, j)),
        out_shape=jax.ShapeDtypeStruct((depth, 8, n), F32),
        compiler_params=_params(("arbitrary", "arbitrary")),
        name="mod",
    )(c8, w_mod, b_mod.reshape(depth, 1, n))


def _proj_kernel(x_ref, mod_ref, n1_ref, w_ref, gq_ref, gk_ref, *refs, layers_before):
    prev = refs[:2] if layers_before else ()
    q_ref, k_ref, v_ref, r_ref, *kt_ref = refs[len(prev):]
    x = x_ref[...]
    d = x.shape[-1]
    sh = mod_ref[0, :, 0:d]
    sc = mod_ref[0, :, d:2 * d]
    h = _rms_rows(x, n1_ref[...]) * (1.0 + sc) + sh
    p = _dot(h.astype(BF16), w_ref[...])
    wq = H_A * 2 * DQK_A

    first = lax.broadcasted_iota(jnp.int32, (1, LANES), 1) < DQK_A

    def qk_norm(z, gain):
        z2 = z * z
        parts = []
        for c in range(0, wq, LANES):
            slab = z2[:, c:c + LANES]
            lo = jnp.sum(jnp.where(first, slab, 0.0), axis=-1, keepdims=True)
            hi = jnp.sum(jnp.where(first, 0.0, slab), axis=-1, keepdims=True)
            parts.append(jnp.where(first, lo, hi))
        ms = jnp.concatenate(parts, axis=1) * (1.0 / DQK_A)
        return z * lax.rsqrt(ms + EPS) * gain

    wv = H_A * DV_A
    q_ref[...] = qk_norm(p[:, 0:wq], gq_ref[...])
    kn = qk_norm(p[:, wq:2 * wq], gk_ref[...])
    k_ref[...] = kn
    if kt_ref:
        for s in range(ROW_TILE // TILE):
            kt_ref[0][s, layers_before] = kn[s * TILE:(s + 1) * TILE].T
        if prev:
            ktp_ref, vp_ref = prev
            kt_ref[0][:, 0:layers_before] = ktp_ref[...]
            v_ref[:, 0:layers_before] = vp_ref[...]
    for s in range(ROW_TILE // TILE):
        v_dst = v_ref.at[s, layers_before] if kt_ref else v_ref.at[pl.ds(s * TILE * H_A, TILE * H_A)]
        for h in range(H_A):
            v_dst[pl.ds(h, TILE, stride=H_A), :] = (
                p[s * TILE:(s + 1) * TILE, 2 * wq + h * DV_A:2 * wq + (h + 1) * DV_A])
    r_ref[...] = p[:, 2 * wq + wv:]


def _in_projection(x, mod, tokens_per_mod, n1, w_in, layer, gq, gk, caches=None):
    t, d = x.shape
    layers_before = caches[0].shape[1] if caches else 0
    n_in = w_in.shape[2]
    wq = gq.shape[1]
    wv = H_A * DV_A
    widths = (wq, wq, wv, n_in - 2 * wq - wv)
    seqs = ROW_TILE // TILE
    out_specs = [pl.BlockSpec((ROW_TILE, w), lambda i: (i, 0)) for w in widths]
    out_shape = [jax.ShapeDtypeStruct((t, w), F32) for w in widths]
    out_specs[2] = pl.BlockSpec((ROW_TILE * H_A, DV_A), lambda i: (i, 0))
    out_shape[2] = jax.ShapeDtypeStruct((t * H_A, DV_A), F32)
    cache_specs = []
    if caches is not None:
        n_seq, nl = t // TILE, layers_before + 1
        out_specs[2] = pl.BlockSpec((seqs, nl, TILE * H_A, DV_A), lambda i: (i, 0, 0, 0))
        out_shape[2] = jax.ShapeDtypeStruct((n_seq, nl, TILE * H_A, DV_A), F32)
        out_specs.append(pl.BlockSpec((seqs, nl, wq, TILE), lambda i: (i, 0, 0, 0)))
        out_shape.append(jax.ShapeDtypeStruct((n_seq, nl, wq, TILE), F32))
        if caches:
            cache_specs = [
                pl.BlockSpec((seqs, layers_before, wq, TILE), lambda i: (i, 0, 0, 0)),
                pl.BlockSpec((seqs, layers_before, TILE * H_A, DV_A), lambda i: (i, 0, 0, 0))]
    return pl.pallas_call(
        functools.partial(_proj_kernel, layers_before=layers_before),
        grid=(t // ROW_TILE,),
        in_specs=[
            pl.BlockSpec((ROW_TILE, d), lambda i: (i, 0)),
            pl.BlockSpec((1, 1, N_MOD * d), lambda i: (i * ROW_TILE // tokens_per_mod, 0, 0)),
            pl.BlockSpec((1, d), lambda i: (0, 0)),
            pl.BlockSpec((None, d, n_in), lambda i: (layer, 0, 0)),
            pl.BlockSpec((1, wq), lambda i: (0, 0)),
            pl.BlockSpec((1, wq), lambda i: (0, 0)),
        ] + cache_specs,
        out_specs=out_specs,
        out_shape=out_shape,
        compiler_params=_params(("arbitrary",)),
        name="proj",
    )(x, mod, n1, w_in, gq, gk, *(caches or ()))


def _diff_lambda(dl_ref, lam_init):
    lv = dl_ref[...]
    a = jnp.sum(lv[0:1] * lv[1:2], axis=-1, keepdims=True)
    b = jnp.sum(lv[2:3] * lv[3:4], axis=-1, keepdims=True)
    return jnp.exp(a) - jnp.exp(b) + lam_init


def _diff_attention(q, k_of, vext_of, lam, sub_gain, lam_init, o_ref):
    hw = 2 * DQK_A
    first = lax.broadcasted_iota(jnp.int32, (1, hw), 1) < DQK_A
    for h in range(H_A):
        hs = slice(h * hw, (h + 1) * hw)
        qh = q[:, hs]
        kh = k_of(h)
        vext = vext_of(h)
        maps = []
        for qm in (jnp.where(first, qh, 0.0), jnp.where(first, 0.0, qh)):
            s = _dot_nt(qm.astype(BF16), kh)
            e = jnp.exp2(s - jnp.max(s, axis=-1, keepdims=True)).astype(BF16)
            oe = _dot(e, vext)
            maps.append(oe[:, 0:DV_A] / oe[:, DV_A:])
        oh = maps[0] - lam * maps[1]
        o_ref[:, hs] = _rms_rows(oh, sub_gain) * (1.0 - lam_init)


def _attn_ctx_kernel(q_ref, k_ref, v_ref, dl_ref, sg_ref, o_ref, *, lam_init, seq):
    lam = _diff_lambda(dl_ref, lam_init)
    ones = jnp.ones((seq, DV_A), BF16)
    hw = 2 * DQK_A
    for s in range(v_ref.shape[0]):
        rows = pl.ds(s * seq, seq)
        k = k_ref[rows, :].astype(BF16)

        def vext(h, s=s):
            return jnp.concatenate(
                [v_ref[s, pl.ds(h, seq, stride=H_A), :].astype(BF16), ones], axis=1)

        _diff_attention(q_ref[rows, :], lambda h, k=k: k[:, h * hw:(h + 1) * hw], vext, lam,
                        sg_ref[...], lam_init, o_ref.at[rows])


def _attention_ctx(q, k, v_all, layer, seq, dl, sg, lam_init):
    t = q.shape[0]
    w = H_A * DV_A
    per_step = ATTN_SEQS_PER_STEP
    return pl.pallas_call(
        functools.partial(_attn_ctx_kernel, lam_init=lam_init, seq=seq),
        grid=(t // (seq * per_step),),
        in_specs=[
            pl.BlockSpec((per_step * seq, w), lambda b: (b, 0)),
            pl.BlockSpec((per_step * seq, w), lambda b: (b, 0)),
            pl.BlockSpec((per_step, None, seq * H_A, DV_A), lambda b: (b, layer, 0, 0)),
            pl.BlockSpec(dl.shape, lambda b: (0, 0)),
            pl.BlockSpec(sg.shape, lambda b: (0, 0)),
        ],
        out_specs=pl.BlockSpec((per_step * seq, w), lambda b: (b, 0)),
        out_shape=jax.ShapeDtypeStruct((t, w), F32),
        compiler_params=_params(("arbitrary",)),
        name="attn_ctx",
    )(q, k, v_all, dl, sg)


def _rope(x, cos, sin_lo, sin_hi):
    return (x * cos + pltpu.roll(x, 2 * DQK_A - 16, 1) * sin_lo + pltpu.roll(x, 16, 1) * sin_hi)


def _attn_lat_kernel(q_ref, k_ref, v_ref, ck_ref, cv_ref, qcos_ref, qslo_ref, qshi_ref,
                     kcos_ref, kslo_ref, kshi_ref, dl_ref, sg_ref, o_ref, kall, vall, *, lam_init):
    hw = 2 * DQK_A
    n_new = k_ref.shape[0]

    @pl.when(pl.program_id(1) == 0)
    def _():
        kcos, kslo, kshi = kcos_ref[...], kslo_ref[...], kshi_ref[...]
        for h in range(H_A):
            hs = slice(h * hw, (h + 1) * hw)
            kall[0:n_new, hs] = _rope(k_ref[:, hs], kcos, kslo, kshi).astype(BF16)
        kall[n_new:, :] = ck_ref[...].astype(BF16)
        ones = jnp.ones((vall.shape[0], DV_A), BF16)
        for h in range(H_A):
            vs = slice(h * DV_A, (h + 1) * DV_A)
            vall[0:n_new, 2 * h * DV_A:(2 * h + 1) * DV_A] = (
                v_ref[pl.ds(h, n_new, stride=H_A), :].astype(BF16))
            vall[n_new:, 2 * h * DV_A:(2 * h + 1) * DV_A] = cv_ref[:, vs].astype(BF16)
            vall[:, (2 * h + 1) * DV_A:(2 * h + 2) * DV_A] = ones

    lam = _diff_lambda(dl_ref, lam_init)
    qcos, qslo, qshi = qcos_ref[...], qslo_ref[...], qshi_ref[...]
    q = jnp.concatenate(
        [_rope(q_ref[:, h * hw:(h + 1) * hw], qcos, qslo, qshi) for h in range(H_A)], axis=-1)
    _diff_attention(q, lambda h: kall[:, h * hw:(h + 1) * hw],
                    lambda h: vall[:, 2 * h * DV_A:(2 * h + 2) * DV_A], lam, sg_ref[...],
                    lam_init, o_ref)


def _attention_lat(q, k, v, n_batch, cache_k, cache_v, layer, rope_tabs, dl, sg, lam_init):
    t = q.shape[0]
    seq = t // n_batch
    past = cache_k.shape[2]
    w = H_A * DV_A
    hw = 2 * DQK_A
    nq = seq // TILE
    cos, slo, shi = rope_tabs
    tab_q = pl.BlockSpec((TILE, hw), lambda b, j: (j, 0))
    tab_k = pl.BlockSpec((seq, hw), lambda b, j: (0, 0))
    return pl.pallas_call(
        functools.partial(_attn_lat_kernel, lam_init=lam_init),
        grid=(n_batch, nq),
        in_specs=[
            pl.BlockSpec((TILE, w), lambda b, j: (b * nq + j, 0)),
            pl.BlockSpec((seq, w), lambda b, j: (b, 0)),
            pl.BlockSpec((seq * H_A, DV_A), lambda b, j: (b, 0)),
            pl.BlockSpec((None, None, past, w), lambda b, j: (b, layer, 0, 0)),
            pl.BlockSpec((None, None, past, w), lambda b, j: (b, layer, 0, 0)),
            tab_q, tab_q, tab_q, tab_k, tab_k, tab_k,
            pl.BlockSpec(dl.shape, lambda b, j: (0, 0)),
            pl.BlockSpec(sg.shape, lambda b, j: (0, 0)),
        ],
        out_specs=pl.BlockSpec((TILE, w), lambda b, j: (b * nq + j, 0)),
        out_shape=jax.ShapeDtypeStruct((t, w), F32),
        scratch_shapes=[pltpu.VMEM((seq + past, w), BF16), pltpu.VMEM((seq + past, 2 * w), BF16)],
        compiler_params=_params(("arbitrary", "arbitrary")),
        name="attn_lat",
    )(q, k, v, cache_k, cache_v, cos, slo, shi, cos, slo, shi, dl, sg)


def _tile_aux():
    rows = lax.broadcasted_iota(jnp.int32, (TILE, 1), 0)
    lane = lax.broadcasted_iota(jnp.int32, (1, W_B), 1)
    head_masks = [(lane // DK_B) == h for h in range(H_B)]
    first_of_pair = lax.broadcasted_iota(jnp.int32, (1, 2 * DK_B), 1) < DK_B
    xor = (lax.broadcasted_iota(jnp.int32, (HALF, HALF), 0)
           ^ lax.broadcasted_iota(jnp.int32, (HALF, HALF), 1))
    levels = [1 << i for i in range(TILE.bit_length() - 1)]
    right = {m: (rows & m) != 0 for m in levels}
    same_block = {m: xor < 2 * m for m in levels if 2 * m < HALF}
    return rows, head_masks, first_of_pair, right, same_block


def _hgrn_lower_bound(lb_ref, layer, direction, depth):
    xs = [lb_ref[2 * j + direction:2 * j + direction + 1, :] for j in range(depth)]
    m = functools.reduce(jnp.maximum, xs)
    es = [jnp.exp(x - m) for x in xs]
    return sum(es[1:layer + 1]) / sum(es)


def _hgrn_gates(fpre, lbd):
    t = jnp.exp(-jnp.abs(fpre))
    log_sig = jnp.minimum(fpre, 0.0) - jnp.log(1.0 + t)
    r = 1.0 / (1.0 + t)
    sig_neg = jnp.where(fpre >= 0.0, t * r, r)
    if lbd is None:
        return log_sig * LOG2E, sig_neg
    a = jnp.log(lbd)
    b = jnp.log1p(-lbd) + log_sig
    logf = jnp.maximum(a, b) + jnp.log1p(jnp.exp(-jnp.abs(a - b)))
    return logf * LOG2E, (1.0 - lbd) * sig_neg


def _chain_row(token):
    return SUBLANES * (token % CHAIN_LEN) + token // CHAIN_LEN


def _tile_prefix(g, g_scr, b_scr):
    g_c = _to_chains(g, g_scr)
    run = [g_c[0]]
    for j in range(1, CHAIN_LEN):
        run.append(run[-1] + g_c[j])
    sub = lax.broadcasted_iota(jnp.int32, (SUBLANES, 1), 0)
    incl = run[-1]
    k = 1
    while k < SUBLANES:
        incl = incl + jnp.where(sub >= k, pltpu.roll(incl, k, 0), 0.0)
        k *= 2
    offset = incl - run[-1]
    prefix = _from_chains([r + offset for r in run], b_scr)
    return prefix, incl[SUBLANES - 1:SUBLANES, :]


def _level_exponent(z, g, b_scr, m, reverse, right):
    if m == 1:
        return jnp.where(right, g, 0.0) if not reverse else jnp.where(right, 0.0, g)
    slabs = b_scr.shape[0]

    def ref_rows(token):
        row = _chain_row(token)
        return jnp.concatenate(
            [jnp.broadcast_to(b_scr[s, row:row + 1, :], (SUBLANES, LANES)) for s in range(slabs)],
            axis=1)

    groups = []
    for gq in range(TILE // SUBLANES):
        first = gq * SUBLANES
        zg = z[first:first + SUBLANES]
        if 2 * m >= SUBLANES:
            block = first // (2 * m) * (2 * m)
            ref = ref_rows(block + m - 1)
            if m >= SUBLANES:
                groups.append(zg - ref if first - block >= m else ref - zg)
                continue
        else:
            sub = lax.broadcasted_iota(jnp.int32, (SUBLANES, 1), 0)
            ref = jnp.where(sub < 2 * m, ref_rows(first + m - 1), ref_rows(first + 3 * m - 1))
        groups.append(-jnp.abs(zg - ref))
    return jnp.concatenate(groups, axis=0)


def _hgrn_tile(qh, kk, vv, g, reverse, aux, ones_bd, state_t, scratch):
    rows, head_masks, first_of_pair, right_rows, same_block = aux
    pair_w = 2 * DK_B
    g_scr, b_scr = scratch
    prefix, total = _tile_prefix(g, g_scr, b_scr)
    z = prefix if not reverse else prefix - g
    diag = [[None, None] for _ in range(H_B)]
    off = [None] * H_B
    m = 1
    while m < TILE:
        right = right_rows[m]
        x = jnp.exp2(_level_exponent(z, g, b_scr, m, reverse, right))
        q_rows = right if not reverse else jnp.logical_not(right)
        qz = jnp.where(q_rows, (qh * x).astype(BF16), 0.0)
        kz = jnp.where(q_rows, 0.0, (kk * x).astype(BF16))
        for h in range(H_B):
            ls = slice((h // 2) * pair_w, (h // 2 + 1) * pair_w)
            qm = jnp.where(first_of_pair, qz[:, ls], 0.0) if h % 2 == 0 else \
                jnp.where(first_of_pair, 0.0, qz[:, ls])
            kh = kz[:, ls]
            if m < HALF:
                for i in range(2):
                    rs = slice(i * HALF, (i + 1) * HALF)
                    pm = _dot_nt(qm[rs], kh[rs])
                    if m in same_block:
                        pm = jnp.where(same_block[m], pm, 0.0)
                    diag[h][i] = pm if diag[h][i] is None else diag[h][i] + pm
            else:
                lo, hi = slice(0, HALF), slice(HALF, TILE)
                off[h] = _dot_nt(qm[hi], kh[lo]) if not reverse else _dot_nt(qm[lo], kh[hi])
        m *= 2
    vb = vv.astype(BF16)
    top = bot = None
    for h in range(H_B):
        vh = jnp.where(head_masks[h], vb, 0.0)
        d0, d1, of = (a.astype(BF16) for a in (diag[h][0], diag[h][1], off[h]))
        if not reverse:
            t = _dot(d0, vh[0:HALF])
            b = _dot(jnp.concatenate([of, d1], axis=1), vh)
        else:
            t = _dot(jnp.concatenate([d0, of], axis=1), vh)
            b = _dot(d1, vh[HALF:])
        top = t if top is None else top + t
        bot = b if bot is None else bot + b
    out = _group_sum(qh * kk, ones_bd) * vv + jnp.concatenate([top, bot], axis=0)
    q_decay, k_decay = (z, total - z) if not reverse else (total - z, z)
    if state_t is not None:
        out = out + _dot_nt((qh * jnp.exp2(q_decay)).astype(BF16), state_t.astype(BF16))
    k_hat = (kk * jnp.exp2(k_decay)).astype(BF16)
    return out, k_hat, total


def _block_diag_mask():
    r = lax.broadcasted_iota(jnp.int32, (W_B, W_B), 0) // DK_B
    c = lax.broadcasted_iota(jnp.int32, (W_B, W_B), 1) // DV_B
    return r == c


def _conv(x, prev_row, next_rows, cw_ref, cb_ref, rows):
    x_m1 = jnp.where(rows == 0, prev_row, pltpu.roll(x, 1, 0))
    x_p1 = jnp.where(rows == TILE - 1, next_rows[0:1], pltpu.roll(x, TILE - 1, 0))
    x_p2 = jnp.where(rows == TILE - 2, next_rows[0:1],
                     jnp.where(rows == TILE - 1, next_rows[1:2], pltpu.roll(x, TILE - 2, 0)))
    y = cb_ref[...] + x_m1 * cw_ref[0:1, :]
    y = y + x * cw_ref[1:2, :]
    y = y + x_p1 * cw_ref[2:3, :]
    return y + x_p2 * cw_ref[3:4, :]


def _chain_address(g):
    per_chain = CHAIN_LEN // SUBLANES
    return SUBLANES * SUBLANES * (g % per_chain) + g // per_chain, SUBLANES


def _to_chains(x, scr):
    slabs = scr.shape[0]
    for s in range(slabs):
        for g in range(TILE // SUBLANES):
            start, stride = _chain_address(g)
            scr[s, pl.ds(start, SUBLANES, stride=stride), :] = (
                x[g * SUBLANES:(g + 1) * SUBLANES, s * LANES:(s + 1) * LANES])
    return [jnp.concatenate([scr[s, j * SUBLANES:(j + 1) * SUBLANES, :] for s in range(slabs)],
                            axis=1) for j in range(CHAIN_LEN)]


def _from_chains(chains, scr):
    slabs = scr.shape[0]
    for s in range(slabs):
        for j, c in enumerate(chains):
            scr[s, j * SUBLANES:(j + 1) * SUBLANES, :] = c[:, s * LANES:(s + 1) * LANES]
    groups = []
    for g in range(TILE // SUBLANES):
        start, stride = _chain_address(g)
        groups.append(jnp.concatenate(
            [scr[s, pl.ds(start, SUBLANES, stride=stride), :] for s in range(slabs)], axis=1))
    return jnp.concatenate(groups, axis=0)


def _rglru_tile(xc, r_pre, i_pre, lam_row, reverse, h_in, scratch):
    a_scr, u_scr, h_scr = scratch
    r = _sigmoid(r_pre)
    i = _sigmoid(i_pre)
    log_a = -RG_C * r * _softplus(-lam_row)
    a = jnp.exp(log_a)
    u = jnp.sqrt(jnp.tanh(-log_a) * (a * a + 1.0)) * (i * xc)
    a_c = _to_chains(a, a_scr)
    u_c = _to_chains(u, u_scr)
    order = list(reversed(range(CHAIN_LEN))) if reverse else list(range(CHAIN_LEN))
    h_c = [None] * CHAIN_LEN
    p_c = [None] * CHAIN_LEN
    h_c[order[0]], p_c[order[0]] = u_c[order[0]], a_c[order[0]]
    for prev, j in zip(order[:-1], order[1:]):
        h_c[j] = a_c[j] * h_c[prev] + u_c[j]
        p_c[j] = a_c[j] * p_c[prev]
    p_tot, h_tot = p_c[order[-1]], h_c[order[-1]]
    sub = lax.broadcasted_iota(jnp.int32, (SUBLANES, 1), 0)
    k = 1
    while k < SUBLANES:
        if not reverse:
            valid = sub >= k
            p_s, h_s = pltpu.roll(p_tot, k, 0), pltpu.roll(h_tot, k, 0)
        else:
            valid = sub < SUBLANES - k
            p_s, h_s = pltpu.roll(p_tot, SUBLANES - k, 0), pltpu.roll(h_tot, SUBLANES - k, 0)
        h_tot = h_tot + p_tot * jnp.where(valid, h_s, 0.0)
        p_tot = p_tot * jnp.where(valid, p_s, 1.0)
        k *= 2
    if h_in is not None:
        h_tot = h_tot + p_tot * h_in
    if not reverse:
        carry = jnp.where(sub == 0, 0.0 if h_in is None else h_in, pltpu.roll(h_tot, 1, 0))
        h_out = h_tot[SUBLANES - 1:SUBLANES, :]
    else:
        carry = jnp.where(sub == SUBLANES - 1, 0.0 if h_in is None else h_in,
                          pltpu.roll(h_tot, SUBLANES - 1, 0))
        h_out = h_tot[0:1, :]
    h_c = [h + p * carry for h, p in zip(h_c, p_c)]
    return _from_chains(h_c, h_scr), h_out


def _rec_inputs(refs):
    qb, ff, fb, ib, gb, xcol, gcol = [r[...] for r in refs]
    return _silu(qb), ff, fb, ib, gb, xcol, gcol


def _hgrn_out(o_sum, gb, onorm_gain, ones_bd):
    ms = _group_sum(o_sum * o_sum, ones_bd) * (1.0 / DV_B)
    return o_sum * lax.rsqrt(ms + EPS) * onorm_gain * _silu(gb)


def _rec_ctx_kernel(q_ref, ff_ref, fb_ref, i_ref, g_ref, x_ref, gc_ref, lb_ref, on_ref, cw_ref,
                    cb_ref, wg_ref, bg_ref, lam_ref, ones_ref, ob_ref, oc_ref, st_ref,
                    hs_ref, *scan_scratch, layer, depth):
    aux = _tile_aux()
    rows = aux[0]
    ones_bd = ones_ref[...]
    qh, ff, fb, ib, gb, xcol, gcol = _rec_inputs((q_ref, ff_ref, fb_ref, i_ref, g_ref, x_ref, gc_ref))
    o_sum = None
    for direction, fpre in enumerate((ff, fb)):
        lbd = None if layer == 0 else _hgrn_lower_bound(lb_ref, layer, direction, depth)
        g, kk = _hgrn_gates(fpre, lbd)
        o, k_hat, _ = _hgrn_tile(qh, kk, ib, g, direction == 1, aux, ones_bd, None,
                                 scan_scratch[3 + 2 * direction:5 + 2 * direction])
        o_sum = o if o_sum is None else o_sum + o
        full = _dot_tn(k_hat, ib.astype(BF16))
        for h in range(H_B):
            blk = full[h * DK_B:(h + 1) * DK_B, (h // 2) * 2 * DV_B:(h // 2 + 1) * 2 * DV_B]
            if h % 2:
                blk = pltpu.roll(blk, DV_B, 1)
            st_ref[0, direction, h] = blk[:, 0:DV_B]
    ob_ref[...] = _hgrn_out(o_sum, gb, on_ref[...], ones_bd)

    zero_row = jnp.zeros((1, W_C), F32)
    xc = _conv(xcol, zero_row, jnp.zeros((2, W_C), F32), cw_ref, cb_ref, rows)
    gates = _dot(xc.astype(BF16), wg_ref[...]) + bg_ref[...]
    y_f, h_f = _rglru_tile(xc, gates[:, 0:W_C], gates[:, W_C:2 * W_C], lam_ref[0:1, :], False,
                           None, scan_scratch[0:3])
    y_b, h_b = _rglru_tile(xc, gates[:, 2 * W_C:3 * W_C], gates[:, 3 * W_C:], lam_ref[1:2, :], True,
                           None, scan_scratch[0:3])
    oc_ref[...] = (y_f + y_b) * _gelu_tanh(gcol)
    hs_ref[0, 0:1, :] = h_f
    hs_ref[0, 1:2, :] = h_b


def _scan_scratch(hgrn_directions):
    return [pltpu.VMEM((W_C // LANES, TILE, LANES), F32) for _ in range(3 + 2 * hgrn_directions)]


def _col_spec(col, row_map):
    return pl.BlockSpec((TILE, W_B), lambda *idx: (row_map(*idx), col))


COL_QB, COL_FF, COL_FB, COL_IB, COL_GB, COL_XC, COL_GC = range(7)
_REC_COLS = (COL_QB, COL_FF, COL_FB, COL_IB, COL_GB, COL_XC, COL_GC)


def _full_spec(a):
    nd = a.ndim
    return pl.BlockSpec(a.shape, lambda *idx: (0,) * nd)


def _recurrence_ctx(proj, layer, depth, consts):
    t = proj.shape[0]
    n = t // TILE
    return pl.pallas_call(
        functools.partial(_rec_ctx_kernel, layer=layer, depth=depth),
        grid=(n,),
        in_specs=[_col_spec(c, lambda b: b) for c in _REC_COLS] + [_full_spec(a) for a in consts],
        out_specs=[
            pl.BlockSpec((TILE, W_B), lambda b: (b, 0)),
            pl.BlockSpec((TILE, W_C), lambda b: (b, 0)),
            pl.BlockSpec((1, 2, H_B, DK_B, DV_B), lambda b: (b, 0, 0, 0, 0)),
            pl.BlockSpec((1, 2, W_C), lambda b: (b, 0, 0)),
        ],
        out_shape=[
            jax.ShapeDtypeStruct((t, W_B), F32),
            jax.ShapeDtypeStruct((t, W_C), F32),
            jax.ShapeDtypeStruct((n, 2, H_B, DK_B, DV_B), F32),
            jax.ShapeDtypeStruct((n, 2, W_C), F32),
        ],
        scratch_shapes=_scan_scratch(2),
        compiler_params=_params(("arbitrary",)),
        name="rec_ctx",
    )(*([proj] * len(_REC_COLS)), *consts)


def _rec_lat_kernel(q_ref, f_ref, i_ref, g_ref, x_ref, xp_ref, xn_ref, gc_ref, s0_ref, h0_ref,
                    *rest, layer, depth, direction, n_tiles):
    if direction == 0:
        (lb_ref, on_ref, cw_ref, cb_ref, wg_ref, bg_ref, lam_ref, ones_ref,
         o_ref, y_ref, state, hstate, *scan_scratch) = rest
    else:
        (of_ref, yf_ref, lb_ref, on_ref, cw_ref, cb_ref, wg_ref, bg_ref, lam_ref, ones_ref,
         o_ref, y_ref, state, hstate, *scan_scratch) = rest
    step = pl.program_id(1)
    tile = step if direction == 0 else n_tiles - 1 - step
    reverse = direction == 1

    @pl.when(step == 0)
    def _():
        state[...] = s0_ref[0, 0]
        hstate[...] = h0_ref[0, 0]

    aux = _tile_aux()
    rows = aux[0]
    ones_bd = ones_ref[...]
    qh = _silu(q_ref[...])
    ib = i_ref[...]
    lbd = None if layer == 0 else _hgrn_lower_bound(lb_ref, layer, direction, depth)
    g, kk = _hgrn_gates(f_ref[...], lbd)
    o, k_hat, total = _hgrn_tile(qh, kk, ib, g, reverse, aux, ones_bd, state[...],
                                 scan_scratch[3:5])
    upd = jnp.where(_block_diag_mask(), _dot_tn(ib.astype(BF16), k_hat), 0.0)
    state[...] = state[...] * jnp.exp2(total) + upd

    prev_row = jnp.where(tile == 0, 0.0, xp_ref[TILE - 1:TILE, :])
    next_rows = jnp.where(tile == n_tiles - 1, 0.0, xn_ref[0:2, :])
    xc = _conv(x_ref[...], prev_row, next_rows, cw_ref, cb_ref, rows)
    gates = _dot(xc.astype(BF16), wg_ref[...]) + bg_ref[...]
    c0 = 2 * W_C * direction
    y, h_next = _rglru_tile(xc, gates[:, c0:c0 + W_C], gates[:, c0 + W_C:c0 + 2 * W_C],
                            lam_ref[direction:direction + 1, :], reverse, hstate[...],
                            scan_scratch[0:3])
    hstate[...] = h_next

    if direction == 0:
        o_ref[...] = o
        y_ref[...] = y
    else:
        o_ref[...] = _hgrn_out(of_ref[...] + o, g_ref[...], on_ref[...], ones_bd)
        y_ref[...] = (yf_ref[...] + y) * _gelu_tanh(gc_ref[...])


def _recurrence_lat(proj, n_batch, layer, depth, direction, state0, h0, prev, consts):
    t = proj.shape[0]
    n_tiles = t // (TILE * n_batch)

    def tile_of(b, s):
        return s if direction == 0 else n_tiles - 1 - s

    def row(b, s):
        return b * n_tiles + tile_of(b, s)

    def row_prev(b, s):
        return b * n_tiles + jnp.maximum(tile_of(b, s) - 1, 0)

    def row_next(b, s):
        return b * n_tiles + jnp.minimum(tile_of(b, s) + 1, n_tiles - 1)

    f_col = COL_FF if direction == 0 else COL_FB
    in_specs = [_col_spec(COL_QB, row), _col_spec(f_col, row), _col_spec(COL_IB, row),
                _col_spec(COL_GB, row), _col_spec(COL_XC, row), _col_spec(COL_XC, row_prev),
                _col_spec(COL_XC, row_next), _col_spec(COL_GC, row),
                pl.BlockSpec((1, 1, W_B, W_B), lambda b, s: (b, direction, 0, 0)),
                pl.BlockSpec((1, 1, 1, W_C), lambda b, s: (b, direction, 0, 0))]
    args = [proj] * 8 + [state0, h0]
    if direction == 1:
        in_specs += [pl.BlockSpec((TILE, W_B), lambda b, s: (row(b, s), 0)),
                     pl.BlockSpec((TILE, W_C), lambda b, s: (row(b, s), 0))]
        args += list(prev)
    in_specs += [_full_spec(a) for a in consts]
    args += list(consts)
    return pl.pallas_call(
        functools.partial(_rec_lat_kernel, layer=layer, depth=depth, direction=direction,
                          n_tiles=n_tiles),
        grid=(n_batch, n_tiles),
        in_specs=in_specs,
        out_specs=[pl.BlockSpec((TILE, W_B), lambda b, s: (row(b, s), 0)),
                   pl.BlockSpec((TILE, W_C), lambda b, s: (row(b, s), 0))],
        out_shape=[jax.ShapeDtypeStruct((t, W_B), F32), jax.ShapeDtypeStruct((t, W_C), F32)],
        scratch_shapes=[pltpu.VMEM((W_B, W_B), F32), pltpu.VMEM((1, W_C), F32)] + _scan_scratch(1),
        compiler_params=_params(("arbitrary", "arbitrary")),
        name="rec_lat_fwd" if direction == 0 else "rec_lat_bwd",
    )(*args)


def _out_kernel(x_ref, oa_ref, ob_ref, oc_ref, mod_ref, n2_ref, wo_ref, w1_ref, w2_ref, y_ref):
    x = x_ref[...]
    d = x.shape[-1]
    g1 = mod_ref[0, :, 2 * d:3 * d]
    sh2 = mod_ref[0, :, 3 * d:4 * d]
    sc2 = mod_ref[0, :, 4 * d:5 * d]
    g2 = mod_ref[0, :, 5 * d:6 * d]
    wa = H_A * DV_A
    mix = _dot(oa_ref[...].astype(BF16), wo_ref[0:wa, :])
    mix = mix + _dot(ob_ref[...].astype(BF16), wo_ref[wa:wa + W_B, :])
    mix = mix + _dot(oc_ref[...].astype(BF16), wo_ref[wa + W_B:, :])
    x1 = x + g1 * mix
    h = (_rms_rows(x1, n2_ref[...]) * (1.0 + sc2) + sh2).astype(BF16)
    acc = jnp.zeros_like(x)
    for c in range(w1_ref.shape[1] // D_FF_CHUNK):
        cs = slice(c * D_FF_CHUNK, (c + 1) * D_FF_CHUNK)
        f = jnp.maximum(_dot(h, w1_ref[:, cs]), 0.0)
        acc = acc + _dot((f * f).astype(BF16), w2_ref[cs, :])
    y_ref[...] = x1 + g2 * acc


def _out_mlp(x, oa, ob, oc, mod, tokens_per_mod, n2, w_out, w_ff1, w_ff2, layer):
    t, d = x.shape

    def resident(w):
        return pl.BlockSpec((None,) + w.shape[1:], lambda i: (layer, 0, 0),
                            pipeline_mode=pl.Buffered(1))

    return pl.pallas_call(
        _out_kernel,
        grid=(t // ROW_TILE,),
        in_specs=[
            pl.BlockSpec((ROW_TILE, d), lambda i: (i, 0)),
            pl.BlockSpec((ROW_TILE, oa.shape[1]), lambda i: (i, 0)),
            pl.BlockSpec((ROW_TILE, ob.shape[1]), lambda i: (i, 0)),
            pl.BlockSpec((ROW_TILE, oc.shape[1]), lambda i: (i, 0)),
            pl.BlockSpec((1, 1, N_MOD * d), lambda i: (i * ROW_TILE // tokens_per_mod, 0, 0)),
            pl.BlockSpec((1, d), lambda i: (0, 0)),
            resident(w_out),
            resident(w_ff1),
            resident(w_ff2),
        ],
        out_specs=pl.BlockSpec((ROW_TILE, d), lambda i: (i, 0)),
        out_shape=jax.ShapeDtypeStruct((t, d), F32),
        compiler_params=_params(("arbitrary",)),
        name="out_mlp",
    )(x, oa, ob, oc, mod, n2, w_out, w_ff1, w_ff2)


def _ones_block_diag(n, group):
    idx = np.arange(n) // group
    return jnp.asarray((idx[:, None] == idx[None, :]).astype(np.float32), dtype=BF16)


def _rope_tables(seq):
    nf = DQK_A // 4
    inv = ROPE_BASE ** (-jnp.arange(nf, dtype=F32) / nf)
    pos = jnp.arange(seq)
    ang_r = (pos // GRID_W).astype(F32)[:, None] * inv
    ang_c = (pos % GRID_W).astype(F32)[:, None] * inv
    z = jnp.zeros_like(ang_r)
    cos = jnp.concatenate([jnp.cos(ang_r)] * 2 + [jnp.cos(ang_c)] * 2, axis=-1)
    sin_lo = jnp.concatenate([-jnp.sin(ang_r), z, -jnp.sin(ang_c), z], axis=-1)
    sin_hi = jnp.concatenate([z, jnp.sin(ang_r), z, jnp.sin(ang_c)], axis=-1)
    return tuple(jnp.tile(a, (1, 2)) for a in (cos, sin_lo, sin_hi))


def _gate_weights(rg_w_l):
    eye = jnp.eye(H_C, dtype=rg_w_l.dtype)
    w = jnp.einsum('dghij,hk->dghikj', rg_w_l, eye)
    w = w.reshape(2, 2, W_C, W_C)
    return jnp.transpose(w, (2, 0, 1, 3)).reshape(W_C, 4 * W_C).astype(BF16)


def _state_block_diag_t(s):
    eye = jnp.eye(H_B, dtype=s.dtype)
    w = jnp.einsum('...hdv,hk->...hvkd', s, eye)
    return w.reshape(s.shape[:-3] + (W_B, W_B))


def kernel(x_prompt, x_sample, cache_k, cache_v, state_hgrn, state_rglru, c, c_ctx, w_mod, b_mod,
           norm1, norm2, w_in, w_out, qk_norm, diff_lambda, subln, hgrn_lb, hgrn_onorm, conv_w,
           conv_b, rg_w, rg_b, rg_lambda, w_ff1, w_ff2):
    batch, seq, d = x_prompt.shape
    dec_batch, dec_seq, _ = x_sample.shape
    depth = w_in.shape[0]
    past = cache_k.shape[2]
    wq = H_A * 2 * DQK_A

    c8 = jnp.concatenate([c_ctx[None], c, jnp.zeros((8 - 1 - dec_batch, d), F32)], axis=0)
    mod_all = _modulation(c8, w_mod, b_mod)

    assert DQK_A == DV_B and seq == TILE
    ones_b = _ones_block_diag(W_B, DV_B)
    rope_tabs = _rope_tables(dec_seq)
    ck = cache_k.reshape(dec_batch, depth, past, wq)
    cv = cache_v.reshape(dec_batch, depth, past, H_A * DV_A)
    lb2 = hgrn_lb.reshape(depth * 2, W_B)

    w_in_b, w_out_b, w1_b, w2_b = (w.astype(BF16) for w in (w_in, w_out, w_ff1, w_ff2))

    yp = x_prompt.reshape(batch * seq, d)
    ys = x_sample.reshape(dec_batch * dec_seq, d)
    caches, shs, srs = (), [], []
    for l in range(depth):
        lam_init = 0.8 - 0.6 * math.exp(-0.3 * l)
        mod_ctx = mod_all[l, 0:1][:, None, :]
        mod_lat = mod_all[l, 1:1 + dec_batch][:, None, :]
        n1 = norm1[l][None]
        n2 = norm2[l][None]
        gq = jnp.tile(qk_norm[l, 0], H_A * 2)[None] * (DQK_A ** -0.5 * LOG2E)
        gk = jnp.tile(qk_norm[l, 1], H_A * 2)[None]
        dl = diff_lambda[l]
        sg = subln[l][None]
        consts = (lb2, jnp.tile(hgrn_onorm[l], H_B)[None], conv_w[l], conv_b[l][None],
                  _gate_weights(rg_w[l]), rg_b[l].reshape(1, 4 * W_C), rg_lambda[l], ones_b)

        q, k, v_all, rest, kt_all = _in_projection(yp, mod_ctx, batch * seq, n1, w_in_b, l,
                                                   gq, gk, caches=caches)
        caches = (kt_all, v_all)
        oa = _attention_ctx(q, k, v_all, l, seq, dl, sg, lam_init)
        ob, oc, st, hs = _recurrence_ctx(rest, l, depth, consts)
        shs.append(st)
        srs.append(hs)
        yp = _out_mlp(yp, oa, ob, oc, mod_ctx, batch * seq, n2, w_out_b, w1_b, w2_b, l)

        q, k, v, rest = _in_projection(ys, mod_lat, dec_seq, n1, w_in_b, l, gq, gk)
        oa = _attention_lat(q, k, v, dec_batch, ck, cv, l, rope_tabs, dl, sg, lam_init)
        s0 = _state_block_diag_t(state_hgrn[:, l])
        h0 = state_rglru[:, l][:, :, None, :]
        fwd = _recurrence_lat(rest, dec_batch, l, depth, 0, s0, h0, None, consts)
        ob, oc = _recurrence_lat(rest, dec_batch, l, depth, 1, s0, h0, fwd, consts)
        ys = _out_mlp(ys, oa, ob, oc, mod_lat, dec_seq, n2, w_out_b, w1_b, w2_b, l)

    kt_all, v_all = caches
    new_k = jnp.transpose(kt_all.reshape(batch, depth, H_A, 2, DQK_A, seq), (0, 1, 5, 2, 3, 4))
    new_v = v_all.reshape(batch, depth, seq, H_A, DV_A)
    return (yp.reshape(batch, seq, d), ys.reshape(dec_batch, dec_seq, d), new_k, new_v,
            jnp.stack(shs, axis=1), jnp.stack(srs, axis=1))
```

```python
import functools
import math

import numpy as np
import jax
import jax.numpy as jnp
from jax import lax
from jax.experimental import pallas as pl
from jax.experimental.pallas import tpu as pltpu

F32 = jnp.float32
BF16 = jnp.bfloat16

GRID_W = 64
H_A, DQK_A, DV_A = 4, 64, 128
H_B, DK_B, DV_B = 4, 64, 64
W_B = H_B * DV_B
H_C, W_C = 4, 256
CONV_K = 4
RG_C = 8.0
ROPE_BASE = 10000.0
EPS = 1e-6
N_MOD = 6
D_FF_CHUNK = 1024

TILE = 256
HALF = TILE // 2
ROW_TILE = 512
ATTN_SEQS_PER_STEP = 4
SUBLANES = 8
LANES = 128
CHAIN_LEN = TILE // SUBLANES
LOG2E = math.log2(math.e)
V7X_VMEM_BYTES = 64 * 1024 * 1024
VMEM_LIMIT = V7X_VMEM_BYTES - 8 * 1024 * 1024


def _dot(a, b):
    return jnp.dot(a, b, preferred_element_type=F32)


def _dot_nt(a, b):
    return lax.dot_general(a, b, (((1,), (1,)), ((), ())), preferred_element_type=F32)


def _dot_tn(a, b):
    return lax.dot_general(a, b, (((0,), (0,)), ((), ())), preferred_element_type=F32)


def _group_sum(x, ones_bd):
    hi = x.astype(BF16)
    lo = (x - hi.astype(F32)).astype(BF16)
    return _dot(hi, ones_bd) + _dot(lo, ones_bd)


def _rms_rows(x, gain):
    return x * lax.rsqrt(jnp.mean(x * x, axis=-1, keepdims=True) + EPS) * gain


def _sigmoid(x):
    return 0.5 * jnp.tanh(0.5 * x) + 0.5


def _silu(x):
    return x * _sigmoid(x)


def _softplus(x):
    return jnp.maximum(x, 0.0) + jnp.log1p(jnp.exp(-jnp.abs(x)))


def _log_sigmoid(x):
    return jnp.minimum(x, 0.0) - jnp.log1p(jnp.exp(-jnp.abs(x)))


def _gelu_tanh(x):
    return x * (0.5 * (1.0 + jnp.tanh(math.sqrt(2.0 / math.pi) * (x + 0.044715 * (x * x * x)))))


def _params(semantics, flags=None):
    return pltpu.CompilerParams(dimension_semantics=semantics, vmem_limit_bytes=VMEM_LIMIT,
                                flags=flags)


def _mod_kernel(c_ref, w_ref, b_ref, o_ref):
    c = c_ref[...]
    o_ref[0] = _dot(_silu(c).astype(BF16), w_ref[0].astype(BF16)) + b_ref[0]


def _modulation(c8, w_mod, b_mod):
    depth, d, n = w_mod.shape
    tn = d
    return pl.pallas_call(
        _mod_kernel,
        grid=(depth, n // tn),
        in_specs=[
            pl.BlockSpec((8, d), lambda l, j: (0, 0)),
            pl.BlockSpec((1, d, tn), lambda l, j: (l, 0, j)),
            pl.BlockSpec((1, 1, tn), lambda l, j: (l, 0, j)),
        ],
        out_specs=pl.BlockSpec((1, 8, tn), lambda l, j: (l, 0, j)),
        out_shape=jax.ShapeDtypeStruct((depth, 8, n), F32),
        compiler_params=_params(("arbitrary", "arbitrary")),
        name="mod",
    )(c8, w_mod, b_mod.reshape(depth, 1, n))


def _proj_kernel(x_ref, mod_ref, n1_ref, w_ref, gq_ref, gk_ref, lb_ref, *refs, layers_before,
                 layer, depth):
    prev = refs[:2] if layers_before else ()
    q_ref, k_ref, v_ref, r_ref, *kt_ref = refs[len(prev):]
    x = x_ref[...]
    d = x.shape[-1]
    sh = mod_ref[0, :, 0:d]
    sc = mod_ref[0, :, d:2 * d]
    h = _rms_rows(x, n1_ref[...]) * (1.0 + sc) + sh
    p = _dot(h.astype(BF16), w_ref[...])
    wq = H_A * 2 * DQK_A

    first = lax.broadcasted_iota(jnp.int32, (1, LANES), 1) < DQK_A

    def qk_norm(z, gain):
        z2 = z * z
        parts = []
        for c in range(0, wq, LANES):
            slab = z2[:, c:c + LANES]
            lo = jnp.sum(jnp.where(first, slab, 0.0), axis=-1, keepdims=True)
            hi = jnp.sum(jnp.where(first, 0.0, slab), axis=-1, keepdims=True)
            parts.append(jnp.where(first, lo, hi))
        ms = jnp.concatenate(parts, axis=1) * (1.0 / DQK_A)
        return z * lax.rsqrt(ms + EPS) * gain

    wv = H_A * DV_A
    q_ref[...] = qk_norm(p[:, 0:wq], gq_ref[...])
    kn = qk_norm(p[:, wq:2 * wq], gk_ref[...])
    k_ref[...] = kn
    if kt_ref:
        for s in range(ROW_TILE // TILE):
            kt_ref[0][s, layers_before] = kn[s * TILE:(s + 1) * TILE].T
        if prev:
            ktp_ref, vp_ref = prev
            kt_ref[0][:, 0:layers_before] = ktp_ref[...]
            v_ref[:, 0:layers_before] = vp_ref[...]
    for s in range(ROW_TILE // TILE):
        v_dst = v_ref.at[s, layers_before] if kt_ref else v_ref.at[pl.ds(s * TILE * H_A, TILE * H_A)]
        for h in range(H_A):
            v_dst[pl.ds(h, TILE, stride=H_A), :] = (
                p[s * TILE:(s + 1) * TILE, 2 * wq + h * DV_A:2 * wq + (h + 1) * DV_A])
    rest = p[:, 2 * wq + wv:]
    blk = [rest[:, c * W_B:(c + 1) * W_B] for c in range(rest.shape[1] // W_B)]
    q_b, f_fwd, f_bwd, i_b, g_b, x_c, g_c = blk
    out = [_silu(q_b)]
    for direction, fpre in enumerate((f_fwd, f_bwd)):
        lbd = None if layer == 0 else _hgrn_lower_bound(lb_ref, layer, direction, depth)
        out.extend(_hgrn_gates(fpre, lbd))
    out += [i_b, _silu(g_b), x_c, _gelu_tanh(g_c)]
    for c, a in enumerate(out):
        r_ref[:, c * W_B:(c + 1) * W_B] = a


def _in_projection(x, mod, tokens_per_mod, n1, w_in, layer, gq, gk, lb, caches=None):
    t, d = x.shape
    layers_before = caches[0].shape[1] if caches else 0
    n_in = w_in.shape[2]
    wq = gq.shape[1]
    wv = H_A * DV_A
    widths = (wq, wq, wv, len(_REC_COLS) * W_B)
    depth = w_in.shape[0]
    seqs = ROW_TILE // TILE
    out_specs = [pl.BlockSpec((ROW_TILE, w), lambda i: (i, 0)) for w in widths]
    out_shape = [jax.ShapeDtypeStruct((t, w), F32) for w in widths]
    out_specs[2] = pl.BlockSpec((ROW_TILE * H_A, DV_A), lambda i: (i, 0))
    out_shape[2] = jax.ShapeDtypeStruct((t * H_A, DV_A), F32)
    cache_specs = []
    if caches is not None:
        n_seq, nl = t // TILE, layers_before + 1
        out_specs[2] = pl.BlockSpec((seqs, nl, TILE * H_A, DV_A), lambda i: (i, 0, 0, 0))
        out_shape[2] = jax.ShapeDtypeStruct((n_seq, nl, TILE * H_A, DV_A), F32)
        out_specs.append(pl.BlockSpec((seqs, nl, wq, TILE), lambda i: (i, 0, 0, 0)))
        out_shape.append(jax.ShapeDtypeStruct((n_seq, nl, wq, TILE), F32))
        if caches:
            cache_specs = [
                pl.BlockSpec((seqs, layers_before, wq, TILE), lambda i: (i, 0, 0, 0)),
                pl.BlockSpec((seqs, layers_before, TILE * H_A, DV_A), lambda i: (i, 0, 0, 0))]
    return pl.pallas_call(
        functools.partial(_proj_kernel, layers_before=layers_before, layer=layer, depth=depth),
        grid=(t // ROW_TILE,),
        in_specs=[
            pl.BlockSpec((ROW_TILE, d), lambda i: (i, 0)),
            pl.BlockSpec((1, 1, N_MOD * d), lambda i: (i * ROW_TILE // tokens_per_mod, 0, 0)),
            pl.BlockSpec((1, d), lambda i: (0, 0)),
            pl.BlockSpec((None, d, n_in), lambda i: (layer, 0, 0)),
            pl.BlockSpec((1, wq), lambda i: (0, 0)),
            pl.BlockSpec((1, wq), lambda i: (0, 0)),
            _full_spec(lb),
        ] + cache_specs,
        out_specs=out_specs,
        out_shape=out_shape,
        compiler_params=_params(("arbitrary",)),
        name="proj",
    )(x, mod, n1, w_in, gq, gk, lb, *(caches or ()))


def _diff_lambda(dl_ref, lam_init):
    lv = dl_ref[...]
    a = jnp.sum(lv[0:1] * lv[1:2], axis=-1, keepdims=True)
    b = jnp.sum(lv[2:3] * lv[3:4], axis=-1, keepdims=True)
    return jnp.exp(a) - jnp.exp(b) + lam_init


def _diff_attention(q, k_of, vext_of, lam, sub_gain, lam_init, o_ref):
    hw = 2 * DQK_A
    first = lax.broadcasted_iota(jnp.int32, (1, hw), 1) < DQK_A
    for h in range(H_A):
        hs = slice(h * hw, (h + 1) * hw)
        qh = q[:, hs]
        kh = k_of(h)
        vext = vext_of(h)
        maps = []
        for qm in (jnp.where(first, qh, 0.0), jnp.where(first, 0.0, qh)):
            s = _dot_nt(qm.astype(BF16), kh)
            e = jnp.exp2(s - jnp.max(s, axis=-1, keepdims=True)).astype(BF16)
            oe = _dot(e, vext)
            maps.append(oe[:, 0:DV_A] / oe[:, DV_A:])
        oh = maps[0] - lam * maps[1]
        o_ref[:, hs] = _rms_rows(oh, sub_gain) * (1.0 - lam_init)


def _attn_ctx_kernel(q_ref, k_ref, v_ref, dl_ref, sg_ref, o_ref, *, lam_init, seq):
    lam = _diff_lambda(dl_ref, lam_init)
    ones = jnp.ones((seq, DV_A), BF16)
    hw = 2 * DQK_A
    for s in range(v_ref.shape[0]):
        rows = pl.ds(s * seq, seq)
        k = k_ref[rows, :].astype(BF16)

        def vext(h, s=s):
            return jnp.concatenate(
                [v_ref[s, pl.ds(h, seq, stride=H_A), :].astype(BF16), ones], axis=1)

        _diff_attention(q_ref[rows, :], lambda h, k=k: k[:, h * hw:(h + 1) * hw], vext, lam,
                        sg_ref[...], lam_init, o_ref.at[rows])


def _attention_ctx(q, k, v_all, layer, seq, dl, sg, lam_init):
    t = q.shape[0]
    w = H_A * DV_A
    per_step = ATTN_SEQS_PER_STEP
    return pl.pallas_call(
        functools.partial(_attn_ctx_kernel, lam_init=lam_init, seq=seq),
        grid=(t // (seq * per_step),),
        in_specs=[
            pl.BlockSpec((per_step * seq, w), lambda b: (b, 0)),
            pl.BlockSpec((per_step * seq, w), lambda b: (b, 0)),
            pl.BlockSpec((per_step, None, seq * H_A, DV_A), lambda b: (b, layer, 0, 0)),
            pl.BlockSpec(dl.shape, lambda b: (0, 0)),
            pl.BlockSpec(sg.shape, lambda b: (0, 0)),
        ],
        out_specs=pl.BlockSpec((per_step * seq, w), lambda b: (b, 0)),
        out_shape=jax.ShapeDtypeStruct((t, w), F32),
        compiler_params=_params(("arbitrary",)),
        name="attn_ctx",
    )(q, k, v_all, dl, sg)


def _rope(x, cos, sin_lo, sin_hi):
    return (x * cos + pltpu.roll(x, 2 * DQK_A - 16, 1) * sin_lo + pltpu.roll(x, 16, 1) * sin_hi)


def _attn_lat_kernel(q_ref, k_ref, v_ref, ck_ref, cv_ref, qcos_ref, qslo_ref, qshi_ref,
                     kcos_ref, kslo_ref, kshi_ref, dl_ref, sg_ref, o_ref, kall, vall, *, lam_init):
    hw = 2 * DQK_A
    n_new = k_ref.shape[0]

    @pl.when(pl.program_id(1) == 0)
    def _():
        kcos, kslo, kshi = kcos_ref[...], kslo_ref[...], kshi_ref[...]
        for h in range(H_A):
            hs = slice(h * hw, (h + 1) * hw)
            kall[0:n_new, hs] = _rope(k_ref[:, hs], kcos, kslo, kshi).astype(BF16)
        kall[n_new:, :] = ck_ref[...].astype(BF16)
        ones = jnp.ones((vall.shape[0], DV_A), BF16)
        for h in range(H_A):
            vs = slice(h * DV_A, (h + 1) * DV_A)
            vall[0:n_new, 2 * h * DV_A:(2 * h + 1) * DV_A] = (
                v_ref[pl.ds(h, n_new, stride=H_A), :].astype(BF16))
            vall[n_new:, 2 * h * DV_A:(2 * h + 1) * DV_A] = cv_ref[:, vs].astype(BF16)
            vall[:, (2 * h + 1) * DV_A:(2 * h + 2) * DV_A] = ones

    lam = _diff_lambda(dl_ref, lam_init)
    qcos, qslo, qshi = qcos_ref[...], qslo_ref[...], qshi_ref[...]
    q = jnp.concatenate(
        [_rope(q_ref[:, h * hw:(h + 1) * hw], qcos, qslo, qshi) for h in range(H_A)], axis=-1)
    _diff_attention(q, lambda h: kall[:, h * hw:(h + 1) * hw],
                    lambda h: vall[:, 2 * h * DV_A:(2 * h + 2) * DV_A], lam, sg_ref[...],
                    lam_init, o_ref)


def _attention_lat(q, k, v, n_batch, cache_k, cache_v, layer, rope_tabs, dl, sg, lam_init):
    t = q.shape[0]
    seq = t // n_batch
    past = cache_k.shape[2]
    w = H_A * DV_A
    hw = 2 * DQK_A
    nq = seq // TILE
    cos, slo, shi = rope_tabs
    tab_q = pl.BlockSpec((TILE, hw), lambda b, j: (j, 0))
    tab_k = pl.BlockSpec((seq, hw), lambda b, j: (0, 0))
    return pl.pallas_call(
        functools.partial(_attn_lat_kernel, lam_init=lam_init),
        grid=(n_batch, nq),
        in_specs=[
            pl.BlockSpec((TILE, w), lambda b, j: (b * nq + j, 0)),
            pl.BlockSpec((seq, w), lambda b, j: (b, 0)),
            pl.BlockSpec((seq * H_A, DV_A), lambda b, j: (b, 0)),
            pl.BlockSpec((None, None, past, w), lambda b, j: (b, layer, 0, 0)),
            pl.BlockSpec((None, None, past, w), lambda b, j: (b, layer, 0, 0)),
            tab_q, tab_q, tab_q, tab_k, tab_k, tab_k,
            pl.BlockSpec(dl.shape, lambda b, j: (0, 0)),
            pl.BlockSpec(sg.shape, lambda b, j: (0, 0)),
        ],
        out_specs=pl.BlockSpec((TILE, w), lambda b, j: (b * nq + j, 0)),
        out_shape=jax.ShapeDtypeStruct((t, w), F32),
        scratch_shapes=[pltpu.VMEM((seq + past, w), BF16), pltpu.VMEM((seq + past, 2 * w), BF16)],
        compiler_params=_params(("arbitrary", "arbitrary")),
        name="attn_lat",
    )(q, k, v, cache_k, cache_v, cos, slo, shi, cos, slo, shi, dl, sg)


def _tile_aux():
    rows = lax.broadcasted_iota(jnp.int32, (TILE, 1), 0)
    lane = lax.broadcasted_iota(jnp.int32, (1, W_B), 1)
    head_masks = [(lane // DK_B) == h for h in range(H_B)]
    first_of_pair = lax.broadcasted_iota(jnp.int32, (1, 2 * DK_B), 1) < DK_B
    xor = (lax.broadcasted_iota(jnp.int32, (HALF, HALF), 0)
           ^ lax.broadcasted_iota(jnp.int32, (HALF, HALF), 1))
    levels = [1 << i for i in range(TILE.bit_length() - 1)]
    right = {m: (rows & m) != 0 for m in levels}
    same_block = {m: xor < 2 * m for m in levels if 2 * m < HALF}
    return rows, head_masks, first_of_pair, right, same_block


def _hgrn_lower_bound(lb_ref, layer, direction, depth):
    xs = [lb_ref[2 * j + direction:2 * j + direction + 1, :] for j in range(depth)]
    m = functools.reduce(jnp.maximum, xs)
    es = [jnp.exp(x - m) for x in xs]
    return sum(es[1:layer + 1]) / sum(es)


def _hgrn_gates(fpre, lbd):
    t = jnp.exp(-jnp.abs(fpre))
    log_sig = jnp.minimum(fpre, 0.0) - jnp.log(1.0 + t)
    r = 1.0 / (1.0 + t)
    sig_neg = jnp.where(fpre >= 0.0, t * r, r)
    if lbd is None:
        return log_sig * LOG2E, sig_neg
    a = jnp.log(lbd)
    b = jnp.log1p(-lbd) + log_sig
    logf = jnp.maximum(a, b) + jnp.log1p(jnp.exp(-jnp.abs(a - b)))
    return logf * LOG2E, (1.0 - lbd) * sig_neg


def _chain_row(token):
    return SUBLANES * (token % CHAIN_LEN) + token // CHAIN_LEN


def _tile_prefix(g, g_scr, b_scr):
    g_c = _to_chains(g, g_scr)
    run = [g_c[0]]
    for j in range(1, CHAIN_LEN):
        run.append(run[-1] + g_c[j])
    sub = lax.broadcasted_iota(jnp.int32, (SUBLANES, 1), 0)
    incl = run[-1]
    k = 1
    while k < SUBLANES:
        incl = incl + jnp.where(sub >= k, pltpu.roll(incl, k, 0), 0.0)
        k *= 2
    offset = incl - run[-1]
    prefix = _from_chains([r + offset for r in run], b_scr)
    return prefix, incl[SUBLANES - 1:SUBLANES, :]


def _level_exponent(z, g, b_scr, m, reverse, right):
    if m == 1:
        return jnp.where(right, g, 0.0) if not reverse else jnp.where(right, 0.0, g)
    slabs = b_scr.shape[0]

    def ref_rows(token):
        row = _chain_row(token)
        return jnp.concatenate(
            [jnp.broadcast_to(b_scr[s, row:row + 1, :], (SUBLANES, LANES)) for s in range(slabs)],
            axis=1)

    groups = []
    for gq in range(TILE // SUBLANES):
        first = gq * SUBLANES
        zg = z[first:first + SUBLANES]
        if 2 * m >= SUBLANES:
            block = first // (2 * m) * (2 * m)
            ref = ref_rows(block + m - 1)
            if m >= SUBLANES:
                groups.append(zg - ref if first - block >= m else ref - zg)
                continue
        else:
            sub = lax.broadcasted_iota(jnp.int32, (SUBLANES, 1), 0)
            ref = jnp.where(sub < 2 * m, ref_rows(first + m - 1), ref_rows(first + 3 * m - 1))
        groups.append(-jnp.abs(zg - ref))
    return jnp.concatenate(groups, axis=0)


def _hgrn_tile(qh, kk, vv, g, reverse, aux, ones_bd, state_t, scratch):
    rows, head_masks, first_of_pair, right_rows, same_block = aux
    pair_w = 2 * DK_B
    g_scr, b_scr = scratch
    prefix, total = _tile_prefix(g, g_scr, b_scr)
    z = prefix if not reverse else prefix - g
    diag = [[None, None] for _ in range(H_B)]
    off = [None] * H_B
    m = 1
    while m < TILE:
        right = right_rows[m]
        x = jnp.exp2(_level_exponent(z, g, b_scr, m, reverse, right))
        q_rows = right if not reverse else jnp.logical_not(right)
        qz = jnp.where(q_rows, (qh * x).astype(BF16), 0.0)
        kz = jnp.where(q_rows, 0.0, (kk * x).astype(BF16))
        for h in range(H_B):
            ls = slice((h // 2) * pair_w, (h // 2 + 1) * pair_w)
            qm = jnp.where(first_of_pair, qz[:, ls], 0.0) if h % 2 == 0 else \
                jnp.where(first_of_pair, 0.0, qz[:, ls])
            kh = kz[:, ls]
            if m < HALF:
                for i in range(2):
                    rs = slice(i * HALF, (i + 1) * HALF)
                    pm = _dot_nt(qm[rs], kh[rs])
                    if m in same_block:
                        pm = jnp.where(same_block[m], pm, 0.0)
                    diag[h][i] = pm if diag[h][i] is None else diag[h][i] + pm
            else:
                lo, hi = slice(0, HALF), slice(HALF, TILE)
                off[h] = _dot_nt(qm[hi], kh[lo]) if not reverse else _dot_nt(qm[lo], kh[hi])
        m *= 2
    vb = vv.astype(BF16)
    top = bot = None
    for h in range(H_B):
        vh = jnp.where(head_masks[h], vb, 0.0)
        d0, d1, of = (a.astype(BF16) for a in (diag[h][0], diag[h][1], off[h]))
        if not reverse:
            t = _dot(d0, vh[0:HALF])
            b = _dot(jnp.concatenate([of, d1], axis=1), vh)
        else:
            t = _dot(jnp.concatenate([d0, of], axis=1), vh)
            b = _dot(d1, vh[HALF:])
        top = t if top is None else top + t
        bot = b if bot is None else bot + b
    out = _group_sum(qh * kk, ones_bd) * vv + jnp.concatenate([top, bot], axis=0)
    q_decay, k_decay = (z, total - z) if not reverse else (total - z, z)
    if state_t is not None:
        out = out + _dot_nt((qh * jnp.exp2(q_decay)).astype(BF16), state_t.astype(BF16))
    k_hat = (kk * jnp.exp2(k_decay)).astype(BF16)
    return out, k_hat, total


def _block_diag_mask():
    r = lax.broadcasted_iota(jnp.int32, (W_B, W_B), 0) // DK_B
    c = lax.broadcasted_iota(jnp.int32, (W_B, W_B), 1) // DV_B
    return r == c


def _conv(x, prev_row, next_rows, cw_ref, cb_ref, rows):
    x_m1 = jnp.where(rows == 0, prev_row, pltpu.roll(x, 1, 0))
    x_p1 = jnp.where(rows == TILE - 1, next_rows[0:1], pltpu.roll(x, TILE - 1, 0))
    x_p2 = jnp.where(rows == TILE - 2, next_rows[0:1],
                     jnp.where(rows == TILE - 1, next_rows[1:2], pltpu.roll(x, TILE - 2, 0)))
    y = cb_ref[...] + x_m1 * cw_ref[0:1, :]
    y = y + x * cw_ref[1:2, :]
    y = y + x_p1 * cw_ref[2:3, :]
    return y + x_p2 * cw_ref[3:4, :]


def _chain_address(g):
    per_chain = CHAIN_LEN // SUBLANES
    return SUBLANES * SUBLANES * (g % per_chain) + g // per_chain, SUBLANES


def _to_chains(x, scr):
    slabs = scr.shape[0]
    for s in range(slabs):
        for g in range(TILE // SUBLANES):
            start, stride = _chain_address(g)
            scr[s, pl.ds(start, SUBLANES, stride=stride), :] = (
                x[g * SUBLANES:(g + 1) * SUBLANES, s * LANES:(s + 1) * LANES])
    return [jnp.concatenate([scr[s, j * SUBLANES:(j + 1) * SUBLANES, :] for s in range(slabs)],
                            axis=1) for j in range(CHAIN_LEN)]


def _from_chains(chains, scr):
    slabs = scr.shape[0]
    for s in range(slabs):
        for j, c in enumerate(chains):
            scr[s, j * SUBLANES:(j + 1) * SUBLANES, :] = c[:, s * LANES:(s + 1) * LANES]
    groups = []
    for g in range(TILE // SUBLANES):
        start, stride = _chain_address(g)
        groups.append(jnp.concatenate(
            [scr[s, pl.ds(start, SUBLANES, stride=stride), :] for s in range(slabs)], axis=1))
    return jnp.concatenate(groups, axis=0)


def _rglru_tile(xc, r_pre, i_pre, lam_row, reverse, h_in, scratch):
    a_scr, u_scr, h_scr = scratch
    r = _sigmoid(r_pre)
    i = _sigmoid(i_pre)
    log_a = -RG_C * r * _softplus(-lam_row)
    a = jnp.exp(log_a)
    u = jnp.sqrt(jnp.tanh(-log_a) * (a * a + 1.0)) * (i * xc)
    a_c = _to_chains(a, a_scr)
    u_c = _to_chains(u, u_scr)
    order = list(reversed(range(CHAIN_LEN))) if reverse else list(range(CHAIN_LEN))
    h_c = [None] * CHAIN_LEN
    p_c = [None] * CHAIN_LEN
    h_c[order[0]], p_c[order[0]] = u_c[order[0]], a_c[order[0]]
    for prev, j in zip(order[:-1], order[1:]):
        h_c[j] = a_c[j] * h_c[prev] + u_c[j]
        p_c[j] = a_c[j] * p_c[prev]
    p_tot, h_tot = p_c[order[-1]], h_c[order[-1]]
    sub = lax.broadcasted_iota(jnp.int32, (SUBLANES, 1), 0)
    k = 1
    while k < SUBLANES:
        if not reverse:
            valid = sub >= k
            p_s, h_s = pltpu.roll(p_tot, k, 0), pltpu.roll(h_tot, k, 0)
        else:
            valid = sub < SUBLANES - k
            p_s, h_s = pltpu.roll(p_tot, SUBLANES - k, 0), pltpu.roll(h_tot, SUBLANES - k, 0)
        h_tot = h_tot + p_tot * jnp.where(valid, h_s, 0.0)
        p_tot = p_tot * jnp.where(valid, p_s, 1.0)
        k *= 2
    if h_in is not None:
        h_tot = h_tot + p_tot * h_in
    if not reverse:
        carry = jnp.where(sub == 0, 0.0 if h_in is None else h_in, pltpu.roll(h_tot, 1, 0))
        h_out = h_tot[SUBLANES - 1:SUBLANES, :]
    else:
        carry = jnp.where(sub == SUBLANES - 1, 0.0 if h_in is None else h_in,
                          pltpu.roll(h_tot, SUBLANES - 1, 0))
        h_out = h_tot[0:1, :]
    h_c = [h + p * carry for h, p in zip(h_c, p_c)]
    return _from_chains(h_c, h_scr), h_out


def _hgrn_out(o_sum, gate, onorm_gain, ones_bd):
    ms = _group_sum(o_sum * o_sum, ones_bd) * (1.0 / DV_B)
    return o_sum * lax.rsqrt(ms + EPS) * onorm_gain * gate


def _rec_ctx_kernel(q_ref, gf_ref, kf_ref, gb_ref, kb_ref, i_ref, sg_ref, x_ref, gg_ref, on_ref,
                    cw_ref, cb_ref, wg_ref, bg_ref, lam_ref, ones_ref, ob_ref, oc_ref, st_ref,
                    hs_ref, *scan_scratch):
    aux = _tile_aux()
    rows = aux[0]
    ones_bd = ones_ref[...]
    qh, ib, xcol = q_ref[...], i_ref[...], x_ref[...]
    o_sum = None
    for direction, (g_ref, k_ref) in enumerate(((gf_ref, kf_ref), (gb_ref, kb_ref))):
        g, kk = g_ref[...], k_ref[...]
        o, k_hat, _ = _hgrn_tile(qh, kk, ib, g, direction == 1, aux, ones_bd, None,
                                 scan_scratch[3 + 2 * direction:5 + 2 * direction])
        o_sum = o if o_sum is None else o_sum + o
        full = _dot_tn(k_hat, ib.astype(BF16))
        for h in range(H_B):
            blk = full[h * DK_B:(h + 1) * DK_B, (h // 2) * 2 * DV_B:(h // 2 + 1) * 2 * DV_B]
            if h % 2:
                blk = pltpu.roll(blk, DV_B, 1)
            st_ref[0, direction, h] = blk[:, 0:DV_B]
    ob_ref[...] = _hgrn_out(o_sum, sg_ref[...], on_ref[...], ones_bd)

    zero_row = jnp.zeros((1, W_C), F32)
    xc = _conv(xcol, zero_row, jnp.zeros((2, W_C), F32), cw_ref, cb_ref, rows)
    gates = _dot(xc.astype(BF16), wg_ref[...]) + bg_ref[...]
    y_f, h_f = _rglru_tile(xc, gates[:, 0:W_C], gates[:, W_C:2 * W_C], lam_ref[0:1, :], False,
                           None, scan_scratch[0:3])
    y_b, h_b = _rglru_tile(xc, gates[:, 2 * W_C:3 * W_C], gates[:, 3 * W_C:], lam_ref[1:2, :], True,
                           None, scan_scratch[0:3])
    oc_ref[...] = (y_f + y_b) * gg_ref[...]
    hs_ref[0, 0:1, :] = h_f
    hs_ref[0, 1:2, :] = h_b


def _scan_scratch(hgrn_directions):
    return [pltpu.VMEM((W_C // LANES, TILE, LANES), F32) for _ in range(3 + 2 * hgrn_directions)]


def _col_spec(col, row_map):
    return pl.BlockSpec((TILE, W_B), lambda *idx: (row_map(*idx), col))


COL_QH, COL_GF, COL_KF, COL_GB, COL_KB, COL_IB, COL_SG, COL_XC, COL_GG = range(9)
_REC_COLS = (COL_QH, COL_GF, COL_KF, COL_GB, COL_KB, COL_IB, COL_SG, COL_XC, COL_GG)


def _full_spec(a):
    nd = a.ndim
    return pl.BlockSpec(a.shape, lambda *idx: (0,) * nd)


def _recurrence_ctx(proj, consts):
    t = proj.shape[0]
    n = t // TILE
    return pl.pallas_call(
        _rec_ctx_kernel,
        grid=(n,),
        in_specs=[_col_spec(c, lambda b: b) for c in _REC_COLS] + [_full_spec(a) for a in consts],
        out_specs=[
            pl.BlockSpec((TILE, W_B), lambda b: (b, 0)),
            pl.BlockSpec((TILE, W_C), lambda b: (b, 0)),
            pl.BlockSpec((1, 2, H_B, DK_B, DV_B), lambda b: (b, 0, 0, 0, 0)),
            pl.BlockSpec((1, 2, W_C), lambda b: (b, 0, 0)),
        ],
        out_shape=[
            jax.ShapeDtypeStruct((t, W_B), F32),
            jax.ShapeDtypeStruct((t, W_C), F32),
            jax.ShapeDtypeStruct((n, 2, H_B, DK_B, DV_B), F32),
            jax.ShapeDtypeStruct((n, 2, W_C), F32),
        ],
        scratch_shapes=_scan_scratch(2),
        compiler_params=_params(("arbitrary",)),
        name="rec_ctx",
    )(*([proj] * len(_REC_COLS)), *consts)


def _rec_lat_kernel(q_ref, g_ref, k_ref, i_ref, sg_ref, x_ref, xp_ref, xn_ref, gg_ref, s0_ref,
                    h0_ref, *rest, direction, n_tiles):
    if direction == 0:
        (on_ref, cw_ref, cb_ref, wg_ref, bg_ref, lam_ref, ones_ref,
         o_ref, y_ref, state, hstate, *scan_scratch) = rest
    else:
        (of_ref, yf_ref, on_ref, cw_ref, cb_ref, wg_ref, bg_ref, lam_ref, ones_ref,
         o_ref, y_ref, state, hstate, *scan_scratch) = rest
    step = pl.program_id(1)
    tile = step if direction == 0 else n_tiles - 1 - step
    reverse = direction == 1

    @pl.when(step == 0)
    def _():
        state[...] = s0_ref[0, 0]
        hstate[...] = h0_ref[0, 0]

    aux = _tile_aux()
    rows = aux[0]
    ones_bd = ones_ref[...]
    qh, ib, g, kk = q_ref[...], i_ref[...], g_ref[...], k_ref[...]
    o, k_hat, total = _hgrn_tile(qh, kk, ib, g, reverse, aux, ones_bd, state[...],
                                 scan_scratch[3:5])
    upd = jnp.where(_block_diag_mask(), _dot_tn(ib.astype(BF16), k_hat), 0.0)
    state[...] = state[...] * jnp.exp2(total) + upd

    prev_row = jnp.where(tile == 0, 0.0, xp_ref[TILE - 1:TILE, :])
    next_rows = jnp.where(tile == n_tiles - 1, 0.0, xn_ref[0:2, :])
    xc = _conv(x_ref[...], prev_row, next_rows, cw_ref, cb_ref, rows)
    gates = _dot(xc.astype(BF16), wg_ref[...]) + bg_ref[...]
    c0 = 2 * W_C * direction
    y, h_next = _rglru_tile(xc, gates[:, c0:c0 + W_C], gates[:, c0 + W_C:c0 + 2 * W_C],
                            lam_ref[direction:direction + 1, :], reverse, hstate[...],
                            scan_scratch[0:3])
    hstate[...] = h_next

    if direction == 0:
        o_ref[...] = o
        y_ref[...] = y
    else:
        o_ref[...] = _hgrn_out(of_ref[...] + o, sg_ref[...], on_ref[...], ones_bd)
        y_ref[...] = (yf_ref[...] + y) * gg_ref[...]


def _recurrence_lat(proj, n_batch, direction, state0, h0, prev, consts):
    t = proj.shape[0]
    n_tiles = t // (TILE * n_batch)

    def tile_of(b, s):
        return s if direction == 0 else n_tiles - 1 - s

    def row(b, s):
        return b * n_tiles + tile_of(b, s)

    def row_prev(b, s):
        return b * n_tiles + jnp.maximum(tile_of(b, s) - 1, 0)

    def row_next(b, s):
        return b * n_tiles + jnp.minimum(tile_of(b, s) + 1, n_tiles - 1)

    g_col, k_col = (COL_GF, COL_KF) if direction == 0 else (COL_GB, COL_KB)
    in_specs = [_col_spec(COL_QH, row), _col_spec(g_col, row), _col_spec(k_col, row),
                _col_spec(COL_IB, row), _col_spec(COL_SG, row), _col_spec(COL_XC, row),
                _col_spec(COL_XC, row_prev), _col_spec(COL_XC, row_next), _col_spec(COL_GG, row),
                pl.BlockSpec((1, 1, W_B, W_B), lambda b, s: (b, direction, 0, 0)),
                pl.BlockSpec((1, 1, 1, W_C), lambda b, s: (b, direction, 0, 0))]
    args = [proj] * 9 + [state0, h0]
    if direction == 1:
        in_specs += [pl.BlockSpec((TILE, W_B), lambda b, s: (row(b, s), 0)),
                     pl.BlockSpec((TILE, W_C), lambda b, s: (row(b, s), 0))]
        args += list(prev)
    in_specs += [_full_spec(a) for a in consts]
    args += list(consts)
    return pl.pallas_call(
        functools.partial(_rec_lat_kernel, direction=direction, n_tiles=n_tiles),
        grid=(n_batch, n_tiles),
        in_specs=in_specs,
        out_specs=[pl.BlockSpec((TILE, W_B), lambda b, s: (row(b, s), 0)),
                   pl.BlockSpec((TILE, W_C), lambda b, s: (row(b, s), 0))],
        out_shape=[jax.ShapeDtypeStruct((t, W_B), F32), jax.ShapeDtypeStruct((t, W_C), F32)],
        scratch_shapes=[pltpu.VMEM((W_B, W_B), F32), pltpu.VMEM((1, W_C), F32)] + _scan_scratch(1),
        compiler_params=_params(("arbitrary", "arbitrary")),
        name="rec_lat_fwd" if direction == 0 else "rec_lat_bwd",
    )(*args)


def _out_kernel(x_ref, oa_ref, ob_ref, oc_ref, mod_ref, n2_ref, wo_ref, w1_ref, w2_ref, y_ref):
    x = x_ref[...]
    d = x.shape[-1]
    g1 = mod_ref[0, :, 2 * d:3 * d]
    sh2 = mod_ref[0, :, 3 * d:4 * d]
    sc2 = mod_ref[0, :, 4 * d:5 * d]
    g2 = mod_ref[0, :, 5 * d:6 * d]
    wa = H_A * DV_A
    mix = _dot(oa_ref[...].astype(BF16), wo_ref[0:wa, :])
    mix = mix + _dot(ob_ref[...].astype(BF16), wo_ref[wa:wa + W_B, :])
    mix = mix + _dot(oc_ref[...].astype(BF16), wo_ref[wa + W_B:, :])
    x1 = x + g1 * mix
    h = (_rms_rows(x1, n2_ref[...]) * (1.0 + sc2) + sh2).astype(BF16)
    acc = jnp.zeros_like(x)
    for c in range(w1_ref.shape[1] // D_FF_CHUNK):
        cs = slice(c * D_FF_CHUNK, (c + 1) * D_FF_CHUNK)
        f = jnp.maximum(_dot(h, w1_ref[:, cs]), 0.0)
        acc = acc + _dot((f * f).astype(BF16), w2_ref[cs, :])
    y_ref[...] = x1 + g2 * acc


def _out_mlp(x, oa, ob, oc, mod, tokens_per_mod, n2, w_out, w_ff1, w_ff2, layer):
    t, d = x.shape

    def resident(w):
        return pl.BlockSpec((None,) + w.shape[1:], lambda i: (layer, 0, 0),
                            pipeline_mode=pl.Buffered(1))

    return pl.pallas_call(
        _out_kernel,
        grid=(t // ROW_TILE,),
        in_specs=[
            pl.BlockSpec((ROW_TILE, d), lambda i: (i, 0)),
            pl.BlockSpec((ROW_TILE, oa.shape[1]), lambda i: (i, 0)),
            pl.BlockSpec((ROW_TILE, ob.shape[1]), lambda i: (i, 0)),
            pl.BlockSpec((ROW_TILE, oc.shape[1]), lambda i: (i, 0)),
            pl.BlockSpec((1, 1, N_MOD * d), lambda i: (i * ROW_TILE // tokens_per_mod, 0, 0)),
            pl.BlockSpec((1, d), lambda i: (0, 0)),
            resident(w_out),
            resident(w_ff1),
            resident(w_ff2),
        ],
        out_specs=pl.BlockSpec((ROW_TILE, d), lambda i: (i, 0)),
        out_shape=jax.ShapeDtypeStruct((t, d), F32),
        compiler_params=_params(("arbitrary",)),
        name="out_mlp",
    )(x, oa, ob, oc, mod, n2, w_out, w_ff1, w_ff2)


def _ones_block_diag(n, group):
    idx = np.arange(n) // group
    return jnp.asarray((idx[:, None] == idx[None, :]).astype(np.float32), dtype=BF16)


def _rope_tables(seq):
    nf = DQK_A // 4
    inv = ROPE_BASE ** (-jnp.arange(nf, dtype=F32) / nf)
    pos = jnp.arange(seq)
    ang_r = (pos // GRID_W).astype(F32)[:, None] * inv
    ang_c = (pos % GRID_W).astype(F32)[:, None] * inv
    z = jnp.zeros_like(ang_r)
    cos = jnp.concatenate([jnp.cos(ang_r)] * 2 + [jnp.cos(ang_c)] * 2, axis=-1)
    sin_lo = jnp.concatenate([-jnp.sin(ang_r), z, -jnp.sin(ang_c), z], axis=-1)
    sin_hi = jnp.concatenate([z, jnp.sin(ang_r), z, jnp.sin(ang_c)], axis=-1)
    return tuple(jnp.tile(a, (1, 2)) for a in (cos, sin_lo, sin_hi))


def _gate_weights(rg_w_l):
    eye = jnp.eye(H_C, dtype=rg_w_l.dtype)
    w = jnp.einsum('dghij,hk->dghikj', rg_w_l, eye)
    w = w.reshape(2, 2, W_C, W_C)
    return jnp.transpose(w, (2, 0, 1, 3)).reshape(W_C, 4 * W_C).astype(BF16)


def _state_block_diag_t(s):
    eye = jnp.eye(H_B, dtype=s.dtype)
    w = jnp.einsum('...hdv,hk->...hvkd', s, eye)
    return w.reshape(s.shape[:-3] + (W_B, W_B))


def kernel(x_prompt, x_sample, cache_k, cache_v, state_hgrn, state_rglru, c, c_ctx, w_mod, b_mod,
           norm1, norm2, w_in, w_out, qk_norm, diff_lambda, subln, hgrn_lb, hgrn_onorm, conv_w,
           conv_b, rg_w, rg_b, rg_lambda, w_ff1, w_ff2):
    batch, seq, d = x_prompt.shape
    dec_batch, dec_seq, _ = x_sample.shape
    depth = w_in.shape[0]
    past = cache_k.shape[2]
    wq = H_A * 2 * DQK_A

    c8 = jnp.concatenate([c_ctx[None], c, jnp.zeros((8 - 1 - dec_batch, d), F32)], axis=0)
    mod_all = _modulation(c8, w_mod, b_mod)

    assert DQK_A == DV_B and seq == TILE
    ones_b = _ones_block_diag(W_B, DV_B)
    rope_tabs = _rope_tables(dec_seq)
    ck = cache_k.reshape(dec_batch, depth, past, wq)
    cv = cache_v.reshape(dec_batch, depth, past, H_A * DV_A)
    lb2 = hgrn_lb.reshape(depth * 2, W_B)

    w_in_b, w_out_b, w1_b, w2_b = (w.astype(BF16) for w in (w_in, w_out, w_ff1, w_ff2))

    yp = x_prompt.reshape(batch * seq, d)
    ys = x_sample.reshape(dec_batch * dec_seq, d)
    caches, shs, srs = (), [], []
    for l in range(depth):
        lam_init = 0.8 - 0.6 * math.exp(-0.3 * l)
        mod_ctx = mod_all[l, 0:1][:, None, :]
        mod_lat = mod_all[l, 1:1 + dec_batch][:, None, :]
        n1 = norm1[l][None]
        n2 = norm2[l][None]
        gq = jnp.tile(qk_norm[l, 0], H_A * 2)[None] * (DQK_A ** -0.5 * LOG2E)
        gk = jnp.tile(qk_norm[l, 1], H_A * 2)[None]
        dl = diff_lambda[l]
        sg = subln[l][None]
        consts = (jnp.tile(hgrn_onorm[l], H_B)[None], conv_w[l], conv_b[l][None],
                  _gate_weights(rg_w[l]), rg_b[l].reshape(1, 4 * W_C), rg_lambda[l], ones_b)

        q, k, v_all, rest, kt_all = _in_projection(yp, mod_ctx, batch * seq, n1, w_in_b, l,
                                                   gq, gk, lb2, caches=caches)
        caches = (kt_all, v_all)
        oa = _attention_ctx(q, k, v_all, l, seq, dl, sg, lam_init)
        ob, oc, st, hs = _recurrence_ctx(rest, consts)
        shs.append(st)
        srs.append(hs)
        yp = _out_mlp(yp, oa, ob, oc, mod_ctx, batch * seq, n2, w_out_b, w1_b, w2_b, l)

        q, k, v, rest = _in_projection(ys, mod_lat, dec_seq, n1, w_in_b, l, gq, gk, lb2)
        oa = _attention_lat(q, k, v, dec_batch, ck, cv, l, rope_tabs, dl, sg, lam_init)
        s0 = _state_block_diag_t(state_hgrn[:, l])
        h0 = state_rglru[:, l][:, :, None, :]
        fwd = _recurrence_lat(rest, dec_batch, 0, s0, h0, None, consts)
        ob, oc = _recurrence_lat(rest, dec_batch, 1, s0, h0, fwd, consts)
        ys = _out_mlp(ys, oa, ob, oc, mod_lat, dec_seq, n2, w_out_b, w1_b, w2_b, l)

    kt_all, v_all = caches
    new_k = jnp.transpose(kt_all.reshape(batch, depth, H_A, 2, DQK_A, seq), (0, 1, 5, 2, 3, 4))
    new_v = v_all.reshape(batch, depth, seq, H_A, DV_A)
    return (yp.reshape(batch, seq, d), ys.reshape(dec_batch, dec_seq, d), new_k, new_v,
            jnp.stack(shs, axis=1), jnp.stack(srs, axis=1))
```

```python
import functools
import math

import numpy as np
import jax
import jax.numpy as jnp
from jax import lax
from jax.experimental import pallas as pl
from jax.experimental.pallas import tpu as pltpu

F32 = jnp.float32
BF16 = jnp.bfloat16

GRID_W = 64
H_A, DQK_A, DV_A = 4, 64, 128
H_B, DK_B, DV_B = 4, 64, 64
W_B = H_B * DV_B
H_C, W_C = 4, 256
CONV_K = 4
RG_C = 8.0
ROPE_BASE = 10000.0
EPS = 1e-6
N_MOD = 6
D_FF_CHUNK = 1024

TILE = 256
HALF = TILE // 2
ROW_TILE = 512
ATTN_SEQS_PER_STEP = 4
SUBLANES = 8
LANES = 128
CHAIN_LEN = TILE // SUBLANES
LOG2E = math.log2(math.e)
V7X_VMEM_BYTES = 64 * 1024 * 1024
VMEM_LIMIT = V7X_VMEM_BYTES - 8 * 1024 * 1024


def _dot(a, b):
    return jnp.dot(a, b, preferred_element_type=F32)


def _dot_nt(a, b):
    return lax.dot_general(a, b, (((1,), (1,)), ((), ())), preferred_element_type=F32)


def _dot_tn(a, b):
    return lax.dot_general(a, b, (((0,), (0,)), ((), ())), preferred_element_type=F32)


def _group_sum(x, ones_bd):
    hi = x.astype(BF16)
    lo = (x - hi.astype(F32)).astype(BF16)
    return _dot(hi, ones_bd) + _dot(lo, ones_bd)


def _rms_rows(x, gain):
    return x * lax.rsqrt(jnp.mean(x * x, axis=-1, keepdims=True) + EPS) * gain


def _sigmoid(x):
    return 0.5 * jnp.tanh(0.5 * x) + 0.5


def _silu(x):
    return x * _sigmoid(x)


def _softplus(x):
    return jnp.maximum(x, 0.0) + jnp.log1p(jnp.exp(-jnp.abs(x)))


def _log_sigmoid(x):
    return jnp.minimum(x, 0.0) - jnp.log1p(jnp.exp(-jnp.abs(x)))


def _gelu_tanh(x):
    return x * (0.5 * (1.0 + jnp.tanh(math.sqrt(2.0 / math.pi) * (x + 0.044715 * (x * x * x)))))


def _params(semantics, flags=None):
    return pltpu.CompilerParams(dimension_semantics=semantics, vmem_limit_bytes=VMEM_LIMIT,
                                flags=flags)


def _mod_kernel(c_ref, w_ref, b_ref, o_ref):
    c = c_ref[...]
    o_ref[0] = _dot(_silu(c).astype(BF16), w_ref[0].astype(BF16)) + b_ref[0]


def _modulation(c8, w_mod, b_mod):
    depth, d, n = w_mod.shape
    tn = d
    return pl.pallas_call(
        _mod_kernel,
        grid=(depth, n // tn),
        in_specs=[
            pl.BlockSpec((8, d), lambda l, j: (0, 0)),
            pl.BlockSpec((1, d, tn), lambda l, j: (l, 0, j)),
            pl.BlockSpec((1, 1, tn), lambda l, j: (l, 0, j)),
        ],
        out_specs=pl.BlockSpec((1, 8, tn), lambda l, j: (l, 0, j)),
        out_shape=jax.ShapeDtypeStruct((depth, 8, n), F32),
        compiler_params=_params(("arbitrary", "arbitrary")),
        name="mod",
    )(c8, w_mod, b_mod.reshape(depth, 1, n))


def _proj_kernel(x_ref, mod_ref, n1_ref, w_ref, gq_ref, gk_ref, *refs, layers_before):
    prev = refs[:2] if layers_before else ()
    q_ref, k_ref, v_ref, r_ref, *kt_ref = refs[len(prev):]
    x = x_ref[...]
    d = x.shape[-1]
    sh = mod_ref[0, :, 0:d]
    sc = mod_ref[0, :, d:2 * d]
    h = _rms_rows(x, n1_ref[...]) * (1.0 + sc) + sh
    p = _dot(h.astype(BF16), w_ref[...])
    wq = H_A * 2 * DQK_A

    first = lax.broadcasted_iota(jnp.int32, (1, LANES), 1) < DQK_A

    def qk_norm(z, gain):
        z2 = z * z
        parts = []
        for c in range(0, wq, LANES):
            slab = z2[:, c:c + LANES]
            lo = jnp.sum(jnp.where(first, slab, 0.0), axis=-1, keepdims=True)
            hi = jnp.sum(jnp.where(first, 0.0, slab), axis=-1, keepdims=True)
            parts.append(jnp.where(first, lo, hi))
        ms = jnp.concatenate(parts, axis=1) * (1.0 / DQK_A)
        return z * lax.rsqrt(ms + EPS) * gain

    wv = H_A * DV_A
    q_ref[...] = qk_norm(p[:, 0:wq], gq_ref[...])
    kn = qk_norm(p[:, wq:2 * wq], gk_ref[...])
    k_ref[...] = kn
    if kt_ref:
        for s in range(ROW_TILE // TILE):
            kt_ref[0][s, layers_before] = kn[s * TILE:(s + 1) * TILE].T
        if prev:
            ktp_ref, vp_ref = prev
            kt_ref[0][:, 0:layers_before] = ktp_ref[...]
            v_ref[:, 0:layers_before] = vp_ref[...]
    for s in range(ROW_TILE // TILE):
        v_dst = v_ref.at[s, layers_before] if kt_ref else v_ref.at[pl.ds(s * TILE * H_A, TILE * H_A)]
        for h in range(H_A):
            v_dst[pl.ds(h, TILE, stride=H_A), :] = (
                p[s * TILE:(s + 1) * TILE, 2 * wq + h * DV_A:2 * wq + (h + 1) * DV_A])
    r_ref[...] = p[:, 2 * wq + wv:]


def _in_projection(x, mod, tokens_per_mod, n1, w_in, layer, gq, gk, caches=None):
    t, d = x.shape
    layers_before = caches[0].shape[1] if caches else 0
    n_in = w_in.shape[2]
    wq = gq.shape[1]
    wv = H_A * DV_A
    widths = (wq, wq, wv, n_in - 2 * wq - wv)
    seqs = ROW_TILE // TILE
    out_specs = [pl.BlockSpec((ROW_TILE, w), lambda i: (i, 0)) for w in widths]
    out_shape = [jax.ShapeDtypeStruct((t, w), F32) for w in widths]
    out_specs[2] = pl.BlockSpec((ROW_TILE * H_A, DV_A), lambda i: (i, 0))
    out_shape[2] = jax.ShapeDtypeStruct((t * H_A, DV_A), F32)
    cache_specs = []
    if caches is not None:
        n_seq, nl = t // TILE, layers_before + 1
        out_specs[2] = pl.BlockSpec((seqs, nl, TILE * H_A, DV_A), lambda i: (i, 0, 0, 0))
        out_shape[2] = jax.ShapeDtypeStruct((n_seq, nl, TILE * H_A, DV_A), F32)
        out_specs.append(pl.BlockSpec((seqs, nl, wq, TILE), lambda i: (i, 0, 0, 0)))
        out_shape.append(jax.ShapeDtypeStruct((n_seq, nl, wq, TILE), F32))
        if caches:
            cache_specs = [
                pl.BlockSpec((seqs, layers_before, wq, TILE), lambda i: (i, 0, 0, 0)),
                pl.BlockSpec((seqs, layers_before, TILE * H_A, DV_A), lambda i: (i, 0, 0, 0))]
    return pl.pallas_call(
        functools.partial(_proj_kernel, layers_before=layers_before),
        grid=(t // ROW_TILE,),
        in_specs=[
            pl.BlockSpec((ROW_TILE, d), lambda i: (i, 0)),
            pl.BlockSpec((1, 1, N_MOD * d), lambda i: (i * ROW_TILE // tokens_per_mod, 0, 0)),
            pl.BlockSpec((1, d), lambda i: (0, 0)),
            pl.BlockSpec((None, d, n_in), lambda i: (layer, 0, 0)),
            pl.BlockSpec((1, wq), lambda i: (0, 0)),
            pl.BlockSpec((1, wq), lambda i: (0, 0)),
        ] + cache_specs,
        out_specs=out_specs,
        out_shape=out_shape,
        compiler_params=_params(("arbitrary",)),
        name="proj",
    )(x, mod, n1, w_in, gq, gk, *(caches or ()))


def _diff_lambda(dl_ref, lam_init):
    lv = dl_ref[...]
    a = jnp.sum(lv[0:1] * lv[1:2], axis=-1, keepdims=True)
    b = jnp.sum(lv[2:3] * lv[3:4], axis=-1, keepdims=True)
    return jnp.exp(a) - jnp.exp(b) + lam_init


def _diff_attention(q, k_of, vext_of, lam, sub_gain, lam_init, o_ref):
    hw = 2 * DQK_A
    first = lax.broadcasted_iota(jnp.int32, (1, hw), 1) < DQK_A
    for h in range(H_A):
        hs = slice(h * hw, (h + 1) * hw)
        qh = q[:, hs]
        kh = k_of(h)
        vext = vext_of(h)
        maps = []
        for qm in (jnp.where(first, qh, 0.0), jnp.where(first, 0.0, qh)):
            s = _dot_nt(qm.astype(BF16), kh)
            e = jnp.exp2(s - jnp.max(s, axis=-1, keepdims=True)).astype(BF16)
            oe = _dot(e, vext)
            maps.append(oe[:, 0:DV_A] / oe[:, DV_A:])
        oh = maps[0] - lam * maps[1]
        o_ref[:, hs] = _rms_rows(oh, sub_gain) * (1.0 - lam_init)


def _attn_ctx_kernel(q_ref, k_ref, v_ref, dl_ref, sg_ref, o_ref, *, lam_init, seq):
    lam = _diff_lambda(dl_ref, lam_init)
    ones = jnp.ones((seq, DV_A), BF16)
    hw = 2 * DQK_A
    for s in range(v_ref.shape[0]):
        rows = pl.ds(s * seq, seq)
        k = k_ref[rows, :].astype(BF16)

        def vext(h, s=s):
            return jnp.concatenate(
                [v_ref[s, pl.ds(h, seq, stride=H_A), :].astype(BF16), ones], axis=1)

        _diff_attention(q_ref[rows, :], lambda h, k=k: k[:, h * hw:(h + 1) * hw], vext, lam,
                        sg_ref[...], lam_init, o_ref.at[rows])


def _attention_ctx(q, k, v_all, layer, seq, dl, sg, lam_init):
    t = q.shape[0]
    w = H_A * DV_A
    per_step = ATTN_SEQS_PER_STEP
    return pl.pallas_call(
        functools.partial(_attn_ctx_kernel, lam_init=lam_init, seq=seq),
        grid=(t // (seq * per_step),),
        in_specs=[
            pl.BlockSpec((per_step * seq, w), lambda b: (b, 0)),
            pl.BlockSpec((per_step * seq, w), lambda b: (b, 0)),
            pl.BlockSpec((per_step, None, seq * H_A, DV_A), lambda b: (b, layer, 0, 0)),
            pl.BlockSpec(dl.shape, lambda b: (0, 0)),
            pl.BlockSpec(sg.shape, lambda b: (0, 0)),
        ],
        out_specs=pl.BlockSpec((per_step * seq, w), lambda b: (b, 0)),
        out_shape=jax.ShapeDtypeStruct((t, w), F32),
        compiler_params=_params(("arbitrary",)),
        name="attn_ctx",
    )(q, k, v_all, dl, sg)


def _rope(x, cos, sin_lo, sin_hi):
    return (x * cos + pltpu.roll(x, 2 * DQK_A - 16, 1) * sin_lo + pltpu.roll(x, 16, 1) * sin_hi)


def _attn_lat_kernel(q_ref, k_ref, v_ref, ck_ref, cv_ref, qcos_ref, qslo_ref, qshi_ref,
                     kcos_ref, kslo_ref, kshi_ref, dl_ref, sg_ref, o_ref, kall, vall, *, lam_init):
    hw = 2 * DQK_A
    n_new = k_ref.shape[0]

    @pl.when(pl.program_id(1) == 0)
    def _():
        kcos, kslo, kshi = kcos_ref[...], kslo_ref[...], kshi_ref[...]
        for h in range(H_A):
            hs = slice(h * hw, (h + 1) * hw)
            kall[0:n_new, hs] = _rope(k_ref[:, hs], kcos, kslo, kshi).astype(BF16)
        kall[n_new:, :] = ck_ref[...].astype(BF16)
        ones = jnp.ones((vall.shape[0], DV_A), BF16)
        for h in range(H_A):
            vs = slice(h * DV_A, (h + 1) * DV_A)
            vall[0:n_new, 2 * h * DV_A:(2 * h + 1) * DV_A] = (
                v_ref[pl.ds(h, n_new, stride=H_A), :].astype(BF16))
            vall[n_new:, 2 * h * DV_A:(2 * h + 1) * DV_A] = cv_ref[:, vs].astype(BF16)
            vall[:, (2 * h + 1) * DV_A:(2 * h + 2) * DV_A] = ones

    lam = _diff_lambda(dl_ref, lam_init)
    qcos, qslo, qshi = qcos_ref[...], qslo_ref[...], qshi_ref[...]
    q = jnp.concatenate(
        [_rope(q_ref[:, h * hw:(h + 1) * hw], qcos, qslo, qshi) for h in range(H_A)], axis=-1)
    _diff_attention(q, lambda h: kall[:, h * hw:(h + 1) * hw],
                    lambda h: vall[:, 2 * h * DV_A:(2 * h + 2) * DV_A], lam, sg_ref[...],
                    lam_init, o_ref)


def _attention_lat(q, k, v, n_batch, cache_k, cache_v, layer, rope_tabs, dl, sg, lam_init):
    t = q.shape[0]
    seq = t // n_batch
    past = cache_k.shape[2]
    w = H_A * DV_A
    hw = 2 * DQK_A
    nq = seq // TILE
    cos, slo, shi = rope_tabs
    tab_q = pl.BlockSpec((TILE, hw), lambda b, j: (j, 0))
    tab_k = pl.BlockSpec((seq, hw), lambda b, j: (0, 0))
    return pl.pallas_call(
        functools.partial(_attn_lat_kernel, lam_init=lam_init),
        grid=(n_batch, nq),
        in_specs=[
            pl.BlockSpec((TILE, w), lambda b, j: (b * nq + j, 0)),
            pl.BlockSpec((seq, w), lambda b, j: (b, 0)),
            pl.BlockSpec((seq * H_A, DV_A), lambda b, j: (b, 0)),
            pl.BlockSpec((None, None, past, w), lambda b, j: (b, layer, 0, 0)),
            pl.BlockSpec((None, None, past, w), lambda b, j: (b, layer, 0, 0)),
            tab_q, tab_q, tab_q, tab_k, tab_k, tab_k,
            pl.BlockSpec(dl.shape, lambda b, j: (0, 0)),
            pl.BlockSpec(sg.shape, lambda b, j: (0, 0)),
        ],
        out_specs=pl.BlockSpec((TILE, w), lambda b, j: (b * nq + j, 0)),
        out_shape=jax.ShapeDtypeStruct((t, w), F32),
        scratch_shapes=[pltpu.VMEM((seq + past, w), BF16), pltpu.VMEM((seq + past, 2 * w), BF16)],
        compiler_params=_params(("arbitrary", "arbitrary")),
        name="attn_lat",
    )(q, k, v, cache_k, cache_v, cos, slo, shi, cos, slo, shi, dl, sg)


def _tile_aux():
    rows = lax.broadcasted_iota(jnp.int32, (TILE, 1), 0)
    lane = lax.broadcasted_iota(jnp.int32, (1, W_B), 1)
    head_masks = [(lane // DK_B) == h for h in range(H_B)]
    first_of_pair = lax.broadcasted_iota(jnp.int32, (1, 2 * DK_B), 1) < DK_B
    xor = (lax.broadcasted_iota(jnp.int32, (HALF, HALF), 0)
           ^ lax.broadcasted_iota(jnp.int32, (HALF, HALF), 1))
    levels = [1 << i for i in range(TILE.bit_length() - 1)]
    right = {m: (rows & m) != 0 for m in levels}
    same_block = {m: xor < 2 * m for m in levels if 2 * m < HALF}
    return rows, head_masks, first_of_pair, right, same_block


def _hgrn_lower_bound(lb_ref, layer, direction, depth):
    xs = [lb_ref[2 * j + direction:2 * j + direction + 1, :] for j in range(depth)]
    m = functools.reduce(jnp.maximum, xs)
    es = [jnp.exp(x - m) for x in xs]
    return sum(es[1:layer + 1]) / sum(es)


def _hgrn_gates(fpre, lbd):
    t = jnp.exp(-jnp.abs(fpre))
    log_sig = jnp.minimum(fpre, 0.0) - jnp.log(1.0 + t)
    r = 1.0 / (1.0 + t)
    sig_neg = jnp.where(fpre >= 0.0, t * r, r)
    if lbd is None:
        return log_sig * LOG2E, sig_neg
    a = jnp.log(lbd)
    b = jnp.log1p(-lbd) + log_sig
    logf = jnp.maximum(a, b) + jnp.log1p(jnp.exp(-jnp.abs(a - b)))
    return logf * LOG2E, (1.0 - lbd) * sig_neg


def _chain_row(token):
    return SUBLANES * (token % CHAIN_LEN) + token // CHAIN_LEN


def _tile_prefix(g, g_scr, b_scr):
    g_c = _to_chains(g, g_scr)
    run = [g_c[0]]
    for j in range(1, CHAIN_LEN):
        run.append(run[-1] + g_c[j])
    sub = lax.broadcasted_iota(jnp.int32, (SUBLANES, 1), 0)
    incl = run[-1]
    k = 1
    while k < SUBLANES:
        incl = incl + jnp.where(sub >= k, pltpu.roll(incl, k, 0), 0.0)
        k *= 2
    offset = incl - run[-1]
    prefix = _from_chains([r + offset for r in run], b_scr)
    return prefix, incl[SUBLANES - 1:SUBLANES, :]


def _level_exponent(z, g, b_scr, m, reverse, right):
    if m == 1:
        return jnp.where(right, g, 0.0) if not reverse else jnp.where(right, 0.0, g)
    slabs = b_scr.shape[0]

    def ref_rows(token):
        row = _chain_row(token)
        return jnp.concatenate(
            [jnp.broadcast_to(b_scr[s, row:row + 1, :], (SUBLANES, LANES)) for s in range(slabs)],
            axis=1)

    groups = []
    for gq in range(TILE // SUBLANES):
        first = gq * SUBLANES
        zg = z[first:first + SUBLANES]
        if 2 * m >= SUBLANES:
            block = first // (2 * m) * (2 * m)
            ref = ref_rows(block + m - 1)
            if m >= SUBLANES:
                groups.append(zg - ref if first - block >= m else ref - zg)
                continue
        else:
            sub = lax.broadcasted_iota(jnp.int32, (SUBLANES, 1), 0)
            ref = jnp.where(sub < 2 * m, ref_rows(first + m - 1), ref_rows(first + 3 * m - 1))
        groups.append(-jnp.abs(zg - ref))
    return jnp.concatenate(groups, axis=0)


def _hgrn_tile(qh, kk, vv, g, reverse, aux, ones_bd, state_t, scratch):
    rows, head_masks, first_of_pair, right_rows, same_block = aux
    pair_w = 2 * DK_B
    g_scr, b_scr = scratch
    prefix, total = _tile_prefix(g, g_scr, b_scr)
    z = prefix if not reverse else prefix - g
    diag = [[None, None] for _ in range(H_B)]
    off = [None] * H_B
    m = 1
    while m < TILE:
        right = right_rows[m]
        x = jnp.exp2(_level_exponent(z, g, b_scr, m, reverse, right))
        q_rows = right if not reverse else jnp.logical_not(right)
        qz = jnp.where(q_rows, (qh * x).astype(BF16), 0.0)
        kz = jnp.where(q_rows, 0.0, (kk * x).astype(BF16))
        for h in range(H_B):
            ls = slice((h // 2) * pair_w, (h // 2 + 1) * pair_w)
            qm = jnp.where(first_of_pair, qz[:, ls], 0.0) if h % 2 == 0 else \
                jnp.where(first_of_pair, 0.0, qz[:, ls])
            kh = kz[:, ls]
            if m < HALF:
                for i in range(2):
                    rs = slice(i * HALF, (i + 1) * HALF)
                    pm = _dot_nt(qm[rs], kh[rs])
                    if m in same_block:
                        pm = jnp.where(same_block[m], pm, 0.0)
                    diag[h][i] = pm if diag[h][i] is None else diag[h][i] + pm
            else:
                lo, hi = slice(0, HALF), slice(HALF, TILE)
                off[h] = _dot_nt(qm[hi], kh[lo]) if not reverse else _dot_nt(qm[lo], kh[hi])
        m *= 2
    vb = vv.astype(BF16)
    top = bot = None
    for h in range(H_B):
        vh = jnp.where(head_masks[h], vb, 0.0)
        d0, d1, of = (a.astype(BF16) for a in (diag[h][0], diag[h][1], off[h]))
        if not reverse:
            t = _dot(d0, vh[0:HALF])
            b = _dot(jnp.concatenate([of, d1], axis=1), vh)
        else:
            t = _dot(jnp.concatenate([d0, of], axis=1), vh)
            b = _dot(d1, vh[HALF:])
        top = t if top is None else top + t
        bot = b if bot is None else bot + b
    out = _group_sum(qh * kk, ones_bd) * vv + jnp.concatenate([top, bot], axis=0)
    q_decay, k_decay = (z, total - z) if not reverse else (total - z, z)
    if state_t is not None:
        out = out + _dot_nt((qh * jnp.exp2(q_decay)).astype(BF16), state_t.astype(BF16))
    k_hat = (kk * jnp.exp2(k_decay)).astype(BF16)
    return out, k_hat, total


def _block_diag_mask():
    r = lax.broadcasted_iota(jnp.int32, (W_B, W_B), 0) // DK_B
    c = lax.broadcasted_iota(jnp.int32, (W_B, W_B), 1) // DV_B
    return r == c


def _conv(x, prev_row, next_rows, cw_ref, cb_ref, rows):
    x_m1 = jnp.where(rows == 0, prev_row, pltpu.roll(x, 1, 0))
    x_p1 = jnp.where(rows == TILE - 1, next_rows[0:1], pltpu.roll(x, TILE - 1, 0))
    x_p2 = jnp.where(rows == TILE - 2, next_rows[0:1],
                     jnp.where(rows == TILE - 1, next_rows[1:2], pltpu.roll(x, TILE - 2, 0)))
    y = cb_ref[...] + x_m1 * cw_ref[0:1, :]
    y = y + x * cw_ref[1:2, :]
    y = y + x_p1 * cw_ref[2:3, :]
    return y + x_p2 * cw_ref[3:4, :]


def _chain_address(g):
    per_chain = CHAIN_LEN // SUBLANES
    return SUBLANES * SUBLANES * (g % per_chain) + g // per_chain, SUBLANES


def _to_chains(x, scr):
    slabs = scr.shape[0]
    for s in range(slabs):
        for g in range(TILE // SUBLANES):
            start, stride = _chain_address(g)
            scr[s, pl.ds(start, SUBLANES, stride=stride), :] = (
                x[g * SUBLANES:(g + 1) * SUBLANES, s * LANES:(s + 1) * LANES])
    return [jnp.concatenate([scr[s, j * SUBLANES:(j + 1) * SUBLANES, :] for s in range(slabs)],
                            axis=1) for j in range(CHAIN_LEN)]


def _from_chains(chains, scr):
    slabs = scr.shape[0]
    for s in range(slabs):
        for j, c in enumerate(chains):
            scr[s, j * SUBLANES:(j + 1) * SUBLANES, :] = c[:, s * LANES:(s + 1) * LANES]
    groups = []
    for g in range(TILE // SUBLANES):
        start, stride = _chain_address(g)
        groups.append(jnp.concatenate(
            [scr[s, pl.ds(start, SUBLANES, stride=stride), :] for s in range(slabs)], axis=1))
    return jnp.concatenate(groups, axis=0)


def _rglru_tile(xc, r_pre, i_pre, lam_row, reverse, h_in, scratch):
    a_scr, u_scr, h_scr = scratch
    r = _sigmoid(r_pre)
    i = _sigmoid(i_pre)
    log_a = -RG_C * r * _softplus(-lam_row)
    a = jnp.exp(log_a)
    u = jnp.sqrt(jnp.tanh(-log_a) * (a * a + 1.0)) * (i * xc)
    a_c = _to_chains(a, a_scr)
    u_c = _to_chains(u, u_scr)
    order = list(reversed(range(CHAIN_LEN))) if reverse else list(range(CHAIN_LEN))
    h_c = [None] * CHAIN_LEN
    p_c = [None] * CHAIN_LEN
    h_c[order[0]], p_c[order[0]] = u_c[order[0]], a_c[order[0]]
    for prev, j in zip(order[:-1], order[1:]):
        h_c[j] = a_c[j] * h_c[prev] + u_c[j]
        p_c[j] = a_c[j] * p_c[prev]
    p_tot, h_tot = p_c[order[-1]], h_c[order[-1]]
    sub = lax.broadcasted_iota(jnp.int32, (SUBLANES, 1), 0)
    k = 1
    while k < SUBLANES:
        if not reverse:
            valid = sub >= k
            p_s, h_s = pltpu.roll(p_tot, k, 0), pltpu.roll(h_tot, k, 0)
        else:
            valid = sub < SUBLANES - k
            p_s, h_s = pltpu.roll(p_tot, SUBLANES - k, 0), pltpu.roll(h_tot, SUBLANES - k, 0)
        h_tot = h_tot + p_tot * jnp.where(valid, h_s, 0.0)
        p_tot = p_tot * jnp.where(valid, p_s, 1.0)
        k *= 2
    if h_in is not None:
        h_tot = h_tot + p_tot * h_in
    if not reverse:
        carry = jnp.where(sub == 0, 0.0 if h_in is None else h_in, pltpu.roll(h_tot, 1, 0))
        h_out = h_tot[SUBLANES - 1:SUBLANES, :]
    else:
        carry = jnp.where(sub == SUBLANES - 1, 0.0 if h_in is None else h_in,
                          pltpu.roll(h_tot, SUBLANES - 1, 0))
        h_out = h_tot[0:1, :]
    h_c = [h + p * carry for h, p in zip(h_c, p_c)]
    return _from_chains(h_c, h_scr), h_out


def _rec_inputs(refs):
    qb, ff, fb, ib, gb, xcol, gcol = [r[...] for r in refs]
    return _silu(qb), ff, fb, ib, gb, xcol, gcol


def _hgrn_out(o_sum, gb, onorm_gain, ones_bd):
    ms = _group_sum(o_sum * o_sum, ones_bd) * (1.0 / DV_B)
    return o_sum * lax.rsqrt(ms + EPS) * onorm_gain * _silu(gb)


def _rec_ctx_kernel(*refs, layer, depth):
    (q_ref, ff_ref, fb_ref, i_ref, g_ref, x_ref, gc_ref, lb_ref, on_ref, cw_ref, cb_ref, wg_ref,
     bg_ref, lam_ref, ones_ref, ob_ref, oc_ref, st_ref, hs_ref,
     *scan_scratch) = refs[2 if layer else 0:]
    if layer == 0 and depth > 1:
        st_ref[0, 1:] = jnp.zeros((depth - 1,) + st_ref.shape[2:], F32)
        hs_ref[0, 1:] = jnp.zeros((depth - 1,) + hs_ref.shape[2:], F32)
    aux = _tile_aux()
    rows = aux[0]
    ones_bd = ones_ref[...]
    qh, ff, fb, ib, gb, xcol, gcol = _rec_inputs((q_ref, ff_ref, fb_ref, i_ref, g_ref, x_ref, gc_ref))
    o_sum = None
    for direction, fpre in enumerate((ff, fb)):
        lbd = None if layer == 0 else _hgrn_lower_bound(lb_ref, layer, direction, depth)
        g, kk = _hgrn_gates(fpre, lbd)
        o, k_hat, _ = _hgrn_tile(qh, kk, ib, g, direction == 1, aux, ones_bd, None,
                                 scan_scratch[3 + 2 * direction:5 + 2 * direction])
        o_sum = o if o_sum is None else o_sum + o
        full = _dot_tn(k_hat, ib.astype(BF16))
        for h in range(H_B):
            blk = full[h * DK_B:(h + 1) * DK_B, (h // 2) * 2 * DV_B:(h // 2 + 1) * 2 * DV_B]
            if h % 2:
                blk = pltpu.roll(blk, DV_B, 1)
            st_ref[0, 0, direction, h] = blk[:, 0:DV_B]
    ob_ref[...] = _hgrn_out(o_sum, gb, on_ref[...], ones_bd)

    zero_row = jnp.zeros((1, W_C), F32)
    xc = _conv(xcol, zero_row, jnp.zeros((2, W_C), F32), cw_ref, cb_ref, rows)
    gates = _dot(xc.astype(BF16), wg_ref[...]) + bg_ref[...]
    y_f, h_f = _rglru_tile(xc, gates[:, 0:W_C], gates[:, W_C:2 * W_C], lam_ref[0:1, :], False,
                           None, scan_scratch[0:3])
    y_b, h_b = _rglru_tile(xc, gates[:, 2 * W_C:3 * W_C], gates[:, 3 * W_C:], lam_ref[1:2, :], True,
                           None, scan_scratch[0:3])
    oc_ref[...] = (y_f + y_b) * _gelu_tanh(gcol)
    hs_ref[0, 0, 0:1, :] = h_f
    hs_ref[0, 0, 1:2, :] = h_b


def _scan_scratch(hgrn_directions):
    return [pltpu.VMEM((W_C // LANES, TILE, LANES), F32) for _ in range(3 + 2 * hgrn_directions)]


def _col_spec(col, row_map):
    return pl.BlockSpec((TILE, W_B), lambda *idx: (row_map(*idx), col))


COL_QB, COL_FF, COL_FB, COL_IB, COL_GB, COL_XC, COL_GC = range(7)
_REC_COLS = (COL_QB, COL_FF, COL_FB, COL_IB, COL_GB, COL_XC, COL_GC)


def _full_spec(a):
    nd = a.ndim
    return pl.BlockSpec(a.shape, lambda *idx: (0,) * nd)


def _recurrence_ctx(proj, layer, depth, consts, states):
    t = proj.shape[0]
    n = t // TILE
    assert bool(states) == (layer > 0)
    layers, at = (1, layer) if states else (depth, 0)
    return pl.pallas_call(
        functools.partial(_rec_ctx_kernel, layer=layer, depth=depth),
        grid=(n,),
        in_specs=([pl.BlockSpec(memory_space=pl.ANY) for _ in states]
                  + [_col_spec(c, lambda b: b) for c in _REC_COLS] + [_full_spec(a) for a in consts]),
        out_specs=[
            pl.BlockSpec((TILE, W_B), lambda b: (b, 0)),
            pl.BlockSpec((TILE, W_C), lambda b: (b, 0)),
            pl.BlockSpec((1, layers, 2, H_B, DK_B, DV_B), lambda b: (b, at, 0, 0, 0, 0)),
            pl.BlockSpec((1, layers, 2, W_C), lambda b: (b, at, 0, 0)),
        ],
        out_shape=[
            jax.ShapeDtypeStruct((t, W_B), F32),
            jax.ShapeDtypeStruct((t, W_C), F32),
            jax.ShapeDtypeStruct((n, depth, 2, H_B, DK_B, DV_B), F32),
            jax.ShapeDtypeStruct((n, depth, 2, W_C), F32),
        ],
        input_output_aliases={i: 2 + i for i in range(len(states))},
        scratch_shapes=_scan_scratch(2),
        compiler_params=_params(("arbitrary",)),
        name="rec_ctx",
    )(*states, *([proj] * len(_REC_COLS)), *consts)


def _rec_lat_kernel(q_ref, f_ref, i_ref, g_ref, x_ref, xp_ref, xn_ref, gc_ref, s0_ref, h0_ref,
                    *rest, layer, depth, direction, n_tiles):
    if direction == 0:
        (lb_ref, on_ref, cw_ref, cb_ref, wg_ref, bg_ref, lam_ref, ones_ref,
         o_ref, y_ref, state, hstate, *scan_scratch) = rest
    else:
        (of_ref, yf_ref, lb_ref, on_ref, cw_ref, cb_ref, wg_ref, bg_ref, lam_ref, ones_ref,
         o_ref, y_ref, state, hstate, *scan_scratch) = rest
    step = pl.program_id(1)
    tile = step if direction == 0 else n_tiles - 1 - step
    reverse = direction == 1

    @pl.when(step == 0)
    def _():
        state[...] = s0_ref[0, 0]
        hstate[...] = h0_ref[0, 0]

    aux = _tile_aux()
    rows = aux[0]
    ones_bd = ones_ref[...]
    qh = _silu(q_ref[...])
    ib = i_ref[...]
    lbd = None if layer == 0 else _hgrn_lower_bound(lb_ref, layer, direction, depth)
    g, kk = _hgrn_gates(f_ref[...], lbd)
    o, k_hat, total = _hgrn_tile(qh, kk, ib, g, reverse, aux, ones_bd, state[...],
                                 scan_scratch[3:5])
    upd = jnp.where(_block_diag_mask(), _dot_tn(ib.astype(BF16), k_hat), 0.0)
    state[...] = state[...] * jnp.exp2(total) + upd

    prev_row = jnp.where(tile == 0, 0.0, xp_ref[TILE - 1:TILE, :])
    next_rows = jnp.where(tile == n_tiles - 1, 0.0, xn_ref[0:2, :])
    xc = _conv(x_ref[...], prev_row, next_rows, cw_ref, cb_ref, rows)
    gates = _dot(xc.astype(BF16), wg_ref[...]) + bg_ref[...]
    c0 = 2 * W_C * direction
    y, h_next = _rglru_tile(xc, gates[:, c0:c0 + W_C], gates[:, c0 + W_C:c0 + 2 * W_C],
                            lam_ref[direction:direction + 1, :], reverse, hstate[...],
                            scan_scratch[0:3])
    hstate[...] = h_next

    if direction == 0:
        o_ref[...] = o
        y_ref[...] = y
    else:
        o_ref[...] = _hgrn_out(of_ref[...] + o, g_ref[...], on_ref[...], ones_bd)
        y_ref[...] = (yf_ref[...] + y) * _gelu_tanh(gc_ref[...])


def _recurrence_lat(proj, n_batch, layer, depth, direction, state0, h0, prev, consts):
    t = proj.shape[0]
    n_tiles = t // (TILE * n_batch)

    def tile_of(b, s):
        return s if direction == 0 else n_tiles - 1 - s

    def row(b, s):
        return b * n_tiles + tile_of(b, s)

    def row_prev(b, s):
        return b * n_tiles + jnp.maximum(tile_of(b, s) - 1, 0)

    def row_next(b, s):
        return b * n_tiles + jnp.minimum(tile_of(b, s) + 1, n_tiles - 1)

    f_col = COL_FF if direction == 0 else COL_FB
    in_specs = [_col_spec(COL_QB, row), _col_spec(f_col, row), _col_spec(COL_IB, row),
                _col_spec(COL_GB, row), _col_spec(COL_XC, row), _col_spec(COL_XC, row_prev),
                _col_spec(COL_XC, row_next), _col_spec(COL_GC, row),
                pl.BlockSpec((1, 1, W_B, W_B), lambda b, s: (b, direction, 0, 0)),
                pl.BlockSpec((1, 1, 1, W_C), lambda b, s: (b, direction, 0, 0))]
    args = [proj] * 8 + [state0, h0]
    if direction == 1:
        in_specs += [pl.BlockSpec((TILE, W_B), lambda b, s: (row(b, s), 0)),
                     pl.BlockSpec((TILE, W_C), lambda b, s: (row(b, s), 0))]
        args += list(prev)
    in_specs += [_full_spec(a) for a in consts]
    args += list(consts)
    return pl.pallas_call(
        functools.partial(_rec_lat_kernel, layer=layer, depth=depth, direction=direction,
                          n_tiles=n_tiles),
        grid=(n_batch, n_tiles),
        in_specs=in_specs,
        out_specs=[pl.BlockSpec((TILE, W_B), lambda b, s: (row(b, s), 0)),
                   pl.BlockSpec((TILE, W_C), lambda b, s: (row(b, s), 0))],
        out_shape=[jax.ShapeDtypeStruct((t, W_B), F32), jax.ShapeDtypeStruct((t, W_C), F32)],
        scratch_shapes=[pltpu.VMEM((W_B, W_B), F32), pltpu.VMEM((1, W_C), F32)] + _scan_scratch(1),
        compiler_params=_params(("arbitrary", "arbitrary")),
        name="rec_lat_fwd" if direction == 0 else "rec_lat_bwd",
    )(*args)


def _out_kernel(x_ref, oa_ref, ob_ref, oc_ref, mod_ref, n2_ref, wo_ref, w1_ref, w2_ref, y_ref):
    x = x_ref[...]
    d = x.shape[-1]
    g1 = mod_ref[0, :, 2 * d:3 * d]
    sh2 = mod_ref[0, :, 3 * d:4 * d]
    sc2 = mod_ref[0, :, 4 * d:5 * d]
    g2 = mod_ref[0, :, 5 * d:6 * d]
    wa = H_A * DV_A
    mix = _dot(oa_ref[...].astype(BF16), wo_ref[0:wa, :])
    mix = mix + _dot(ob_ref[...].astype(BF16), wo_ref[wa:wa + W_B, :])
    mix = mix + _dot(oc_ref[...].astype(BF16), wo_ref[wa + W_B:, :])
    x1 = x + g1 * mix
    h = (_rms_rows(x1, n2_ref[...]) * (1.0 + sc2) + sh2).astype(BF16)
    acc = jnp.zeros_like(x)
    for c in range(w1_ref.shape[1] // D_FF_CHUNK):
        cs = slice(c * D_FF_CHUNK, (c + 1) * D_FF_CHUNK)
        f = jnp.maximum(_dot(h, w1_ref[:, cs]), 0.0)
        acc = acc + _dot((f * f).astype(BF16), w2_ref[cs, :])
    y_ref[...] = x1 + g2 * acc


def _out_mlp(x, oa, ob, oc, mod, tokens_per_mod, n2, w_out, w_ff1, w_ff2, layer):
    t, d = x.shape

    def resident(w):
        return pl.BlockSpec((None,) + w.shape[1:], lambda i: (layer, 0, 0),
                            pipeline_mode=pl.Buffered(1))

    return pl.pallas_call(
        _out_kernel,
        grid=(t // ROW_TILE,),
        in_specs=[
            pl.BlockSpec((ROW_TILE, d), lambda i: (i, 0)),
            pl.BlockSpec((ROW_TILE, oa.shape[1]), lambda i: (i, 0)),
            pl.BlockSpec((ROW_TILE, ob.shape[1]), lambda i: (i, 0)),
            pl.BlockSpec((ROW_TILE, oc.shape[1]), lambda i: (i, 0)),
            pl.BlockSpec((1, 1, N_MOD * d), lambda i: (i * ROW_TILE // tokens_per_mod, 0, 0)),
            pl.BlockSpec((1, d), lambda i: (0, 0)),
            resident(w_out),
            resident(w_ff1),
            resident(w_ff2),
        ],
        out_specs=pl.BlockSpec((ROW_TILE, d), lambda i: (i, 0)),
        out_shape=jax.ShapeDtypeStruct((t, d), F32),
        compiler_params=_params(("arbitrary",)),
        name="out_mlp",
    )(x, oa, ob, oc, mod, n2, w_out, w_ff1, w_ff2)


def _ones_block_diag(n, group):
    idx = np.arange(n) // group
    return jnp.asarray((idx[:, None] == idx[None, :]).astype(np.float32), dtype=BF16)


def _rope_tables(seq):
    nf = DQK_A // 4
    inv = ROPE_BASE ** (-jnp.arange(nf, dtype=F32) / nf)
    pos = jnp.arange(seq)
    ang_r = (pos // GRID_W).astype(F32)[:, None] * inv
    ang_c = (pos % GRID_W).astype(F32)[:, None] * inv
    z = jnp.zeros_like(ang_r)
    cos = jnp.concatenate([jnp.cos(ang_r)] * 2 + [jnp.cos(ang_c)] * 2, axis=-1)
    sin_lo = jnp.concatenate([-jnp.sin(ang_r), z, -jnp.sin(ang_c), z], axis=-1)
    sin_hi = jnp.concatenate([z, jnp.sin(ang_r), z, jnp.sin(ang_c)], axis=-1)
    return tuple(jnp.tile(a, (1, 2)) for a in (cos, sin_lo, sin_hi))


def _gate_weights(rg_w_l):
    eye = jnp.eye(H_C, dtype=rg_w_l.dtype)
    w = jnp.einsum('dghij,hk->dghikj', rg_w_l, eye)
    w = w.reshape(2, 2, W_C, W_C)
    return jnp.transpose(w, (2, 0, 1, 3)).reshape(W_C, 4 * W_C).astype(BF16)


def _state_block_diag_t(s):
    eye = jnp.eye(H_B, dtype=s.dtype)
    w = jnp.einsum('...hdv,hk->...hvkd', s, eye)
    return w.reshape(s.shape[:-3] + (W_B, W_B))


def kernel(x_prompt, x_sample, cache_k, cache_v, state_hgrn, state_rglru, c, c_ctx, w_mod, b_mod,
           norm1, norm2, w_in, w_out, qk_norm, diff_lambda, subln, hgrn_lb, hgrn_onorm, conv_w,
           conv_b, rg_w, rg_b, rg_lambda, w_ff1, w_ff2):
    batch, seq, d = x_prompt.shape
    dec_batch, dec_seq, _ = x_sample.shape
    depth = w_in.shape[0]
    past = cache_k.shape[2]
    wq = H_A * 2 * DQK_A

    c8 = jnp.concatenate([c_ctx[None], c, jnp.zeros((8 - 1 - dec_batch, d), F32)], axis=0)
    mod_all = _modulation(c8, w_mod, b_mod)

    assert DQK_A == DV_B and seq == TILE
    ones_b = _ones_block_diag(W_B, DV_B)
    rope_tabs = _rope_tables(dec_seq)
    ck = cache_k.reshape(dec_batch, depth, past, wq)
    cv = cache_v.reshape(dec_batch, depth, past, H_A * DV_A)
    lb2 = hgrn_lb.reshape(depth * 2, W_B)

    w_in_b, w_out_b, w1_b, w2_b = (w.astype(BF16) for w in (w_in, w_out, w_ff1, w_ff2))

    yp = x_prompt.reshape(batch * seq, d)
    ys = x_sample.reshape(dec_batch * dec_seq, d)
    caches, states = (), ()
    for l in range(depth):
        lam_init = 0.8 - 0.6 * math.exp(-0.3 * l)
        mod_ctx = mod_all[l, 0:1][:, None, :]
        mod_lat = mod_all[l, 1:1 + dec_batch][:, None, :]
        n1 = norm1[l][None]
        n2 = norm2[l][None]
        gq = jnp.tile(qk_norm[l, 0], H_A * 2)[None] * (DQK_A ** -0.5 * LOG2E)
        gk = jnp.tile(qk_norm[l, 1], H_A * 2)[None]
        dl = diff_lambda[l]
        sg = subln[l][None]
        consts = (lb2, jnp.tile(hgrn_onorm[l], H_B)[None], conv_w[l], conv_b[l][None],
                  _gate_weights(rg_w[l]), rg_b[l].reshape(1, 4 * W_C), rg_lambda[l], ones_b)

        q, k, v_all, rest, kt_all = _in_projection(yp, mod_ctx, batch * seq, n1, w_in_b, l,
                                                   gq, gk, caches=caches)
        caches = (kt_all, v_all)
        oa = _attention_ctx(q, k, v_all, l, seq, dl, sg, lam_init)
        ob, oc, *states = _recurrence_ctx(rest, l, depth, consts, states)
        yp = _out_mlp(yp, oa, ob, oc, mod_ctx, batch * seq, n2, w_out_b, w1_b, w2_b, l)

        q, k, v, rest = _in_projection(ys, mod_lat, dec_seq, n1, w_in_b, l, gq, gk)
        oa = _attention_lat(q, k, v, dec_batch, ck, cv, l, rope_tabs, dl, sg, lam_init)
        s0 = _state_block_diag_t(state_hgrn[:, l])
        h0 = state_rglru[:, l][:, :, None, :]
        fwd = _recurrence_lat(rest, dec_batch, l, depth, 0, s0, h0, None, consts)
        ob, oc = _recurrence_lat(rest, dec_batch, l, depth, 1, s0, h0, fwd, consts)
        ys = _out_mlp(ys, oa, ob, oc, mod_lat, dec_seq, n2, w_out_b, w1_b, w2_b, l)

    kt_all, v_all = caches
    new_k = jnp.transpose(kt_all.reshape(batch, depth, H_A, 2, DQK_A, seq), (0, 1, 5, 2, 3, 4))
    new_v = v_all.reshape(batch, depth, seq, H_A, DV_A)
    return (yp.reshape(batch, seq, d), ys.reshape(dec_batch, dec_seq, d), new_k, new_v,
            states[0], states[1])
```

```python
import functools
import math

import numpy as np
import jax
import jax.numpy as jnp
from jax import lax
from jax.experimental import pallas as pl
from jax.experimental.pallas import tpu as pltpu

F32 = jnp.float32
BF16 = jnp.bfloat16

GRID_W = 64
H_A, DQK_A, DV_A = 4, 64, 128
H_B, DK_B, DV_B = 4, 64, 64
W_B = H_B * DV_B
H_C, W_C = 4, 256
CONV_K = 4
RG_C = 8.0
ROPE_BASE = 10000.0
EPS = 1e-6
N_MOD = 6
D_FF_CHUNK = 1024

TILE = 256
HALF = TILE // 2
ROW_TILE = 512
ATTN_SEQS_PER_STEP = 4
SUBLANES = 8
LANES = 128
CHAIN_LEN = TILE // SUBLANES
LOG2E = math.log2(math.e)
V7X_VMEM_BYTES = 64 * 1024 * 1024
VMEM_LIMIT = V7X_VMEM_BYTES - 8 * 1024 * 1024


def _dot(a, b):
    return jnp.dot(a, b, preferred_element_type=F32)


def _dot_nt(a, b):
    return lax.dot_general(a, b, (((1,), (1,)), ((), ())), preferred_element_type=F32)


def _dot_tn(a, b):
    return lax.dot_general(a, b, (((0,), (0,)), ((), ())), preferred_element_type=F32)


def _group_sum(x, ones_bd):
    hi = x.astype(BF16)
    lo = (x - hi.astype(F32)).astype(BF16)
    return _dot(hi, ones_bd) + _dot(lo, ones_bd)


def _rms_rows(x, gain):
    return x * lax.rsqrt(jnp.mean(x * x, axis=-1, keepdims=True) + EPS) * gain


def _sigmoid(x):
    return 0.5 * jnp.tanh(0.5 * x) + 0.5


def _silu(x):
    return x * _sigmoid(x)


def _softplus(x):
    return jnp.maximum(x, 0.0) + jnp.log1p(jnp.exp(-jnp.abs(x)))


def _log_sigmoid(x):
    return jnp.minimum(x, 0.0) - jnp.log1p(jnp.exp(-jnp.abs(x)))


def _gelu_tanh(x):
    return x * (0.5 * (1.0 + jnp.tanh(math.sqrt(2.0 / math.pi) * (x + 0.044715 * (x * x * x)))))


def _params(semantics, flags=None):
    return pltpu.CompilerParams(dimension_semantics=semantics, vmem_limit_bytes=VMEM_LIMIT,
                                flags=flags)


def _mod_kernel(c_ref, w_ref, b_ref, o_ref):
    c = c_ref[...]
    o_ref[0] = _dot(_silu(c).astype(BF16), w_ref[0].astype(BF16)) + b_ref[0]


def _modulation(c8, w_mod, b_mod):
    depth, d, n = w_mod.shape
    tn = d
    return pl.pallas_call(
        _mod_kernel,
        grid=(depth, n // tn),
        in_specs=[
            pl.BlockSpec((8, d), lambda l, j: (0, 0)),
            pl.BlockSpec((1, d, tn), lambda l, j: (l, 0, j)),
            pl.BlockSpec((1, 1, tn), lambda l, j: (l, 0, j)),
        ],
        out_specs=pl.BlockSpec((1, 8, tn), lambda l, j: (l, 0, j)),
        out_shape=jax.ShapeDtypeStruct((depth, 8, n), F32),
        compiler_params=_params(("arbitrary", "arbitrary")),
        name="mod",
    )(c8, w_mod, b_mod.reshape(depth, 1, n))


def _proj_kernel(x_ref, mod_ref, n1_ref, w_ref, gq_ref, gk_ref, *refs, layers_before):
    prev = refs[:2] if layers_before else ()
    q_ref, k_ref, v_ref, r_ref, *kt_ref = refs[len(prev):]
    x = x_ref[...]
    d = x.shape[-1]
    sh = mod_ref[0, :, 0:d]
    sc = mod_ref[0, :, d:2 * d]
    h = _rms_rows(x, n1_ref[...]) * (1.0 + sc) + sh
    p = _dot(h.astype(BF16), w_ref[...])
    wq = H_A * 2 * DQK_A

    first = lax.broadcasted_iota(jnp.int32, (1, LANES), 1) < DQK_A

    def qk_norm(z, gain):
        z2 = z * z
        parts = []
        for c in range(0, wq, LANES):
            slab = z2[:, c:c + LANES]
            lo = jnp.sum(jnp.where(first, slab, 0.0), axis=-1, keepdims=True)
            hi = jnp.sum(jnp.where(first, 0.0, slab), axis=-1, keepdims=True)
            parts.append(jnp.where(first, lo, hi))
        ms = jnp.concatenate(parts, axis=1) * (1.0 / DQK_A)
        return z * lax.rsqrt(ms + EPS) * gain

    wv = H_A * DV_A
    q_ref[...] = qk_norm(p[:, 0:wq], gq_ref[...])
    kn = qk_norm(p[:, wq:2 * wq], gk_ref[...])
    k_ref[...] = kn
    if kt_ref:
        for s in range(ROW_TILE // TILE):
            kt_ref[0][s, layers_before] = kn[s * TILE:(s + 1) * TILE].T
        if prev:
            ktp_ref, vp_ref = prev
            kt_ref[0][:, 0:layers_before] = ktp_ref[...]
            v_ref[:, 0:layers_before] = vp_ref[...]
    for s in range(ROW_TILE // TILE):
        v_dst = v_ref.at[s, layers_before] if kt_ref else v_ref.at[pl.ds(s * TILE * H_A, TILE * H_A)]
        for h in range(H_A):
            v_dst[pl.ds(h, TILE, stride=H_A), :] = (
                p[s * TILE:(s + 1) * TILE, 2 * wq + h * DV_A:2 * wq + (h + 1) * DV_A])
    r_ref[...] = p[:, 2 * wq + wv:]


def _in_projection(x, mod, tokens_per_mod, n1, w_in, layer, gq, gk, caches=None):
    t, d = x.shape
    layers_before = caches[0].shape[1] if caches else 0
    n_in = w_in.shape[2]
    wq = gq.shape[1]
    wv = H_A * DV_A
    widths = (wq, wq, wv, n_in - 2 * wq - wv)
    seqs = ROW_TILE // TILE
    out_specs = [pl.BlockSpec((ROW_TILE, w), lambda i: (i, 0)) for w in widths]
    out_shape = [jax.ShapeDtypeStruct((t, w), F32) for w in widths]
    out_specs[2] = pl.BlockSpec((ROW_TILE * H_A, DV_A), lambda i: (i, 0))
    out_shape[2] = jax.ShapeDtypeStruct((t * H_A, DV_A), F32)
    cache_specs = []
    if caches is not None:
        n_seq, nl = t // TILE, layers_before + 1
        out_specs[2] = pl.BlockSpec((seqs, nl, TILE * H_A, DV_A), lambda i: (i, 0, 0, 0))
        out_shape[2] = jax.ShapeDtypeStruct((n_seq, nl, TILE * H_A, DV_A), F32)
        out_specs.append(pl.BlockSpec((seqs, nl, wq, TILE), lambda i: (i, 0, 0, 0)))
        out_shape.append(jax.ShapeDtypeStruct((n_seq, nl, wq, TILE), F32))
        if caches:
            cache_specs = [
                pl.BlockSpec((seqs, layers_before, wq, TILE), lambda i: (i, 0, 0, 0)),
                pl.BlockSpec((seqs, layers_before, TILE * H_A, DV_A), lambda i: (i, 0, 0, 0))]
    return pl.pallas_call(
        functools.partial(_proj_kernel, layers_before=layers_before),
        grid=(t // ROW_TILE,),
        in_specs=[
            pl.BlockSpec((ROW_TILE, d), lambda i: (i, 0)),
            pl.BlockSpec((1, 1, N_MOD * d), lambda i: (i * ROW_TILE // tokens_per_mod, 0, 0)),
            pl.BlockSpec((1, d), lambda i: (0, 0)),
            pl.BlockSpec((None, d, n_in), lambda i: (layer, 0, 0)),
            pl.BlockSpec((1, wq), lambda i: (0, 0)),
            pl.BlockSpec((1, wq), lambda i: (0, 0)),
        ] + cache_specs,
        out_specs=out_specs,
        out_shape=out_shape,
        compiler_params=_params(("arbitrary",)),
        name="proj",
    )(x, mod, n1, w_in, gq, gk, *(caches or ()))


def _diff_lambda(dl_ref, lam_init):
    lv = dl_ref[...]
    a = jnp.sum(lv[0:1] * lv[1:2], axis=-1, keepdims=True)
    b = jnp.sum(lv[2:3] * lv[3:4], axis=-1, keepdims=True)
    return jnp.exp(a) - jnp.exp(b) + lam_init


def _diff_attention(q, k_of, vext_of, lam, sub_gain, lam_init, o_ref):
    hw = 2 * DQK_A
    first = lax.broadcasted_iota(jnp.int32, (1, hw), 1) < DQK_A
    for h in range(H_A):
        hs = slice(h * hw, (h + 1) * hw)
        qh = q[:, hs]
        kh = k_of(h)
        vext = vext_of(h)
        maps = []
        for qm in (jnp.where(first, qh, 0.0), jnp.where(first, 0.0, qh)):
            s = _dot_nt(qm.astype(BF16), kh)
            e = jnp.exp2(s - jnp.max(s, axis=-1, keepdims=True)).astype(BF16)
            oe = _dot(e, vext)
            maps.append(oe[:, 0:DV_A] / oe[:, DV_A:])
        oh = maps[0] - lam * maps[1]
        o_ref[:, hs] = _rms_rows(oh, sub_gain) * (1.0 - lam_init)


def _attn_ctx_kernel(q_ref, k_ref, v_ref, dl_ref, sg_ref, o_ref, *, lam_init, seq):
    lam = _diff_lambda(dl_ref, lam_init)
    ones = jnp.ones((seq, DV_A), BF16)
    hw = 2 * DQK_A
    for s in range(v_ref.shape[0]):
        rows = pl.ds(s * seq, seq)
        k = k_ref[rows, :].astype(BF16)

        def vext(h, s=s):
            return jnp.concatenate(
                [v_ref[s, pl.ds(h, seq, stride=H_A), :].astype(BF16), ones], axis=1)

        _diff_attention(q_ref[rows, :], lambda h, k=k: k[:, h * hw:(h + 1) * hw], vext, lam,
                        sg_ref[...], lam_init, o_ref.at[rows])


def _attention_ctx(q, k, v_all, layer, seq, dl, sg, lam_init):
    t = q.shape[0]
    w = H_A * DV_A
    per_step = ATTN_SEQS_PER_STEP
    return pl.pallas_call(
        functools.partial(_attn_ctx_kernel, lam_init=lam_init, seq=seq),
        grid=(t // (seq * per_step),),
        in_specs=[
            pl.BlockSpec((per_step * seq, w), lambda b: (b, 0)),
            pl.BlockSpec((per_step * seq, w), lambda b: (b, 0)),
            pl.BlockSpec((per_step, None, seq * H_A, DV_A), lambda b: (b, layer, 0, 0)),
            pl.BlockSpec(dl.shape, lambda b: (0, 0)),
            pl.BlockSpec(sg.shape, lambda b: (0, 0)),
        ],
        out_specs=pl.BlockSpec((per_step * seq, w), lambda b: (b, 0)),
        out_shape=jax.ShapeDtypeStruct((t, w), F32),
        compiler_params=_params(("arbitrary",)),
        name="attn_ctx",
    )(q, k, v_all, dl, sg)


def _rope(x, cos, sin_lo, sin_hi):
    return (x * cos + pltpu.roll(x, 2 * DQK_A - 16, 1) * sin_lo + pltpu.roll(x, 16, 1) * sin_hi)


def _attn_lat_kernel(q_ref, k_ref, v_ref, ck_ref, cv_ref, qcos_ref, qslo_ref, qshi_ref,
                     kcos_ref, kslo_ref, kshi_ref, dl_ref, sg_ref, o_ref, kall, vall, *, lam_init):
    hw = 2 * DQK_A
    n_new = k_ref.shape[0]

    @pl.when(pl.program_id(1) == 0)
    def _():
        kcos, kslo, kshi = kcos_ref[...], kslo_ref[...], kshi_ref[...]
        for h in range(H_A):
            hs = slice(h * hw, (h + 1) * hw)
            kall[0:n_new, hs] = _rope(k_ref[:, hs], kcos, kslo, kshi).astype(BF16)
        kall[n_new:, :] = ck_ref[...].astype(BF16)
        ones = jnp.ones((vall.shape[0], DV_A), BF16)
        for h in range(H_A):
            vs = slice(h * DV_A, (h + 1) * DV_A)
            vall[0:n_new, 2 * h * DV_A:(2 * h + 1) * DV_A] = (
                v_ref[pl.ds(h, n_new, stride=H_A), :].astype(BF16))
            vall[n_new:, 2 * h * DV_A:(2 * h + 1) * DV_A] = cv_ref[:, vs].astype(BF16)
            vall[:, (2 * h + 1) * DV_A:(2 * h + 2) * DV_A] = ones

    lam = _diff_lambda(dl_ref, lam_init)
    qcos, qslo, qshi = qcos_ref[...], qslo_ref[...], qshi_ref[...]
    q = jnp.concatenate(
        [_rope(q_ref[:, h * hw:(h + 1) * hw], qcos, qslo, qshi) for h in range(H_A)], axis=-1)
    _diff_attention(q, lambda h: kall[:, h * hw:(h + 1) * hw],
                    lambda h: vall[:, 2 * h * DV_A:(2 * h + 2) * DV_A], lam, sg_ref[...],
                    lam_init, o_ref)


def _attention_lat(q, k, v, n_batch, cache_k, cache_v, layer, rope_tabs, dl, sg, lam_init):
    t = q.shape[0]
    seq = t // n_batch
    past = cache_k.shape[2]
    w = H_A * DV_A
    hw = 2 * DQK_A
    nq = seq // TILE
    cos, slo, shi = rope_tabs
    tab_q = pl.BlockSpec((TILE, hw), lambda b, j: (j, 0))
    tab_k = pl.BlockSpec((seq, hw), lambda b, j: (0, 0))
    return pl.pallas_call(
        functools.partial(_attn_lat_kernel, lam_init=lam_init),
        grid=(n_batch, nq),
        in_specs=[
            pl.BlockSpec((TILE, w), lambda b, j: (b * nq + j, 0)),
            pl.BlockSpec((seq, w), lambda b, j: (b, 0)),
            pl.BlockSpec((seq * H_A, DV_A), lambda b, j: (b, 0)),
            pl.BlockSpec((None, None, past, w), lambda b, j: (b, layer, 0, 0)),
            pl.BlockSpec((None, None, past, w), lambda b, j: (b, layer, 0, 0)),
            tab_q, tab_q, tab_q, tab_k, tab_k, tab_k,
            pl.BlockSpec(dl.shape, lambda b, j: (0, 0)),
            pl.BlockSpec(sg.shape, lambda b, j: (0, 0)),
        ],
        out_specs=pl.BlockSpec((TILE, w), lambda b, j: (b * nq + j, 0)),
        out_shape=jax.ShapeDtypeStruct((t, w), F32),
        scratch_shapes=[pltpu.VMEM((seq + past, w), BF16), pltpu.VMEM((seq + past, 2 * w), BF16)],
        compiler_params=_params(("arbitrary", "arbitrary")),
        name="attn_lat",
    )(q, k, v, cache_k, cache_v, cos, slo, shi, cos, slo, shi, dl, sg)


def _tile_aux():
    rows = lax.broadcasted_iota(jnp.int32, (TILE, 1), 0)
    lane = lax.broadcasted_iota(jnp.int32, (1, W_B), 1)
    head_masks = [(lane // DK_B) == h for h in range(H_B)]
    first_of_pair = lax.broadcasted_iota(jnp.int32, (1, 2 * DK_B), 1) < DK_B
    xor = (lax.broadcasted_iota(jnp.int32, (HALF, HALF), 0)
           ^ lax.broadcasted_iota(jnp.int32, (HALF, HALF), 1))
    levels = [1 << i for i in range(TILE.bit_length() - 1)]
    right = {m: (rows & m) != 0 for m in levels}
    same_block = {m: xor < 2 * m for m in levels if 2 * m < HALF}
    return rows, head_masks, first_of_pair, right, same_block


def _hgrn_lower_bound(lb_ref, layer, direction, depth):
    xs = [lb_ref[2 * j + direction:2 * j + direction + 1, :] for j in range(depth)]
    m = functools.reduce(jnp.maximum, xs)
    es = [jnp.exp(x - m) for x in xs]
    return sum(es[1:layer + 1]) / sum(es)


def _hgrn_gates(fpre, lbd):
    t = jnp.exp(-jnp.abs(fpre))
    log_sig = jnp.minimum(fpre, 0.0) - jnp.log(1.0 + t)
    r = 1.0 / (1.0 + t)
    sig_neg = jnp.where(fpre >= 0.0, t * r, r)
    if lbd is None:
        return log_sig * LOG2E, sig_neg
    a = jnp.log(lbd)
    b = jnp.log1p(-lbd) + log_sig
    logf = jnp.maximum(a, b) + jnp.log1p(jnp.exp(-jnp.abs(a - b)))
    return logf * LOG2E, (1.0 - lbd) * sig_neg


def _chain_row(token):
    return SUBLANES * (token % CHAIN_LEN) + token // CHAIN_LEN


def _tile_prefix(g, g_scr, b_scr):
    g_c = _to_chains(g, g_scr)
    run = [g_c[0]]
    for j in range(1, CHAIN_LEN):
        run.append(run[-1] + g_c[j])
    sub = lax.broadcasted_iota(jnp.int32, (SUBLANES, 1), 0)
    incl = run[-1]
    k = 1
    while k < SUBLANES:
        incl = incl + jnp.where(sub >= k, pltpu.roll(incl, k, 0), 0.0)
        k *= 2
    offset = incl - run[-1]
    prefix = _from_chains([r + offset for r in run], b_scr)
    return prefix, incl[SUBLANES - 1:SUBLANES, :]


def _level_exponent(z, g, b_scr, m, reverse, right):
    if m == 1:
        return jnp.where(right, g, 0.0) if not reverse else jnp.where(right, 0.0, g)
    slabs = b_scr.shape[0]

    def ref_rows(token):
        row = _chain_row(token)
        return jnp.concatenate(
            [jnp.broadcast_to(b_scr[s, row:row + 1, :], (SUBLANES, LANES)) for s in range(slabs)],
            axis=1)

    groups = []
    for gq in range(TILE // SUBLANES):
        first = gq * SUBLANES
        zg = z[first:first + SUBLANES]
        if 2 * m >= SUBLANES:
            block = first // (2 * m) * (2 * m)
            ref = ref_rows(block + m - 1)
            if m >= SUBLANES:
                groups.append(zg - ref if first - block >= m else ref - zg)
                continue
        else:
            sub = lax.broadcasted_iota(jnp.int32, (SUBLANES, 1), 0)
            ref = jnp.where(sub < 2 * m, ref_rows(first + m - 1), ref_rows(first + 3 * m - 1))
        groups.append(-jnp.abs(zg - ref))
    return jnp.concatenate(groups, axis=0)


def _hgrn_tile(qh, kk, vv, g, reverse, aux, ones_bd, state_t, scratch):
    rows, head_masks, first_of_pair, right_rows, same_block = aux
    pair_w = 2 * DK_B
    g_scr, b_scr = scratch
    prefix, total = _tile_prefix(g, g_scr, b_scr)
    z = prefix if not reverse else prefix - g
    diag = [[None, None] for _ in range(H_B)]
    off = [None] * H_B
    m = 1
    while m < TILE:
        right = right_rows[m]
        x = jnp.exp2(_level_exponent(z, g, b_scr, m, reverse, right))
        q_rows = right if not reverse else jnp.logical_not(right)
        qz = jnp.where(q_rows, (qh * x).astype(BF16), 0.0)
        kz = jnp.where(q_rows, 0.0, (kk * x).astype(BF16))
        for h in range(H_B):
            ls = slice((h // 2) * pair_w, (h // 2 + 1) * pair_w)
            qm = jnp.where(first_of_pair, qz[:, ls], 0.0) if h % 2 == 0 else \
                jnp.where(first_of_pair, 0.0, qz[:, ls])
            kh = kz[:, ls]
            if m < HALF:
                for i in range(2):
                    rs = slice(i * HALF, (i + 1) * HALF)
                    pm = _dot_nt(qm[rs], kh[rs])
                    if m in same_block:
                        pm = jnp.where(same_block[m], pm, 0.0)
                    diag[h][i] = pm if diag[h][i] is None else diag[h][i] + pm
            else:
                lo, hi = slice(0, HALF), slice(HALF, TILE)
                off[h] = _dot_nt(qm[hi], kh[lo]) if not reverse else _dot_nt(qm[lo], kh[hi])
        m *= 2
    vb = vv.astype(BF16)
    top = bot = None
    for h in range(H_B):
        vh = jnp.where(head_masks[h], vb, 0.0)
        d0, d1, of = (a.astype(BF16) for a in (diag[h][0], diag[h][1], off[h]))
        if not reverse:
            t = _dot(d0, vh[0:HALF])
            b = _dot(jnp.concatenate([of, d1], axis=1), vh)
        else:
            t = _dot(jnp.concatenate([d0, of], axis=1), vh)
            b = _dot(d1, vh[HALF:])
        top = t if top is None else top + t
        bot = b if bot is None else bot + b
    out = _group_sum(qh * kk, ones_bd) * vv + jnp.concatenate([top, bot], axis=0)
    q_decay, k_decay = (z, total - z) if not reverse else (total - z, z)
    if state_t is not None:
        out = out + _dot_nt((qh * jnp.exp2(q_decay)).astype(BF16), state_t.astype(BF16))
    k_hat = (kk * jnp.exp2(k_decay)).astype(BF16)
    return out, k_hat, total


def _block_diag_mask():
    r = lax.broadcasted_iota(jnp.int32, (W_B, W_B), 0) // DK_B
    c = lax.broadcasted_iota(jnp.int32, (W_B, W_B), 1) // DV_B
    return r == c


def _conv(x, prev_row, next_rows, cw_ref, cb_ref, rows):
    x_m1 = jnp.where(rows == 0, prev_row, pltpu.roll(x, 1, 0))
    x_p1 = jnp.where(rows == TILE - 1, next_rows[0:1], pltpu.roll(x, TILE - 1, 0))
    x_p2 = jnp.where(rows == TILE - 2, next_rows[0:1],
                     jnp.where(rows == TILE - 1, next_rows[1:2], pltpu.roll(x, TILE - 2, 0)))
    y = cb_ref[...] + x_m1 * cw_ref[0:1, :]
    y = y + x * cw_ref[1:2, :]
    y = y + x_p1 * cw_ref[2:3, :]
    return y + x_p2 * cw_ref[3:4, :]


def _chain_address(g):
    per_chain = CHAIN_LEN // SUBLANES
    return SUBLANES * SUBLANES * (g % per_chain) + g // per_chain, SUBLANES


def _to_chains(x, scr):
    slabs = scr.shape[0]
    for s in range(slabs):
        for g in range(TILE // SUBLANES):
            start, stride = _chain_address(g)
            scr[s, pl.ds(start, SUBLANES, stride=stride), :] = (
                x[g * SUBLANES:(g + 1) * SUBLANES, s * LANES:(s + 1) * LANES])
    return [jnp.concatenate([scr[s, j * SUBLANES:(j + 1) * SUBLANES, :] for s in range(slabs)],
                            axis=1) for j in range(CHAIN_LEN)]


def _from_chains(chains, scr):
    slabs = scr.shape[0]
    for s in range(slabs):
        for j, c in enumerate(chains):
            scr[s, j * SUBLANES:(j + 1) * SUBLANES, :] = c[:, s * LANES:(s + 1) * LANES]
    groups = []
    for g in range(TILE // SUBLANES):
        start, stride = _chain_address(g)
        groups.append(jnp.concatenate(
            [scr[s, pl.ds(start, SUBLANES, stride=stride), :] for s in range(slabs)], axis=1))
    return jnp.concatenate(groups, axis=0)


def _rglru_tile(xc, r_pre, i_pre, lam_row, reverse, h_in, scratch):
    a_scr, u_scr, h_scr = scratch
    r = _sigmoid(r_pre)
    i = _sigmoid(i_pre)
    log_a = -RG_C * r * _softplus(-lam_row)
    a = jnp.exp(log_a)
    u = jnp.sqrt(jnp.tanh(-log_a) * (a * a + 1.0)) * (i * xc)
    a_c = _to_chains(a, a_scr)
    u_c = _to_chains(u, u_scr)
    order = list(reversed(range(CHAIN_LEN))) if reverse else list(range(CHAIN_LEN))
    h_c = [None] * CHAIN_LEN
    p_c = [None] * CHAIN_LEN
    h_c[order[0]], p_c[order[0]] = u_c[order[0]], a_c[order[0]]
    for prev, j in zip(order[:-1], order[1:]):
        h_c[j] = a_c[j] * h_c[prev] + u_c[j]
        p_c[j] = a_c[j] * p_c[prev]
    p_tot, h_tot = p_c[order[-1]], h_c[order[-1]]
    sub = lax.broadcasted_iota(jnp.int32, (SUBLANES, 1), 0)
    k = 1
    while k < SUBLANES:
        if not reverse:
            valid = sub >= k
            p_s, h_s = pltpu.roll(p_tot, k, 0), pltpu.roll(h_tot, k, 0)
        else:
            valid = sub < SUBLANES - k
            p_s, h_s = pltpu.roll(p_tot, SUBLANES - k, 0), pltpu.roll(h_tot, SUBLANES - k, 0)
        h_tot = h_tot + p_tot * jnp.where(valid, h_s, 0.0)
        p_tot = p_tot * jnp.where(valid, p_s, 1.0)
        k *= 2
    if h_in is not None:
        h_tot = h_tot + p_tot * h_in
    if not reverse:
        carry = jnp.where(sub == 0, 0.0 if h_in is None else h_in, pltpu.roll(h_tot, 1, 0))
        h_out = h_tot[SUBLANES - 1:SUBLANES, :]
    else:
        carry = jnp.where(sub == SUBLANES - 1, 0.0 if h_in is None else h_in,
                          pltpu.roll(h_tot, SUBLANES - 1, 0))
        h_out = h_tot[0:1, :]
    h_c = [h + p * carry for h, p in zip(h_c, p_c)]
    return _from_chains(h_c, h_scr), h_out


def _rec_inputs(refs):
    qb, ff, fb, ib, gb, xcol, gcol = [r[...] for r in refs]
    return _silu(qb), ff, fb, ib, gb, xcol, gcol


def _hgrn_out(o_sum, gb, onorm_gain, ones_bd):
    ms = _group_sum(o_sum * o_sum, ones_bd) * (1.0 / DV_B)
    return o_sum * lax.rsqrt(ms + EPS) * onorm_gain * _silu(gb)


def _rec_ctx_kernel(*refs, layer, depth):
    (q_ref, ff_ref, fb_ref, i_ref, g_ref, x_ref, gc_ref, lb_ref, on_ref, cw_ref, cb_ref, wg_ref,
     bg_ref, lam_ref, ones_ref, ob_ref, oc_ref, st_ref, hs_ref,
     *scan_scratch) = refs[2 if layer else 0:]
    aux = _tile_aux()
    rows = aux[0]
    ones_bd = ones_ref[...]
    qh, ff, fb, ib, gb, xcol, gcol = _rec_inputs((q_ref, ff_ref, fb_ref, i_ref, g_ref, x_ref, gc_ref))
    o_sum = None
    for direction, fpre in enumerate((ff, fb)):
        lbd = None if layer == 0 else _hgrn_lower_bound(lb_ref, layer, direction, depth)
        g, kk = _hgrn_gates(fpre, lbd)
        o, k_hat, _ = _hgrn_tile(qh, kk, ib, g, direction == 1, aux, ones_bd, None,
                                 scan_scratch[3 + 2 * direction:5 + 2 * direction])
        o_sum = o if o_sum is None else o_sum + o
        full = _dot_tn(k_hat, ib.astype(BF16))
        for h in range(H_B):
            blk = full[h * DK_B:(h + 1) * DK_B, (h // 2) * 2 * DV_B:(h // 2 + 1) * 2 * DV_B]
            if h % 2:
                blk = pltpu.roll(blk, DV_B, 1)
            st_ref[0, 0, direction, h] = blk[:, 0:DV_B]
    ob_ref[...] = _hgrn_out(o_sum, gb, on_ref[...], ones_bd)

    zero_row = jnp.zeros((1, W_C), F32)
    xc = _conv(xcol, zero_row, jnp.zeros((2, W_C), F32), cw_ref, cb_ref, rows)
    gates = _dot(xc.astype(BF16), wg_ref[...]) + bg_ref[...]
    y_f, h_f = _rglru_tile(xc, gates[:, 0:W_C], gates[:, W_C:2 * W_C], lam_ref[0:1, :], False,
                           None, scan_scratch[0:3])
    y_b, h_b = _rglru_tile(xc, gates[:, 2 * W_C:3 * W_C], gates[:, 3 * W_C:], lam_ref[1:2, :], True,
                           None, scan_scratch[0:3])
    oc_ref[...] = (y_f + y_b) * _gelu_tanh(gcol)
    hs_ref[0, 0, 0:1, :] = h_f
    hs_ref[0, 0, 1:2, :] = h_b


def _scan_scratch(hgrn_directions):
    return [pltpu.VMEM((W_C // LANES, TILE, LANES), F32) for _ in range(3 + 2 * hgrn_directions)]


def _col_spec(col, row_map, rows=TILE):
    return pl.BlockSpec((rows, W_B), lambda *idx: (row_map(*idx), col))


COL_QB, COL_FF, COL_FB, COL_IB, COL_GB, COL_XC, COL_GC = range(7)
_REC_COLS = (COL_QB, COL_FF, COL_FB, COL_IB, COL_GB, COL_XC, COL_GC)


def _full_spec(a):
    nd = a.ndim
    return pl.BlockSpec(a.shape, lambda *idx: (0,) * nd)


def _recurrence_ctx(proj, layer, depth, consts, states):
    t = proj.shape[0]
    n = t // TILE
    assert bool(states) == (layer > 0)
    return pl.pallas_call(
        functools.partial(_rec_ctx_kernel, layer=layer, depth=depth),
        grid=(n,),
        in_specs=([pl.BlockSpec(memory_space=pl.ANY) for _ in states]
                  + [_col_spec(c, lambda b: b) for c in _REC_COLS] + [_full_spec(a) for a in consts]),
        out_specs=[
            pl.BlockSpec((TILE, W_B), lambda b: (b, 0)),
            pl.BlockSpec((TILE, W_C), lambda b: (b, 0)),
            pl.BlockSpec((1, 1, 2, H_B, DK_B, DV_B), lambda b: (b, layer, 0, 0, 0, 0)),
            pl.BlockSpec((1, 1, 2, W_C), lambda b: (b, layer, 0, 0)),
        ],
        out_shape=[
            jax.ShapeDtypeStruct((t, W_B), F32),
            jax.ShapeDtypeStruct((t, W_C), F32),
            jax.ShapeDtypeStruct((n, depth, 2, H_B, DK_B, DV_B), F32),
            jax.ShapeDtypeStruct((n, depth, 2, W_C), F32),
        ],
        input_output_aliases={i: 2 + i for i in range(len(states))},
        scratch_shapes=_scan_scratch(2),
        compiler_params=_params(("arbitrary",)),
        name="rec_ctx",
    )(*states, *([proj] * len(_REC_COLS)), *consts)


def _rec_lat_kernel(q_ref, f_ref, i_ref, g_ref, x_ref, xp_ref, xn_ref, gc_ref, s0_ref, h0_ref,
                    *rest, layer, depth, direction, n_tiles):
    if direction == 0:
        (lb_ref, on_ref, cw_ref, cb_ref, wg_ref, bg_ref, lam_ref, ones_ref,
         o_ref, y_ref, state, hstate, *scan_scratch) = rest
    else:
        (of_ref, yf_ref, lb_ref, on_ref, cw_ref, cb_ref, wg_ref, bg_ref, lam_ref, ones_ref,
         o_ref, y_ref, state, hstate, *scan_scratch) = rest
    step = pl.program_id(1)
    tile = step if direction == 0 else n_tiles - 1 - step
    reverse = direction == 1

    @pl.when(step == 0)
    def _():
        state[...] = s0_ref[0, 0]
        hstate[...] = h0_ref[0, 0]

    aux = _tile_aux()
    rows = aux[0]
    ones_bd = ones_ref[...]
    qh = _silu(q_ref[...])
    ib = i_ref[...]
    lbd = None if layer == 0 else _hgrn_lower_bound(lb_ref, layer, direction, depth)
    g, kk = _hgrn_gates(f_ref[...], lbd)
    o, k_hat, total = _hgrn_tile(qh, kk, ib, g, reverse, aux, ones_bd, state[...],
                                 scan_scratch[3:5])
    upd = jnp.where(_block_diag_mask(), _dot_tn(ib.astype(BF16), k_hat), 0.0)
    state[...] = state[...] * jnp.exp2(total) + upd

    prev_row = jnp.where(tile == 0, 0.0, xp_ref[SUBLANES - 1:SUBLANES, :])
    next_rows = jnp.where(tile == n_tiles - 1, 0.0, xn_ref[0:2, :])
    xc = _conv(x_ref[...], prev_row, next_rows, cw_ref, cb_ref, rows)
    gates = _dot(xc.astype(BF16), wg_ref[...]) + bg_ref[...]
    c0 = 2 * W_C * direction
    y, h_next = _rglru_tile(xc, gates[:, c0:c0 + W_C], gates[:, c0 + W_C:c0 + 2 * W_C],
                            lam_ref[direction:direction + 1, :], reverse, hstate[...],
                            scan_scratch[0:3])
    hstate[...] = h_next

    if direction == 0:
        o_ref[...] = o
        y_ref[...] = y
    else:
        o_ref[...] = _hgrn_out(of_ref[...] + o, g_ref[...], on_ref[...], ones_bd)
        y_ref[...] = (yf_ref[...] + y) * _gelu_tanh(gc_ref[...])


def _recurrence_lat(proj, n_batch, layer, depth, direction, state0, h0, prev, consts):
    t = proj.shape[0]
    n_tiles = t // (TILE * n_batch)

    def tile_of(b, s):
        return s if direction == 0 else n_tiles - 1 - s

    def row(b, s):
        return b * n_tiles + tile_of(b, s)

    groups = TILE // SUBLANES

    def group_prev(b, s):
        return jnp.maximum(row(b, s) * groups - 1, 0)

    def group_next(b, s):
        return jnp.minimum((row(b, s) + 1) * groups, t // SUBLANES - 1)

    f_col = COL_FF if direction == 0 else COL_FB
    in_specs = [_col_spec(COL_QB, row), _col_spec(f_col, row), _col_spec(COL_IB, row),
                _col_spec(COL_GB, row), _col_spec(COL_XC, row),
                _col_spec(COL_XC, group_prev, SUBLANES), _col_spec(COL_XC, group_next, SUBLANES),
                _col_spec(COL_GC, row),
                pl.BlockSpec((1, 1, W_B, W_B), lambda b, s: (b, direction, 0, 0)),
                pl.BlockSpec((1, 1, 1, W_C), lambda b, s: (b, direction, 0, 0))]
    args = [proj] * 8 + [state0, h0]
    if direction == 1:
        in_specs += [pl.BlockSpec((TILE, W_B), lambda b, s: (row(b, s), 0)),
                     pl.BlockSpec((TILE, W_C), lambda b, s: (row(b, s), 0))]
        args += list(prev)
    in_specs += [_full_spec(a) for a in consts]
    args += list(consts)
    return pl.pallas_call(
        functools.partial(_rec_lat_kernel, layer=layer, depth=depth, direction=direction,
                          n_tiles=n_tiles),
        grid=(n_batch, n_tiles),
        in_specs=in_specs,
        out_specs=[pl.BlockSpec((TILE, W_B), lambda b, s: (row(b, s), 0)),
                   pl.BlockSpec((TILE, W_C), lambda b, s: (row(b, s), 0))],
        out_shape=[jax.ShapeDtypeStruct((t, W_B), F32), jax.ShapeDtypeStruct((t, W_C), F32)],
        scratch_shapes=[pltpu.VMEM((W_B, W_B), F32), pltpu.VMEM((1, W_C), F32)] + _scan_scratch(1),
        compiler_params=_params(("arbitrary", "arbitrary")),
        name="rec_lat_fwd" if direction == 0 else "rec_lat_bwd",
    )(*args)


def _out_kernel(x_ref, oa_ref, ob_ref, oc_ref, mod_ref, n2_ref, wo_ref, w1_ref, w2_ref, y_ref):
    x = x_ref[...]
    d = x.shape[-1]
    g1 = mod_ref[0, :, 2 * d:3 * d]
    sh2 = mod_ref[0, :, 3 * d:4 * d]
    sc2 = mod_ref[0, :, 4 * d:5 * d]
    g2 = mod_ref[0, :, 5 * d:6 * d]
    wa = H_A * DV_A
    mix = _dot(oa_ref[...].astype(BF16), wo_ref[0:wa, :])
    mix = mix + _dot(ob_ref[...].astype(BF16), wo_ref[wa:wa + W_B, :])
    mix = mix + _dot(oc_ref[...].astype(BF16), wo_ref[wa + W_B:, :])
    x1 = x + g1 * mix
    h = (_rms_rows(x1, n2_ref[...]) * (1.0 + sc2) + sh2).astype(BF16)
    acc = jnp.zeros_like(x)
    for c in range(w1_ref.shape[1] // D_FF_CHUNK):
        cs = slice(c * D_FF_CHUNK, (c + 1) * D_FF_CHUNK)
        f = jnp.maximum(_dot(h, w1_ref[:, cs]), 0.0)
        acc = acc + _dot((f * f).astype(BF16), w2_ref[cs, :])
    y_ref[...] = x1 + g2 * acc


def _out_mlp(x, oa, ob, oc, mod, tokens_per_mod, n2, w_out, w_ff1, w_ff2, layer):
    t, d = x.shape

    def resident(w):
        return pl.BlockSpec((None,) + w.shape[1:], lambda i: (layer, 0, 0),
                            pipeline_mode=pl.Buffered(1))

    return pl.pallas_call(
        _out_kernel,
        grid=(t // ROW_TILE,),
        in_specs=[
            pl.BlockSpec((ROW_TILE, d), lambda i: (i, 0)),
            pl.BlockSpec((ROW_TILE, oa.shape[1]), lambda i: (i, 0)),
            pl.BlockSpec((ROW_TILE, ob.shape[1]), lambda i: (i, 0)),
            pl.BlockSpec((ROW_TILE, oc.shape[1]), lambda i: (i, 0)),
            pl.BlockSpec((1, 1, N_MOD * d), lambda i: (i * ROW_TILE // tokens_per_mod, 0, 0)),
            pl.BlockSpec((1, d), lambda i: (0, 0)),
            resident(w_out),
            resident(w_ff1),
            resident(w_ff2),
        ],
        out_specs=pl.BlockSpec((ROW_TILE, d), lambda i: (i, 0)),
        out_shape=jax.ShapeDtypeStruct((t, d), F32),
        compiler_params=_params(("arbitrary",)),
        name="out_mlp",
    )(x, oa, ob, oc, mod, n2, w_out, w_ff1, w_ff2)


def _ones_block_diag(n, group):
    idx = np.arange(n) // group
    return jnp.asarray((idx[:, None] == idx[None, :]).astype(np.float32), dtype=BF16)


def _rope_tables(seq):
    nf = DQK_A // 4
    inv = ROPE_BASE ** (-jnp.arange(nf, dtype=F32) / nf)
    pos = jnp.arange(seq)
    ang_r = (pos // GRID_W).astype(F32)[:, None] * inv
    ang_c = (pos % GRID_W).astype(F32)[:, None] * inv
    z = jnp.zeros_like(ang_r)
    cos = jnp.concatenate([jnp.cos(ang_r)] * 2 + [jnp.cos(ang_c)] * 2, axis=-1)
    sin_lo = jnp.concatenate([-jnp.sin(ang_r), z, -jnp.sin(ang_c), z], axis=-1)
    sin_hi = jnp.concatenate([z, jnp.sin(ang_r), z, jnp.sin(ang_c)], axis=-1)
    return tuple(jnp.tile(a, (1, 2)) for a in (cos, sin_lo, sin_hi))


def _gate_weights(rg_w_l):
    eye = jnp.eye(H_C, dtype=rg_w_l.dtype)
    w = jnp.einsum('dghij,hk->dghikj', rg_w_l, eye)
    w = w.reshape(2, 2, W_C, W_C)
    return jnp.transpose(w, (2, 0, 1, 3)).reshape(W_C, 4 * W_C).astype(BF16)


def _state_block_diag_t(s):
    eye = jnp.eye(H_B, dtype=s.dtype)
    w = jnp.einsum('...hdv,hk->...hvkd', s, eye)
    return w.reshape(s.shape[:-3] + (W_B, W_B))


def kernel(x_prompt, x_sample, cache_k, cache_v, state_hgrn, state_rglru, c, c_ctx, w_mod, b_mod,
           norm1, norm2, w_in, w_out, qk_norm, diff_lambda, subln, hgrn_lb, hgrn_onorm, conv_w,
           conv_b, rg_w, rg_b, rg_lambda, w_ff1, w_ff2):
    batch, seq, d = x_prompt.shape
    dec_batch, dec_seq, _ = x_sample.shape
    depth = w_in.shape[0]
    past = cache_k.shape[2]
    wq = H_A * 2 * DQK_A

    c8 = jnp.concatenate([c_ctx[None], c, jnp.zeros((8 - 1 - dec_batch, d), F32)], axis=0)
    mod_all = _modulation(c8, w_mod, b_mod)

    assert DQK_A == DV_B and seq == TILE
    ones_b = _ones_block_diag(W_B, DV_B)
    rope_tabs = _rope_tables(dec_seq)
    ck = cache_k.reshape(dec_batch, depth, past, wq)
    cv = cache_v.reshape(dec_batch, depth, past, H_A * DV_A)
    lb2 = hgrn_lb.reshape(depth * 2, W_B)

    w_in_b, w_out_b, w1_b, w2_b = (w.astype(BF16) for w in (w_in, w_out, w_ff1, w_ff2))

    yp = x_prompt.reshape(batch * seq, d)
    ys = x_sample.reshape(dec_batch * dec_seq, d)
    caches, states = (), ()
    for l in range(depth):
        lam_init = 0.8 - 0.6 * math.exp(-0.3 * l)
        mod_ctx = mod_all[l, 0:1][:, None, :]
        mod_lat = mod_all[l, 1:1 + dec_batch][:, None, :]
        n1 = norm1[l][None]
        n2 = norm2[l][None]
        gq = jnp.tile(qk_norm[l, 0], H_A * 2)[None] * (DQK_A ** -0.5 * LOG2E)
        gk = jnp.tile(qk_norm[l, 1], H_A * 2)[None]
        dl = diff_lambda[l]
        sg = subln[l][None]
        consts = (lb2, jnp.tile(hgrn_onorm[l], H_B)[None], conv_w[l], conv_b[l][None],
                  _gate_weights(rg_w[l]), rg_b[l].reshape(1, 4 * W_C), rg_lambda[l], ones_b)

        q, k, v_all, rest, kt_all = _in_projection(yp, mod_ctx, batch * seq, n1, w_in_b, l,
                                                   gq, gk, caches=caches)
        caches = (kt_all, v_all)
        oa = _attention_ctx(q, k, v_all, l, seq, dl, sg, lam_init)
        ob, oc, *states = _recurrence_ctx(rest, l, depth, consts, states)
        yp = _out_mlp(yp, oa, ob, oc, mod_ctx, batch * seq, n2, w_out_b, w1_b, w2_b, l)

        q, k, v, rest = _in_projection(ys, mod_lat, dec_seq, n1, w_in_b, l, gq, gk)
        oa = _attention_lat(q, k, v, dec_batch, ck, cv, l, rope_tabs, dl, sg, lam_init)
        s0 = _state_block_diag_t(state_hgrn[:, l])
        h0 = state_rglru[:, l][:, :, None, :]
        fwd = _recurrence_lat(rest, dec_batch, l, depth, 0, s0, h0, None, consts)
        ob, oc = _recurrence_lat(rest, dec_batch, l, depth, 1, s0, h0, fwd, consts)
        ys = _out_mlp(ys, oa, ob, oc, mod_lat, dec_seq, n2, w_out_b, w1_b, w2_b, l)

    kt_all, v_all = caches
    new_k = jnp.transpose(kt_all.reshape(batch, depth, H_A, 2, DQK_A, seq), (0, 1, 5, 2, 3, 4))
    new_v = v_all.reshape(batch, depth, seq, H_A, DV_A)
    return (yp.reshape(batch, seq, d), ys.reshape(dec_batch, dec_seq, d), new_k, new_v,
            states[0], states[1])
```

```python
import functools
import math

import numpy as np
import jax
import jax.numpy as jnp
from jax import lax
from jax.experimental import pallas as pl
from jax.experimental.pallas import tpu as pltpu

F32 = jnp.float32
BF16 = jnp.bfloat16

GRID_W = 64
H_A, DQK_A, DV_A = 4, 64, 128
H_B, DK_B, DV_B = 4, 64, 64
W_B = H_B * DV_B
H_C, W_C = 4, 256
CONV_K = 4
RG_C = 8.0
ROPE_BASE = 10000.0
EPS = 1e-6
N_MOD = 6
D_FF_CHUNK = 1024

TILE = 256
HALF = TILE // 2
ROW_TILE = 512
ATTN_SEQS_PER_STEP = 4
SUBLANES = 8
LANES = 128
CHAIN_LEN = TILE // SUBLANES
LOG2E = math.log2(math.e)
V7X_VMEM_BYTES = 64 * 1024 * 1024
VMEM_LIMIT = V7X_VMEM_BYTES - 8 * 1024 * 1024


def _dot(a, b):
    return jnp.dot(a, b, preferred_element_type=F32)


def _dot_nt(a, b):
    return lax.dot_general(a, b, (((1,), (1,)), ((), ())), preferred_element_type=F32)


def _dot_tn(a, b):
    return lax.dot_general(a, b, (((0,), (0,)), ((), ())), preferred_element_type=F32)


def _group_sum(x, ones_bd):
    hi = x.astype(BF16)
    lo = (x - hi.astype(F32)).astype(BF16)
    return _dot(hi, ones_bd) + _dot(lo, ones_bd)


def _rms_rows(x, gain):
    return x * lax.rsqrt(jnp.mean(x * x, axis=-1, keepdims=True) + EPS) * gain


def _sigmoid(x):
    return 0.5 * jnp.tanh(0.5 * x) + 0.5


def _silu(x):
    return x * _sigmoid(x)


def _softplus(x):
    return jnp.maximum(x, 0.0) + jnp.log1p(jnp.exp(-jnp.abs(x)))


def _log_sigmoid(x):
    return jnp.minimum(x, 0.0) - jnp.log1p(jnp.exp(-jnp.abs(x)))


def _gelu_tanh(x):
    return x * (0.5 * (1.0 + jnp.tanh(math.sqrt(2.0 / math.pi) * (x + 0.044715 * (x * x * x)))))


def _params(semantics, flags=None):
    return pltpu.CompilerParams(dimension_semantics=semantics, vmem_limit_bytes=VMEM_LIMIT,
                                flags=flags)


def _mod_kernel(c_ref, w_ref, b_ref, o_ref):
    c = c_ref[...]
    o_ref[0] = _dot(_silu(c).astype(BF16), w_ref[0].astype(BF16)) + b_ref[0]


def _modulation(c8, w_mod, b_mod):
    depth, d, n = w_mod.shape
    tn = d
    return pl.pallas_call(
        _mod_kernel,
        grid=(depth, n // tn),
        in_specs=[
            pl.BlockSpec((8, d), lambda l, j: (0, 0)),
            pl.BlockSpec((1, d, tn), lambda l, j: (l, 0, j)),
            pl.BlockSpec((1, 1, tn), lambda l, j: (l, 0, j)),
        ],
        out_specs=pl.BlockSpec((1, 8, tn), lambda l, j: (l, 0, j)),
        out_shape=jax.ShapeDtypeStruct((depth, 8, n), F32),
        compiler_params=_params(("arbitrary", "arbitrary")),
        name="mod",
    )(c8, w_mod, b_mod.reshape(depth, 1, n))


def _proj_kernel(x_ref, mod_ref, n1_ref, w_ref, gq_ref, gk_ref, *refs, layers_before):
    prev = refs[:2] if layers_before else ()
    q_ref, k_ref, v_ref, r_ref, *kt_ref = refs[len(prev):]
    x = x_ref[...]
    d = x.shape[-1]
    sh = mod_ref[0, :, 0:d]
    sc = mod_ref[0, :, d:2 * d]
    h = _rms_rows(x, n1_ref[...]) * (1.0 + sc) + sh
    p = _dot(h.astype(BF16), w_ref[...])
    wq = H_A * 2 * DQK_A

    first = lax.broadcasted_iota(jnp.int32, (1, LANES), 1) < DQK_A

    def qk_norm(z, gain):
        z2 = z * z
        parts = []
        for c in range(0, wq, LANES):
            slab = z2[:, c:c + LANES]
            lo = jnp.sum(jnp.where(first, slab, 0.0), axis=-1, keepdims=True)
            hi = jnp.sum(jnp.where(first, 0.0, slab), axis=-1, keepdims=True)
            parts.append(jnp.where(first, lo, hi))
        ms = jnp.concatenate(parts, axis=1) * (1.0 / DQK_A)
        return z * lax.rsqrt(ms + EPS) * gain

    wv = H_A * DV_A
    q_ref[...] = qk_norm(p[:, 0:wq], gq_ref[...])
    kn = qk_norm(p[:, wq:2 * wq], gk_ref[...])
    k_ref[...] = kn
    if kt_ref:
        for s in range(ROW_TILE // TILE):
            kt_ref[0][s, layers_before] = kn[s * TILE:(s + 1) * TILE].T
        if prev:
            ktp_ref, vp_ref = prev
            kt_ref[0][:, 0:layers_before] = ktp_ref[...]
            v_ref[:, 0:layers_before] = vp_ref[...]
    for s in range(ROW_TILE // TILE):
        v_dst = v_ref.at[s, layers_before] if kt_ref else v_ref.at[pl.ds(s * TILE * H_A, TILE * H_A)]
        for h in range(H_A):
            v_dst[pl.ds(h, TILE, stride=H_A), :] = (
                p[s * TILE:(s + 1) * TILE, 2 * wq + h * DV_A:2 * wq + (h + 1) * DV_A])
    r_ref[...] = p[:, 2 * wq + wv:]


def _in_projection(x, mod, tokens_per_mod, n1, w_in, layer, gq, gk, caches=None):
    t, d = x.shape
    layers_before = caches[0].shape[1] if caches else 0
    n_in = w_in.shape[2]
    wq = gq.shape[1]
    wv = H_A * DV_A
    widths = (wq, wq, wv, n_in - 2 * wq - wv)
    seqs = ROW_TILE // TILE
    out_specs = [pl.BlockSpec((ROW_TILE, w), lambda i: (i, 0)) for w in widths]
    out_shape = [jax.ShapeDtypeStruct((t, w), F32) for w in widths]
    out_specs[2] = pl.BlockSpec((ROW_TILE * H_A, DV_A), lambda i: (i, 0))
    out_shape[2] = jax.ShapeDtypeStruct((t * H_A, DV_A), F32)
    cache_specs = []
    if caches is not None:
        n_seq, nl = t // TILE, layers_before + 1
        out_specs[2] = pl.BlockSpec((seqs, nl, TILE * H_A, DV_A), lambda i: (i, 0, 0, 0))
        out_shape[2] = jax.ShapeDtypeStruct((n_seq, nl, TILE * H_A, DV_A), F32)
        out_specs.append(pl.BlockSpec((seqs, nl, wq, TILE), lambda i: (i, 0, 0, 0)))
        out_shape.append(jax.ShapeDtypeStruct((n_seq, nl, wq, TILE), F32))
        if caches:
            cache_specs = [
                pl.BlockSpec((seqs, layers_before, wq, TILE), lambda i: (i, 0, 0, 0)),
                pl.BlockSpec((seqs, layers_before, TILE * H_A, DV_A), lambda i: (i, 0, 0, 0))]
    return pl.pallas_call(
        functools.partial(_proj_kernel, layers_before=layers_before),
        grid=(t // ROW_TILE,),
        in_specs=[
            pl.BlockSpec((ROW_TILE, d), lambda i: (i, 0)),
            pl.BlockSpec((1, 1, N_MOD * d), lambda i: (i * ROW_TILE // tokens_per_mod, 0, 0)),
            pl.BlockSpec((1, d), lambda i: (0, 0)),
            pl.BlockSpec((None, d, n_in), lambda i: (layer, 0, 0)),
            pl.BlockSpec((1, wq), lambda i: (0, 0)),
            pl.BlockSpec((1, wq), lambda i: (0, 0)),
        ] + cache_specs,
        out_specs=out_specs,
        out_shape=out_shape,
        compiler_params=_params(("arbitrary",)),
        name="proj",
    )(x, mod, n1, w_in, gq, gk, *(caches or ()))


def _diff_lambda(dl_ref, lam_init):
    lv = dl_ref[...]
    a = jnp.sum(lv[0:1] * lv[1:2], axis=-1, keepdims=True)
    b = jnp.sum(lv[2:3] * lv[3:4], axis=-1, keepdims=True)
    return jnp.exp(a) - jnp.exp(b) + lam_init


def _diff_attention(q, k_of, vext_of, lam, sub_gain, lam_init, o_ref):
    hw = 2 * DQK_A
    first = lax.broadcasted_iota(jnp.int32, (1, hw), 1) < DQK_A
    for h in range(H_A):
        hs = slice(h * hw, (h + 1) * hw)
        qh = q[:, hs]
        kh = k_of(h)
        vext = vext_of(h)
        maps = []
        for qm in (jnp.where(first, qh, 0.0), jnp.where(first, 0.0, qh)):
            s = _dot_nt(qm.astype(BF16), kh)
            e = jnp.exp2(s - jnp.max(s, axis=-1, keepdims=True)).astype(BF16)
            oe = _dot(e, vext)
            maps.append(oe[:, 0:DV_A] / oe[:, DV_A:])
        oh = maps[0] - lam * maps[1]
        o_ref[:, hs] = _rms_rows(oh, sub_gain) * (1.0 - lam_init)


def _attn_ctx_kernel(q_ref, k_ref, v_ref, dl_ref, sg_ref, o_ref, *, lam_init, seq):
    lam = _diff_lambda(dl_ref, lam_init)
    ones = jnp.ones((seq, DV_A), BF16)
    hw = 2 * DQK_A
    for s in range(v_ref.shape[0]):
        rows = pl.ds(s * seq, seq)
        k = k_ref[rows, :].astype(BF16)

        def vext(h, s=s):
            return jnp.concatenate(
                [v_ref[s, pl.ds(h, seq, stride=H_A), :].astype(BF16), ones], axis=1)

        _diff_attention(q_ref[rows, :], lambda h, k=k: k[:, h * hw:(h + 1) * hw], vext, lam,
                        sg_ref[...], lam_init, o_ref.at[rows])


def _attention_ctx(q, k, v_all, layer, seq, dl, sg, lam_init):
    t = q.shape[0]
    w = H_A * DV_A
    per_step = ATTN_SEQS_PER_STEP
    return pl.pallas_call(
        functools.partial(_attn_ctx_kernel, lam_init=lam_init, seq=seq),
        grid=(t // (seq * per_step),),
        in_specs=[
            pl.BlockSpec((per_step * seq, w), lambda b: (b, 0)),
            pl.BlockSpec((per_step * seq, w), lambda b: (b, 0)),
            pl.BlockSpec((per_step, None, seq * H_A, DV_A), lambda b: (b, layer, 0, 0)),
            pl.BlockSpec(dl.shape, lambda b: (0, 0)),
            pl.BlockSpec(sg.shape, lambda b: (0, 0)),
        ],
        out_specs=pl.BlockSpec((per_step * seq, w), lambda b: (b, 0)),
        out_shape=jax.ShapeDtypeStruct((t, w), F32),
        compiler_params=_params(("arbitrary",)),
        name="attn_ctx",
    )(q, k, v_all, dl, sg)


def _rope(x, cos, sin_lo, sin_hi):
    return (x * cos + pltpu.roll(x, 2 * DQK_A - 16, 1) * sin_lo + pltpu.roll(x, 16, 1) * sin_hi)


def _attn_lat_kernel(q_ref, k_ref, v_ref, ck_ref, cv_ref, qcos_ref, qslo_ref, qshi_ref,
                     kcos_ref, kslo_ref, kshi_ref, dl_ref, sg_ref, o_ref, kall, vall, *, lam_init):
    hw = 2 * DQK_A
    n_new = k_ref.shape[0]

    @pl.when(pl.program_id(1) == 0)
    def _():
        kcos, kslo, kshi = kcos_ref[...], kslo_ref[...], kshi_ref[...]
        for h in range(H_A):
            hs = slice(h * hw, (h + 1) * hw)
            kall[0:n_new, hs] = _rope(k_ref[:, hs], kcos, kslo, kshi).astype(BF16)
        kall[n_new:, :] = ck_ref[...].astype(BF16)
        ones = jnp.ones((vall.shape[0], DV_A), BF16)
        for h in range(H_A):
            vs = slice(h * DV_A, (h + 1) * DV_A)
            vall[0:n_new, 2 * h * DV_A:(2 * h + 1) * DV_A] = (
                v_ref[pl.ds(h, n_new, stride=H_A), :].astype(BF16))
            vall[n_new:, 2 * h * DV_A:(2 * h + 1) * DV_A] = cv_ref[:, vs].astype(BF16)
            vall[:, (2 * h + 1) * DV_A:(2 * h + 2) * DV_A] = ones

    lam = _diff_lambda(dl_ref, lam_init)
    qcos, qslo, qshi = qcos_ref[...], qslo_ref[...], qshi_ref[...]
    q = jnp.concatenate(
        [_rope(q_ref[:, h * hw:(h + 1) * hw], qcos, qslo, qshi) for h in range(H_A)], axis=-1)
    _diff_attention(q, lambda h: kall[:, h * hw:(h + 1) * hw],
                    lambda h: vall[:, 2 * h * DV_A:(2 * h + 2) * DV_A], lam, sg_ref[...],
                    lam_init, o_ref)


def _attention_lat(q, k, v, n_batch, cache_k, cache_v, layer, rope_tabs, dl, sg, lam_init):
    t = q.shape[0]
    seq = t // n_batch
    past = cache_k.shape[2]
    w = H_A * DV_A
    hw = 2 * DQK_A
    nq = seq // TILE
    cos, slo, shi = rope_tabs
    tab_q = pl.BlockSpec((TILE, hw), lambda b, j: (j, 0))
    tab_k = pl.BlockSpec((seq, hw), lambda b, j: (0, 0))
    return pl.pallas_call(
        functools.partial(_attn_lat_kernel, lam_init=lam_init),
        grid=(n_batch, nq),
        in_specs=[
            pl.BlockSpec((TILE, w), lambda b, j: (b * nq + j, 0)),
            pl.BlockSpec((seq, w), lambda b, j: (b, 0)),
            pl.BlockSpec((seq * H_A, DV_A), lambda b, j: (b, 0)),
            pl.BlockSpec((None, None, past, w), lambda b, j: (b, layer, 0, 0)),
            pl.BlockSpec((None, None, past, w), lambda b, j: (b, layer, 0, 0)),
            tab_q, tab_q, tab_q, tab_k, tab_k, tab_k,
            pl.BlockSpec(dl.shape, lambda b, j: (0, 0)),
            pl.BlockSpec(sg.shape, lambda b, j: (0, 0)),
        ],
        out_specs=pl.BlockSpec((TILE, w), lambda b, j: (b * nq + j, 0)),
        out_shape=jax.ShapeDtypeStruct((t, w), F32),
        scratch_shapes=[pltpu.VMEM((seq + past, w), BF16), pltpu.VMEM((seq + past, 2 * w), BF16)],
        compiler_params=_params(("arbitrary", "arbitrary")),
        name="attn_lat",
    )(q, k, v, cache_k, cache_v, cos, slo, shi, cos, slo, shi, dl, sg)


def _tile_aux():
    rows = lax.broadcasted_iota(jnp.int32, (TILE, 1), 0)
    lane = lax.broadcasted_iota(jnp.int32, (1, W_B), 1)
    head_masks = [(lane // DK_B) == h for h in range(H_B)]
    first_of_pair = lax.broadcasted_iota(jnp.int32, (1, 2 * DK_B), 1) < DK_B
    xor = (lax.broadcasted_iota(jnp.int32, (HALF, HALF), 0)
           ^ lax.broadcasted_iota(jnp.int32, (HALF, HALF), 1))
    levels = [1 << i for i in range(TILE.bit_length() - 1)]
    right = {m: (rows & m) != 0 for m in levels}
    same_block = {m: xor < 2 * m for m in levels if 2 * m < HALF}
    return rows, head_masks, first_of_pair, right, same_block


def _hgrn_lower_bound(lb_ref, layer, direction, depth):
    xs = [lb_ref[2 * j + direction:2 * j + direction + 1, :] for j in range(depth)]
    m = functools.reduce(jnp.maximum, xs)
    es = [jnp.exp(x - m) for x in xs]
    return sum(es[1:layer + 1]) / sum(es)


def _hgrn_gates(fpre, lbd):
    t = jnp.exp(-jnp.abs(fpre))
    log_sig = jnp.minimum(fpre, 0.0) - jnp.log(1.0 + t)
    r = 1.0 / (1.0 + t)
    sig_neg = jnp.where(fpre >= 0.0, t * r, r)
    if lbd is None:
        return log_sig * LOG2E, sig_neg
    a = jnp.log(lbd)
    b = jnp.log1p(-lbd) + log_sig
    logf = jnp.maximum(a, b) + jnp.log1p(jnp.exp(-jnp.abs(a - b)))
    return logf * LOG2E, (1.0 - lbd) * sig_neg


def _chain_row(token):
    return SUBLANES * (token % CHAIN_LEN) + token // CHAIN_LEN


def _tile_prefix(g, g_scr, b_scr):
    g_c = _to_chains(g, g_scr)
    run = [g_c[0]]
    for j in range(1, CHAIN_LEN):
        run.append(run[-1] + g_c[j])
    sub = lax.broadcasted_iota(jnp.int32, (SUBLANES, 1), 0)
    incl = run[-1]
    k = 1
    while k < SUBLANES:
        incl = incl + jnp.where(sub >= k, pltpu.roll(incl, k, 0), 0.0)
        k *= 2
    offset = incl - run[-1]
    prefix = _from_chains([r + offset for r in run], b_scr)
    return prefix, incl[SUBLANES - 1:SUBLANES, :]


def _level_exponent(z, g, b_scr, m, reverse, right):
    if m == 1:
        return jnp.where(right, g, 0.0) if not reverse else jnp.where(right, 0.0, g)
    slabs = b_scr.shape[0]

    def ref_rows(token):
        row = _chain_row(token)
        return jnp.concatenate(
            [jnp.broadcast_to(b_scr[s, row:row + 1, :], (SUBLANES, LANES)) for s in range(slabs)],
            axis=1)

    groups = []
    for gq in range(TILE // SUBLANES):
        first = gq * SUBLANES
        zg = z[first:first + SUBLANES]
        if 2 * m >= SUBLANES:
            block = first // (2 * m) * (2 * m)
            ref = ref_rows(block + m - 1)
            if m >= SUBLANES:
                groups.append(zg - ref if first - block >= m else ref - zg)
                continue
        else:
            sub = lax.broadcasted_iota(jnp.int32, (SUBLANES, 1), 0)
            ref = jnp.where(sub < 2 * m, ref_rows(first + m - 1), ref_rows(first + 3 * m - 1))
        groups.append(-jnp.abs(zg - ref))
    return jnp.concatenate(groups, axis=0)


def _hgrn_tile(qh, kk, vv, g, reverse, aux, ones_bd, state_t, scratch):
    rows, head_masks, first_of_pair, right_rows, same_block = aux
    pair_w = 2 * DK_B
    g_scr, b_scr = scratch
    prefix, total = _tile_prefix(g, g_scr, b_scr)
    z = prefix if not reverse else prefix - g
    diag = [[None, None] for _ in range(H_B)]
    off = [None] * H_B
    m = 1
    while m < TILE:
        right = right_rows[m]
        x = jnp.exp2(_level_exponent(z, g, b_scr, m, reverse, right))
        q_rows = right if not reverse else jnp.logical_not(right)
        qz = jnp.where(q_rows, (qh * x).astype(BF16), 0.0)
        kz = jnp.where(q_rows, 0.0, (kk * x).astype(BF16))
        for h in range(H_B):
            ls = slice((h // 2) * pair_w, (h // 2 + 1) * pair_w)
            qm = jnp.where(first_of_pair, qz[:, ls], 0.0) if h % 2 == 0 else \
                jnp.where(first_of_pair, 0.0, qz[:, ls])
            kh = kz[:, ls]
            if m < HALF:
                for i in range(2):
                    rs = slice(i * HALF, (i + 1) * HALF)
                    pm = _dot_nt(qm[rs], kh[rs])
                    if m in same_block:
                        pm = jnp.where(same_block[m], pm, 0.0)
                    diag[h][i] = pm if diag[h][i] is None else diag[h][i] + pm
            else:
                lo, hi = slice(0, HALF), slice(HALF, TILE)
                off[h] = _dot_nt(qm[hi], kh[lo]) if not reverse else _dot_nt(qm[lo], kh[hi])
        m *= 2
    vb = vv.astype(BF16)
    top = bot = None
    for h in range(H_B):
        vh = jnp.where(head_masks[h], vb, 0.0)
        d0, d1, of = (a.astype(BF16) for a in (diag[h][0], diag[h][1], off[h]))
        if not reverse:
            t = _dot(d0, vh[0:HALF])
            b = _dot(jnp.concatenate([of, d1], axis=1), vh)
        else:
            t = _dot(jnp.concatenate([d0, of], axis=1), vh)
            b = _dot(d1, vh[HALF:])
        top = t if top is None else top + t
        bot = b if bot is None else bot + b
    out = _group_sum(qh * kk, ones_bd) * vv + jnp.concatenate([top, bot], axis=0)
    q_decay, k_decay = (z, total - z) if not reverse else (total - z, z)
    if state_t is not None:
        out = out + _dot_nt((qh * jnp.exp2(q_decay)).astype(BF16), state_t.astype(BF16))
    k_hat = (kk * jnp.exp2(k_decay)).astype(BF16)
    return out, k_hat, total


def _block_diag_mask():
    r = lax.broadcasted_iota(jnp.int32, (W_B, W_B), 0) // DK_B
    c = lax.broadcasted_iota(jnp.int32, (W_B, W_B), 1) // DV_B
    return r == c


def _conv(x, prev_row, next_rows, cw_ref, cb_ref, rows):
    x_m1 = jnp.where(rows == 0, prev_row, pltpu.roll(x, 1, 0))
    x_p1 = jnp.where(rows == TILE - 1, next_rows[0:1], pltpu.roll(x, TILE - 1, 0))
    x_p2 = jnp.where(rows == TILE - 2, next_rows[0:1],
                     jnp.where(rows == TILE - 1, next_rows[1:2], pltpu.roll(x, TILE - 2, 0)))
    y = cb_ref[...] + x_m1 * cw_ref[0:1, :]
    y = y + x * cw_ref[1:2, :]
    y = y + x_p1 * cw_ref[2:3, :]
    return y + x_p2 * cw_ref[3:4, :]


def _chain_address(g):
    per_chain = CHAIN_LEN // SUBLANES
    return SUBLANES * SUBLANES * (g % per_chain) + g // per_chain, SUBLANES


def _to_chains(x, scr):
    slabs = scr.shape[0]
    for s in range(slabs):
        for g in range(TILE // SUBLANES):
            start, stride = _chain_address(g)
            scr[s, pl.ds(start, SUBLANES, stride=stride), :] = (
                x[g * SUBLANES:(g + 1) * SUBLANES, s * LANES:(s + 1) * LANES])
    return [jnp.concatenate([scr[s, j * SUBLANES:(j + 1) * SUBLANES, :] for s in range(slabs)],
                            axis=1) for j in range(CHAIN_LEN)]


def _from_chains(chains, scr):
    slabs = scr.shape[0]
    for s in range(slabs):
        for j, c in enumerate(chains):
            scr[s, j * SUBLANES:(j + 1) * SUBLANES, :] = c[:, s * LANES:(s + 1) * LANES]
    groups = []
    for g in range(TILE // SUBLANES):
        start, stride = _chain_address(g)
        groups.append(jnp.concatenate(
            [scr[s, pl.ds(start, SUBLANES, stride=stride), :] for s in range(slabs)], axis=1))
    return jnp.concatenate(groups, axis=0)


def _rglru_tile(xc, r_pre, i_pre, lam_row, reverse, h_in, scratch):
    a_scr, u_scr, h_scr = scratch
    r = _sigmoid(r_pre)
    i = _sigmoid(i_pre)
    log_a = -RG_C * r * _softplus(-lam_row)
    a = jnp.exp(log_a)
    u = jnp.sqrt(jnp.tanh(-log_a) * (a * a + 1.0)) * (i * xc)
    a_c = _to_chains(a, a_scr)
    u_c = _to_chains(u, u_scr)
    order = list(reversed(range(CHAIN_LEN))) if reverse else list(range(CHAIN_LEN))
    h_c = [None] * CHAIN_LEN
    p_c = [None] * CHAIN_LEN
    h_c[order[0]], p_c[order[0]] = u_c[order[0]], a_c[order[0]]
    for prev, j in zip(order[:-1], order[1:]):
        h_c[j] = a_c[j] * h_c[prev] + u_c[j]
        p_c[j] = a_c[j] * p_c[prev]
    p_tot, h_tot = p_c[order[-1]], h_c[order[-1]]
    sub = lax.broadcasted_iota(jnp.int32, (SUBLANES, 1), 0)
    k = 1
    while k < SUBLANES:
        if not reverse:
            valid = sub >= k
            p_s, h_s = pltpu.roll(p_tot, k, 0), pltpu.roll(h_tot, k, 0)
        else:
            valid = sub < SUBLANES - k
            p_s, h_s = pltpu.roll(p_tot, SUBLANES - k, 0), pltpu.roll(h_tot, SUBLANES - k, 0)
        h_tot = h_tot + p_tot * jnp.where(valid, h_s, 0.0)
        p_tot = p_tot * jnp.where(valid, p_s, 1.0)
        k *= 2
    if h_in is not None:
        h_tot = h_tot + p_tot * h_in
    if not reverse:
        carry = jnp.where(sub == 0, 0.0 if h_in is None else h_in, pltpu.roll(h_tot, 1, 0))
        h_out = h_tot[SUBLANES - 1:SUBLANES, :]
    else:
        carry = jnp.where(sub == SUBLANES - 1, 0.0 if h_in is None else h_in,
                          pltpu.roll(h_tot, SUBLANES - 1, 0))
        h_out = h_tot[0:1, :]
    h_c = [h + p * carry for h, p in zip(h_c, p_c)]
    return _from_chains(h_c, h_scr), h_out


def _rec_inputs(refs):
    qb, ff, fb, ib, gb, xcol, gcol = [r[...] for r in refs]
    return _silu(qb), ff, fb, ib, gb, xcol, gcol


def _hgrn_out(o_sum, gb, onorm_gain, ones_bd):
    ms = _group_sum(o_sum * o_sum, ones_bd) * (1.0 / DV_B)
    return o_sum * lax.rsqrt(ms + EPS) * onorm_gain * _silu(gb)


def _rec_ctx_kernel(*refs, layer, depth):
    (q_ref, ff_ref, fb_ref, i_ref, g_ref, x_ref, gc_ref, lb_ref, on_ref, cw_ref, cb_ref, wg_ref,
     bg_ref, lam_ref, ones_ref, ob_ref, oc_ref, st_ref, hs_ref,
     *scan_scratch) = refs[2 if layer else 0:]
    if layer == 0 and depth > 1:
        st_ref[0, 1:] = jnp.zeros((depth - 1,) + st_ref.shape[2:], F32)
        hs_ref[0, 1:] = jnp.zeros((depth - 1,) + hs_ref.shape[2:], F32)
    aux = _tile_aux()
    rows = aux[0]
    ones_bd = ones_ref[...]
    qh, ff, fb, ib, gb, xcol, gcol = _rec_inputs((q_ref, ff_ref, fb_ref, i_ref, g_ref, x_ref, gc_ref))
    o_sum = None
    for direction, fpre in enumerate((ff, fb)):
        lbd = None if layer == 0 else _hgrn_lower_bound(lb_ref, layer, direction, depth)
        g, kk = _hgrn_gates(fpre, lbd)
        o, k_hat, _ = _hgrn_tile(qh, kk, ib, g, direction == 1, aux, ones_bd, None,
                                 scan_scratch[3 + 2 * direction:5 + 2 * direction])
        o_sum = o if o_sum is None else o_sum + o
        full = _dot_tn(k_hat, ib.astype(BF16))
        for h in range(H_B):
            blk = full[h * DK_B:(h + 1) * DK_B, (h // 2) * 2 * DV_B:(h // 2 + 1) * 2 * DV_B]
            if h % 2:
                blk = pltpu.roll(blk, DV_B, 1)
            st_ref[0, 0, direction, h] = blk[:, 0:DV_B]
    ob_ref[...] = _hgrn_out(o_sum, gb, on_ref[...], ones_bd)

    zero_row = jnp.zeros((1, W_C), F32)
    xc = _conv(xcol, zero_row, jnp.zeros((2, W_C), F32), cw_ref, cb_ref, rows)
    gates = _dot(xc.astype(BF16), wg_ref[...]) + bg_ref[...]
    y_f, h_f = _rglru_tile(xc, gates[:, 0:W_C], gates[:, W_C:2 * W_C], lam_ref[0:1, :], False,
                           None, scan_scratch[0:3])
    y_b, h_b = _rglru_tile(xc, gates[:, 2 * W_C:3 * W_C], gates[:, 3 * W_C:], lam_ref[1:2, :], True,
                           None, scan_scratch[0:3])
    oc_ref[...] = (y_f + y_b) * _gelu_tanh(gcol)
    hs_ref[0, 0, 0:1, :] = h_f
    hs_ref[0, 0, 1:2, :] = h_b


def _scan_scratch(hgrn_directions):
    return [pltpu.VMEM((W_C // LANES, TILE, LANES), F32) for _ in range(3 + 2 * hgrn_directions)]


def _col_spec(col, row_map):
    return pl.BlockSpec((TILE, W_B), lambda *idx: (row_map(*idx), col))


COL_QB, COL_FF, COL_FB, COL_IB, COL_GB, COL_XC, COL_GC = range(7)
_REC_COLS = (COL_QB, COL_FF, COL_FB, COL_IB, COL_GB, COL_XC, COL_GC)


def _full_spec(a):
    nd = a.ndim
    return pl.BlockSpec(a.shape, lambda *idx: (0,) * nd)


def _recurrence_ctx(proj, layer, depth, consts, states):
    t = proj.shape[0]
    n = t // TILE
    assert bool(states) == (layer > 0)
    layers, at = (1, layer) if states else (depth, 0)
    return pl.pallas_call(
        functools.partial(_rec_ctx_kernel, layer=layer, depth=depth),
        grid=(n,),
        in_specs=([pl.BlockSpec(memory_space=pl.ANY) for _ in states]
                  + [_col_spec(c, lambda b: b) for c in _REC_COLS] + [_full_spec(a) for a in consts]),
        out_specs=[
            pl.BlockSpec((TILE, W_B), lambda b: (b, 0)),
            pl.BlockSpec((TILE, W_C), lambda b: (b, 0)),
            pl.BlockSpec((1, layers, 2, H_B, DK_B, DV_B), lambda b: (b, at, 0, 0, 0, 0)),
            pl.BlockSpec((1, layers, 2, W_C), lambda b: (b, at, 0, 0)),
        ],
        out_shape=[
            jax.ShapeDtypeStruct((t, W_B), F32),
            jax.ShapeDtypeStruct((t, W_C), F32),
            jax.ShapeDtypeStruct((n, depth, 2, H_B, DK_B, DV_B), F32),
            jax.ShapeDtypeStruct((n, depth, 2, W_C), F32),
        ],
        input_output_aliases={i: 2 + i for i in range(len(states))},
        scratch_shapes=_scan_scratch(2),
        compiler_params=_params(("arbitrary",)),
        name="rec_ctx",
    )(*states, *([proj] * len(_REC_COLS)), *consts)


def _rec_lat_kernel(q_ref, f_ref, i_ref, g_ref, x_ref, xp_ref, xn_ref, gc_ref, s0_ref, h0_ref,
                    *rest, layer, depth, direction, n_tiles):
    if direction == 0:
        (lb_ref, on_ref, cw_ref, cb_ref, wg_ref, bg_ref, lam_ref, ones_ref,
         o_ref, y_ref, state, hstate, *scan_scratch) = rest
    else:
        (of_ref, yf_ref, lb_ref, on_ref, cw_ref, cb_ref, wg_ref, bg_ref, lam_ref, ones_ref,
         o_ref, y_ref, state, hstate, *scan_scratch) = rest
    step = pl.program_id(1)
    tile = step if direction == 0 else n_tiles - 1 - step
    reverse = direction == 1

    @pl.when(step == 0)
    def _():
        state[...] = s0_ref[0, 0]
        hstate[...] = h0_ref[0, 0]

    aux = _tile_aux()
    rows = aux[0]
    ones_bd = ones_ref[...]
    qh = _silu(q_ref[...])
    ib = i_ref[...]
    lbd = None if layer == 0 else _hgrn_lower_bound(lb_ref, layer, direction, depth)
    g, kk = _hgrn_gates(f_ref[...], lbd)
    o, k_hat, total = _hgrn_tile(qh, kk, ib, g, reverse, aux, ones_bd, state[...],
                                 scan_scratch[3:5])
    upd = jnp.where(_block_diag_mask(), _dot_tn(ib.astype(BF16), k_hat), 0.0)
    state[...] = state[...] * jnp.exp2(total) + upd

    prev_row = jnp.where(tile == 0, 0.0, xp_ref[TILE - 1:TILE, :])
    next_rows = jnp.where(tile == n_tiles - 1, 0.0, xn_ref[0:2, :])
    xc = _conv(x_ref[...], prev_row, next_rows, cw_ref, cb_ref, rows)
    gates = _dot(xc.astype(BF16), wg_ref[...]) + bg_ref[...]
    c0 = 2 * W_C * direction
    y, h_next = _rglru_tile(xc, gates[:, c0:c0 + W_C], gates[:, c0 + W_C:c0 + 2 * W_C],
                            lam_ref[direction:direction + 1, :], reverse, hstate[...],
                            scan_scratch[0:3])
    hstate[...] = h_next

    if direction == 0:
        o_ref[...] = o
        y_ref[...] = y
    else:
        o_ref[...] = _hgrn_out(of_ref[...] + o, g_ref[...], on_ref[...], ones_bd)
        y_ref[...] = (yf_ref[...] + y) * _gelu_tanh(gc_ref[...])


def _recurrence_lat(proj, n_batch, layer, depth, direction, state0, h0, prev, consts):
    t = proj.shape[0]
    n_tiles = t // (TILE * n_batch)

    def tile_of(b, s):
        return s if direction == 0 else n_tiles - 1 - s

    def row(b, s):
        return b * n_tiles + tile_of(b, s)

    def row_prev(b, s):
        return b * n_tiles + jnp.maximum(tile_of(b, s) - 1, 0)

    def row_next(b, s):
        return b * n_tiles + jnp.minimum(tile_of(b, s) + 1, n_tiles - 1)

    f_col = COL_FF if direction == 0 else COL_FB
    in_specs = [_col_spec(COL_QB, row), _col_spec(f_col, row), _col_spec(COL_IB, row),
                _col_spec(COL_GB, row), _col_spec(COL_XC, row), _col_spec(COL_XC, row_prev),
                _col_spec(COL_XC, row_next), _col_spec(COL_GC, row),
                pl.BlockSpec((1, 1, W_B, W_B), lambda b, s: (b, direction, 0, 0)),
                pl.BlockSpec((1, 1, 1, W_C), lambda b, s: (b, direction, 0, 0))]
    args = [proj] * 8 + [state0, h0]
    if direction == 1:
        in_specs += [pl.BlockSpec((TILE, W_B), lambda b, s: (row(b, s), 0)),
                     pl.BlockSpec((TILE, W_C), lambda b, s: (row(b, s), 0))]
        args += list(prev)
    in_specs += [_full_spec(a) for a in consts]
    args += list(consts)
    return pl.pallas_call(
        functools.partial(_rec_lat_kernel, layer=layer, depth=depth, direction=direction,
                          n_tiles=n_tiles),
        grid=(n_batch, n_tiles),
        in_specs=in_specs,
        out_specs=[pl.BlockSpec((TILE, W_B), lambda b, s: (row(b, s), 0)),
                   pl.BlockSpec((TILE, W_C), lambda b, s: (row(b, s), 0))],
        out_shape=[jax.ShapeDtypeStruct((t, W_B), F32), jax.ShapeDtypeStruct((t, W_C), F32)],
        scratch_shapes=[pltpu.VMEM((W_B, W_B), F32), pltpu.VMEM((1, W_C), F32)] + _scan_scratch(1),
        compiler_params=_params(("arbitrary", "arbitrary")),
        name="rec_lat_fwd" if direction == 0 else "rec_lat_bwd",
    )(*args)


def _out_kernel(x_ref, oa_ref, ob_ref, oc_ref, mod_ref, n2_ref, wo_ref, w1_ref, w2_ref, *refs):
    n_cast = len(refs) // 2
    y_ref = refs[n_cast]
    for src, dst in zip(refs[:n_cast], refs[n_cast + 1:]):
        dst[0] = src[...].astype(BF16)
    x = x_ref[...]
    d = x.shape[-1]
    g1 = mod_ref[0, :, 2 * d:3 * d]
    sh2 = mod_ref[0, :, 3 * d:4 * d]
    sc2 = mod_ref[0, :, 4 * d:5 * d]
    g2 = mod_ref[0, :, 5 * d:6 * d]
    wa = H_A * DV_A
    mix = _dot(oa_ref[...].astype(BF16), wo_ref[0:wa, :])
    mix = mix + _dot(ob_ref[...].astype(BF16), wo_ref[wa:wa + W_B, :])
    mix = mix + _dot(oc_ref[...].astype(BF16), wo_ref[wa + W_B:, :])
    x1 = x + g1 * mix
    h = (_rms_rows(x1, n2_ref[...]) * (1.0 + sc2) + sh2).astype(BF16)
    acc = jnp.zeros_like(x)
    for c in range(w1_ref.shape[1] // D_FF_CHUNK):
        cs = slice(c * D_FF_CHUNK, (c + 1) * D_FF_CHUNK)
        f = jnp.maximum(_dot(h, w1_ref[:, cs]), 0.0)
        acc = acc + _dot((f * f).astype(BF16), w2_ref[cs, :])
    y_ref[...] = x1 + g2 * acc


def _out_mlp(x, oa, ob, oc, mod, tokens_per_mod, n2, w_out, w_ff1, w_ff2, cast=(), cast_layer=0):
    t, d = x.shape
    steps = t // ROW_TILE

    def resident(w):
        return pl.BlockSpec((None,) + w.shape[1:], lambda i: (0, 0, 0),
                            pipeline_mode=pl.Buffered(1))

    assert all(w.shape[1] % (steps * 2 * SUBLANES) == 0 for w in cast)
    slabs = [(w.shape[1] // steps, w.shape[2]) for w in cast]
    return pl.pallas_call(
        _out_kernel,
        grid=(steps,),
        in_specs=[
            pl.BlockSpec((ROW_TILE, d), lambda i: (i, 0)),
            pl.BlockSpec((ROW_TILE, oa.shape[1]), lambda i: (i, 0)),
            pl.BlockSpec((ROW_TILE, ob.shape[1]), lambda i: (i, 0)),
            pl.BlockSpec((ROW_TILE, oc.shape[1]), lambda i: (i, 0)),
            pl.BlockSpec((1, 1, N_MOD * d), lambda i: (i * ROW_TILE // tokens_per_mod, 0, 0)),
            pl.BlockSpec((1, d), lambda i: (0, 0)),
            resident(w_out),
            resident(w_ff1),
            resident(w_ff2),
        ] + [pl.BlockSpec((None,) + s, lambda i: (cast_layer, i, 0)) for s in slabs],
        out_specs=[pl.BlockSpec((ROW_TILE, d), lambda i: (i, 0))]
        + [pl.BlockSpec((1,) + s, lambda i: (0, i, 0)) for s in slabs],
        out_shape=[jax.ShapeDtypeStruct((t, d), F32)]
        + [jax.ShapeDtypeStruct((1,) + w.shape[1:], BF16) for w in cast],
        compiler_params=_params(("arbitrary",)),
        name="out_mlp",
    )(x, oa, ob, oc, mod, n2, w_out, w_ff1, w_ff2, *cast)


def _ones_block_diag(n, group):
    idx = np.arange(n) // group
    return jnp.asarray((idx[:, None] == idx[None, :]).astype(np.float32), dtype=BF16)


def _rope_tables(seq):
    nf = DQK_A // 4
    inv = ROPE_BASE ** (-jnp.arange(nf, dtype=F32) / nf)
    pos = jnp.arange(seq)
    ang_r = (pos // GRID_W).astype(F32)[:, None] * inv
    ang_c = (pos % GRID_W).astype(F32)[:, None] * inv
    z = jnp.zeros_like(ang_r)
    cos = jnp.concatenate([jnp.cos(ang_r)] * 2 + [jnp.cos(ang_c)] * 2, axis=-1)
    sin_lo = jnp.concatenate([-jnp.sin(ang_r), z, -jnp.sin(ang_c), z], axis=-1)
    sin_hi = jnp.concatenate([z, jnp.sin(ang_r), z, jnp.sin(ang_c)], axis=-1)
    return tuple(jnp.tile(a, (1, 2)) for a in (cos, sin_lo, sin_hi))


def _gate_weights(rg_w_l):
    eye = jnp.eye(H_C, dtype=rg_w_l.dtype)
    w = jnp.einsum('dghij,hk->dghikj', rg_w_l, eye)
    w = w.reshape(2, 2, W_C, W_C)
    return jnp.transpose(w, (2, 0, 1, 3)).reshape(W_C, 4 * W_C).astype(BF16)


def _state_block_diag_t(s):
    eye = jnp.eye(H_B, dtype=s.dtype)
    w = jnp.einsum('...hdv,hk->...hvkd', s, eye)
    return w.reshape(s.shape[:-3] + (W_B, W_B))


def kernel(x_prompt, x_sample, cache_k, cache_v, state_hgrn, state_rglru, c, c_ctx, w_mod, b_mod,
           norm1, norm2, w_in, w_out, qk_norm, diff_lambda, subln, hgrn_lb, hgrn_onorm, conv_w,
           conv_b, rg_w, rg_b, rg_lambda, w_ff1, w_ff2):
    batch, seq, d = x_prompt.shape
    dec_batch, dec_seq, _ = x_sample.shape
    depth = w_in.shape[0]
    past = cache_k.shape[2]
    wq = H_A * 2 * DQK_A

    c8 = jnp.concatenate([c_ctx[None], c, jnp.zeros((8 - 1 - dec_batch, d), F32)], axis=0)
    mod_all = _modulation(c8, w_mod, b_mod)

    assert DQK_A == DV_B and seq == TILE
    ones_b = _ones_block_diag(W_B, DV_B)
    rope_tabs = _rope_tables(dec_seq)
    ck = cache_k.reshape(dec_batch, depth, past, wq)
    cv = cache_v.reshape(dec_batch, depth, past, H_A * DV_A)
    lb2 = hgrn_lb.reshape(depth * 2, W_B)

    w_f32 = (w_in, w_out, w_ff1, w_ff2)
    w_next = [w[0:1].astype(BF16) for w in w_f32]

    yp = x_prompt.reshape(batch * seq, d)
    ys = x_sample.reshape(dec_batch * dec_seq, d)
    caches, states = (), ()
    for l in range(depth):
        lam_init = 0.8 - 0.6 * math.exp(-0.3 * l)
        mod_ctx = mod_all[l, 0:1][:, None, :]
        mod_lat = mod_all[l, 1:1 + dec_batch][:, None, :]
        n1 = norm1[l][None]
        n2 = norm2[l][None]
        gq = jnp.tile(qk_norm[l, 0], H_A * 2)[None] * (DQK_A ** -0.5 * LOG2E)
        gk = jnp.tile(qk_norm[l, 1], H_A * 2)[None]
        dl = diff_lambda[l]
        sg = subln[l][None]
        consts = (lb2, jnp.tile(hgrn_onorm[l], H_B)[None], conv_w[l], conv_b[l][None],
                  _gate_weights(rg_w[l]), rg_b[l].reshape(1, 4 * W_C), rg_lambda[l], ones_b)

        w_in_b, w_out_b, w1_b, w2_b = w_next

        q, k, v_all, rest, kt_all = _in_projection(yp, mod_ctx, batch * seq, n1, w_in_b, 0,
                                                   gq, gk, caches=caches)
        caches = (kt_all, v_all)
        oa = _attention_ctx(q, k, v_all, l, seq, dl, sg, lam_init)
        ob, oc, *states = _recurrence_ctx(rest, l, depth, consts, states)
        yp, *w_next = _out_mlp(yp, oa, ob, oc, mod_ctx, batch * seq, n2, w_out_b, w1_b, w2_b,
                               cast=w_f32 if l + 1 < depth else (), cast_layer=l + 1)

        q, k, v, rest = _in_projection(ys, mod_lat, dec_seq, n1, w_in_b, 0, gq, gk)
        oa = _attention_lat(q, k, v, dec_batch, ck, cv, l, rope_tabs, dl, sg, lam_init)
        s0 = _state_block_diag_t(state_hgrn[:, l])
        h0 = state_rglru[:, l][:, :, None, :]
        fwd = _recurrence_lat(rest, dec_batch, l, depth, 0, s0, h0, None, consts)
        ob, oc = _recurrence_lat(rest, dec_batch, l, depth, 1, s0, h0, fwd, consts)
        ys, = _out_mlp(ys, oa, ob, oc, mod_lat, dec_seq, n2, w_out_b, w1_b, w2_b)

    kt_all, v_all = caches
    new_k = jnp.transpose(kt_all.reshape(batch, depth, H_A, 2, DQK_A, seq), (0, 1, 5, 2, 3, 4))
    new_v = v_all.reshape(batch, depth, seq, H_A, DV_A)
    return (yp.reshape(batch, seq, d), ys.reshape(dec_batch, dec_seq, d), new_k, new_v,
            states[0], states[1])
```

```python
import functools
import math

import numpy as np
import jax
import jax.numpy as jnp
from jax import lax
from jax.experimental import pallas as pl
from jax.experimental.pallas import tpu as pltpu

F32 = jnp.float32
BF16 = jnp.bfloat16

GRID_W = 64
H_A, DQK_A, DV_A = 4, 64, 128
H_B, DK_B, DV_B = 4, 64, 64
W_B = H_B * DV_B
H_C, W_C = 4, 256
CONV_K = 4
RG_C = 8.0
ROPE_BASE = 10000.0
EPS = 1e-6
N_MOD = 6
D_FF_CHUNK = 1024

TILE = 256
HALF = TILE // 2
ROW_TILE = 512
ATTN_SEQS_PER_STEP = 4
SUBLANES = 8
LANES = 128
CHAIN_LEN = TILE // SUBLANES
LOG2E = math.log2(math.e)
V7X_VMEM_BYTES = 64 * 1024 * 1024
VMEM_LIMIT = V7X_VMEM_BYTES - 8 * 1024 * 1024


def _dot(a, b):
    return jnp.dot(a, b, preferred_element_type=F32)


def _dot_nt(a, b):
    return lax.dot_general(a, b, (((1,), (1,)), ((), ())), preferred_element_type=F32)


def _dot_tn(a, b):
    return lax.dot_general(a, b, (((0,), (0,)), ((), ())), preferred_element_type=F32)


def _group_sum(x, ones_bd):
    hi = x.astype(BF16)
    lo = (x - hi.astype(F32)).astype(BF16)
    return _dot(hi, ones_bd) + _dot(lo, ones_bd)


def _rms_rows(x, gain):
    return x * lax.rsqrt(jnp.mean(x * x, axis=-1, keepdims=True) + EPS) * gain


def _sigmoid(x):
    return 0.5 * jnp.tanh(0.5 * x) + 0.5


def _silu(x):
    return x * _sigmoid(x)


def _softplus(x):
    return jnp.maximum(x, 0.0) + jnp.log1p(jnp.exp(-jnp.abs(x)))


def _log_sigmoid(x):
    return jnp.minimum(x, 0.0) - jnp.log1p(jnp.exp(-jnp.abs(x)))


def _gelu_tanh(x):
    return x * (0.5 * (1.0 + jnp.tanh(math.sqrt(2.0 / math.pi) * (x + 0.044715 * (x * x * x)))))


def _params(semantics, flags=None):
    return pltpu.CompilerParams(dimension_semantics=semantics, vmem_limit_bytes=VMEM_LIMIT,
                                flags=flags)


def _cast_specs(cast, layer, steps):
    assert all(w.shape[1] % (steps * 2 * SUBLANES) == 0 for w in cast)
    slabs = [(w.shape[1] // steps, w.shape[2]) for w in cast]
    return ([pl.BlockSpec((None,) + s, lambda i: (layer, i, 0)) for s in slabs],
            [pl.BlockSpec((1,) + s, lambda i: (0, i, 0)) for s in slabs],
            [jax.ShapeDtypeStruct((1,) + w.shape[1:], BF16) for w in cast])


def _cast_slabs(srcs, dsts):
    for src, dst in zip(srcs, dsts):
        dst[0] = src[...].astype(BF16)


def _mod_kernel(c_ref, w_ref, b_ref, o_ref):
    c = c_ref[...]
    o_ref[0] = _dot(_silu(c).astype(BF16), w_ref[0].astype(BF16)) + b_ref[0]


def _modulation(c8, w_mod, b_mod):
    depth, d, n = w_mod.shape
    tn = d
    return pl.pallas_call(
        _mod_kernel,
        grid=(depth, n // tn),
        in_specs=[
            pl.BlockSpec((8, d), lambda l, j: (0, 0)),
            pl.BlockSpec((1, d, tn), lambda l, j: (l, 0, j)),
            pl.BlockSpec((1, 1, tn), lambda l, j: (l, 0, j)),
        ],
        out_specs=pl.BlockSpec((1, 8, tn), lambda l, j: (l, 0, j)),
        out_shape=jax.ShapeDtypeStruct((depth, 8, n), F32),
        compiler_params=_params(("arbitrary", "arbitrary")),
        name="mod",
    )(c8, w_mod, b_mod.reshape(depth, 1, n))


def _proj_kernel(x_ref, mod_ref, n1_ref, w_ref, gq_ref, gk_ref, *refs, layers_before):
    prev = refs[:2] if layers_before else ()
    q_ref, k_ref, v_ref, r_ref, *kt_ref = refs[len(prev):]
    x = x_ref[...]
    d = x.shape[-1]
    sh = mod_ref[0, :, 0:d]
    sc = mod_ref[0, :, d:2 * d]
    h = _rms_rows(x, n1_ref[...]) * (1.0 + sc) + sh
    p = _dot(h.astype(BF16), w_ref[...])
    wq = H_A * 2 * DQK_A

    first = lax.broadcasted_iota(jnp.int32, (1, LANES), 1) < DQK_A

    def qk_norm(z, gain):
        z2 = z * z
        parts = []
        for c in range(0, wq, LANES):
            slab = z2[:, c:c + LANES]
            lo = jnp.sum(jnp.where(first, slab, 0.0), axis=-1, keepdims=True)
            hi = jnp.sum(jnp.where(first, 0.0, slab), axis=-1, keepdims=True)
            parts.append(jnp.where(first, lo, hi))
        ms = jnp.concatenate(parts, axis=1) * (1.0 / DQK_A)
        return z * lax.rsqrt(ms + EPS) * gain

    wv = H_A * DV_A
    q_ref[...] = qk_norm(p[:, 0:wq], gq_ref[...])
    kn = qk_norm(p[:, wq:2 * wq], gk_ref[...])
    k_ref[...] = kn
    if kt_ref:
        for s in range(ROW_TILE // TILE):
            kt_ref[0][s, layers_before] = kn[s * TILE:(s + 1) * TILE].T
        if prev:
            ktp_ref, vp_ref = prev
            kt_ref[0][:, 0:layers_before] = ktp_ref[...]
            v_ref[:, 0:layers_before] = vp_ref[...]
    for s in range(ROW_TILE // TILE):
        v_dst = v_ref.at[s, layers_before] if kt_ref else v_ref.at[pl.ds(s * TILE * H_A, TILE * H_A)]
        for h in range(H_A):
            v_dst[pl.ds(h, TILE, stride=H_A), :] = (
                p[s * TILE:(s + 1) * TILE, 2 * wq + h * DV_A:2 * wq + (h + 1) * DV_A])
    r_ref[...] = p[:, 2 * wq + wv:]


def _in_projection(x, mod, tokens_per_mod, n1, w_in, layer, gq, gk, caches=None):
    t, d = x.shape
    layers_before = caches[0].shape[1] if caches else 0
    n_in = w_in.shape[2]
    wq = gq.shape[1]
    wv = H_A * DV_A
    widths = (wq, wq, wv, n_in - 2 * wq - wv)
    seqs = ROW_TILE // TILE
    out_specs = [pl.BlockSpec((ROW_TILE, w), lambda i: (i, 0)) for w in widths]
    out_shape = [jax.ShapeDtypeStruct((t, w), F32) for w in widths]
    out_specs[2] = pl.BlockSpec((ROW_TILE * H_A, DV_A), lambda i: (i, 0))
    out_shape[2] = jax.ShapeDtypeStruct((t * H_A, DV_A), F32)
    cache_specs = []
    if caches is not None:
        n_seq, nl = t // TILE, layers_before + 1
        out_specs[2] = pl.BlockSpec((seqs, nl, TILE * H_A, DV_A), lambda i: (i, 0, 0, 0))
        out_shape[2] = jax.ShapeDtypeStruct((n_seq, nl, TILE * H_A, DV_A), F32)
        out_specs.append(pl.BlockSpec((seqs, nl, wq, TILE), lambda i: (i, 0, 0, 0)))
        out_shape.append(jax.ShapeDtypeStruct((n_seq, nl, wq, TILE), F32))
        if caches:
            cache_specs = [
                pl.BlockSpec((seqs, layers_before, wq, TILE), lambda i: (i, 0, 0, 0)),
                pl.BlockSpec((seqs, layers_before, TILE * H_A, DV_A), lambda i: (i, 0, 0, 0))]
    return pl.pallas_call(
        functools.partial(_proj_kernel, layers_before=layers_before),
        grid=(t // ROW_TILE,),
        in_specs=[
            pl.BlockSpec((ROW_TILE, d), lambda i: (i, 0)),
            pl.BlockSpec((1, 1, N_MOD * d), lambda i: (i * ROW_TILE // tokens_per_mod, 0, 0)),
            pl.BlockSpec((1, d), lambda i: (0, 0)),
            pl.BlockSpec((None, d, n_in), lambda i: (layer, 0, 0)),
            pl.BlockSpec((1, wq), lambda i: (0, 0)),
            pl.BlockSpec((1, wq), lambda i: (0, 0)),
        ] + cache_specs,
        out_specs=out_specs,
        out_shape=out_shape,
        compiler_params=_params(("arbitrary",)),
        name="proj",
    )(x, mod, n1, w_in, gq, gk, *(caches or ()))


def _diff_lambda(dl_ref, lam_init):
    lv = dl_ref[...]
    a = jnp.sum(lv[0:1] * lv[1:2], axis=-1, keepdims=True)
    b = jnp.sum(lv[2:3] * lv[3:4], axis=-1, keepdims=True)
    return jnp.exp(a) - jnp.exp(b) + lam_init


def _diff_attention(q, k_of, vext_of, lam, sub_gain, lam_init, o_ref):
    hw = 2 * DQK_A
    first = lax.broadcasted_iota(jnp.int32, (1, hw), 1) < DQK_A
    for h in range(H_A):
        hs = slice(h * hw, (h + 1) * hw)
        qh = q[:, hs]
        kh = k_of(h)
        vext = vext_of(h)
        maps = []
        for qm in (jnp.where(first, qh, 0.0), jnp.where(first, 0.0, qh)):
            s = _dot_nt(qm.astype(BF16), kh)
            e = jnp.exp2(s - jnp.max(s, axis=-1, keepdims=True)).astype(BF16)
            oe = _dot(e, vext)
            maps.append(oe[:, 0:DV_A] / oe[:, DV_A:])
        oh = maps[0] - lam * maps[1]
        o_ref[:, hs] = _rms_rows(oh, sub_gain) * (1.0 - lam_init)


def _attn_ctx_kernel(q_ref, k_ref, v_ref, dl_ref, sg_ref, o_ref, *, lam_init, seq):
    lam = _diff_lambda(dl_ref, lam_init)
    ones = jnp.ones((seq, DV_A), BF16)
    hw = 2 * DQK_A
    for s in range(v_ref.shape[0]):
        rows = pl.ds(s * seq, seq)
        k = k_ref[rows, :].astype(BF16)

        def vext(h, s=s):
            return jnp.concatenate(
                [v_ref[s, pl.ds(h, seq, stride=H_A), :].astype(BF16), ones], axis=1)

        _diff_attention(q_ref[rows, :], lambda h, k=k: k[:, h * hw:(h + 1) * hw], vext, lam,
                        sg_ref[...], lam_init, o_ref.at[rows])


def _attention_ctx(q, k, v_all, layer, seq, dl, sg, lam_init):
    t = q.shape[0]
    w = H_A * DV_A
    per_step = ATTN_SEQS_PER_STEP
    return pl.pallas_call(
        functools.partial(_attn_ctx_kernel, lam_init=lam_init, seq=seq),
        grid=(t // (seq * per_step),),
        in_specs=[
            pl.BlockSpec((per_step * seq, w), lambda b: (b, 0)),
            pl.BlockSpec((per_step * seq, w), lambda b: (b, 0)),
            pl.BlockSpec((per_step, None, seq * H_A, DV_A), lambda b: (b, layer, 0, 0)),
            pl.BlockSpec(dl.shape, lambda b: (0, 0)),
            pl.BlockSpec(sg.shape, lambda b: (0, 0)),
        ],
        out_specs=pl.BlockSpec((per_step * seq, w), lambda b: (b, 0)),
        out_shape=jax.ShapeDtypeStruct((t, w), F32),
        compiler_params=_params(("arbitrary",)),
        name="attn_ctx",
    )(q, k, v_all, dl, sg)


def _rope(x, cos, sin_lo, sin_hi):
    return (x * cos + pltpu.roll(x, 2 * DQK_A - 16, 1) * sin_lo + pltpu.roll(x, 16, 1) * sin_hi)


def _attn_lat_kernel(q_ref, k_ref, v_ref, ck_ref, cv_ref, qcos_ref, qslo_ref, qshi_ref,
                     kcos_ref, kslo_ref, kshi_ref, dl_ref, sg_ref, o_ref, kall, vall, *, lam_init):
    hw = 2 * DQK_A
    n_new = k_ref.shape[0]

    @pl.when(pl.program_id(1) == 0)
    def _():
        kcos, kslo, kshi = kcos_ref[...], kslo_ref[...], kshi_ref[...]
        for h in range(H_A):
            hs = slice(h * hw, (h + 1) * hw)
            kall[0:n_new, hs] = _rope(k_ref[:, hs], kcos, kslo, kshi).astype(BF16)
        kall[n_new:, :] = ck_ref[...].astype(BF16)
        ones = jnp.ones((vall.shape[0], DV_A), BF16)
        for h in range(H_A):
            vs = slice(h * DV_A, (h + 1) * DV_A)
            vall[0:n_new, 2 * h * DV_A:(2 * h + 1) * DV_A] = (
                v_ref[pl.ds(h, n_new, stride=H_A), :].astype(BF16))
            vall[n_new:, 2 * h * DV_A:(2 * h + 1) * DV_A] = cv_ref[:, vs].astype(BF16)
            vall[:, (2 * h + 1) * DV_A:(2 * h + 2) * DV_A] = ones

    lam = _diff_lambda(dl_ref, lam_init)
    qcos, qslo, qshi = qcos_ref[...], qslo_ref[...], qshi_ref[...]
    q = jnp.concatenate(
        [_rope(q_ref[:, h * hw:(h + 1) * hw], qcos, qslo, qshi) for h in range(H_A)], axis=-1)
    _diff_attention(q, lambda h: kall[:, h * hw:(h + 1) * hw],
                    lambda h: vall[:, 2 * h * DV_A:(2 * h + 2) * DV_A], lam, sg_ref[...],
                    lam_init, o_ref)


def _attention_lat(q, k, v, n_batch, cache_k, cache_v, layer, rope_tabs, dl, sg, lam_init):
    t = q.shape[0]
    seq = t // n_batch
    past = cache_k.shape[2]
    w = H_A * DV_A
    hw = 2 * DQK_A
    nq = seq // TILE
    cos, slo, shi = rope_tabs
    tab_q = pl.BlockSpec((TILE, hw), lambda b, j: (j, 0))
    tab_k = pl.BlockSpec((seq, hw), lambda b, j: (0, 0))
    return pl.pallas_call(
        functools.partial(_attn_lat_kernel, lam_init=lam_init),
        grid=(n_batch, nq),
        in_specs=[
            pl.BlockSpec((TILE, w), lambda b, j: (b * nq + j, 0)),
            pl.BlockSpec((seq, w), lambda b, j: (b, 0)),
            pl.BlockSpec((seq * H_A, DV_A), lambda b, j: (b, 0)),
            pl.BlockSpec((None, None, past, w), lambda b, j: (b, layer, 0, 0)),
            pl.BlockSpec((None, None, past, w), lambda b, j: (b, layer, 0, 0)),
            tab_q, tab_q, tab_q, tab_k, tab_k, tab_k,
            pl.BlockSpec(dl.shape, lambda b, j: (0, 0)),
            pl.BlockSpec(sg.shape, lambda b, j: (0, 0)),
        ],
        out_specs=pl.BlockSpec((TILE, w), lambda b, j: (b * nq + j, 0)),
        out_shape=jax.ShapeDtypeStruct((t, w), F32),
        scratch_shapes=[pltpu.VMEM((seq + past, w), BF16), pltpu.VMEM((seq + past, 2 * w), BF16)],
        compiler_params=_params(("arbitrary", "arbitrary")),
        name="attn_lat",
    )(q, k, v, cache_k, cache_v, cos, slo, shi, cos, slo, shi, dl, sg)


def _tile_aux():
    rows = lax.broadcasted_iota(jnp.int32, (TILE, 1), 0)
    lane = lax.broadcasted_iota(jnp.int32, (1, W_B), 1)
    head_masks = [(lane // DK_B) == h for h in range(H_B)]
    first_of_pair = lax.broadcasted_iota(jnp.int32, (1, 2 * DK_B), 1) < DK_B
    xor = (lax.broadcasted_iota(jnp.int32, (HALF, HALF), 0)
           ^ lax.broadcasted_iota(jnp.int32, (HALF, HALF), 1))
    levels = [1 << i for i in range(TILE.bit_length() - 1)]
    right = {m: (rows & m) != 0 for m in levels}
    same_block = {m: xor < 2 * m for m in levels if 2 * m < HALF}
    return rows, head_masks, first_of_pair, right, same_block


def _hgrn_lower_bound(lb_ref, layer, direction, depth):
    xs = [lb_ref[2 * j + direction:2 * j + direction + 1, :] for j in range(depth)]
    m = functools.reduce(jnp.maximum, xs)
    es = [jnp.exp(x - m) for x in xs]
    return sum(es[1:layer + 1]) / sum(es)


def _hgrn_gates(fpre, lbd):
    t = jnp.exp(-jnp.abs(fpre))
    log_sig = jnp.minimum(fpre, 0.0) - jnp.log(1.0 + t)
    r = 1.0 / (1.0 + t)
    sig_neg = jnp.where(fpre >= 0.0, t * r, r)
    if lbd is None:
        return log_sig * LOG2E, sig_neg
    a = jnp.log(lbd)
    b = jnp.log1p(-lbd) + log_sig
    logf = jnp.maximum(a, b) + jnp.log1p(jnp.exp(-jnp.abs(a - b)))
    return logf * LOG2E, (1.0 - lbd) * sig_neg


def _chain_row(token):
    return SUBLANES * (token % CHAIN_LEN) + token // CHAIN_LEN


def _tile_prefix(g, g_scr, b_scr):
    g_c = _to_chains(g, g_scr)
    run = [g_c[0]]
    for j in range(1, CHAIN_LEN):
        run.append(run[-1] + g_c[j])
    sub = lax.broadcasted_iota(jnp.int32, (SUBLANES, 1), 0)
    incl = run[-1]
    k = 1
    while k < SUBLANES:
        incl = incl + jnp.where(sub >= k, pltpu.roll(incl, k, 0), 0.0)
        k *= 2
    offset = incl - run[-1]
    prefix = _from_chains([r + offset for r in run], b_scr)
    return prefix, incl[SUBLANES - 1:SUBLANES, :]


def _level_exponent(z, g, b_scr, m, reverse, right):
    if m == 1:
        return jnp.where(right, g, 0.0) if not reverse else jnp.where(right, 0.0, g)
    slabs = b_scr.shape[0]

    def ref_rows(token):
        row = _chain_row(token)
        return jnp.concatenate(
            [jnp.broadcast_to(b_scr[s, row:row + 1, :], (SUBLANES, LANES)) for s in range(slabs)],
            axis=1)

    groups = []
    for gq in range(TILE // SUBLANES):
        first = gq * SUBLANES
        zg = z[first:first + SUBLANES]
        if 2 * m >= SUBLANES:
            block = first // (2 * m) * (2 * m)
            ref = ref_rows(block + m - 1)
            if m >= SUBLANES:
                groups.append(zg - ref if first - block >= m else ref - zg)
                continue
        else:
            sub = lax.broadcasted_iota(jnp.int32, (SUBLANES, 1), 0)
            ref = jnp.where(sub < 2 * m, ref_rows(first + m - 1), ref_rows(first + 3 * m - 1))
        groups.append(-jnp.abs(zg - ref))
    return jnp.concatenate(groups, axis=0)


def _hgrn_tile(qh, kk, vv, g, reverse, aux, ones_bd, state_t, scratch):
    rows, head_masks, first_of_pair, right_rows, same_block = aux
    pair_w = 2 * DK_B
    g_scr, b_scr = scratch
    prefix, total = _tile_prefix(g, g_scr, b_scr)
    z = prefix if not reverse else prefix - g
    diag = [[None, None] for _ in range(H_B)]
    off = [None] * H_B
    m = 1
    while m < TILE:
        right = right_rows[m]
        x = jnp.exp2(_level_exponent(z, g, b_scr, m, reverse, right))
        q_rows = right if not reverse else jnp.logical_not(right)
        qz = jnp.where(q_rows, (qh * x).astype(BF16), 0.0)
        kz = jnp.where(q_rows, 0.0, (kk * x).astype(BF16))
        for h in range(H_B):
            ls = slice((h // 2) * pair_w, (h // 2 + 1) * pair_w)
            qm = jnp.where(first_of_pair, qz[:, ls], 0.0) if h % 2 == 0 else \
                jnp.where(first_of_pair, 0.0, qz[:, ls])
            kh = kz[:, ls]
            if m < HALF:
                for i in range(2):
                    rs = slice(i * HALF, (i + 1) * HALF)
                    pm = _dot_nt(qm[rs], kh[rs])
                    if m in same_block:
                        pm = jnp.where(same_block[m], pm, 0.0)
                    diag[h][i] = pm if diag[h][i] is None else diag[h][i] + pm
            else:
                lo, hi = slice(0, HALF), slice(HALF, TILE)
                off[h] = _dot_nt(qm[hi], kh[lo]) if not reverse else _dot_nt(qm[lo], kh[hi])
        m *= 2
    vb = vv.astype(BF16)
    top = bot = None
    for h in range(H_B):
        vh = jnp.where(head_masks[h], vb, 0.0)
        d0, d1, of = (a.astype(BF16) for a in (diag[h][0], diag[h][1], off[h]))
        if not reverse:
            t = _dot(d0, vh[0:HALF])
            b = _dot(jnp.concatenate([of, d1], axis=1), vh)
        else:
            t = _dot(jnp.concatenate([d0, of], axis=1), vh)
            b = _dot(d1, vh[HALF:])
        top = t if top is None else top + t
        bot = b if bot is None else bot + b
    out = _group_sum(qh * kk, ones_bd) * vv + jnp.concatenate([top, bot], axis=0)
    q_decay, k_decay = (z, total - z) if not reverse else (total - z, z)
    if state_t is not None:
        out = out + _dot_nt((qh * jnp.exp2(q_decay)).astype(BF16), state_t.astype(BF16))
    k_hat = (kk * jnp.exp2(k_decay)).astype(BF16)
    return out, k_hat, total


def _block_diag_mask():
    r = lax.broadcasted_iota(jnp.int32, (W_B, W_B), 0) // DK_B
    c = lax.broadcasted_iota(jnp.int32, (W_B, W_B), 1) // DV_B
    return r == c


def _conv(x, prev_row, next_rows, cw_ref, cb_ref, rows):
    x_m1 = jnp.where(rows == 0, prev_row, pltpu.roll(x, 1, 0))
    x_p1 = jnp.where(rows == TILE - 1, next_rows[0:1], pltpu.roll(x, TILE - 1, 0))
    x_p2 = jnp.where(rows == TILE - 2, next_rows[0:1],
                     jnp.where(rows == TILE - 1, next_rows[1:2], pltpu.roll(x, TILE - 2, 0)))
    y = cb_ref[...] + x_m1 * cw_ref[0:1, :]
    y = y + x * cw_ref[1:2, :]
    y = y + x_p1 * cw_ref[2:3, :]
    return y + x_p2 * cw_ref[3:4, :]


def _chain_address(g):
    per_chain = CHAIN_LEN // SUBLANES
    return SUBLANES * SUBLANES * (g % per_chain) + g // per_chain, SUBLANES


def _to_chains(x, scr):
    slabs = scr.shape[0]
    for s in range(slabs):
        for g in range(TILE // SUBLANES):
            start, stride = _chain_address(g)
            scr[s, pl.ds(start, SUBLANES, stride=stride), :] = (
                x[g * SUBLANES:(g + 1) * SUBLANES, s * LANES:(s + 1) * LANES])
    return [jnp.concatenate([scr[s, j * SUBLANES:(j + 1) * SUBLANES, :] for s in range(slabs)],
                            axis=1) for j in range(CHAIN_LEN)]


def _from_chains(chains, scr):
    slabs = scr.shape[0]
    for s in range(slabs):
        for j, c in enumerate(chains):
            scr[s, j * SUBLANES:(j + 1) * SUBLANES, :] = c[:, s * LANES:(s + 1) * LANES]
    groups = []
    for g in range(TILE // SUBLANES):
        start, stride = _chain_address(g)
        groups.append(jnp.concatenate(
            [scr[s, pl.ds(start, SUBLANES, stride=stride), :] for s in range(slabs)], axis=1))
    return jnp.concatenate(groups, axis=0)


def _rglru_tile(xc, r_pre, i_pre, lam_row, reverse, h_in, scratch):
    a_scr, u_scr, h_scr = scratch
    r = _sigmoid(r_pre)
    i = _sigmoid(i_pre)
    log_a = -RG_C * r * _softplus(-lam_row)
    a = jnp.exp(log_a)
    u = jnp.sqrt(jnp.tanh(-log_a) * (a * a + 1.0)) * (i * xc)
    a_c = _to_chains(a, a_scr)
    u_c = _to_chains(u, u_scr)
    order = list(reversed(range(CHAIN_LEN))) if reverse else list(range(CHAIN_LEN))
    h_c = [None] * CHAIN_LEN
    p_c = [None] * CHAIN_LEN
    h_c[order[0]], p_c[order[0]] = u_c[order[0]], a_c[order[0]]
    for prev, j in zip(order[:-1], order[1:]):
        h_c[j] = a_c[j] * h_c[prev] + u_c[j]
        p_c[j] = a_c[j] * p_c[prev]
    p_tot, h_tot = p_c[order[-1]], h_c[order[-1]]
    sub = lax.broadcasted_iota(jnp.int32, (SUBLANES, 1), 0)
    k = 1
    while k < SUBLANES:
        if not reverse:
            valid = sub >= k
            p_s, h_s = pltpu.roll(p_tot, k, 0), pltpu.roll(h_tot, k, 0)
        else:
            valid = sub < SUBLANES - k
            p_s, h_s = pltpu.roll(p_tot, SUBLANES - k, 0), pltpu.roll(h_tot, SUBLANES - k, 0)
        h_tot = h_tot + p_tot * jnp.where(valid, h_s, 0.0)
        p_tot = p_tot * jnp.where(valid, p_s, 1.0)
        k *= 2
    if h_in is not None:
        h_tot = h_tot + p_tot * h_in
    if not reverse:
        carry = jnp.where(sub == 0, 0.0 if h_in is None else h_in, pltpu.roll(h_tot, 1, 0))
        h_out = h_tot[SUBLANES - 1:SUBLANES, :]
    else:
        carry = jnp.where(sub == SUBLANES - 1, 0.0 if h_in is None else h_in,
                          pltpu.roll(h_tot, SUBLANES - 1, 0))
        h_out = h_tot[0:1, :]
    h_c = [h + p * carry for h, p in zip(h_c, p_c)]
    return _from_chains(h_c, h_scr), h_out


def _rec_inputs(refs):
    qb, ff, fb, ib, gb, xcol, gcol = [r[...] for r in refs]
    return _silu(qb), ff, fb, ib, gb, xcol, gcol


def _hgrn_out(o_sum, gb, onorm_gain, ones_bd):
    ms = _group_sum(o_sum * o_sum, ones_bd) * (1.0 / DV_B)
    return o_sum * lax.rsqrt(ms + EPS) * onorm_gain * _silu(gb)


def _rec_ctx_kernel(*refs, layer, depth, n_cast):
    refs = refs[2 if layer else 0:]
    (q_ref, ff_ref, fb_ref, i_ref, g_ref, x_ref, gc_ref, lb_ref, on_ref, cw_ref, cb_ref, wg_ref,
     bg_ref, lam_ref, ones_ref) = refs[:15]
    ob_ref, oc_ref, st_ref, hs_ref = refs[15 + n_cast:19 + n_cast]
    scan_scratch = refs[19 + 2 * n_cast:]
    _cast_slabs(refs[15:15 + n_cast], refs[19 + n_cast:19 + 2 * n_cast])
    if layer == 0 and depth > 1:
        st_ref[0, 1:] = jnp.zeros((depth - 1,) + st_ref.shape[2:], F32)
        hs_ref[0, 1:] = jnp.zeros((depth - 1,) + hs_ref.shape[2:], F32)
    aux = _tile_aux()
    rows = aux[0]
    ones_bd = ones_ref[...]
    qh, ff, fb, ib, gb, xcol, gcol = _rec_inputs((q_ref, ff_ref, fb_ref, i_ref, g_ref, x_ref, gc_ref))
    o_sum = None
    for direction, fpre in enumerate((ff, fb)):
        lbd = None if layer == 0 else _hgrn_lower_bound(lb_ref, layer, direction, depth)
        g, kk = _hgrn_gates(fpre, lbd)
        o, k_hat, _ = _hgrn_tile(qh, kk, ib, g, direction == 1, aux, ones_bd, None,
                                 scan_scratch[3 + 2 * direction:5 + 2 * direction])
        o_sum = o if o_sum is None else o_sum + o
        full = _dot_tn(k_hat, ib.astype(BF16))
        for h in range(H_B):
            blk = full[h * DK_B:(h + 1) * DK_B, (h // 2) * 2 * DV_B:(h // 2 + 1) * 2 * DV_B]
            if h % 2:
                blk = pltpu.roll(blk, DV_B, 1)
            st_ref[0, 0, direction, h] = blk[:, 0:DV_B]
    ob_ref[...] = _hgrn_out(o_sum, gb, on_ref[...], ones_bd)

    zero_row = jnp.zeros((1, W_C), F32)
    xc = _conv(xcol, zero_row, jnp.zeros((2, W_C), F32), cw_ref, cb_ref, rows)
    gates = _dot(xc.astype(BF16), wg_ref[...]) + bg_ref[...]
    y_f, h_f = _rglru_tile(xc, gates[:, 0:W_C], gates[:, W_C:2 * W_C], lam_ref[0:1, :], False,
                           None, scan_scratch[0:3])
    y_b, h_b = _rglru_tile(xc, gates[:, 2 * W_C:3 * W_C], gates[:, 3 * W_C:], lam_ref[1:2, :], True,
                           None, scan_scratch[0:3])
    oc_ref[...] = (y_f + y_b) * _gelu_tanh(gcol)
    hs_ref[0, 0, 0:1, :] = h_f
    hs_ref[0, 0, 1:2, :] = h_b


def _scan_scratch(hgrn_directions):
    return [pltpu.VMEM((W_C // LANES, TILE, LANES), F32) for _ in range(3 + 2 * hgrn_directions)]


def _col_spec(col, row_map):
    return pl.BlockSpec((TILE, W_B), lambda *idx: (row_map(*idx), col))


COL_QB, COL_FF, COL_FB, COL_IB, COL_GB, COL_XC, COL_GC = range(7)
_REC_COLS = (COL_QB, COL_FF, COL_FB, COL_IB, COL_GB, COL_XC, COL_GC)


def _full_spec(a):
    nd = a.ndim
    return pl.BlockSpec(a.shape, lambda *idx: (0,) * nd)


def _recurrence_ctx(proj, layer, depth, consts, states, cast=()):
    t = proj.shape[0]
    n = t // TILE
    assert bool(states) == (layer > 0) and len(consts) == 8
    layers, at = (1, layer) if states else (depth, 0)
    cast_in, cast_out, cast_shape = _cast_specs(cast, layer, n)
    return pl.pallas_call(
        functools.partial(_rec_ctx_kernel, layer=layer, depth=depth, n_cast=len(cast)),
        grid=(n,),
        in_specs=([pl.BlockSpec(memory_space=pl.ANY) for _ in states]
                  + [_col_spec(c, lambda b: b) for c in _REC_COLS] + [_full_spec(a) for a in consts]
                  + cast_in),
        out_specs=[
            pl.BlockSpec((TILE, W_B), lambda b: (b, 0)),
            pl.BlockSpec((TILE, W_C), lambda b: (b, 0)),
            pl.BlockSpec((1, layers, 2, H_B, DK_B, DV_B), lambda b: (b, at, 0, 0, 0, 0)),
            pl.BlockSpec((1, layers, 2, W_C), lambda b: (b, at, 0, 0)),
        ] + cast_out,
        out_shape=[
            jax.ShapeDtypeStruct((t, W_B), F32),
            jax.ShapeDtypeStruct((t, W_C), F32),
            jax.ShapeDtypeStruct((n, depth, 2, H_B, DK_B, DV_B), F32),
            jax.ShapeDtypeStruct((n, depth, 2, W_C), F32),
        ] + cast_shape,
        input_output_aliases={i: 2 + i for i in range(len(states))},
        scratch_shapes=_scan_scratch(2),
        compiler_params=_params(("arbitrary",)),
        name="rec_ctx",
    )(*states, *([proj] * len(_REC_COLS)), *consts, *cast)


def _rec_lat_kernel(q_ref, f_ref, i_ref, g_ref, x_ref, xp_ref, xn_ref, gc_ref, s0_ref, h0_ref,
                    *rest, layer, depth, direction, n_tiles):
    if direction == 0:
        (lb_ref, on_ref, cw_ref, cb_ref, wg_ref, bg_ref, lam_ref, ones_ref,
         o_ref, y_ref, state, hstate, *scan_scratch) = rest
    else:
        (of_ref, yf_ref, lb_ref, on_ref, cw_ref, cb_ref, wg_ref, bg_ref, lam_ref, ones_ref,
         o_ref, y_ref, state, hstate, *scan_scratch) = rest
    step = pl.program_id(1)
    tile = step if direction == 0 else n_tiles - 1 - step
    reverse = direction == 1

    @pl.when(step == 0)
    def _():
        state[...] = s0_ref[0, 0]
        hstate[...] = h0_ref[0, 0]

    aux = _tile_aux()
    rows = aux[0]
    ones_bd = ones_ref[...]
    qh = _silu(q_ref[...])
    ib = i_ref[...]
    lbd = None if layer == 0 else _hgrn_lower_bound(lb_ref, layer, direction, depth)
    g, kk = _hgrn_gates(f_ref[...], lbd)
    o, k_hat, total = _hgrn_tile(qh, kk, ib, g, reverse, aux, ones_bd, state[...],
                                 scan_scratch[3:5])
    upd = jnp.where(_block_diag_mask(), _dot_tn(ib.astype(BF16), k_hat), 0.0)
    state[...] = state[...] * jnp.exp2(total) + upd

    prev_row = jnp.where(tile == 0, 0.0, xp_ref[TILE - 1:TILE, :])
    next_rows = jnp.where(tile == n_tiles - 1, 0.0, xn_ref[0:2, :])
    xc = _conv(x_ref[...], prev_row, next_rows, cw_ref, cb_ref, rows)
    gates = _dot(xc.astype(BF16), wg_ref[...]) + bg_ref[...]
    c0 = 2 * W_C * direction
    y, h_next = _rglru_tile(xc, gates[:, c0:c0 + W_C], gates[:, c0 + W_C:c0 + 2 * W_C],
                            lam_ref[direction:direction + 1, :], reverse, hstate[...],
                            scan_scratch[0:3])
    hstate[...] = h_next

    if direction == 0:
        o_ref[...] = o
        y_ref[...] = y
    else:
        o_ref[...] = _hgrn_out(of_ref[...] + o, g_ref[...], on_ref[...], ones_bd)
        y_ref[...] = (yf_ref[...] + y) * _gelu_tanh(gc_ref[...])


def _recurrence_lat(proj, n_batch, layer, depth, direction, state0, h0, prev, consts):
    t = proj.shape[0]
    n_tiles = t // (TILE * n_batch)

    def tile_of(b, s):
        return s if direction == 0 else n_tiles - 1 - s

    def row(b, s):
        return b * n_tiles + tile_of(b, s)

    def row_prev(b, s):
        return b * n_tiles + jnp.maximum(tile_of(b, s) - 1, 0)

    def row_next(b, s):
        return b * n_tiles + jnp.minimum(tile_of(b, s) + 1, n_tiles - 1)

    f_col = COL_FF if direction == 0 else COL_FB
    in_specs = [_col_spec(COL_QB, row), _col_spec(f_col, row), _col_spec(COL_IB, row),
                _col_spec(COL_GB, row), _col_spec(COL_XC, row), _col_spec(COL_XC, row_prev),
                _col_spec(COL_XC, row_next), _col_spec(COL_GC, row),
                pl.BlockSpec((1, 1, W_B, W_B), lambda b, s: (b, direction, 0, 0)),
                pl.BlockSpec((1, 1, 1, W_C), lambda b, s: (b, direction, 0, 0))]
    args = [proj] * 8 + [state0, h0]
    if direction == 1:
        in_specs += [pl.BlockSpec((TILE, W_B), lambda b, s: (row(b, s), 0)),
                     pl.BlockSpec((TILE, W_C), lambda b, s: (row(b, s), 0))]
        args += list(prev)
    in_specs += [_full_spec(a) for a in consts]
    args += list(consts)
    return pl.pallas_call(
        functools.partial(_rec_lat_kernel, layer=layer, depth=depth, direction=direction,
                          n_tiles=n_tiles),
        grid=(n_batch, n_tiles),
        in_specs=in_specs,
        out_specs=[pl.BlockSpec((TILE, W_B), lambda b, s: (row(b, s), 0)),
                   pl.BlockSpec((TILE, W_C), lambda b, s: (row(b, s), 0))],
        out_shape=[jax.ShapeDtypeStruct((t, W_B), F32), jax.ShapeDtypeStruct((t, W_C), F32)],
        scratch_shapes=[pltpu.VMEM((W_B, W_B), F32), pltpu.VMEM((1, W_C), F32)] + _scan_scratch(1),
        compiler_params=_params(("arbitrary", "arbitrary")),
        name="rec_lat_fwd" if direction == 0 else "rec_lat_bwd",
    )(*args)


def _out_kernel(x_ref, oa_ref, ob_ref, oc_ref, mod_ref, n2_ref, wo_ref, w1_ref, w2_ref, *refs):
    n_cast = len(refs) // 2
    y_ref = refs[n_cast]
    _cast_slabs(refs[:n_cast], refs[n_cast + 1:])
    x = x_ref[...]
    d = x.shape[-1]
    g1 = mod_ref[0, :, 2 * d:3 * d]
    sh2 = mod_ref[0, :, 3 * d:4 * d]
    sc2 = mod_ref[0, :, 4 * d:5 * d]
    g2 = mod_ref[0, :, 5 * d:6 * d]
    wa = H_A * DV_A
    mix = _dot(oa_ref[...].astype(BF16), wo_ref[0:wa, :])
    mix = mix + _dot(ob_ref[...].astype(BF16), wo_ref[wa:wa + W_B, :])
    mix = mix + _dot(oc_ref[...].astype(BF16), wo_ref[wa + W_B:, :])
    x1 = x + g1 * mix
    h = (_rms_rows(x1, n2_ref[...]) * (1.0 + sc2) + sh2).astype(BF16)
    acc = jnp.zeros_like(x)
    for c in range(w1_ref.shape[1] // D_FF_CHUNK):
        cs = slice(c * D_FF_CHUNK, (c + 1) * D_FF_CHUNK)
        f = jnp.maximum(_dot(h, w1_ref[:, cs]), 0.0)
        acc = acc + _dot((f * f).astype(BF16), w2_ref[cs, :])
    y_ref[...] = x1 + g2 * acc


def _out_mlp(x, oa, ob, oc, mod, tokens_per_mod, n2, w_out, w_ff1, w_ff2, cast=(), cast_layer=0):
    t, d = x.shape
    steps = t // ROW_TILE

    def resident(w):
        return pl.BlockSpec((None,) + w.shape[1:], lambda i: (0, 0, 0),
                            pipeline_mode=pl.Buffered(1))

    cast_in, cast_out, cast_shape = _cast_specs(cast, cast_layer, steps)
    return pl.pallas_call(
        _out_kernel,
        grid=(steps,),
        in_specs=[
            pl.BlockSpec((ROW_TILE, d), lambda i: (i, 0)),
            pl.BlockSpec((ROW_TILE, oa.shape[1]), lambda i: (i, 0)),
            pl.BlockSpec((ROW_TILE, ob.shape[1]), lambda i: (i, 0)),
            pl.BlockSpec((ROW_TILE, oc.shape[1]), lambda i: (i, 0)),
            pl.BlockSpec((1, 1, N_MOD * d), lambda i: (i * ROW_TILE // tokens_per_mod, 0, 0)),
            pl.BlockSpec((1, d), lambda i: (0, 0)),
            resident(w_out),
            resident(w_ff1),
            resident(w_ff2),
        ] + cast_in,
        out_specs=[pl.BlockSpec((ROW_TILE, d), lambda i: (i, 0))] + cast_out,
        out_shape=[jax.ShapeDtypeStruct((t, d), F32)] + cast_shape,
        compiler_params=_params(("arbitrary",)),
        name="out_mlp",
    )(x, oa, ob, oc, mod, n2, w_out, w_ff1, w_ff2, *cast)


def _ones_block_diag(n, group):
    idx = np.arange(n) // group
    return jnp.asarray((idx[:, None] == idx[None, :]).astype(np.float32), dtype=BF16)


def _rope_tables(seq):
    nf = DQK_A // 4
    inv = ROPE_BASE ** (-jnp.arange(nf, dtype=F32) / nf)
    pos = jnp.arange(seq)
    ang_r = (pos // GRID_W).astype(F32)[:, None] * inv
    ang_c = (pos % GRID_W).astype(F32)[:, None] * inv
    z = jnp.zeros_like(ang_r)
    cos = jnp.concatenate([jnp.cos(ang_r)] * 2 + [jnp.cos(ang_c)] * 2, axis=-1)
    sin_lo = jnp.concatenate([-jnp.sin(ang_r), z, -jnp.sin(ang_c), z], axis=-1)
    sin_hi = jnp.concatenate([z, jnp.sin(ang_r), z, jnp.sin(ang_c)], axis=-1)
    return tuple(jnp.tile(a, (1, 2)) for a in (cos, sin_lo, sin_hi))


def _gate_weights(rg_w_l):
    eye = jnp.eye(H_C, dtype=rg_w_l.dtype)
    w = jnp.einsum('dghij,hk->dghikj', rg_w_l, eye)
    w = w.reshape(2, 2, W_C, W_C)
    return jnp.transpose(w, (2, 0, 1, 3)).reshape(W_C, 4 * W_C).astype(BF16)


def _state_block_diag_t(s):
    eye = jnp.eye(H_B, dtype=s.dtype)
    w = jnp.einsum('...hdv,hk->...hvkd', s, eye)
    return w.reshape(s.shape[:-3] + (W_B, W_B))


def kernel(x_prompt, x_sample, cache_k, cache_v, state_hgrn, state_rglru, c, c_ctx, w_mod, b_mod,
           norm1, norm2, w_in, w_out, qk_norm, diff_lambda, subln, hgrn_lb, hgrn_onorm, conv_w,
           conv_b, rg_w, rg_b, rg_lambda, w_ff1, w_ff2):
    batch, seq, d = x_prompt.shape
    dec_batch, dec_seq, _ = x_sample.shape
    depth = w_in.shape[0]
    past = cache_k.shape[2]
    wq = H_A * 2 * DQK_A

    c8 = jnp.concatenate([c_ctx[None], c, jnp.zeros((8 - 1 - dec_batch, d), F32)], axis=0)
    mod_all = _modulation(c8, w_mod, b_mod)

    assert DQK_A == DV_B and seq == TILE
    ones_b = _ones_block_diag(W_B, DV_B)
    rope_tabs = _rope_tables(dec_seq)
    ck = cache_k.reshape(dec_batch, depth, past, wq)
    cv = cache_v.reshape(dec_batch, depth, past, H_A * DV_A)
    lb2 = hgrn_lb.reshape(depth * 2, W_B)

    w_f32 = (w_in, w_out, w_ff1, w_ff2)
    w_next = [w_in[0:1].astype(BF16)]

    yp = x_prompt.reshape(batch * seq, d)
    ys = x_sample.reshape(dec_batch * dec_seq, d)
    caches, states = (), ()
    for l in range(depth):
        lam_init = 0.8 - 0.6 * math.exp(-0.3 * l)
        mod_ctx = mod_all[l, 0:1][:, None, :]
        mod_lat = mod_all[l, 1:1 + dec_batch][:, None, :]
        n1 = norm1[l][None]
        n2 = norm2[l][None]
        gq = jnp.tile(qk_norm[l, 0], H_A * 2)[None] * (DQK_A ** -0.5 * LOG2E)
        gk = jnp.tile(qk_norm[l, 1], H_A * 2)[None]
        dl = diff_lambda[l]
        sg = subln[l][None]
        consts = (lb2, jnp.tile(hgrn_onorm[l], H_B)[None], conv_w[l], conv_b[l][None],
                  _gate_weights(rg_w[l]), rg_b[l].reshape(1, 4 * W_C), rg_lambda[l], ones_b)

        w_in_b = w_next[0]

        q, k, v_all, rest, kt_all = _in_projection(yp, mod_ctx, batch * seq, n1, w_in_b, 0,
                                                   gq, gk, caches=caches)
        caches = (kt_all, v_all)
        oa = _attention_ctx(q, k, v_all, l, seq, dl, sg, lam_init)
        ob, oc, st, hs, *w_rest = _recurrence_ctx(rest, l, depth, consts, states,
                                                  cast=w_f32[1:] if l == 0 else ())
        states = (st, hs)
        w_out_b, w1_b, w2_b = w_rest if l == 0 else w_next[1:]
        yp, *w_next = _out_mlp(yp, oa, ob, oc, mod_ctx, batch * seq, n2, w_out_b, w1_b, w2_b,
                               cast=w_f32 if l + 1 < depth else (), cast_layer=l + 1)

        q, k, v, rest = _in_projection(ys, mod_lat, dec_seq, n1, w_in_b, 0, gq, gk)
        oa = _attention_lat(q, k, v, dec_batch, ck, cv, l, rope_tabs, dl, sg, lam_init)
        s0 = _state_block_diag_t(state_hgrn[:, l])
        h0 = state_rglru[:, l][:, :, None, :]
        fwd = _recurrence_lat(rest, dec_batch, l, depth, 0, s0, h0, None, consts)
        ob, oc = _recurrence_lat(rest, dec_batch, l, depth, 1, s0, h0, fwd, consts)
        ys, = _out_mlp(ys, oa, ob, oc, mod_lat, dec_seq, n2, w_out_b, w1_b, w2_b)

    kt_all, v_all = caches
    new_k = jnp.transpose(kt_all.reshape(batch, depth, H_A, 2, DQK_A, seq), (0, 1, 5, 2, 3, 4))
    new_v = v_all.reshape(batch, depth, seq, H_A, DV_A)
    return (yp.reshape(batch, seq, d), ys.reshape(dec_batch, dec_seq, d), new_k, new_v,
            states[0], states[1])
```

```python
import functools
import math

import numpy as np
import jax
import jax.numpy as jnp
from jax import lax
from jax.experimental import pallas as pl
from jax.experimental.pallas import tpu as pltpu

F32 = jnp.float32
BF16 = jnp.bfloat16

GRID_W = 64
H_A, DQK_A, DV_A = 4, 64, 128
H_B, DK_B, DV_B = 4, 64, 64
W_B = H_B * DV_B
H_C, W_C = 4, 256
CONV_K = 4
RG_C = 8.0
ROPE_BASE = 10000.0
EPS = 1e-6
N_MOD = 6
D_FF_CHUNK = 1024

TILE = 256
HALF = TILE // 2
ROW_TILE = 512
ATTN_SEQS_PER_STEP = 4
ATTN_LAT_QUERIES = 512
SUBLANES = 8
LANES = 128
CHAIN_LEN = TILE // SUBLANES
LOG2E = math.log2(math.e)
V7X_VMEM_BYTES = 64 * 1024 * 1024
VMEM_LIMIT = V7X_VMEM_BYTES - 8 * 1024 * 1024


def _dot(a, b):
    return jnp.dot(a, b, preferred_element_type=F32)


def _dot_nt(a, b):
    return lax.dot_general(a, b, (((1,), (1,)), ((), ())), preferred_element_type=F32)


def _dot_tn(a, b):
    return lax.dot_general(a, b, (((0,), (0,)), ((), ())), preferred_element_type=F32)


def _group_sum(x, ones_bd):
    hi = x.astype(BF16)
    lo = (x - hi.astype(F32)).astype(BF16)
    return _dot(hi, ones_bd) + _dot(lo, ones_bd)


def _rms_rows(x, gain):
    return x * lax.rsqrt(jnp.mean(x * x, axis=-1, keepdims=True) + EPS) * gain


def _sigmoid(x):
    return 0.5 * jnp.tanh(0.5 * x) + 0.5


def _silu(x):
    return x * _sigmoid(x)


def _softplus(x):
    return jnp.maximum(x, 0.0) + jnp.log1p(jnp.exp(-jnp.abs(x)))


def _log_sigmoid(x):
    return jnp.minimum(x, 0.0) - jnp.log1p(jnp.exp(-jnp.abs(x)))


def _gelu_tanh(x):
    return x * (0.5 * (1.0 + jnp.tanh(math.sqrt(2.0 / math.pi) * (x + 0.044715 * (x * x * x)))))


def _params(semantics, flags=None):
    return pltpu.CompilerParams(dimension_semantics=semantics, vmem_limit_bytes=VMEM_LIMIT,
                                flags=flags)


def _cast_specs(cast, layer, steps):
    assert all(w.shape[1] % (steps * 2 * SUBLANES) == 0 for w in cast)
    slabs = [(w.shape[1] // steps, w.shape[2]) for w in cast]
    return ([pl.BlockSpec((None,) + s, lambda i: (layer, i, 0)) for s in slabs],
            [pl.BlockSpec((1,) + s, lambda i: (0, i, 0)) for s in slabs],
            [jax.ShapeDtypeStruct((1,) + w.shape[1:], BF16) for w in cast])


def _cast_slabs(srcs, dsts):
    for src, dst in zip(srcs, dsts):
        dst[0] = src[...].astype(BF16)


def _mod_kernel(c_ref, w_ref, b_ref, o_ref):
    c = c_ref[...]
    o_ref[0] = _dot(_silu(c).astype(BF16), w_ref[0].astype(BF16)) + b_ref[0]


def _modulation(c8, w_mod, b_mod):
    depth, d, n = w_mod.shape
    tn = d
    return pl.pallas_call(
        _mod_kernel,
        grid=(depth, n // tn),
        in_specs=[
            pl.BlockSpec((8, d), lambda l, j: (0, 0)),
            pl.BlockSpec((1, d, tn), lambda l, j: (l, 0, j)),
            pl.BlockSpec((1, 1, tn), lambda l, j: (l, 0, j)),
        ],
        out_specs=pl.BlockSpec((1, 8, tn), lambda l, j: (l, 0, j)),
        out_shape=jax.ShapeDtypeStruct((depth, 8, n), F32),
        compiler_params=_params(("arbitrary", "arbitrary")),
        name="mod",
    )(c8, w_mod, b_mod.reshape(depth, 1, n))


def _proj_kernel(x_ref, mod_ref, n1_ref, w_ref, gq_ref, gk_ref, *refs, layers_before):
    prev = refs[:2] if layers_before else ()
    q_ref, k_ref, v_ref, r_ref, *kt_ref = refs[len(prev):]
    x = x_ref[...]
    d = x.shape[-1]
    sh = mod_ref[0, :, 0:d]
    sc = mod_ref[0, :, d:2 * d]
    h = _rms_rows(x, n1_ref[...]) * (1.0 + sc) + sh
    p = _dot(h.astype(BF16), w_ref[...])
    wq = H_A * 2 * DQK_A

    first = lax.broadcasted_iota(jnp.int32, (1, LANES), 1) < DQK_A

    def qk_norm(z, gain):
        z2 = z * z
        parts = []
        for c in range(0, wq, LANES):
            slab = z2[:, c:c + LANES]
            lo = jnp.sum(jnp.where(first, slab, 0.0), axis=-1, keepdims=True)
            hi = jnp.sum(jnp.where(first, 0.0, slab), axis=-1, keepdims=True)
            parts.append(jnp.where(first, lo, hi))
        ms = jnp.concatenate(parts, axis=1) * (1.0 / DQK_A)
        return z * lax.rsqrt(ms + EPS) * gain

    wv = H_A * DV_A
    q_ref[...] = qk_norm(p[:, 0:wq], gq_ref[...])
    kn = qk_norm(p[:, wq:2 * wq], gk_ref[...])
    k_ref[...] = kn
    if kt_ref:
        for s in range(ROW_TILE // TILE):
            kt_ref[0][s, layers_before] = kn[s * TILE:(s + 1) * TILE].T
        if prev:
            ktp_ref, vp_ref = prev
            kt_ref[0][:, 0:layers_before] = ktp_ref[...]
            v_ref[:, 0:layers_before] = vp_ref[...]
    for s in range(ROW_TILE // TILE):
        v_dst = v_ref.at[s, layers_before] if kt_ref else v_ref.at[pl.ds(s * TILE * H_A, TILE * H_A)]
        for h in range(H_A):
            v_dst[pl.ds(h, TILE, stride=H_A), :] = (
                p[s * TILE:(s + 1) * TILE, 2 * wq + h * DV_A:2 * wq + (h + 1) * DV_A])
    r_ref[...] = p[:, 2 * wq + wv:]


def _in_projection(x, mod, tokens_per_mod, n1, w_in, layer, gq, gk, caches=None):
    t, d = x.shape
    layers_before = caches[0].shape[1] if caches else 0
    n_in = w_in.shape[2]
    wq = gq.shape[1]
    wv = H_A * DV_A
    widths = (wq, wq, wv, n_in - 2 * wq - wv)
    seqs = ROW_TILE // TILE
    out_specs = [pl.BlockSpec((ROW_TILE, w), lambda i: (i, 0)) for w in widths]
    out_shape = [jax.ShapeDtypeStruct((t, w), F32) for w in widths]
    out_specs[2] = pl.BlockSpec((ROW_TILE * H_A, DV_A), lambda i: (i, 0))
    out_shape[2] = jax.ShapeDtypeStruct((t * H_A, DV_A), F32)
    cache_specs = []
    if caches is not None:
        n_seq, nl = t // TILE, layers_before + 1
        out_specs[2] = pl.BlockSpec((seqs, nl, TILE * H_A, DV_A), lambda i: (i, 0, 0, 0))
        out_shape[2] = jax.ShapeDtypeStruct((n_seq, nl, TILE * H_A, DV_A), F32)
        out_specs.append(pl.BlockSpec((seqs, nl, wq, TILE), lambda i: (i, 0, 0, 0)))
        out_shape.append(jax.ShapeDtypeStruct((n_seq, nl, wq, TILE), F32))
        if caches:
            cache_specs = [
                pl.BlockSpec((seqs, layers_before, wq, TILE), lambda i: (i, 0, 0, 0)),
                pl.BlockSpec((seqs, layers_before, TILE * H_A, DV_A), lambda i: (i, 0, 0, 0))]
    return pl.pallas_call(
        functools.partial(_proj_kernel, layers_before=layers_before),
        grid=(t // ROW_TILE,),
        in_specs=[
            pl.BlockSpec((ROW_TILE, d), lambda i: (i, 0)),
            pl.BlockSpec((1, 1, N_MOD * d), lambda i: (i * ROW_TILE // tokens_per_mod, 0, 0)),
            pl.BlockSpec((1, d), lambda i: (0, 0)),
            pl.BlockSpec((None, d, n_in), lambda i: (layer, 0, 0)),
            pl.BlockSpec((1, wq), lambda i: (0, 0)),
            pl.BlockSpec((1, wq), lambda i: (0, 0)),
        ] + cache_specs,
        out_specs=out_specs,
        out_shape=out_shape,
        compiler_params=_params(("arbitrary",)),
        name="proj",
    )(x, mod, n1, w_in, gq, gk, *(caches or ()))


def _diff_lambda(dl_ref, lam_init):
    lv = dl_ref[...]
    a = jnp.sum(lv[0:1] * lv[1:2], axis=-1, keepdims=True)
    b = jnp.sum(lv[2:3] * lv[3:4], axis=-1, keepdims=True)
    return jnp.exp(a) - jnp.exp(b) + lam_init


def _diff_attention(q, k_of, vext_of, lam, sub_gain, lam_init, o_ref):
    hw = 2 * DQK_A
    first = lax.broadcasted_iota(jnp.int32, (1, hw), 1) < DQK_A
    for h in range(H_A):
        hs = slice(h * hw, (h + 1) * hw)
        qh = q[:, hs]
        kh = k_of(h)
        vext = vext_of(h)
        maps = []
        for qm in (jnp.where(first, qh, 0.0), jnp.where(first, 0.0, qh)):
            s = _dot_nt(qm.astype(BF16), kh)
            e = jnp.exp2(s - jnp.max(s, axis=-1, keepdims=True)).astype(BF16)
            oe = _dot(e, vext)
            maps.append(oe[:, 0:DV_A] / oe[:, DV_A:])
        oh = maps[0] - lam * maps[1]
        o_ref[:, hs] = _rms_rows(oh, sub_gain) * (1.0 - lam_init)


def _attn_ctx_kernel(q_ref, k_ref, v_ref, dl_ref, sg_ref, o_ref, *, lam_init, seq):
    lam = _diff_lambda(dl_ref, lam_init)
    ones = jnp.ones((seq, DV_A), BF16)
    hw = 2 * DQK_A
    for s in range(v_ref.shape[0]):
        rows = pl.ds(s * seq, seq)
        k = k_ref[rows, :].astype(BF16)

        def vext(h, s=s):
            return jnp.concatenate(
                [v_ref[s, pl.ds(h, seq, stride=H_A), :].astype(BF16), ones], axis=1)

        _diff_attention(q_ref[rows, :], lambda h, k=k: k[:, h * hw:(h + 1) * hw], vext, lam,
                        sg_ref[...], lam_init, o_ref.at[rows])


def _attention_ctx(q, k, v_all, layer, seq, dl, sg, lam_init):
    t = q.shape[0]
    w = H_A * DV_A
    per_step = ATTN_SEQS_PER_STEP
    return pl.pallas_call(
        functools.partial(_attn_ctx_kernel, lam_init=lam_init, seq=seq),
        grid=(t // (seq * per_step),),
        in_specs=[
            pl.BlockSpec((per_step * seq, w), lambda b: (b, 0)),
            pl.BlockSpec((per_step * seq, w), lambda b: (b, 0)),
            pl.BlockSpec((per_step, None, seq * H_A, DV_A), lambda b: (b, layer, 0, 0)),
            pl.BlockSpec(dl.shape, lambda b: (0, 0)),
            pl.BlockSpec(sg.shape, lambda b: (0, 0)),
        ],
        out_specs=pl.BlockSpec((per_step * seq, w), lambda b: (b, 0)),
        out_shape=jax.ShapeDtypeStruct((t, w), F32),
        compiler_params=_params(("arbitrary",)),
        name="attn_ctx",
    )(q, k, v_all, dl, sg)


def _rope(x, cos, sin_lo, sin_hi):
    return (x * cos + pltpu.roll(x, 2 * DQK_A - 16, 1) * sin_lo + pltpu.roll(x, 16, 1) * sin_hi)


def _attn_lat_kernel(q_ref, k_ref, v_ref, ck_ref, cv_ref, qcos_ref, qslo_ref, qshi_ref,
                     kcos_ref, kslo_ref, kshi_ref, dl_ref, sg_ref, o_ref, kall, vall, *, lam_init):
    hw = 2 * DQK_A
    n_new = k_ref.shape[0]

    @pl.when(pl.program_id(1) == 0)
    def _():
        kcos, kslo, kshi = kcos_ref[...], kslo_ref[...], kshi_ref[...]
        for h in range(H_A):
            hs = slice(h * hw, (h + 1) * hw)
            kall[0:n_new, hs] = _rope(k_ref[:, hs], kcos, kslo, kshi).astype(BF16)
        kall[n_new:, :] = ck_ref[...].astype(BF16)
        ones = jnp.ones((vall.shape[0], DV_A), BF16)
        for h in range(H_A):
            vs = slice(h * DV_A, (h + 1) * DV_A)
            vall[0:n_new, 2 * h * DV_A:(2 * h + 1) * DV_A] = (
                v_ref[pl.ds(h, n_new, stride=H_A), :].astype(BF16))
            vall[n_new:, 2 * h * DV_A:(2 * h + 1) * DV_A] = cv_ref[:, vs].astype(BF16)
            vall[:, (2 * h + 1) * DV_A:(2 * h + 2) * DV_A] = ones

    lam = _diff_lambda(dl_ref, lam_init)
    qcos, qslo, qshi = qcos_ref[...], qslo_ref[...], qshi_ref[...]
    q = jnp.concatenate(
        [_rope(q_ref[:, h * hw:(h + 1) * hw], qcos, qslo, qshi) for h in range(H_A)], axis=-1)
    _diff_attention(q, lambda h: kall[:, h * hw:(h + 1) * hw],
                    lambda h: vall[:, 2 * h * DV_A:(2 * h + 2) * DV_A], lam, sg_ref[...],
                    lam_init, o_ref)


def _attention_lat(q, k, v, n_batch, cache_k, cache_v, layer, rope_tabs, dl, sg, lam_init):
    t = q.shape[0]
    seq = t // n_batch
    past = cache_k.shape[2]
    w = H_A * DV_A
    hw = 2 * DQK_A
    tq = ATTN_LAT_QUERIES
    nq = seq // tq
    cos, slo, shi = rope_tabs
    tab_q = pl.BlockSpec((tq, hw), lambda b, j: (j, 0))
    tab_k = pl.BlockSpec((seq, hw), lambda b, j: (0, 0))
    return pl.pallas_call(
        functools.partial(_attn_lat_kernel, lam_init=lam_init),
        grid=(n_batch, nq),
        in_specs=[
            pl.BlockSpec((tq, w), lambda b, j: (b * nq + j, 0)),
            pl.BlockSpec((seq, w), lambda b, j: (b, 0)),
            pl.BlockSpec((seq * H_A, DV_A), lambda b, j: (b, 0)),
            pl.BlockSpec((None, None, past, w), lambda b, j: (b, layer, 0, 0)),
            pl.BlockSpec((None, None, past, w), lambda b, j: (b, layer, 0, 0)),
            tab_q, tab_q, tab_q, tab_k, tab_k, tab_k,
            pl.BlockSpec(dl.shape, lambda b, j: (0, 0)),
            pl.BlockSpec(sg.shape, lambda b, j: (0, 0)),
        ],
        out_specs=pl.BlockSpec((tq, w), lambda b, j: (b * nq + j, 0)),
        out_shape=jax.ShapeDtypeStruct((t, w), F32),
        scratch_shapes=[pltpu.VMEM((seq + past, w), BF16), pltpu.VMEM((seq + past, 2 * w), BF16)],
        compiler_params=_params(("arbitrary", "arbitrary")),
        name="attn_lat",
    )(q, k, v, cache_k, cache_v, cos, slo, shi, cos, slo, shi, dl, sg)


def _tile_aux():
    rows = lax.broadcasted_iota(jnp.int32, (TILE, 1), 0)
    lane = lax.broadcasted_iota(jnp.int32, (1, W_B), 1)
    head_masks = [(lane // DK_B) == h for h in range(H_B)]
    first_of_pair = lax.broadcasted_iota(jnp.int32, (1, 2 * DK_B), 1) < DK_B
    xor = (lax.broadcasted_iota(jnp.int32, (HALF, HALF), 0)
           ^ lax.broadcasted_iota(jnp.int32, (HALF, HALF), 1))
    levels = [1 << i for i in range(TILE.bit_length() - 1)]
    right = {m: (rows & m) != 0 for m in levels}
    same_block = {m: xor < 2 * m for m in levels if 2 * m < HALF}
    return rows, head_masks, first_of_pair, right, same_block


def _hgrn_lower_bound(lb_ref, layer, direction, depth):
    xs = [lb_ref[2 * j + direction:2 * j + direction + 1, :] for j in range(depth)]
    m = functools.reduce(jnp.maximum, xs)
    es = [jnp.exp(x - m) for x in xs]
    return sum(es[1:layer + 1]) / sum(es)


def _hgrn_gates(fpre, lbd):
    t = jnp.exp(-jnp.abs(fpre))
    log_sig = jnp.minimum(fpre, 0.0) - jnp.log(1.0 + t)
    r = 1.0 / (1.0 + t)
    sig_neg = jnp.where(fpre >= 0.0, t * r, r)
    if lbd is None:
        return log_sig * LOG2E, sig_neg
    a = jnp.log(lbd)
    b = jnp.log1p(-lbd) + log_sig
    logf = jnp.maximum(a, b) + jnp.log1p(jnp.exp(-jnp.abs(a - b)))
    return logf * LOG2E, (1.0 - lbd) * sig_neg


def _chain_row(token):
    return SUBLANES * (token % CHAIN_LEN) + token // CHAIN_LEN


def _tile_prefix(g, g_scr, b_scr):
    g_c = _to_chains(g, g_scr)
    run = [g_c[0]]
    for j in range(1, CHAIN_LEN):
        run.append(run[-1] + g_c[j])
    sub = lax.broadcasted_iota(jnp.int32, (SUBLANES, 1), 0)
    incl = run[-1]
    k = 1
    while k < SUBLANES:
        incl = incl + jnp.where(sub >= k, pltpu.roll(incl, k, 0), 0.0)
        k *= 2
    offset = incl - run[-1]
    prefix = _from_chains([r + offset for r in run], b_scr)
    return prefix, incl[SUBLANES - 1:SUBLANES, :]


def _level_exponent(z, g, b_scr, m, reverse, right):
    if m == 1:
        return jnp.where(right, g, 0.0) if not reverse else jnp.where(right, 0.0, g)
    slabs = b_scr.shape[0]

    def ref_rows(token):
        row = _chain_row(token)
        return jnp.concatenate(
            [jnp.broadcast_to(b_scr[s, row:row + 1, :], (SUBLANES, LANES)) for s in range(slabs)],
            axis=1)

    groups = []
    for gq in range(TILE // SUBLANES):
        first = gq * SUBLANES
        zg = z[first:first + SUBLANES]
        if 2 * m >= SUBLANES:
            block = first // (2 * m) * (2 * m)
            ref = ref_rows(block + m - 1)
            if m >= SUBLANES:
                groups.append(zg - ref if first - block >= m else ref - zg)
                continue
        else:
            sub = lax.broadcasted_iota(jnp.int32, (SUBLANES, 1), 0)
            ref = jnp.where(sub < 2 * m, ref_rows(first + m - 1), ref_rows(first + 3 * m - 1))
        groups.append(-jnp.abs(zg - ref))
    return jnp.concatenate(groups, axis=0)


def _hgrn_tile(qh, kk, vv, g, reverse, aux, ones_bd, state_t, scratch):
    rows, head_masks, first_of_pair, right_rows, same_block = aux
    pair_w = 2 * DK_B
    g_scr, b_scr = scratch
    prefix, total = _tile_prefix(g, g_scr, b_scr)
    z = prefix if not reverse else prefix - g
    diag = [[None, None] for _ in range(H_B)]
    off = [None] * H_B
    m = 1
    while m < TILE:
        right = right_rows[m]
        x = jnp.exp2(_level_exponent(z, g, b_scr, m, reverse, right))
        q_rows = right if not reverse else jnp.logical_not(right)
        qz = jnp.where(q_rows, (qh * x).astype(BF16), 0.0)
        kz = jnp.where(q_rows, 0.0, (kk * x).astype(BF16))
        for h in range(H_B):
            ls = slice((h // 2) * pair_w, (h // 2 + 1) * pair_w)
            qm = jnp.where(first_of_pair, qz[:, ls], 0.0) if h % 2 == 0 else \
                jnp.where(first_of_pair, 0.0, qz[:, ls])
            kh = kz[:, ls]
            if m < HALF:
                for i in range(2):
                    rs = slice(i * HALF, (i + 1) * HALF)
                    pm = _dot_nt(qm[rs], kh[rs])
                    if m in same_block:
                        pm = jnp.where(same_block[m], pm, 0.0)
                    diag[h][i] = pm if diag[h][i] is None else diag[h][i] + pm
            else:
                lo, hi = slice(0, HALF), slice(HALF, TILE)
                off[h] = _dot_nt(qm[hi], kh[lo]) if not reverse else _dot_nt(qm[lo], kh[hi])
        m *= 2
    vb = vv.astype(BF16)
    top = bot = None
    for h in range(H_B):
        vh = jnp.where(head_masks[h], vb, 0.0)
        d0, d1, of = (a.astype(BF16) for a in (diag[h][0], diag[h][1], off[h]))
        if not reverse:
            t = _dot(d0, vh[0:HALF])
            b = _dot(jnp.concatenate([of, d1], axis=1), vh)
        else:
            t = _dot(jnp.concatenate([d0, of], axis=1), vh)
            b = _dot(d1, vh[HALF:])
        top = t if top is None else top + t
        bot = b if bot is None else bot + b
    out = _group_sum(qh * kk, ones_bd) * vv + jnp.concatenate([top, bot], axis=0)
    q_decay, k_decay = (z, total - z) if not reverse else (total - z, z)
    if state_t is not None:
        out = out + _dot_nt((qh * jnp.exp2(q_decay)).astype(BF16), state_t.astype(BF16))
    k_hat = (kk * jnp.exp2(k_decay)).astype(BF16)
    return out, k_hat, total


def _block_diag_mask():
    r = lax.broadcasted_iota(jnp.int32, (W_B, W_B), 0) // DK_B
    c = lax.broadcasted_iota(jnp.int32, (W_B, W_B), 1) // DV_B
    return r == c


def _conv(x, prev_row, next_rows, cw_ref, cb_ref, rows):
    x_m1 = jnp.where(rows == 0, prev_row, pltpu.roll(x, 1, 0))
    x_p1 = jnp.where(rows == TILE - 1, next_rows[0:1], pltpu.roll(x, TILE - 1, 0))
    x_p2 = jnp.where(rows == TILE - 2, next_rows[0:1],
                     jnp.where(rows == TILE - 1, next_rows[1:2], pltpu.roll(x, TILE - 2, 0)))
    y = cb_ref[...] + x_m1 * cw_ref[0:1, :]
    y = y + x * cw_ref[1:2, :]
    y = y + x_p1 * cw_ref[2:3, :]
    return y + x_p2 * cw_ref[3:4, :]


def _chain_address(g):
    per_chain = CHAIN_LEN // SUBLANES
    return SUBLANES * SUBLANES * (g % per_chain) + g // per_chain, SUBLANES


def _to_chains(x, scr):
    slabs = scr.shape[0]
    for s in range(slabs):
        for g in range(TILE // SUBLANES):
            start, stride = _chain_address(g)
            scr[s, pl.ds(start, SUBLANES, stride=stride), :] = (
                x[g * SUBLANES:(g + 1) * SUBLANES, s * LANES:(s + 1) * LANES])
    return [jnp.concatenate([scr[s, j * SUBLANES:(j + 1) * SUBLANES, :] for s in range(slabs)],
                            axis=1) for j in range(CHAIN_LEN)]


def _from_chains(chains, scr):
    slabs = scr.shape[0]
    for s in range(slabs):
        for j, c in enumerate(chains):
            scr[s, j * SUBLANES:(j + 1) * SUBLANES, :] = c[:, s * LANES:(s + 1) * LANES]
    groups = []
    for g in range(TILE // SUBLANES):
        start, stride = _chain_address(g)
        groups.append(jnp.concatenate(
            [scr[s, pl.ds(start, SUBLANES, stride=stride), :] for s in range(slabs)], axis=1))
    return jnp.concatenate(groups, axis=0)


def _rglru_tile(xc, r_pre, i_pre, lam_row, reverse, h_in, scratch):
    a_scr, u_scr, h_scr = scratch
    r = _sigmoid(r_pre)
    i = _sigmoid(i_pre)
    log_a = -RG_C * r * _softplus(-lam_row)
    a = jnp.exp(log_a)
    u = jnp.sqrt(jnp.tanh(-log_a) * (a * a + 1.0)) * (i * xc)
    a_c = _to_chains(a, a_scr)
    u_c = _to_chains(u, u_scr)
    order = list(reversed(range(CHAIN_LEN))) if reverse else list(range(CHAIN_LEN))
    h_c = [None] * CHAIN_LEN
    p_c = [None] * CHAIN_LEN
    h_c[order[0]], p_c[order[0]] = u_c[order[0]], a_c[order[0]]
    for prev, j in zip(order[:-1], order[1:]):
        h_c[j] = a_c[j] * h_c[prev] + u_c[j]
        p_c[j] = a_c[j] * p_c[prev]
    p_tot, h_tot = p_c[order[-1]], h_c[order[-1]]
    sub = lax.broadcasted_iota(jnp.int32, (SUBLANES, 1), 0)
    k = 1
    while k < SUBLANES:
        if not reverse:
            valid = sub >= k
            p_s, h_s = pltpu.roll(p_tot, k, 0), pltpu.roll(h_tot, k, 0)
        else:
            valid = sub < SUBLANES - k
            p_s, h_s = pltpu.roll(p_tot, SUBLANES - k, 0), pltpu.roll(h_tot, SUBLANES - k, 0)
        h_tot = h_tot + p_tot * jnp.where(valid, h_s, 0.0)
        p_tot = p_tot * jnp.where(valid, p_s, 1.0)
        k *= 2
    if h_in is not None:
        h_tot = h_tot + p_tot * h_in
    if not reverse:
        carry = jnp.where(sub == 0, 0.0 if h_in is None else h_in, pltpu.roll(h_tot, 1, 0))
        h_out = h_tot[SUBLANES - 1:SUBLANES, :]
    else:
        carry = jnp.where(sub == SUBLANES - 1, 0.0 if h_in is None else h_in,
                          pltpu.roll(h_tot, SUBLANES - 1, 0))
        h_out = h_tot[0:1, :]
    h_c = [h + p * carry for h, p in zip(h_c, p_c)]
    return _from_chains(h_c, h_scr), h_out


def _rec_inputs(refs):
    qb, ff, fb, ib, gb, xcol, gcol = [r[...] for r in refs]
    return _silu(qb), ff, fb, ib, gb, xcol, gcol


def _hgrn_out(o_sum, gb, onorm_gain, ones_bd):
    ms = _group_sum(o_sum * o_sum, ones_bd) * (1.0 / DV_B)
    return o_sum * lax.rsqrt(ms + EPS) * onorm_gain * _silu(gb)


def _rec_ctx_kernel(*refs, layer, depth, n_cast):
    refs = refs[2 if layer else 0:]
    (q_ref, ff_ref, fb_ref, i_ref, g_ref, x_ref, gc_ref, lb_ref, on_ref, cw_ref, cb_ref, wg_ref,
     bg_ref, lam_ref, ones_ref) = refs[:15]
    ob_ref, oc_ref, st_ref, hs_ref = refs[15 + n_cast:19 + n_cast]
    scan_scratch = refs[19 + 2 * n_cast:]
    _cast_slabs(refs[15:15 + n_cast], refs[19 + n_cast:19 + 2 * n_cast])
    if layer == 0 and depth > 1:
        st_ref[0, 1:] = jnp.zeros((depth - 1,) + st_ref.shape[2:], F32)
        hs_ref[0, 1:] = jnp.zeros((depth - 1,) + hs_ref.shape[2:], F32)
    aux = _tile_aux()
    rows = aux[0]
    ones_bd = ones_ref[...]
    qh, ff, fb, ib, gb, xcol, gcol = _rec_inputs((q_ref, ff_ref, fb_ref, i_ref, g_ref, x_ref, gc_ref))
    o_sum = None
    for direction, fpre in enumerate((ff, fb)):
        lbd = None if layer == 0 else _hgrn_lower_bound(lb_ref, layer, direction, depth)
        g, kk = _hgrn_gates(fpre, lbd)
        o, k_hat, _ = _hgrn_tile(qh, kk, ib, g, direction == 1, aux, ones_bd, None,
                                 scan_scratch[3 + 2 * direction:5 + 2 * direction])
        o_sum = o if o_sum is None else o_sum + o
        full = _dot_tn(k_hat, ib.astype(BF16))
        for h in range(H_B):
            blk = full[h * DK_B:(h + 1) * DK_B, (h // 2) * 2 * DV_B:(h // 2 + 1) * 2 * DV_B]
            if h % 2:
                blk = pltpu.roll(blk, DV_B, 1)
            st_ref[0, 0, direction, h] = blk[:, 0:DV_B]
    ob_ref[...] = _hgrn_out(o_sum, gb, on_ref[...], ones_bd)

    zero_row = jnp.zeros((1, W_C), F32)
    xc = _conv(xcol, zero_row, jnp.zeros((2, W_C), F32), cw_ref, cb_ref, rows)
    gates = _dot(xc.astype(BF16), wg_ref[...]) + bg_ref[...]
    y_f, h_f = _rglru_tile(xc, gates[:, 0:W_C], gates[:, W_C:2 * W_C], lam_ref[0:1, :], False,
                           None, scan_scratch[0:3])
    y_b, h_b = _rglru_tile(xc, gates[:, 2 * W_C:3 * W_C], gates[:, 3 * W_C:], lam_ref[1:2, :], True,
                           None, scan_scratch[0:3])
    oc_ref[...] = (y_f + y_b) * _gelu_tanh(gcol)
    hs_ref[0, 0, 0:1, :] = h_f
    hs_ref[0, 0, 1:2, :] = h_b


def _scan_scratch(hgrn_directions):
    return [pltpu.VMEM((W_C // LANES, TILE, LANES), F32) for _ in range(3 + 2 * hgrn_directions)]


def _col_spec(col, row_map):
    return pl.BlockSpec((TILE, W_B), lambda *idx: (row_map(*idx), col))


COL_QB, COL_FF, COL_FB, COL_IB, COL_GB, COL_XC, COL_GC = range(7)
_REC_COLS = (COL_QB, COL_FF, COL_FB, COL_IB, COL_GB, COL_XC, COL_GC)


def _full_spec(a):
    nd = a.ndim
    return pl.BlockSpec(a.shape, lambda *idx: (0,) * nd)


def _recurrence_ctx(proj, layer, depth, consts, states, cast=()):
    t = proj.shape[0]
    n = t // TILE
    assert bool(states) == (layer > 0) and len(consts) == 8
    layers, at = (1, layer) if states else (depth, 0)
    cast_in, cast_out, cast_shape = _cast_specs(cast, layer, n)
    return pl.pallas_call(
        functools.partial(_rec_ctx_kernel, layer=layer, depth=depth, n_cast=len(cast)),
        grid=(n,),
        in_specs=([pl.BlockSpec(memory_space=pl.ANY) for _ in states]
                  + [_col_spec(c, lambda b: b) for c in _REC_COLS] + [_full_spec(a) for a in consts]
                  + cast_in),
        out_specs=[
            pl.BlockSpec((TILE, W_B), lambda b: (b, 0)),
            pl.BlockSpec((TILE, W_C), lambda b: (b, 0)),
            pl.BlockSpec((1, layers, 2, H_B, DK_B, DV_B), lambda b: (b, at, 0, 0, 0, 0)),
            pl.BlockSpec((1, layers, 2, W_C), lambda b: (b, at, 0, 0)),
        ] + cast_out,
        out_shape=[
            jax.ShapeDtypeStruct((t, W_B), F32),
            jax.ShapeDtypeStruct((t, W_C), F32),
            jax.ShapeDtypeStruct((n, depth, 2, H_B, DK_B, DV_B), F32),
            jax.ShapeDtypeStruct((n, depth, 2, W_C), F32),
        ] + cast_shape,
        input_output_aliases={i: 2 + i for i in range(len(states))},
        scratch_shapes=_scan_scratch(2),
        compiler_params=_params(("arbitrary",)),
        name="rec_ctx",
    )(*states, *([proj] * len(_REC_COLS)), *consts, *cast)


def _rec_lat_kernel(q_ref, f_ref, i_ref, g_ref, x_ref, xp_ref, xn_ref, gc_ref, s0_ref, h0_ref,
                    *rest, layer, depth, direction, n_tiles):
    if direction == 0:
        (lb_ref, on_ref, cw_ref, cb_ref, wg_ref, bg_ref, lam_ref, ones_ref,
         o_ref, y_ref, state, hstate, *scan_scratch) = rest
    else:
        (of_ref, yf_ref, lb_ref, on_ref, cw_ref, cb_ref, wg_ref, bg_ref, lam_ref, ones_ref,
         o_ref, y_ref, state, hstate, *scan_scratch) = rest
    step = pl.program_id(1)
    tile = step if direction == 0 else n_tiles - 1 - step
    reverse = direction == 1

    @pl.when(step == 0)
    def _():
        state[...] = s0_ref[0, 0]
        hstate[...] = h0_ref[0, 0]

    aux = _tile_aux()
    rows = aux[0]
    ones_bd = ones_ref[...]
    qh = _silu(q_ref[...])
    ib = i_ref[...]
    lbd = None if layer == 0 else _hgrn_lower_bound(lb_ref, layer, direction, depth)
    g, kk = _hgrn_gates(f_ref[...], lbd)
    o, k_hat, total = _hgrn_tile(qh, kk, ib, g, reverse, aux, ones_bd, state[...],
                                 scan_scratch[3:5])
    upd = jnp.where(_block_diag_mask(), _dot_tn(ib.astype(BF16), k_hat), 0.0)
    state[...] = state[...] * jnp.exp2(total) + upd

    prev_row = jnp.where(tile == 0, 0.0, xp_ref[TILE - 1:TILE, :])
    next_rows = jnp.where(tile == n_tiles - 1, 0.0, xn_ref[0:2, :])
    xc = _conv(x_ref[...], prev_row, next_rows, cw_ref, cb_ref, rows)
    gates = _dot(xc.astype(BF16), wg_ref[...]) + bg_ref[...]
    c0 = 2 * W_C * direction
    y, h_next = _rglru_tile(xc, gates[:, c0:c0 + W_C], gates[:, c0 + W_C:c0 + 2 * W_C],
                            lam_ref[direction:direction + 1, :], reverse, hstate[...],
                            scan_scratch[0:3])
    hstate[...] = h_next

    if direction == 0:
        o_ref[...] = o
        y_ref[...] = y
    else:
        o_ref[...] = _hgrn_out(of_ref[...] + o, g_ref[...], on_ref[...], ones_bd)
        y_ref[...] = (yf_ref[...] + y) * _gelu_tanh(gc_ref[...])


def _recurrence_lat(proj, n_batch, layer, depth, direction, state0, h0, prev, consts):
    t = proj.shape[0]
    n_tiles = t // (TILE * n_batch)

    def tile_of(b, s):
        return s if direction == 0 else n_tiles - 1 - s

    def row(b, s):
        return b * n_tiles + tile_of(b, s)

    def row_prev(b, s):
        return b * n_tiles + jnp.maximum(tile_of(b, s) - 1, 0)

    def row_next(b, s):
        return b * n_tiles + jnp.minimum(tile_of(b, s) + 1, n_tiles - 1)

    f_col = COL_FF if direction == 0 else COL_FB
    in_specs = [_col_spec(COL_QB, row), _col_spec(f_col, row), _col_spec(COL_IB, row),
                _col_spec(COL_GB, row), _col_spec(COL_XC, row), _col_spec(COL_XC, row_prev),
                _col_spec(COL_XC, row_next), _col_spec(COL_GC, row),
                pl.BlockSpec((1, 1, W_B, W_B), lambda b, s: (b, direction, 0, 0)),
                pl.BlockSpec((1, 1, 1, W_C), lambda b, s: (b, direction, 0, 0))]
    args = [proj] * 8 + [state0, h0]
    if direction == 1:
        in_specs += [pl.BlockSpec((TILE, W_B), lambda b, s: (row(b, s), 0)),
                     pl.BlockSpec((TILE, W_C), lambda b, s: (row(b, s), 0))]
        args += list(prev)
    in_specs += [_full_spec(a) for a in consts]
    args += list(consts)
    return pl.pallas_call(
        functools.partial(_rec_lat_kernel, layer=layer, depth=depth, direction=direction,
                          n_tiles=n_tiles),
        grid=(n_batch, n_tiles),
        in_specs=in_specs,
        out_specs=[pl.BlockSpec((TILE, W_B), lambda b, s: (row(b, s), 0)),
                   pl.BlockSpec((TILE, W_C), lambda b, s: (row(b, s), 0))],
        out_shape=[jax.ShapeDtypeStruct((t, W_B), F32), jax.ShapeDtypeStruct((t, W_C), F32)],
        scratch_shapes=[pltpu.VMEM((W_B, W_B), F32), pltpu.VMEM((1, W_C), F32)] + _scan_scratch(1),
        compiler_params=_params(("arbitrary", "arbitrary")),
        name="rec_lat_fwd" if direction == 0 else "rec_lat_bwd",
    )(*args)


def _out_kernel(x_ref, oa_ref, ob_ref, oc_ref, mod_ref, n2_ref, wo_ref, w1_ref, w2_ref, *refs):
    n_cast = len(refs) // 2
    y_ref = refs[n_cast]
    _cast_slabs(refs[:n_cast], refs[n_cast + 1:])
    x = x_ref[...]
    d = x.shape[-1]
    g1 = mod_ref[0, :, 2 * d:3 * d]
    sh2 = mod_ref[0, :, 3 * d:4 * d]
    sc2 = mod_ref[0, :, 4 * d:5 * d]
    g2 = mod_ref[0, :, 5 * d:6 * d]
    wa = H_A * DV_A
    mix = _dot(oa_ref[...].astype(BF16), wo_ref[0:wa, :])
    mix = mix + _dot(ob_ref[...].astype(BF16), wo_ref[wa:wa + W_B, :])
    mix = mix + _dot(oc_ref[...].astype(BF16), wo_ref[wa + W_B:, :])
    x1 = x + g1 * mix
    h = (_rms_rows(x1, n2_ref[...]) * (1.0 + sc2) + sh2).astype(BF16)
    acc = jnp.zeros_like(x)
    for c in range(w1_ref.shape[1] // D_FF_CHUNK):
        cs = slice(c * D_FF_CHUNK, (c + 1) * D_FF_CHUNK)
        f = jnp.maximum(_dot(h, w1_ref[:, cs]), 0.0)
        acc = acc + _dot((f * f).astype(BF16), w2_ref[cs, :])
    y_ref[...] = x1 + g2 * acc


def _out_mlp(x, oa, ob, oc, mod, tokens_per_mod, n2, w_out, w_ff1, w_ff2, cast=(), cast_layer=0):
    t, d = x.shape
    steps = t // ROW_TILE

    def resident(w):
        return pl.BlockSpec((None,) + w.shape[1:], lambda i: (0, 0, 0),
                            pipeline_mode=pl.Buffered(1))

    cast_in, cast_out, cast_shape = _cast_specs(cast, cast_layer, steps)
    return pl.pallas_call(
        _out_kernel,
        grid=(steps,),
        in_specs=[
            pl.BlockSpec((ROW_TILE, d), lambda i: (i, 0)),
            pl.BlockSpec((ROW_TILE, oa.shape[1]), lambda i: (i, 0)),
            pl.BlockSpec((ROW_TILE, ob.shape[1]), lambda i: (i, 0)),
            pl.BlockSpec((ROW_TILE, oc.shape[1]), lambda i: (i, 0)),
            pl.BlockSpec((1, 1, N_MOD * d), lambda i: (i * ROW_TILE // tokens_per_mod, 0, 0)),
            pl.BlockSpec((1, d), lambda i: (0, 0)),
            resident(w_out),
            resident(w_ff1),
            resident(w_ff2),
        ] + cast_in,
        out_specs=[pl.BlockSpec((ROW_TILE, d), lambda i: (i, 0))] + cast_out,
        out_shape=[jax.ShapeDtypeStruct((t, d), F32)] + cast_shape,
        compiler_params=_params(("arbitrary",)),
        name="out_mlp",
    )(x, oa, ob, oc, mod, n2, w_out, w_ff1, w_ff2, *cast)


def _ones_block_diag(n, group):
    idx = np.arange(n) // group
    return jnp.asarray((idx[:, None] == idx[None, :]).astype(np.float32), dtype=BF16)


def _rope_tables(seq):
    nf = DQK_A // 4
    inv = ROPE_BASE ** (-jnp.arange(nf, dtype=F32) / nf)
    pos = jnp.arange(seq)
    ang_r = (pos // GRID_W).astype(F32)[:, None] * inv
    ang_c = (pos % GRID_W).astype(F32)[:, None] * inv
    z = jnp.zeros_like(ang_r)
    cos = jnp.concatenate([jnp.cos(ang_r)] * 2 + [jnp.cos(ang_c)] * 2, axis=-1)
    sin_lo = jnp.concatenate([-jnp.sin(ang_r), z, -jnp.sin(ang_c), z], axis=-1)
    sin_hi = jnp.concatenate([z, jnp.sin(ang_r), z, jnp.sin(ang_c)], axis=-1)
    return tuple(jnp.tile(a, (1, 2)) for a in (cos, sin_lo, sin_hi))


def _gate_weights(rg_w_l):
    eye = jnp.eye(H_C, dtype=rg_w_l.dtype)
    w = jnp.einsum('dghij,hk->dghikj', rg_w_l, eye)
    w = w.reshape(2, 2, W_C, W_C)
    return jnp.transpose(w, (2, 0, 1, 3)).reshape(W_C, 4 * W_C).astype(BF16)


def _state_block_diag_t(s):
    eye = jnp.eye(H_B, dtype=s.dtype)
    w = jnp.einsum('...hdv,hk->...hvkd', s, eye)
    return w.reshape(s.shape[:-3] + (W_B, W_B))


def kernel(x_prompt, x_sample, cache_k, cache_v, state_hgrn, state_rglru, c, c_ctx, w_mod, b_mod,
           norm1, norm2, w_in, w_out, qk_norm, diff_lambda, subln, hgrn_lb, hgrn_onorm, conv_w,
           conv_b, rg_w, rg_b, rg_lambda, w_ff1, w_ff2):
    batch, seq, d = x_prompt.shape
    dec_batch, dec_seq, _ = x_sample.shape
    depth = w_in.shape[0]
    past = cache_k.shape[2]
    wq = H_A * 2 * DQK_A

    c8 = jnp.concatenate([c_ctx[None], c, jnp.zeros((8 - 1 - dec_batch, d), F32)], axis=0)
    mod_all = _modulation(c8, w_mod, b_mod)

    assert DQK_A == DV_B and seq == TILE
    ones_b = _ones_block_diag(W_B, DV_B)
    rope_tabs = _rope_tables(dec_seq)
    ck = cache_k.reshape(dec_batch, depth, past, wq)
    cv = cache_v.reshape(dec_batch, depth, past, H_A * DV_A)
    lb2 = hgrn_lb.reshape(depth * 2, W_B)

    w_f32 = (w_in, w_out, w_ff1, w_ff2)
    w_next = [w_in[0:1].astype(BF16)]

    yp = x_prompt.reshape(batch * seq, d)
    ys = x_sample.reshape(dec_batch * dec_seq, d)
    caches, states = (), ()
    for l in range(depth):
        lam_init = 0.8 - 0.6 * math.exp(-0.3 * l)
        mod_ctx = mod_all[l, 0:1][:, None, :]
        mod_lat = mod_all[l, 1:1 + dec_batch][:, None, :]
        n1 = norm1[l][None]
        n2 = norm2[l][None]
        gq = jnp.tile(qk_norm[l, 0], H_A * 2)[None] * (DQK_A ** -0.5 * LOG2E)
        gk = jnp.tile(qk_norm[l, 1], H_A * 2)[None]
        dl = diff_lambda[l]
        sg = subln[l][None]
        consts = (lb2, jnp.tile(hgrn_onorm[l], H_B)[None], conv_w[l], conv_b[l][None],
                  _gate_weights(rg_w[l]), rg_b[l].reshape(1, 4 * W_C), rg_lambda[l], ones_b)

        w_in_b = w_next[0]

        q, k, v_all, rest, kt_all = _in_projection(yp, mod_ctx, batch * seq, n1, w_in_b, 0,
                                                   gq, gk, caches=caches)
        caches = (kt_all, v_all)
        oa = _attention_ctx(q, k, v_all, l, seq, dl, sg, lam_init)
        ob, oc, st, hs, *w_rest = _recurrence_ctx(rest, l, depth, consts, states,
                                                  cast=w_f32[1:] if l == 0 else ())
        states = (st, hs)
        w_out_b, w1_b, w2_b = w_rest if l == 0 else w_next[1:]
        yp, *w_next = _out_mlp(yp, oa, ob, oc, mod_ctx, batch * seq, n2, w_out_b, w1_b, w2_b,
                               cast=w_f32 if l + 1 < depth else (), cast_layer=l + 1)

        q, k, v, rest = _in_projection(ys, mod_lat, dec_seq, n1, w_in_b, 0, gq, gk)
        oa = _attention_lat(q, k, v, dec_batch, ck, cv, l, rope_tabs, dl, sg, lam_init)
        s0 = _state_block_diag_t(state_hgrn[:, l])
        h0 = state_rglru[:, l][:, :, None, :]
        fwd = _recurrence_lat(rest, dec_batch, l, depth, 0, s0, h0, None, consts)
        ob, oc = _recurrence_lat(rest, dec_batch, l, depth, 1, s0, h0, fwd, consts)
        ys, = _out_mlp(ys, oa, ob, oc, mod_lat, dec_seq, n2, w_out_b, w1_b, w2_b)

    kt_all, v_all = caches
    new_k = jnp.transpose(kt_all.reshape(batch, depth, H_A, 2, DQK_A, seq), (0, 1, 5, 2, 3, 4))
    new_v = v_all.reshape(batch, depth, seq, H_A, DV_A)
    return (yp.reshape(batch, seq, d), ys.reshape(dec_batch, dec_seq, d), new_k, new_v,
            states[0], states[1])
```

```python
import functools
import math

import numpy as np
import jax
import jax.numpy as jnp
from jax import lax
from jax.experimental import pallas as pl
from jax.experimental.pallas import tpu as pltpu

F32 = jnp.float32
BF16 = jnp.bfloat16

GRID_W = 64
H_A, DQK_A, DV_A = 4, 64, 128
H_B, DK_B, DV_B = 4, 64, 64
W_B = H_B * DV_B
H_C, W_C = 4, 256
CONV_K = 4
RG_C = 8.0
ROPE_BASE = 10000.0
EPS = 1e-6
N_MOD = 6
D_FF_CHUNK = 1024

TILE = 256
HALF = TILE // 2
ROW_TILE = 512
ATTN_SEQS_PER_STEP = 4
SUBLANES = 8
LANES = 128
CHAIN_LEN = TILE // SUBLANES
LOG2E = math.log2(math.e)
V7X_VMEM_BYTES = 64 * 1024 * 1024
VMEM_LIMIT = V7X_VMEM_BYTES - 8 * 1024 * 1024


def _dot(a, b):
    return jnp.dot(a, b, preferred_element_type=F32)


def _dot_nt(a, b):
    return lax.dot_general(a, b, (((1,), (1,)), ((), ())), preferred_element_type=F32)


def _dot_tn(a, b):
    return lax.dot_general(a, b, (((0,), (0,)), ((), ())), preferred_element_type=F32)


def _group_sum(x, ones_bd):
    hi = x.astype(BF16)
    lo = (x - hi.astype(F32)).astype(BF16)
    return _dot(hi, ones_bd) + _dot(lo, ones_bd)


def _rms_rows(x, gain):
    return x * lax.rsqrt(jnp.mean(x * x, axis=-1, keepdims=True) + EPS) * gain


def _sigmoid(x):
    return 0.5 * jnp.tanh(0.5 * x) + 0.5


def _silu(x):
    return x * _sigmoid(x)


def _softplus(x):
    return jnp.maximum(x, 0.0) + jnp.log1p(jnp.exp(-jnp.abs(x)))


def _log_sigmoid(x):
    return jnp.minimum(x, 0.0) - jnp.log1p(jnp.exp(-jnp.abs(x)))


def _gelu_tanh(x):
    return x * (0.5 * (1.0 + jnp.tanh(math.sqrt(2.0 / math.pi) * (x + 0.044715 * (x * x * x)))))


def _params(semantics, flags=None):
    return pltpu.CompilerParams(dimension_semantics=semantics, vmem_limit_bytes=VMEM_LIMIT,
                                flags=flags)


def _cast_specs(cast, layer, steps):
    assert all(w.shape[1] % (steps * 2 * SUBLANES) == 0 for w in cast)
    slabs = [(w.shape[1] // steps, w.shape[2]) for w in cast]
    return ([pl.BlockSpec((None,) + s, lambda i: (layer, i, 0)) for s in slabs],
            [pl.BlockSpec((1,) + s, lambda i: (0, i, 0)) for s in slabs],
            [jax.ShapeDtypeStruct((1,) + w.shape[1:], BF16) for w in cast])


def _cast_slabs(srcs, dsts):
    for src, dst in zip(srcs, dsts):
        dst[0] = src[...].astype(BF16)


def _mod_kernel(c_ref, w_ref, b_ref, o_ref):
    c = c_ref[...]
    o_ref[0] = _dot(_silu(c).astype(BF16), w_ref[0].astype(BF16)) + b_ref[0]


def _modulation(c8, w_mod, b_mod):
    depth, d, n = w_mod.shape
    tn = d
    return pl.pallas_call(
        _mod_kernel,
        grid=(depth, n // tn),
        in_specs=[
            pl.BlockSpec((8, d), lambda l, j: (0, 0)),
            pl.BlockSpec((1, d, tn), lambda l, j: (l, 0, j)),
            pl.BlockSpec((1, 1, tn), lambda l, j: (l, 0, j)),
        ],
        out_specs=pl.BlockSpec((1, 8, tn), lambda l, j: (l, 0, j)),
        out_shape=jax.ShapeDtypeStruct((depth, 8, n), F32),
        compiler_params=_params(("arbitrary", "arbitrary")),
        name="mod",
    )(c8, w_mod, b_mod.reshape(depth, 1, n))


def _proj_kernel(x_ref, mod_ref, n1_ref, w_ref, gq_ref, gk_ref, *refs, layers_before):
    prev = refs[:2] if layers_before else ()
    q_ref, k_ref, v_ref, r_ref, *kt_ref = refs[len(prev):]
    x = x_ref[...]
    d = x.shape[-1]
    sh = mod_ref[0, :, 0:d]
    sc = mod_ref[0, :, d:2 * d]
    h = _rms_rows(x, n1_ref[...]) * (1.0 + sc) + sh
    p = _dot(h.astype(BF16), w_ref[...])
    wq = H_A * 2 * DQK_A

    first = lax.broadcasted_iota(jnp.int32, (1, LANES), 1) < DQK_A

    def qk_norm(z, gain):
        z2 = z * z
        parts = []
        for c in range(0, wq, LANES):
            slab = z2[:, c:c + LANES]
            lo = jnp.sum(jnp.where(first, slab, 0.0), axis=-1, keepdims=True)
            hi = jnp.sum(jnp.where(first, 0.0, slab), axis=-1, keepdims=True)
            parts.append(jnp.where(first, lo, hi))
        ms = jnp.concatenate(parts, axis=1) * (1.0 / DQK_A)
        return z * lax.rsqrt(ms + EPS) * gain

    wv = H_A * DV_A
    q_ref[...] = qk_norm(p[:, 0:wq], gq_ref[...])
    kn = qk_norm(p[:, wq:2 * wq], gk_ref[...])
    k_ref[...] = kn
    if kt_ref:
        for s in range(ROW_TILE // TILE):
            kt_ref[0][s, layers_before] = kn[s * TILE:(s + 1) * TILE].T
        if prev:
            ktp_ref, vp_ref = prev
            kt_ref[0][:, 0:layers_before] = ktp_ref[...]
            v_ref[:, 0:layers_before] = vp_ref[...]
    for s in range(ROW_TILE // TILE):
        v_dst = v_ref.at[s, layers_before] if kt_ref else v_ref.at[pl.ds(s * TILE * H_A, TILE * H_A)]
        for h in range(H_A):
            v_dst[pl.ds(h, TILE, stride=H_A), :] = (
                p[s * TILE:(s + 1) * TILE, 2 * wq + h * DV_A:2 * wq + (h + 1) * DV_A])
    r_ref[...] = p[:, 2 * wq + wv:]


def _in_projection(x, mod, tokens_per_mod, n1, w_in, layer, gq, gk, caches=None):
    t, d = x.shape
    layers_before = caches[0].shape[1] if caches else 0
    n_in = w_in.shape[2]
    wq = gq.shape[1]
    wv = H_A * DV_A
    widths = (wq, wq, wv, n_in - 2 * wq - wv)
    seqs = ROW_TILE // TILE
    out_specs = [pl.BlockSpec((ROW_TILE, w), lambda i: (i, 0)) for w in widths]
    out_shape = [jax.ShapeDtypeStruct((t, w), F32) for w in widths]
    out_specs[2] = pl.BlockSpec((ROW_TILE * H_A, DV_A), lambda i: (i, 0))
    out_shape[2] = jax.ShapeDtypeStruct((t * H_A, DV_A), F32)
    cache_specs = []
    if caches is not None:
        n_seq, nl = t // TILE, layers_before + 1
        out_specs[2] = pl.BlockSpec((seqs, nl, TILE * H_A, DV_A), lambda i: (i, 0, 0, 0))
        out_shape[2] = jax.ShapeDtypeStruct((n_seq, nl, TILE * H_A, DV_A), F32)
        out_specs.append(pl.BlockSpec((seqs, nl, wq, TILE), lambda i: (i, 0, 0, 0)))
        out_shape.append(jax.ShapeDtypeStruct((n_seq, nl, wq, TILE), F32))
        if caches:
            cache_specs = [
                pl.BlockSpec((seqs, layers_before, wq, TILE), lambda i: (i, 0, 0, 0)),
                pl.BlockSpec((seqs, layers_before, TILE * H_A, DV_A), lambda i: (i, 0, 0, 0))]
    return pl.pallas_call(
        functools.partial(_proj_kernel, layers_before=layers_before),
        grid=(t // ROW_TILE,),
        in_specs=[
            pl.BlockSpec((ROW_TILE, d), lambda i: (i, 0)),
            pl.BlockSpec((1, 1, N_MOD * d), lambda i: (i * ROW_TILE // tokens_per_mod, 0, 0)),
            pl.BlockSpec((1, d), lambda i: (0, 0)),
            pl.BlockSpec((None, d, n_in), lambda i: (layer, 0, 0)),
            pl.BlockSpec((1, wq), lambda i: (0, 0)),
            pl.BlockSpec((1, wq), lambda i: (0, 0)),
        ] + cache_specs,
        out_specs=out_specs,
        out_shape=out_shape,
        compiler_params=_params(("arbitrary",)),
        name="proj",
    )(x, mod, n1, w_in, gq, gk, *(caches or ()))


def _diff_lambda(dl_ref, lam_init):
    lv = dl_ref[...]
    a = jnp.sum(lv[0:1] * lv[1:2], axis=-1, keepdims=True)
    b = jnp.sum(lv[2:3] * lv[3:4], axis=-1, keepdims=True)
    return jnp.exp(a) - jnp.exp(b) + lam_init


def _diff_attention(q, k_of, vext_of, lam, sub_gain, lam_init, o_ref):
    hw = 2 * DQK_A
    first = lax.broadcasted_iota(jnp.int32, (1, hw), 1) < DQK_A
    for h in range(H_A):
        hs = slice(h * hw, (h + 1) * hw)
        qh = q[:, hs]
        kh = k_of(h)
        vext = vext_of(h)
        tq = qh.shape[0]
        q2 = jnp.concatenate([jnp.where(first, qh, 0.0), jnp.where(first, 0.0, qh)], axis=0)
        s = _dot_nt(q2.astype(BF16), kh)
        e = jnp.exp2(s - jnp.max(s, axis=-1, keepdims=True)).astype(BF16)
        oe = _dot(e, vext)
        o = oe[:, 0:DV_A] / oe[:, DV_A:]
        oh = o[0:tq] - lam * o[tq:]
        o_ref[:, hs] = _rms_rows(oh, sub_gain) * (1.0 - lam_init)


def _attn_ctx_kernel(q_ref, k_ref, v_ref, dl_ref, sg_ref, o_ref, *, lam_init, seq):
    lam = _diff_lambda(dl_ref, lam_init)
    ones = jnp.ones((seq, DV_A), BF16)
    hw = 2 * DQK_A
    for s in range(v_ref.shape[0]):
        rows = pl.ds(s * seq, seq)
        k = k_ref[rows, :].astype(BF16)

        def vext(h, s=s):
            return jnp.concatenate(
                [v_ref[s, pl.ds(h, seq, stride=H_A), :].astype(BF16), ones], axis=1)

        _diff_attention(q_ref[rows, :], lambda h, k=k: k[:, h * hw:(h + 1) * hw], vext, lam,
                        sg_ref[...], lam_init, o_ref.at[rows])


def _attention_ctx(q, k, v_all, layer, seq, dl, sg, lam_init):
    t = q.shape[0]
    w = H_A * DV_A
    per_step = ATTN_SEQS_PER_STEP
    return pl.pallas_call(
        functools.partial(_attn_ctx_kernel, lam_init=lam_init, seq=seq),
        grid=(t // (seq * per_step),),
        in_specs=[
            pl.BlockSpec((per_step * seq, w), lambda b: (b, 0)),
            pl.BlockSpec((per_step * seq, w), lambda b: (b, 0)),
            pl.BlockSpec((per_step, None, seq * H_A, DV_A), lambda b: (b, layer, 0, 0)),
            pl.BlockSpec(dl.shape, lambda b: (0, 0)),
            pl.BlockSpec(sg.shape, lambda b: (0, 0)),
        ],
        out_specs=pl.BlockSpec((per_step * seq, w), lambda b: (b, 0)),
        out_shape=jax.ShapeDtypeStruct((t, w), F32),
        compiler_params=_params(("arbitrary",)),
        name="attn_ctx",
    )(q, k, v_all, dl, sg)


def _rope(x, cos, sin_lo, sin_hi):
    return (x * cos + pltpu.roll(x, 2 * DQK_A - 16, 1) * sin_lo + pltpu.roll(x, 16, 1) * sin_hi)


def _attn_lat_kernel(q_ref, k_ref, v_ref, ck_ref, cv_ref, qcos_ref, qslo_ref, qshi_ref,
                     kcos_ref, kslo_ref, kshi_ref, dl_ref, sg_ref, o_ref, kall, vall, *, lam_init):
    hw = 2 * DQK_A
    n_new = k_ref.shape[0]

    @pl.when(pl.program_id(1) == 0)
    def _():
        kcos, kslo, kshi = kcos_ref[...], kslo_ref[...], kshi_ref[...]
        for h in range(H_A):
            hs = slice(h * hw, (h + 1) * hw)
            kall[0:n_new, hs] = _rope(k_ref[:, hs], kcos, kslo, kshi).astype(BF16)
        kall[n_new:, :] = ck_ref[...].astype(BF16)
        ones = jnp.ones((vall.shape[0], DV_A), BF16)
        for h in range(H_A):
            vs = slice(h * DV_A, (h + 1) * DV_A)
            vall[0:n_new, 2 * h * DV_A:(2 * h + 1) * DV_A] = (
                v_ref[pl.ds(h, n_new, stride=H_A), :].astype(BF16))
            vall[n_new:, 2 * h * DV_A:(2 * h + 1) * DV_A] = cv_ref[:, vs].astype(BF16)
            vall[:, (2 * h + 1) * DV_A:(2 * h + 2) * DV_A] = ones

    lam = _diff_lambda(dl_ref, lam_init)
    qcos, qslo, qshi = qcos_ref[...], qslo_ref[...], qshi_ref[...]
    q = jnp.concatenate(
        [_rope(q_ref[:, h * hw:(h + 1) * hw], qcos, qslo, qshi) for h in range(H_A)], axis=-1)
    _diff_attention(q, lambda h: kall[:, h * hw:(h + 1) * hw],
                    lambda h: vall[:, 2 * h * DV_A:(2 * h + 2) * DV_A], lam, sg_ref[...],
                    lam_init, o_ref)


def _attention_lat(q, k, v, n_batch, cache_k, cache_v, layer, rope_tabs, dl, sg, lam_init):
    t = q.shape[0]
    seq = t // n_batch
    past = cache_k.shape[2]
    w = H_A * DV_A
    hw = 2 * DQK_A
    nq = seq // TILE
    cos, slo, shi = rope_tabs
    tab_q = pl.BlockSpec((TILE, hw), lambda b, j: (j, 0))
    tab_k = pl.BlockSpec((seq, hw), lambda b, j: (0, 0))
    return pl.pallas_call(
        functools.partial(_attn_lat_kernel, lam_init=lam_init),
        grid=(n_batch, nq),
        in_specs=[
            pl.BlockSpec((TILE, w), lambda b, j: (b * nq + j, 0)),
            pl.BlockSpec((seq, w), lambda b, j: (b, 0)),
            pl.BlockSpec((seq * H_A, DV_A), lambda b, j: (b, 0)),
            pl.BlockSpec((None, None, past, w), lambda b, j: (b, layer, 0, 0)),
            pl.BlockSpec((None, None, past, w), lambda b, j: (b, layer, 0, 0)),
            tab_q, tab_q, tab_q, tab_k, tab_k, tab_k,
            pl.BlockSpec(dl.shape, lambda b, j: (0, 0)),
            pl.BlockSpec(sg.shape, lambda b, j: (0, 0)),
        ],
        out_specs=pl.BlockSpec((TILE, w), lambda b, j: (b * nq + j, 0)),
        out_shape=jax.ShapeDtypeStruct((t, w), F32),
        scratch_shapes=[pltpu.VMEM((seq + past, w), BF16), pltpu.VMEM((seq + past, 2 * w), BF16)],
        compiler_params=_params(("arbitrary", "arbitrary")),
        name="attn_lat",
    )(q, k, v, cache_k, cache_v, cos, slo, shi, cos, slo, shi, dl, sg)


def _tile_aux():
    rows = lax.broadcasted_iota(jnp.int32, (TILE, 1), 0)
    lane = lax.broadcasted_iota(jnp.int32, (1, W_B), 1)
    head_masks = [(lane // DK_B) == h for h in range(H_B)]
    first_of_pair = lax.broadcasted_iota(jnp.int32, (1, 2 * DK_B), 1) < DK_B
    xor = (lax.broadcasted_iota(jnp.int32, (HALF, HALF), 0)
           ^ lax.broadcasted_iota(jnp.int32, (HALF, HALF), 1))
    levels = [1 << i for i in range(TILE.bit_length() - 1)]
    right = {m: (rows & m) != 0 for m in levels}
    same_block = {m: xor < 2 * m for m in levels if 2 * m < HALF}
    return rows, head_masks, first_of_pair, right, same_block


def _hgrn_lower_bound(lb_ref, layer, direction, depth):
    xs = [lb_ref[2 * j + direction:2 * j + direction + 1, :] for j in range(depth)]
    m = functools.reduce(jnp.maximum, xs)
    es = [jnp.exp(x - m) for x in xs]
    return sum(es[1:layer + 1]) / sum(es)


def _hgrn_gates(fpre, lbd):
    t = jnp.exp(-jnp.abs(fpre))
    log_sig = jnp.minimum(fpre, 0.0) - jnp.log(1.0 + t)
    r = 1.0 / (1.0 + t)
    sig_neg = jnp.where(fpre >= 0.0, t * r, r)
    if lbd is None:
        return log_sig * LOG2E, sig_neg
    a = jnp.log(lbd)
    b = jnp.log1p(-lbd) + log_sig
    logf = jnp.maximum(a, b) + jnp.log1p(jnp.exp(-jnp.abs(a - b)))
    return logf * LOG2E, (1.0 - lbd) * sig_neg


def _chain_row(token):
    return SUBLANES * (token % CHAIN_LEN) + token // CHAIN_LEN


def _tile_prefix(g, g_scr, b_scr):
    g_c = _to_chains(g, g_scr)
    run = [g_c[0]]
    for j in range(1, CHAIN_LEN):
        run.append(run[-1] + g_c[j])
    sub = lax.broadcasted_iota(jnp.int32, (SUBLANES, 1), 0)
    incl = run[-1]
    k = 1
    while k < SUBLANES:
        incl = incl + jnp.where(sub >= k, pltpu.roll(incl, k, 0), 0.0)
        k *= 2
    offset = incl - run[-1]
    prefix = _from_chains([r + offset for r in run], b_scr)
    return prefix, incl[SUBLANES - 1:SUBLANES, :]


def _level_exponent(z, g, b_scr, m, reverse, right):
    if m == 1:
        return jnp.where(right, g, 0.0) if not reverse else jnp.where(right, 0.0, g)
    slabs = b_scr.shape[0]

    def ref_rows(token):
        row = _chain_row(token)
        return jnp.concatenate(
            [jnp.broadcast_to(b_scr[s, row:row + 1, :], (SUBLANES, LANES)) for s in range(slabs)],
            axis=1)

    groups = []
    for gq in range(TILE // SUBLANES):
        first = gq * SUBLANES
        zg = z[first:first + SUBLANES]
        if 2 * m >= SUBLANES:
            block = first // (2 * m) * (2 * m)
            ref = ref_rows(block + m - 1)
            if m >= SUBLANES:
                groups.append(zg - ref if first - block >= m else ref - zg)
                continue
        else:
            sub = lax.broadcasted_iota(jnp.int32, (SUBLANES, 1), 0)
            ref = jnp.where(sub < 2 * m, ref_rows(first + m - 1), ref_rows(first + 3 * m - 1))
        groups.append(-jnp.abs(zg - ref))
    return jnp.concatenate(groups, axis=0)


def _hgrn_tile(qh, kk, vv, g, reverse, aux, ones_bd, state_t, scratch):
    rows, head_masks, first_of_pair, right_rows, same_block = aux
    pair_w = 2 * DK_B
    g_scr, b_scr = scratch
    prefix, total = _tile_prefix(g, g_scr, b_scr)
    z = prefix if not reverse else prefix - g
    diag = [[None, None] for _ in range(H_B)]
    off = [None] * H_B
    m = 1
    while m < TILE:
        right = right_rows[m]
        x = jnp.exp2(_level_exponent(z, g, b_scr, m, reverse, right))
        q_rows = right if not reverse else jnp.logical_not(right)
        qz = jnp.where(q_rows, (qh * x).astype(BF16), 0.0)
        kz = jnp.where(q_rows, 0.0, (kk * x).astype(BF16))
        for h in range(H_B):
            ls = slice((h // 2) * pair_w, (h // 2 + 1) * pair_w)
            qm = jnp.where(first_of_pair, qz[:, ls], 0.0) if h % 2 == 0 else \
                jnp.where(first_of_pair, 0.0, qz[:, ls])
            kh = kz[:, ls]
            if m < HALF:
                for i in range(2):
                    rs = slice(i * HALF, (i + 1) * HALF)
                    pm = _dot_nt(qm[rs], kh[rs])
                    if m in same_block:
                        pm = jnp.where(same_block[m], pm, 0.0)
                    diag[h][i] = pm if diag[h][i] is None else diag[h][i] + pm
            else:
                lo, hi = slice(0, HALF), slice(HALF, TILE)
                off[h] = _dot_nt(qm[hi], kh[lo]) if not reverse else _dot_nt(qm[lo], kh[hi])
        m *= 2
    vb = vv.astype(BF16)
    top = bot = None
    for h in range(H_B):
        vh = jnp.where(head_masks[h], vb, 0.0)
        d0, d1, of = (a.astype(BF16) for a in (diag[h][0], diag[h][1], off[h]))
        if not reverse:
            t = _dot(d0, vh[0:HALF])
            b = _dot(jnp.concatenate([of, d1], axis=1), vh)
        else:
            t = _dot(jnp.concatenate([d0, of], axis=1), vh)
            b = _dot(d1, vh[HALF:])
        top = t if top is None else top + t
        bot = b if bot is None else bot + b
    out = _group_sum(qh * kk, ones_bd) * vv + jnp.concatenate([top, bot], axis=0)
    q_decay, k_decay = (z, total - z) if not reverse else (total - z, z)
    if state_t is not None:
        out = out + _dot_nt((qh * jnp.exp2(q_decay)).astype(BF16), state_t.astype(BF16))
    k_hat = (kk * jnp.exp2(k_decay)).astype(BF16)
    return out, k_hat, total


def _block_diag_mask():
    r = lax.broadcasted_iota(jnp.int32, (W_B, W_B), 0) // DK_B
    c = lax.broadcasted_iota(jnp.int32, (W_B, W_B), 1) // DV_B
    return r == c


def _conv(x, prev_row, next_rows, cw_ref, cb_ref, rows):
    x_m1 = jnp.where(rows == 0, prev_row, pltpu.roll(x, 1, 0))
    x_p1 = jnp.where(rows == TILE - 1, next_rows[0:1], pltpu.roll(x, TILE - 1, 0))
    x_p2 = jnp.where(rows == TILE - 2, next_rows[0:1],
                     jnp.where(rows == TILE - 1, next_rows[1:2], pltpu.roll(x, TILE - 2, 0)))
    y = cb_ref[...] + x_m1 * cw_ref[0:1, :]
    y = y + x * cw_ref[1:2, :]
    y = y + x_p1 * cw_ref[2:3, :]
    return y + x_p2 * cw_ref[3:4, :]


def _chain_address(g):
    per_chain = CHAIN_LEN // SUBLANES
    return SUBLANES * SUBLANES * (g % per_chain) + g // per_chain, SUBLANES


def _to_chains(x, scr):
    slabs = scr.shape[0]
    for s in range(slabs):
        for g in range(TILE // SUBLANES):
            start, stride = _chain_address(g)
            scr[s, pl.ds(start, SUBLANES, stride=stride), :] = (
                x[g * SUBLANES:(g + 1) * SUBLANES, s * LANES:(s + 1) * LANES])
    return [jnp.concatenate([scr[s, j * SUBLANES:(j + 1) * SUBLANES, :] for s in range(slabs)],
                            axis=1) for j in range(CHAIN_LEN)]


def _from_chains(chains, scr):
    slabs = scr.shape[0]
    for s in range(slabs):
        for j, c in enumerate(chains):
            scr[s, j * SUBLANES:(j + 1) * SUBLANES, :] = c[:, s * LANES:(s + 1) * LANES]
    groups = []
    for g in range(TILE // SUBLANES):
        start, stride = _chain_address(g)
        groups.append(jnp.concatenate(
            [scr[s, pl.ds(start, SUBLANES, stride=stride), :] for s in range(slabs)], axis=1))
    return jnp.concatenate(groups, axis=0)


def _rglru_tile(xc, r_pre, i_pre, lam_row, reverse, h_in, scratch):
    a_scr, u_scr, h_scr = scratch
    r = _sigmoid(r_pre)
    i = _sigmoid(i_pre)
    log_a = -RG_C * r * _softplus(-lam_row)
    a = jnp.exp(log_a)
    u = jnp.sqrt(jnp.tanh(-log_a) * (a * a + 1.0)) * (i * xc)
    a_c = _to_chains(a, a_scr)
    u_c = _to_chains(u, u_scr)
    order = list(reversed(range(CHAIN_LEN))) if reverse else list(range(CHAIN_LEN))
    h_c = [None] * CHAIN_LEN
    p_c = [None] * CHAIN_LEN
    h_c[order[0]], p_c[order[0]] = u_c[order[0]], a_c[order[0]]
    for prev, j in zip(order[:-1], order[1:]):
        h_c[j] = a_c[j] * h_c[prev] + u_c[j]
        p_c[j] = a_c[j] * p_c[prev]
    p_tot, h_tot = p_c[order[-1]], h_c[order[-1]]
    sub = lax.broadcasted_iota(jnp.int32, (SUBLANES, 1), 0)
    k = 1
    while k < SUBLANES:
        if not reverse:
            valid = sub >= k
            p_s, h_s = pltpu.roll(p_tot, k, 0), pltpu.roll(h_tot, k, 0)
        else:
            valid = sub < SUBLANES - k
            p_s, h_s = pltpu.roll(p_tot, SUBLANES - k, 0), pltpu.roll(h_tot, SUBLANES - k, 0)
        h_tot = h_tot + p_tot * jnp.where(valid, h_s, 0.0)
        p_tot = p_tot * jnp.where(valid, p_s, 1.0)
        k *= 2
    if h_in is not None:
        h_tot = h_tot + p_tot * h_in
    if not reverse:
        carry = jnp.where(sub == 0, 0.0 if h_in is None else h_in, pltpu.roll(h_tot, 1, 0))
        h_out = h_tot[SUBLANES - 1:SUBLANES, :]
    else:
        carry = jnp.where(sub == SUBLANES - 1, 0.0 if h_in is None else h_in,
                          pltpu.roll(h_tot, SUBLANES - 1, 0))
        h_out = h_tot[0:1, :]
    h_c = [h + p * carry for h, p in zip(h_c, p_c)]
    return _from_chains(h_c, h_scr), h_out


def _rec_inputs(refs):
    qb, ff, fb, ib, gb, xcol, gcol = [r[...] for r in refs]
    return _silu(qb), ff, fb, ib, gb, xcol, gcol


def _hgrn_out(o_sum, gb, onorm_gain, ones_bd):
    ms = _group_sum(o_sum * o_sum, ones_bd) * (1.0 / DV_B)
    return o_sum * lax.rsqrt(ms + EPS) * onorm_gain * _silu(gb)


def _rec_ctx_kernel(*refs, layer, depth, n_cast):
    refs = refs[2 if layer else 0:]
    (q_ref, ff_ref, fb_ref, i_ref, g_ref, x_ref, gc_ref, lb_ref, on_ref, cw_ref, cb_ref, wg_ref,
     bg_ref, lam_ref, ones_ref) = refs[:15]
    ob_ref, oc_ref, st_ref, hs_ref = refs[15 + n_cast:19 + n_cast]
    scan_scratch = refs[19 + 2 * n_cast:]
    _cast_slabs(refs[15:15 + n_cast], refs[19 + n_cast:19 + 2 * n_cast])
    if layer == 0 and depth > 1:
        st_ref[0, 1:] = jnp.zeros((depth - 1,) + st_ref.shape[2:], F32)
        hs_ref[0, 1:] = jnp.zeros((depth - 1,) + hs_ref.shape[2:], F32)
    aux = _tile_aux()
    rows = aux[0]
    ones_bd = ones_ref[...]
    qh, ff, fb, ib, gb, xcol, gcol = _rec_inputs((q_ref, ff_ref, fb_ref, i_ref, g_ref, x_ref, gc_ref))
    o_sum = None
    for direction, fpre in enumerate((ff, fb)):
        lbd = None if layer == 0 else _hgrn_lower_bound(lb_ref, layer, direction, depth)
        g, kk = _hgrn_gates(fpre, lbd)
        o, k_hat, _ = _hgrn_tile(qh, kk, ib, g, direction == 1, aux, ones_bd, None,
                                 scan_scratch[3 + 2 * direction:5 + 2 * direction])
        o_sum = o if o_sum is None else o_sum + o
        full = _dot_tn(k_hat, ib.astype(BF16))
        for h in range(H_B):
            blk = full[h * DK_B:(h + 1) * DK_B, (h // 2) * 2 * DV_B:(h // 2 + 1) * 2 * DV_B]
            if h % 2:
                blk = pltpu.roll(blk, DV_B, 1)
            st_ref[0, 0, direction, h] = blk[:, 0:DV_B]
    ob_ref[...] = _hgrn_out(o_sum, gb, on_ref[...], ones_bd)

    zero_row = jnp.zeros((1, W_C), F32)
    xc = _conv(xcol, zero_row, jnp.zeros((2, W_C), F32), cw_ref, cb_ref, rows)
    gates = _dot(xc.astype(BF16), wg_ref[...]) + bg_ref[...]
    y_f, h_f = _rglru_tile(xc, gates[:, 0:W_C], gates[:, W_C:2 * W_C], lam_ref[0:1, :], False,
                           None, scan_scratch[0:3])
    y_b, h_b = _rglru_tile(xc, gates[:, 2 * W_C:3 * W_C], gates[:, 3 * W_C:], lam_ref[1:2, :], True,
                           None, scan_scratch[0:3])
    oc_ref[...] = (y_f + y_b) * _gelu_tanh(gcol)
    hs_ref[0, 0, 0:1, :] = h_f
    hs_ref[0, 0, 1:2, :] = h_b


def _scan_scratch(hgrn_directions):
    return [pltpu.VMEM((W_C // LANES, TILE, LANES), F32) for _ in range(3 + 2 * hgrn_directions)]


def _col_spec(col, row_map):
    return pl.BlockSpec((TILE, W_B), lambda *idx: (row_map(*idx), col))


COL_QB, COL_FF, COL_FB, COL_IB, COL_GB, COL_XC, COL_GC = range(7)
_REC_COLS = (COL_QB, COL_FF, COL_FB, COL_IB, COL_GB, COL_XC, COL_GC)


def _full_spec(a):
    nd = a.ndim
    return pl.BlockSpec(a.shape, lambda *idx: (0,) * nd)


def _recurrence_ctx(proj, layer, depth, consts, states, cast=()):
    t = proj.shape[0]
    n = t // TILE
    assert bool(states) == (layer > 0) and len(consts) == 8
    layers, at = (1, layer) if states else (depth, 0)
    cast_in, cast_out, cast_shape = _cast_specs(cast, layer, n)
    return pl.pallas_call(
        functools.partial(_rec_ctx_kernel, layer=layer, depth=depth, n_cast=len(cast)),
        grid=(n,),
        in_specs=([pl.BlockSpec(memory_space=pl.ANY) for _ in states]
                  + [_col_spec(c, lambda b: b) for c in _REC_COLS] + [_full_spec(a) for a in consts]
                  + cast_in),
        out_specs=[
            pl.BlockSpec((TILE, W_B), lambda b: (b, 0)),
            pl.BlockSpec((TILE, W_C), lambda b: (b, 0)),
            pl.BlockSpec((1, layers, 2, H_B, DK_B, DV_B), lambda b: (b, at, 0, 0, 0, 0)),
            pl.BlockSpec((1, layers, 2, W_C), lambda b: (b, at, 0, 0)),
        ] + cast_out,
        out_shape=[
            jax.ShapeDtypeStruct((t, W_B), F32),
            jax.ShapeDtypeStruct((t, W_C), F32),
            jax.ShapeDtypeStruct((n, depth, 2, H_B, DK_B, DV_B), F32),
            jax.ShapeDtypeStruct((n, depth, 2, W_C), F32),
        ] + cast_shape,
        input_output_aliases={i: 2 + i for i in range(len(states))},
        scratch_shapes=_scan_scratch(2),
        compiler_params=_params(("arbitrary",)),
        name="rec_ctx",
    )(*states, *([proj] * len(_REC_COLS)), *consts, *cast)


def _rec_lat_kernel(q_ref, f_ref, i_ref, g_ref, x_ref, xp_ref, xn_ref, gc_ref, s0_ref, h0_ref,
                    *rest, layer, depth, direction, n_tiles):
    if direction == 0:
        (lb_ref, on_ref, cw_ref, cb_ref, wg_ref, bg_ref, lam_ref, ones_ref,
         o_ref, y_ref, state, hstate, *scan_scratch) = rest
    else:
        (of_ref, yf_ref, lb_ref, on_ref, cw_ref, cb_ref, wg_ref, bg_ref, lam_ref, ones_ref,
         o_ref, y_ref, state, hstate, *scan_scratch) = rest
    step = pl.program_id(1)
    tile = step if direction == 0 else n_tiles - 1 - step
    reverse = direction == 1

    @pl.when(step == 0)
    def _():
        state[...] = s0_ref[0, 0]
        hstate[...] = h0_ref[0, 0]

    aux = _tile_aux()
    rows = aux[0]
    ones_bd = ones_ref[...]
    qh = _silu(q_ref[...])
    ib = i_ref[...]
    lbd = None if layer == 0 else _hgrn_lower_bound(lb_ref, layer, direction, depth)
    g, kk = _hgrn_gates(f_ref[...], lbd)
    o, k_hat, total = _hgrn_tile(qh, kk, ib, g, reverse, aux, ones_bd, state[...],
                                 scan_scratch[3:5])
    upd = jnp.where(_block_diag_mask(), _dot_tn(ib.astype(BF16), k_hat), 0.0)
    state[...] = state[...] * jnp.exp2(total) + upd

    prev_row = jnp.where(tile == 0, 0.0, xp_ref[TILE - 1:TILE, :])
    next_rows = jnp.where(tile == n_tiles - 1, 0.0, xn_ref[0:2, :])
    xc = _conv(x_ref[...], prev_row, next_rows, cw_ref, cb_ref, rows)
    gates = _dot(xc.astype(BF16), wg_ref[...]) + bg_ref[...]
    c0 = 2 * W_C * direction
    y, h_next = _rglru_tile(xc, gates[:, c0:c0 + W_C], gates[:, c0 + W_C:c0 + 2 * W_C],
                            lam_ref[direction:direction + 1, :], reverse, hstate[...],
                            scan_scratch[0:3])
    hstate[...] = h_next

    if direction == 0:
        o_ref[...] = o
        y_ref[...] = y
    else:
        o_ref[...] = _hgrn_out(of_ref[...] + o, g_ref[...], on_ref[...], ones_bd)
        y_ref[...] = (yf_ref[...] + y) * _gelu_tanh(gc_ref[...])


def _recurrence_lat(proj, n_batch, layer, depth, direction, state0, h0, prev, consts):
    t = proj.shape[0]
    n_tiles = t // (TILE * n_batch)

    def tile_of(b, s):
        return s if direction == 0 else n_tiles - 1 - s

    def row(b, s):
        return b * n_tiles + tile_of(b, s)

    def row_prev(b, s):
        return b * n_tiles + jnp.maximum(tile_of(b, s) - 1, 0)

    def row_next(b, s):
        return b * n_tiles + jnp.minimum(tile_of(b, s) + 1, n_tiles - 1)

    f_col = COL_FF if direction == 0 else COL_FB
    in_specs = [_col_spec(COL_QB, row), _col_spec(f_col, row), _col_spec(COL_IB, row),
                _col_spec(COL_GB, row), _col_spec(COL_XC, row), _col_spec(COL_XC, row_prev),
                _col_spec(COL_XC, row_next), _col_spec(COL_GC, row),
                pl.BlockSpec((1, 1, W_B, W_B), lambda b, s: (b, direction, 0, 0)),
                pl.BlockSpec((1, 1, 1, W_C), lambda b, s: (b, direction, 0, 0))]
    args = [proj] * 8 + [state0, h0]
    if direction == 1:
        in_specs += [pl.BlockSpec((TILE, W_B), lambda b, s: (row(b, s), 0)),
                     pl.BlockSpec((TILE, W_C), lambda b, s: (row(b, s), 0))]
        args += list(prev)
    in_specs += [_full_spec(a) for a in consts]
    args += list(consts)
    return pl.pallas_call(
        functools.partial(_rec_lat_kernel, layer=layer, depth=depth, direction=direction,
                          n_tiles=n_tiles),
        grid=(n_batch, n_tiles),
        in_specs=in_specs,
        out_specs=[pl.BlockSpec((TILE, W_B), lambda b, s: (row(b, s), 0)),
                   pl.BlockSpec((TILE, W_C), lambda b, s: (row(b, s), 0))],
        out_shape=[jax.ShapeDtypeStruct((t, W_B), F32), jax.ShapeDtypeStruct((t, W_C), F32)],
        scratch_shapes=[pltpu.VMEM((W_B, W_B), F32), pltpu.VMEM((1, W_C), F32)] + _scan_scratch(1),
        compiler_params=_params(("arbitrary", "arbitrary")),
        name="rec_lat_fwd" if direction == 0 else "rec_lat_bwd",
    )(*args)


def _out_kernel(x_ref, oa_ref, ob_ref, oc_ref, mod_ref, n2_ref, wo_ref, w1_ref, w2_ref, *refs):
    n_cast = len(refs) // 2
    y_ref = refs[n_cast]
    _cast_slabs(refs[:n_cast], refs[n_cast + 1:])
    x = x_ref[...]
    d = x.shape[-1]
    g1 = mod_ref[0, :, 2 * d:3 * d]
    sh2 = mod_ref[0, :, 3 * d:4 * d]
    sc2 = mod_ref[0, :, 4 * d:5 * d]
    g2 = mod_ref[0, :, 5 * d:6 * d]
    wa = H_A * DV_A
    mix = _dot(oa_ref[...].astype(BF16), wo_ref[0:wa, :])
    mix = mix + _dot(ob_ref[...].astype(BF16), wo_ref[wa:wa + W_B, :])
    mix = mix + _dot(oc_ref[...].astype(BF16), wo_ref[wa + W_B:, :])
    x1 = x + g1 * mix
    h = (_rms_rows(x1, n2_ref[...]) * (1.0 + sc2) + sh2).astype(BF16)
    acc = jnp.zeros_like(x)
    for c in range(w1_ref.shape[1] // D_FF_CHUNK):
        cs = slice(c * D_FF_CHUNK, (c + 1) * D_FF_CHUNK)
        f = jnp.maximum(_dot(h, w1_ref[:, cs]), 0.0)
        acc = acc + _dot((f * f).astype(BF16), w2_ref[cs, :])
    y_ref[...] = x1 + g2 * acc


def _out_mlp(x, oa, ob, oc, mod, tokens_per_mod, n2, w_out, w_ff1, w_ff2, cast=(), cast_layer=0):
    t, d = x.shape
    steps = t // ROW_TILE

    def resident(w):
        return pl.BlockSpec((None,) + w.shape[1:], lambda i: (0, 0, 0),
                            pipeline_mode=pl.Buffered(1))

    cast_in, cast_out, cast_shape = _cast_specs(cast, cast_layer, steps)
    return pl.pallas_call(
        _out_kernel,
        grid=(steps,),
        in_specs=[
            pl.BlockSpec((ROW_TILE, d), lambda i: (i, 0)),
            pl.BlockSpec((ROW_TILE, oa.shape[1]), lambda i: (i, 0)),
            pl.BlockSpec((ROW_TILE, ob.shape[1]), lambda i: (i, 0)),
            pl.BlockSpec((ROW_TILE, oc.shape[1]), lambda i: (i, 0)),
            pl.BlockSpec((1, 1, N_MOD * d), lambda i: (i * ROW_TILE // tokens_per_mod, 0, 0)),
            pl.BlockSpec((1, d), lambda i: (0, 0)),
            resident(w_out),
            resident(w_ff1),
            resident(w_ff2),
        ] + cast_in,
        out_specs=[pl.BlockSpec((ROW_TILE, d), lambda i: (i, 0))] + cast_out,
        out_shape=[jax.ShapeDtypeStruct((t, d), F32)] + cast_shape,
        compiler_params=_params(("arbitrary",)),
        name="out_mlp",
    )(x, oa, ob, oc, mod, n2, w_out, w_ff1, w_ff2, *cast)


def _ones_block_diag(n, group):
    idx = np.arange(n) // group
    return jnp.asarray((idx[:, None] == idx[None, :]).astype(np.float32), dtype=BF16)


def _rope_tables(seq):
    nf = DQK_A // 4
    inv = ROPE_BASE ** (-jnp.arange(nf, dtype=F32) / nf)
    pos = jnp.arange(seq)
    ang_r = (pos // GRID_W).astype(F32)[:, None] * inv
    ang_c = (pos % GRID_W).astype(F32)[:, None] * inv
    z = jnp.zeros_like(ang_r)
    cos = jnp.concatenate([jnp.cos(ang_r)] * 2 + [jnp.cos(ang_c)] * 2, axis=-1)
    sin_lo = jnp.concatenate([-jnp.sin(ang_r), z, -jnp.sin(ang_c), z], axis=-1)
    sin_hi = jnp.concatenate([z, jnp.sin(ang_r), z, jnp.sin(ang_c)], axis=-1)
    return tuple(jnp.tile(a, (1, 2)) for a in (cos, sin_lo, sin_hi))


def _gate_weights(rg_w_l):
    eye = jnp.eye(H_C, dtype=rg_w_l.dtype)
    w = jnp.einsum('dghij,hk->dghikj', rg_w_l, eye)
    w = w.reshape(2, 2, W_C, W_C)
    return jnp.transpose(w, (2, 0, 1, 3)).reshape(W_C, 4 * W_C).astype(BF16)


def _state_block_diag_t(s):
    eye = jnp.eye(H_B, dtype=s.dtype)
    w = jnp.einsum('...hdv,hk->...hvkd', s, eye)
    return w.reshape(s.shape[:-3] + (W_B, W_B))


def kernel(x_prompt, x_sample, cache_k, cache_v, state_hgrn, state_rglru, c, c_ctx, w_mod, b_mod,
           norm1, norm2, w_in, w_out, qk_norm, diff_lambda, subln, hgrn_lb, hgrn_onorm, conv_w,
           conv_b, rg_w, rg_b, rg_lambda, w_ff1, w_ff2):
    batch, seq, d = x_prompt.shape
    dec_batch, dec_seq, _ = x_sample.shape
    depth = w_in.shape[0]
    past = cache_k.shape[2]
    wq = H_A * 2 * DQK_A

    c8 = jnp.concatenate([c_ctx[None], c, jnp.zeros((8 - 1 - dec_batch, d), F32)], axis=0)
    mod_all = _modulation(c8, w_mod, b_mod)

    assert DQK_A == DV_B and seq == TILE
    ones_b = _ones_block_diag(W_B, DV_B)
    rope_tabs = _rope_tables(dec_seq)
    ck = cache_k.reshape(dec_batch, depth, past, wq)
    cv = cache_v.reshape(dec_batch, depth, past, H_A * DV_A)
    lb2 = hgrn_lb.reshape(depth * 2, W_B)

    w_f32 = (w_in, w_out, w_ff1, w_ff2)
    w_next = [w_in[0:1].astype(BF16)]

    yp = x_prompt.reshape(batch * seq, d)
    ys = x_sample.reshape(dec_batch * dec_seq, d)
    caches, states = (), ()
    for l in range(depth):
        lam_init = 0.8 - 0.6 * math.exp(-0.3 * l)
        mod_ctx = mod_all[l, 0:1][:, None, :]
        mod_lat = mod_all[l, 1:1 + dec_batch][:, None, :]
        n1 = norm1[l][None]
        n2 = norm2[l][None]
        gq = jnp.tile(qk_norm[l, 0], H_A * 2)[None] * (DQK_A ** -0.5 * LOG2E)
        gk = jnp.tile(qk_norm[l, 1], H_A * 2)[None]
        dl = diff_lambda[l]
        sg = subln[l][None]
        consts = (lb2, jnp.tile(hgrn_onorm[l], H_B)[None], conv_w[l], conv_b[l][None],
                  _gate_weights(rg_w[l]), rg_b[l].reshape(1, 4 * W_C), rg_lambda[l], ones_b)

        w_in_b = w_next[0]

        q, k, v_all, rest, kt_all = _in_projection(yp, mod_ctx, batch * seq, n1, w_in_b, 0,
                                                   gq, gk, caches=caches)
        caches = (kt_all, v_all)
        oa = _attention_ctx(q, k, v_all, l, seq, dl, sg, lam_init)
        ob, oc, st, hs, *w_rest = _recurrence_ctx(rest, l, depth, consts, states,
                                                  cast=w_f32[1:] if l == 0 else ())
        states = (st, hs)
        w_out_b, w1_b, w2_b = w_rest if l == 0 else w_next[1:]
        yp, *w_next = _out_mlp(yp, oa, ob, oc, mod_ctx, batch * seq, n2, w_out_b, w1_b, w2_b,
                               cast=w_f32 if l + 1 < depth else (), cast_layer=l + 1)

        q, k, v, rest = _in_projection(ys, mod_lat, dec_seq, n1, w_in_b, 0, gq, gk)
        oa = _attention_lat(q, k, v, dec_batch, ck, cv, l, rope_tabs, dl, sg, lam_init)
        s0 = _state_block_diag_t(state_hgrn[:, l])
        h0 = state_rglru[:, l][:, :, None, :]
        fwd = _recurrence_lat(rest, dec_batch, l, depth, 0, s0, h0, None, consts)
        ob, oc = _recurrence_lat(rest, dec_batch, l, depth, 1, s0, h0, fwd, consts)
        ys, = _out_mlp(ys, oa, ob, oc, mod_lat, dec_seq, n2, w_out_b, w1_b, w2_b)

    kt_all, v_all = caches
    new_k = jnp.transpose(kt_all.reshape(batch, depth, H_A, 2, DQK_A, seq), (0, 1, 5, 2, 3, 4))
    new_v = v_all.reshape(batch, depth, seq, H_A, DV_A)
    return (yp.reshape(batch, seq, d), ys.reshape(dec_batch, dec_seq, d), new_k, new_v,
            states[0], states[1])
```
